```python
import math
import jax, jax.numpy as jnp
from jax import lax
import numpy as np

D_MODEL = 1024
BATCH = 8
SEQ = 2048
DEPTH = 1
DEC_BATCH = 128
DEC_SEQ = 4
PAST_LEN = 16384
PAGE_SIZE = 128

HG_WIDTH = D_MODEL // 2
HG_DK = 128
HG_DV = 128
HG_HEADS = HG_WIDTH // HG_DK
HG_CHUNK = 64
CM_WIDTH = D_MODEL - HG_WIDTH
CM_GROUPS = 4
CM_GROUP_DIM = CM_WIDTH // CM_GROUPS
CM_CHUNK = 128
IN_WIDTH = 4 * HG_WIDTH + 2 * CM_WIDTH
N_MEM = 256
X_HEADS = 4
X_HEAD_DIM = D_MODEL // X_HEADS
D_FF = 4 * D_MODEL
EPS = 1e-6

kernel_name = "hybrid_hgrn2_gmlp_memxattn_step"


def rmsnorm(x, g):
    xf = x.astype(jnp.float32)
    y = xf * lax.rsqrt(jnp.mean(xf * xf, axis=-1, keepdims=True) + EPS)
    return (y * g.astype(jnp.float32)).astype(x.dtype)


def layernorm(x, g, b):
    xf = x.astype(jnp.float32)
    mu = jnp.mean(xf, axis=-1, keepdims=True)
    var = jnp.mean(jnp.square(xf - mu), axis=-1, keepdims=True)
    y = (xf - mu) * lax.rsqrt(var + EPS)
    return (y * g.astype(jnp.float32) + b.astype(jnp.float32)).astype(x.dtype)


def hgrn2_chunk(S, inp):
    q, k, logf, v = inp
    C = q.shape[1]
    b = jnp.cumsum(logf, axis=1)
    causal = jnp.tril(jnp.ones((C, C), dtype=bool))
    diff = b[:, :, None] - b[:, None, :]
    decay = jnp.exp(jnp.where(causal[None, :, :, None, None], diff, -jnp.inf))
    A = jnp.einsum('bthk,bshk,btshk->bhts', q, k, decay)
    o = (jnp.einsum('bhts,bshv->bthv', A, v)
         + jnp.einsum('bthk,bhkv->bthv', q * jnp.exp(b), S))
    b_last = b[:, -1]
    k_dec = k * jnp.exp(b_last[:, None] - b)
    S_new = jnp.exp(b_last)[..., None] * S + jnp.einsum('bshk,bshv->bhkv', k_dec, v)
    return S_new, o


def hgrn2_scan(S0, q, k, logf, v):
    B, L = q.shape[0], q.shape[1]
    C = min(HG_CHUNK, L)
    n = L // C
    def to_chunks(a):
        return jnp.moveaxis(a.reshape(B, n, C, HG_HEADS, a.shape[-1]), 1, 0)
    S_last, o = lax.scan(hgrn2_chunk, S0, (to_chunks(q), to_chunks(k), to_chunks(logf), to_chunks(v)))
    o = jnp.moveaxis(o, 0, 1).reshape(B, L, HG_HEADS, HG_DV)
    return S_last, o


def chunk_gmlp(u, vn, w_s, b_s):
    B, L, _ = u.shape
    T = min(CM_CHUNK, L)
    n = L // T
    uu = u.reshape(B, n, T, CM_GROUPS, CM_GROUP_DIM)
    vv = vn.reshape(B, n, T, CM_GROUPS, CM_GROUP_DIM)
    wm = jnp.where(jnp.tril(jnp.ones((T, T), dtype=bool))[None], w_s[:, :T, :T], 0.0).astype(vn.dtype)
    mixed = jnp.einsum('gts,bnsgc->bntgc', wm, vv) + b_s[:, :T].T.astype(vn.dtype)[None, None, :, :, None]
    return (uu * mixed).reshape(B, L, CM_WIDTH)


def decoder_layer(x, S0, mem_k, mem_v, lb, g_mix, w_in, hg_norm_g, ln_v_g, ln_v_b, w_s, b_s,
                  w_out, g_cross, w_cq, w_co, g_mlp, w_up, w_down):
    B, L, _ = x.shape
    hn = rmsnorm(x, g_mix)
    z = hn @ w_in
    zq, zf, zi, zg, zu, zv = jnp.split(
        z, [HG_WIDTH, 2 * HG_WIDTH, 3 * HG_WIDTH, 4 * HG_WIDTH, 4 * HG_WIDTH + CM_WIDTH], axis=-1)
    hs = (B, L, HG_HEADS, HG_DK)
    zf32 = zf.astype(jnp.float32).reshape(hs)
    q = jax.nn.silu(zq.astype(jnp.float32)).reshape(hs)
    f = lb + (1.0 - lb) * jax.nn.sigmoid(zf32)
    logf = jnp.log(f)
    k = (1.0 - lb) * jax.nn.sigmoid(-zf32)
    v = zi.astype(jnp.float32).reshape(B, L, HG_HEADS, HG_DV)
    S_new, o = hgrn2_scan(S0.astype(jnp.float32), q, k, logf, v)
    o = o * lax.rsqrt(jnp.mean(o * o, axis=-1, keepdims=True) + EPS) * hg_norm_g.astype(jnp.float32).reshape(HG_HEADS, HG_DV)
    o_hg = (o.reshape(B, L, HG_WIDTH) * jax.nn.sigmoid(zg.astype(jnp.float32))).astype(x.dtype)
    u = jax.nn.gelu(zu)
    vn = layernorm(jax.nn.gelu(zv), ln_v_g, ln_v_b)
    o_cm = chunk_gmlp(u, vn, w_s, b_s)
    x = x + jnp.concatenate([o_hg, o_cm], axis=-1) @ w_out
    hq = (rmsnorm(x, g_cross) @ w_cq).reshape(B, L, X_HEADS, X_HEAD_DIM)
    s = jnp.einsum('blhd,bmhd->bhlm', hq.astype(jnp.float32), mem_k.astype(jnp.float32)) * (X_HEAD_DIM ** -0.5)
    p = jax.nn.softmax(s, axis=-1).astype(x.dtype)
    ca = jnp.einsum('bhlm,bmhd->blhd', p, mem_v.astype(x.dtype)).reshape(B, L, D_MODEL)
    x = x + ca @ w_co
    hm = rmsnorm(x, g_mlp) @ w_up
    x = x + jnp.square(jax.nn.relu(hm)) @ w_down
    return x, S_new, vn


def setup_inputs(seed: int = 0) -> dict:
    key = jax.random.key(seed)
    ks = jax.random.split(key, 32)
    def nrm(k, shape, scale):
        return jax.random.normal(k, shape, jnp.float32) * scale
    def gain(k, shape):
        return 1.0 + 0.02 * jax.random.normal(k, shape, jnp.float32)
    return {
        "x_prompt": nrm(ks[0], (BATCH, SEQ, D_MODEL), 1.0),
        "x_sample": nrm(ks[1], (DEC_BATCH, DEC_SEQ, D_MODEL), 1.0),
        "mem_prompt": nrm(ks[2], (BATCH, N_MEM, D_MODEL), 1.0),
        "state_hgrn": nrm(ks[3], (DEPTH, DEC_BATCH, HG_HEADS, HG_DK, HG_DV), 0.3),
        "cache_mem_k": nrm(ks[4], (DEPTH, DEC_BATCH, N_MEM, X_HEADS, X_HEAD_DIM), 1.0),
        "cache_mem_v": nrm(ks[5], (DEPTH, DEC_BATCH, N_MEM, X_HEADS, X_HEAD_DIM), 1.0),
        "lb_param": nrm(ks[6], (DEPTH + 1, HG_WIDTH), 0.5),
        "g_mix": gain(ks[7], (DEPTH, D_MODEL)),
        "w_in": nrm(ks[8], (DEPTH, D_MODEL, IN_WIDTH), D_MODEL ** -0.5),
        "hg_norm_g": gain(ks[9], (DEPTH, HG_WIDTH)),
        "ln_v_g": gain(ks[10], (DEPTH, CM_WIDTH)),
        "ln_v_b": nrm(ks[11], (DEPTH, CM_WIDTH), 0.02),
        "w_s": nrm(ks[12], (DEPTH, CM_GROUPS, CM_CHUNK, CM_CHUNK), CM_CHUNK ** -0.5),
        "b_s": gain(ks[13], (DEPTH, CM_GROUPS, CM_CHUNK)),
        "w_out": nrm(ks[14], (DEPTH, D_MODEL, D_MODEL), D_MODEL ** -0.5),
        "g_cross": gain(ks[15], (DEPTH, D_MODEL)),
        "g_mem": gain(ks[16], (DEPTH, D_MODEL)),
        "w_cq": nrm(ks[17], (DEPTH, D_MODEL, D_MODEL), D_MODEL ** -0.5),
        "w_ck": nrm(ks[18], (DEPTH, D_MODEL, D_MODEL), D_MODEL ** -0.5),
        "w_cv": nrm(ks[19], (DEPTH, D_MODEL, D_MODEL), D_MODEL ** -0.5),
        "w_co": nrm(ks[20], (DEPTH, D_MODEL, D_MODEL), D_MODEL ** -0.5),
        "g_mlp": gain(ks[21], (DEPTH, D_MODEL)),
        "w_up": nrm(ks[22], (DEPTH, D_MODEL, D_FF), D_MODEL ** -0.5),
        "w_down": nrm(ks[23], (DEPTH, D_FF, D_MODEL), D_FF ** -0.5),
        "g_final": gain(ks[24], (D_MODEL,)),
    }


def reference(x_prompt, x_sample, mem_prompt, state_hgrn, cache_mem_k, cache_mem_v,
              lb_param, g_mix, w_in, hg_norm_g, ln_v_g, ln_v_b, w_s, b_s, w_out,
              g_cross, g_mem, w_cq, w_ck, w_cv, w_co, g_mlp, w_up, w_down, g_final):
    lb_all = jnp.cumsum(jax.nn.softmax(lb_param.astype(jnp.float32), axis=0), axis=0)
    xp, xs = x_prompt, x_sample
    S_p_list, S_s_list, mk_list, mv_list, cmv_list = [], [], [], [], []
    for l in range(DEPTH):
        lb = lb_all[l].reshape(HG_HEADS, HG_DK)
        mem_n = rmsnorm(mem_prompt, g_mem[l])
        mk = (mem_n @ w_ck[l]).reshape(BATCH if False else mem_prompt.shape[0], N_MEM, X_HEADS, X_HEAD_DIM)
        mv = (mem_n @ w_cv[l]).reshape(mem_prompt.shape[0], N_MEM, X_HEADS, X_HEAD_DIM)
        S0_p = jnp.zeros((xp.shape[0], HG_HEADS, HG_DK, HG_DV), jnp.float32)
        xp, S_p, _ = decoder_layer(xp, S0_p, mk, mv, lb, g_mix[l], w_in[l], hg_norm_g[l], ln_v_g[l], ln_v_b[l],
                                   w_s[l], b_s[l], w_out[l], g_cross[l], w_cq[l], w_co[l], g_mlp[l], w_up[l], w_down[l])
        xs, S_s, vn_s = decoder_layer(xs, state_hgrn[l], cache_mem_k[l], cache_mem_v[l], lb, g_mix[l], w_in[l],
                                      hg_norm_g[l], ln_v_g[l], ln_v_b[l], w_s[l], b_s[l], w_out[l], g_cross[l],
                                      w_cq[l], w_co[l], g_mlp[l], w_up[l], w_down[l])
        S_p_list.append(S_p.astype(state_hgrn.dtype))
        S_s_list.append(S_s.astype(state_hgrn.dtype))
        mk_list.append(mk.astype(cache_mem_k.dtype))
        mv_list.append(mv.astype(cache_mem_v.dtype))
        cmv_list.append(vn_s)
    y_prompt = rmsnorm(xp, g_final)
    y_sample = rmsnorm(xs, g_final)
    state_hgrn_prompt = jnp.stack(S_p_list)
    state_hgrn_sample = jnp.stack(S_s_list)
    mem_k_prompt = jnp.stack(mk_list)
    mem_v_prompt = jnp.stack(mv_list)
    cm_v_sample = jnp.stack(cmv_list)
    return (y_prompt, y_sample, state_hgrn_prompt, state_hgrn_sample, mem_k_prompt, mem_v_prompt, cm_v_sample)
```

```python
import functools

import jax
import jax.numpy as jnp
from jax import lax
from jax.experimental import pallas as pl
from jax.experimental.pallas import tpu as pltpu

F32 = jnp.float32
BF16 = jnp.bfloat16

D_MODEL = 1024
HG_WIDTH = 512
HG_HEADS = 4
HG_DK = 128
CM_WIDTH = 512
CM_GROUPS = 4
CM_GROUP_DIM = 128
IN_WIDTH = 4 * HG_WIDTH + 2 * CM_WIDTH
N_MEM = 256
X_HEADS = 4
X_HEAD_DIM = 256
D_FF = 4096
EPS = 1e-6

CHUNK = 128
SUB = 16
VMEM_LIMIT_BYTES = 56 * 1024 * 1024


def _bdot(a, b):
    return jnp.dot(a.astype(BF16), b.astype(BF16), preferred_element_type=F32)


def _bdot_nt(a, b):
    return lax.dot_general(a.astype(BF16), b.astype(BF16), (((1,), (1,)), ((), ())),
                           preferred_element_type=F32)


def _rmsnorm(x, g):
    ms = jnp.mean(x * x, axis=-1, keepdims=True)
    return x * lax.rsqrt(ms + EPS) * g


def _sigmoid(x):
    return 0.5 * jnp.tanh(0.5 * x) + 0.5


def _gelu(x):
    return 0.5 * x * (1.0 + jnp.tanh(0.7978845608028654 * (x + 0.044715 * (x * x * x))))


def _seg_cumsum(x, seg):
    n = x.shape[0]
    pos = lax.broadcasted_iota(jnp.int32, x.shape, 0) % seg
    s = 1
    while s < seg:
        x = x + jnp.where(pos >= s, pltpu.roll(x, s, 0), 0.0)
        s *= 2
    del n
    return x


def _mix_kernel(*refs, sample, tm):
    if sample:
        (x_ref, s0_ref, lb_ref, gmix_ref, win_ref, hgn_ref, lng_ref, lnb_ref, wmix_ref, bcol_ref,
         wout_ref, h_ref, sout_ref, vn_ref, z_ref, ocat_ref) = refs
        st_ref = None
    else:
        (x_ref, lb_ref, gmix_ref, win_ref, hgn_ref, lng_ref, lnb_ref, wmix_ref, bcol_ref,
         wout_ref, h_ref, sout_ref, z_ref, ocat_ref, st_ref) = refs
        s0_ref = vn_ref = None

    x = x_ref[0]
    xn = _rmsnorm(x, gmix_ref[...]).astype(BF16)
    z_ref[...] = jnp.dot(xn, win_ref[...], preferred_element_type=F32)

    lbp = lb_ref[...]
    lbe = jnp.exp(lbp - jnp.max(lbp, axis=0, keepdims=True))
    lb_all = lbe[0:1] / jnp.sum(lbe, axis=0, keepdims=True)

    if not sample:
        @pl.when(pl.program_id(1) == 0)
        def _():
            st_ref[...] = jnp.zeros_like(st_ref)

    row = lax.broadcasted_iota(jnp.int32, (CHUNK, CHUNK), 0)
    col = lax.broadcasted_iota(jnp.int32, (CHUNK, CHUNK), 1)
    if sample:
        mixmask = (row // 4 == col // 4) & (row >= col)
    else:
        mixmask = row >= col
    eye = row == col
    r16 = lax.broadcasted_iota(jnp.int32, (SUB, HG_DK), 0)
    r8 = lax.broadcasted_iota(jnp.int32, (8, HG_DK), 0)

    def chunk_body(c, carry):
        r0 = pl.multiple_of(c * CHUNK, CHUNK)
        rows = pl.ds(r0, CHUNK)
        for h in range(HG_HEADS):
            cs = slice(h * HG_DK, (h + 1) * HG_DK)
            zq = z_ref[rows, cs]
            zf = z_ref[rows, HG_WIDTH + h * HG_DK:HG_WIDTH + (h + 1) * HG_DK]
            v = z_ref[rows, 2 * HG_WIDTH + h * HG_DK:2 * HG_WIDTH + (h + 1) * HG_DK]
            zg = z_ref[rows, 3 * HG_WIDTH + h * HG_DK:3 * HG_WIDTH + (h + 1) * HG_DK]
            lb = lb_all[:, cs]
            q = zq * _sigmoid(zq)
            th = jnp.tanh(0.5 * zf)
            logf = jnp.log(lb + (1.0 - lb) * (0.5 + 0.5 * th))
            k = (1.0 - lb) * (0.5 - 0.5 * th)
            bcum = _seg_cumsum(logf, 4 if sample else CHUNK)
            eb = jnp.exp(bcum)
            qd = q * eb

            o_parts = []
            if sample:
                for gb in range(CHUNK // SUB):
                    blk = slice(gb * SUB, (gb + 1) * SUB)
                    qd_b, v_b, eb_b = qd[blk], v[blk], eb[blk]
                    k_b, b_b = k[blk], bcum[blk]
                    inter = jnp.zeros((SUB, HG_DK), F32)
                    for j in range(4):
                        req = c * (CHUNK // 4) + gb * 4 + j
                        s0 = s0_ref[req, h]
                        last = 4 * j + 3
                        res = _bdot(qd_b, s0)
                        inter = jnp.where(r16 // 4 == j, res, inter)
                        kd = jnp.where(r16 // 4 == j, k_b * jnp.exp(b_b[last:last + 1] - b_b), 0.0)
                        upd = lax.dot_general(kd.astype(BF16), v_b.astype(BF16),
                                              (((0,), (0,)), ((), ())), preferred_element_type=F32)
                        dcol = jnp.sum(jnp.where(eye, eb_b[last:last + 1], 0.0), axis=-1, keepdims=True)
                        sout_ref[req, h] = dcol * s0 + upd
                    o_parts.append(inter)
            else:
                blast = bcum[CHUNK - 1:CHUNK]
                kd = k * jnp.exp(blast - bcum)
                st = st_ref[h]
                inter = _bdot_nt(qd, st)
                st_ref[h] = st * eb[CHUNK - 1:CHUNK] + _bdot(v.T, kd)
                a_rows = [jnp.zeros((SUB, CHUNK), F32)]
                for i in range(1, CHUNK // SUB):
                    n = i * SUB
                    ref_b = bcum[n - 1:n]
                    qt = q[n:n + SUB] * jnp.exp(bcum[n:n + SUB] - ref_b)
                    kt = k[:n] * jnp.exp(ref_b - bcum[:n])
                    kt = jnp.concatenate([kt, jnp.zeros((CHUNK - n, HG_DK), F32)], axis=0)
                    a_rows.append(_bdot_nt(qt, kt))
                a_off = jnp.concatenate(a_rows, axis=0)
                inter = inter + _bdot(a_off, v)
                for gb in range(CHUNK // SUB):
                    o_parts.append(inter[gb * SUB:(gb + 1) * SUB])

            o_rows = []
            for gb in range(CHUNK // SUB):
                blk = slice(gb * SUB, (gb + 1) * SUB)
                q_b, k_b, v_b, b_b = q[blk], k[blk], v[blk], bcum[blk]
                acc_lo = o_parts[gb][:8]
                acc_hi = o_parts[gb][8:]
                for s in range(SUB):
                    b_s, k_s, v_s = b_b[s:s + 1], k_b[s:s + 1], v_b[s:s + 1]
                    if s < 8:
                        if sample:
                            m = (r8 >= s) & (r8 // 4 == s // 4)
                        else:
                            m = r8 >= s
                        p = q_b[:8] * jnp.exp(jnp.where(m, b_b[:8] - b_s, -jnp.inf)) * k_s
                        acc_lo = acc_lo + jnp.sum(p, axis=-1, keepdims=True) * v_s
                    if (not sample) or s >= 8:
                        if s < 8:
                            p = q_b[8:] * jnp.exp(b_b[8:] - b_s) * k_s
                        else:
                            if sample:
                                m = (r8 + 8 >= s) & ((r8 + 8) // 4 == s // 4)
                            else:
                                m = r8 + 8 >= s
                            p = q_b[8:] * jnp.exp(jnp.where(m, b_b[8:] - b_s, -jnp.inf)) * k_s
                        acc_hi = acc_hi + jnp.sum(p, axis=-1, keepdims=True) * v_s
                o_rows.append(acc_lo)
                o_rows.append(acc_hi)
            o = jnp.concatenate(o_rows, axis=0)
            o = o * lax.rsqrt(jnp.mean(o * o, axis=-1, keepdims=True) + EPS) * hgn_ref[:, cs]
            ocat_ref[rows, cs] = (o * _sigmoid(zg)).astype(BF16)

        u = _gelu(z_ref[rows, 4 * HG_WIDTH:4 * HG_WIDTH + CM_WIDTH])
        gv = _gelu(z_ref[rows, 4 * HG_WIDTH + CM_WIDTH:IN_WIDTH])
        mu = jnp.mean(gv, axis=-1, keepdims=True)
        dv = gv - mu
        var = jnp.mean(dv * dv, axis=-1, keepdims=True)
        vn = dv * lax.rsqrt(var + EPS) * lng_ref[...] + lnb_ref[...]
        if sample:
            vn_ref[rows, :] = vn
        for g in range(CM_GROUPS):
            gs = slice(g * CM_GROUP_DIM, (g + 1) * CM_GROUP_DIM)
            wm = jnp.where(mixmask, wmix_ref[g], 0.0)
            mixed = _bdot(wm, vn[:, gs]) + bcol_ref[:, g:g + 1]
            ocat_ref[rows, HG_WIDTH + g * CM_GROUP_DIM:HG_WIDTH + (g + 1) * CM_GROUP_DIM] = (
                u[:, gs] * mixed).astype(BF16)
        return carry

    lax.fori_loop(0, tm // CHUNK, chunk_body, 0)

    h_ref[0] = x + jnp.dot(ocat_ref[...], wout_ref[...], preferred_element_type=F32)

    if not sample:
        @pl.when(pl.program_id(1) == pl.num_programs(1) - 1)
        def _():
            for h in range(HG_HEADS):
                sout_ref[0, h] = st_ref[h].T


def _const_spec(shape):
    nd = len(shape)
    return pl.BlockSpec(shape, lambda *_: (0,) * nd, pipeline_mode=pl.Buffered(1))


def _mix_prompt(x, lb_param, g_mix, w_in, hg_norm_g, ln_v_g, ln_v_b, wmix, bcol, w_out, tm):
    B, L, _ = x.shape
    nt = L // tm
    in_specs = [
        pl.BlockSpec((1, tm, D_MODEL), lambda b, i: (b, i, 0)),
        _const_spec(lb_param.shape), _const_spec(g_mix.shape), _const_spec(w_in.shape),
        _const_spec(hg_norm_g.shape), _const_spec(ln_v_g.shape), _const_spec(ln_v_b.shape),
        _const_spec(wmix.shape), _const_spec(bcol.shape), _const_spec(w_out.shape),
    ]
    out_specs = [
        pl.BlockSpec((1, tm, D_MODEL), lambda b, i: (b, i, 0)),
        pl.BlockSpec((1, HG_HEADS, HG_DK, HG_DK), lambda b, i: (b, 0, 0, 0)),
    ]
    return pl.pallas_call(
        functools.partial(_mix_kernel, sample=False, tm=tm),
        grid=(B, nt),
        in_specs=in_specs,
        out_specs=out_specs,
        out_shape=[jax.ShapeDtypeStruct((B, L, D_MODEL), F32),
                   jax.ShapeDtypeStruct((B, HG_HEADS, HG_DK, HG_DK), F32)],
        scratch_shapes=[pltpu.VMEM((tm, IN_WIDTH), F32), pltpu.VMEM((tm, D_MODEL), BF16),
                        pltpu.VMEM((HG_HEADS, HG_DK, HG_DK), F32)],
        compiler_params=pltpu.CompilerParams(dimension_semantics=("arbitrary", "arbitrary"),
                                             vmem_limit_bytes=VMEM_LIMIT_BYTES),
        name="mix_prompt",
    )(x, lb_param, g_mix, w_in, hg_norm_g, ln_v_g, ln_v_b, wmix, bcol, w_out)


def _mix_sample(x, s0, lb_param, g_mix, w_in, hg_norm_g, ln_v_g, ln_v_b, wmix, bcol, w_out, tm):
    nt, _, _ = x.shape
    nreq = tm // 4
    in_specs = [
        pl.BlockSpec((1, tm, D_MODEL), lambda i: (i, 0, 0)),
        pl.BlockSpec((nreq, HG_HEADS, HG_DK, HG_DK), lambda i: (i, 0, 0, 0)),
        _const_spec(lb_param.shape), _const_spec(g_mix.shape), _const_spec(w_in.shape),
        _const_spec(hg_norm_g.shape), _const_spec(ln_v_g.shape), _const_spec(ln_v_b.shape),
        _const_spec(wmix.shape), _const_spec(bcol.shape), _const_spec(w_out.shape),
    ]
    out_specs = [
        pl.BlockSpec((1, tm, D_MODEL), lambda i: (i, 0, 0)),
        pl.BlockSpec((nreq, HG_HEADS, HG_DK, HG_DK), lambda i: (i, 0, 0, 0)),
        pl.BlockSpec((tm, CM_WIDTH), lambda i: (i, 0)),
    ]
    return pl.pallas_call(
        functools.partial(_mix_kernel, sample=True, tm=tm),
        grid=(nt,),
        in_specs=in_specs,
        out_specs=out_specs,
        out_shape=[jax.ShapeDtypeStruct((nt, tm, D_MODEL), F32),
                   jax.ShapeDtypeStruct(s0.shape, F32),
                   jax.ShapeDtypeStruct((nt * tm, CM_WIDTH), F32)],
        scratch_shapes=[pltpu.VMEM((tm, IN_WIDTH), F32), pltpu.VMEM((tm, D_MODEL), BF16)],
        compiler_params=pltpu.CompilerParams(dimension_semantics=("arbitrary",),
                                             vmem_limit_bytes=VMEM_LIMIT_BYTES),
        name="mix_sample",
    )(x, s0, lb_param, g_mix, w_in, hg_norm_g, ln_v_g, ln_v_b, wmix, bcol, w_out)


def _memkv_kernel(m_ref, g_ref, wk_ref, wv_ref, k_ref, v_ref):
    mn = _rmsnorm(m_ref[...], g_ref[...]).astype(BF16)
    k_ref[...] = jnp.dot(mn, wk_ref[...], preferred_element_type=F32)
    v_ref[...] = jnp.dot(mn, wv_ref[...], preferred_element_type=F32)


def _memkv(mem, g_mem, w_ck, w_cv, tm):
    n = mem.shape[0]
    return pl.pallas_call(
        _memkv_kernel,
        grid=(n // tm,),
        in_specs=[pl.BlockSpec((tm, D_MODEL), lambda i: (i, 0)), _const_spec(g_mem.shape),
                  _const_spec(w_ck.shape), _const_spec(w_cv.shape)],
        out_specs=[pl.BlockSpec((tm, D_MODEL), lambda i: (i, 0)),
                   pl.BlockSpec((tm, D_MODEL), lambda i: (i, 0))],
        out_shape=[jax.ShapeDtypeStruct((n, D_MODEL), F32), jax.ShapeDtypeStruct((n, D_MODEL), F32)],
        compiler_params=pltpu.CompilerParams(dimension_semantics=("arbitrary",),
                                             vmem_limit_bytes=VMEM_LIMIT_BYTES),
        name="memkv",
    )(mem, g_mem, w_ck, w_cv)


def _softmax_rows(s):
    m = jnp.max(s, axis=-1, keepdims=True)
    e = jnp.exp(s - m)
    return e / jnp.sum(e, axis=-1, keepdims=True)


def _post_kernel(*refs, with_attn):
    if with_attn:
        (x_ref, mk_ref, mv_ref, gc_ref, wcq_ref, wco_ref, gm_ref, wup_ref, wdn_ref, gf_ref,
         y_ref, ca_s, hm_s) = refs
        x = x_ref[0]
        hq = jnp.dot(_rmsnorm(x, gc_ref[...]).astype(BF16), wcq_ref[...], preferred_element_type=F32)
        scale = X_HEAD_DIM ** -0.5
        for h in range(X_HEADS):
            hs = slice(h * X_HEAD_DIM, (h + 1) * X_HEAD_DIM)
            s = _bdot_nt(hq[:, hs], mk_ref[0, :, hs]) * scale
            p = _softmax_rows(s)
            ca_s[:, hs] = _bdot(p, mv_ref[0, :, hs]).astype(BF16)
        ca = ca_s[...]
    else:
        (x_ref, ca_ref, wco_ref, gm_ref, wup_ref, wdn_ref, gf_ref, y_ref, hm_s) = refs
        x = x_ref[0]
        ca = ca_ref[...]
    x = x + jnp.dot(ca, wco_ref[...], preferred_element_type=F32)
    hn = _rmsnorm(x, gm_ref[...]).astype(BF16)
    nc = D_FF // D_MODEL
    for c in range(nc):
        fs = slice(c * D_MODEL, (c + 1) * D_MODEL)
        hm = jnp.maximum(jnp.dot(hn, wup_ref[:, fs], preferred_element_type=F32), 0.0)
        hm_s[:, fs] = (hm * hm).astype(BF16)
    x = x + jnp.dot(hm_s[...], wdn_ref[...], preferred_element_type=F32)
    y_ref[0] = _rmsnorm(x, gf_ref[...])


def _post_prompt(x, mk, mv, g_cross, w_cq, w_co, g_mlp, w_up, w_down, g_final, tm):
    B, L, _ = x.shape
    in_specs = [
        pl.BlockSpec((1, tm, D_MODEL), lambda b, i: (b, i, 0)),
        pl.BlockSpec((1, N_MEM, D_MODEL), lambda b, i: (b, 0, 0)),
        pl.BlockSpec((1, N_MEM, D_MODEL), lambda b, i: (b, 0, 0)),
        _const_spec(g_cross.shape), _const_spec(w_cq.shape), _const_spec(w_co.shape),
        _const_spec(g_mlp.shape), _const_spec(w_up.shape), _const_spec(w_down.shape),
        _const_spec(g_final.shape),
    ]
    return pl.pallas_call(
        functools.partial(_post_kernel, with_attn=True),
        grid=(B, L // tm),
        in_specs=in_specs,
        out_specs=pl.BlockSpec((1, tm, D_MODEL), lambda b, i: (b, i, 0)),
        out_shape=jax.ShapeDtypeStruct((B, L, D_MODEL), F32),
        scratch_shapes=[pltpu.VMEM((tm, D_MODEL), BF16), pltpu.VMEM((tm, D_FF), BF16)],
        compiler_params=pltpu.CompilerParams(dimension_semantics=("arbitrary", "arbitrary"),
                                             vmem_limit_bytes=VMEM_LIMIT_BYTES),
        name="post_prompt",
    )(x, mk, mv, g_cross, w_cq, w_co, g_mlp, w_up, w_down, g_final)


def _post_sample(x, ca, w_co, g_mlp, w_up, w_down, g_final):
    nt, tm, _ = x.shape
    in_specs = [
        pl.BlockSpec((1, tm, D_MODEL), lambda i: (i, 0, 0)),
        pl.BlockSpec((tm, D_MODEL), lambda i: (i, 0)),
        _const_spec(w_co.shape), _const_spec(g_mlp.shape), _const_spec(w_up.shape),
        _const_spec(w_down.shape), _const_spec(g_final.shape),
    ]
    return pl.pallas_call(
        functools.partial(_post_kernel, with_attn=False),
        grid=(nt,),
        in_specs=in_specs,
        out_specs=pl.BlockSpec((1, tm, D_MODEL), lambda i: (i, 0, 0)),
        out_shape=jax.ShapeDtypeStruct((nt, tm, D_MODEL), F32),
        scratch_shapes=[pltpu.VMEM((tm, D_FF), BF16)],
        compiler_params=pltpu.CompilerParams(dimension_semantics=("arbitrary",),
                                             vmem_limit_bytes=VMEM_LIMIT_BYTES),
        name="post_sample",
    )(x, ca, w_co, g_mlp, w_up, w_down, g_final)


def _xattn_sample_kernel(x_ref, k_ref, v_ref, gc_ref, wcq_ref, ca_ref, *, nreq):
    hq = jnp.dot(_rmsnorm(x_ref[...], gc_ref[...]).astype(BF16), wcq_ref[...],
                 preferred_element_type=F32)
    scale = X_HEAD_DIM ** -0.5
    first = lax.broadcasted_iota(jnp.int32, (8, X_HEAD_DIM), 0) < 4
    for pr in range(nreq // 2):
        for h in range(X_HEADS):
            hs = slice(h * X_HEAD_DIM, (h + 1) * X_HEAD_DIM)
            q = hq[8 * pr:8 * pr + 8, hs].astype(BF16)
            outs = []
            for r in (2 * pr, 2 * pr + 1):
                s = _bdot_nt(q, k_ref[r, :, hs]) * scale
                outs.append(_bdot(_softmax_rows(s), v_ref[r, :, hs]))
            ca_ref[8 * pr:8 * pr + 8, hs] = jnp.where(first, outs[0], outs[1]).astype(BF16)


def _xattn_sample(x, ck, cv, g_cross, w_cq, nreq):
    n = x.shape[0]
    nb = ck.shape[0]
    return pl.pallas_call(
        functools.partial(_xattn_sample_kernel, nreq=nreq),
        grid=(nb // nreq,),
        in_specs=[pl.BlockSpec((4 * nreq, D_MODEL), lambda i: (i, 0)),
                  pl.BlockSpec((nreq, N_MEM, D_MODEL), lambda i: (i, 0, 0)),
                  pl.BlockSpec((nreq, N_MEM, D_MODEL), lambda i: (i, 0, 0)),
                  _const_spec(g_cross.shape), _const_spec(w_cq.shape)],
        out_specs=pl.BlockSpec((4 * nreq, D_MODEL), lambda i: (i, 0)),
        out_shape=jax.ShapeDtypeStruct((n, D_MODEL), BF16),
        compiler_params=pltpu.CompilerParams(dimension_semantics=("arbitrary",),
                                             vmem_limit_bytes=VMEM_LIMIT_BYTES),
        name="xattn_sample",
    )(x, ck, cv, g_cross, w_cq)


def kernel(x_prompt, x_sample, mem_prompt, state_hgrn, cache_mem_k, cache_mem_v, lb_param, g_mix,
           w_in, hg_norm_g, ln_v_g, ln_v_b, w_s, b_s, w_out, g_cross, g_mem, w_cq, w_ck, w_cv, w_co,
           g_mlp, w_up, w_down, g_final):
    B, L, _ = x_prompt.shape
    DB, DL, _ = x_sample.shape
    assert DL == 4 and g_mix.shape[0] == 1

    row = lambda a: a.reshape(1, -1)
    win_b, wout_b = w_in[0].astype(BF16), w_out[0].astype(BF16)
    wcq_b, wco_b = w_cq[0].astype(BF16), w_co[0].astype(BF16)
    wck_b, wcv_b = w_ck[0].astype(BF16), w_cv[0].astype(BF16)
    wup_b, wdn_b = w_up[0].astype(BF16), w_down[0].astype(BF16)
    gmix, hgn, lng, lnb = row(g_mix[0]), row(hg_norm_g[0]), row(ln_v_g[0]), row(ln_v_b[0])
    gcr, gmem, gmlp, gfin = row(g_cross[0]), row(g_mem[0]), row(g_mlp[0]), row(g_final)
    wmix_p = w_s[0]
    bcol_p = b_s[0].T
    wmix_s = jnp.tile(w_s[0][:, :DL, :DL], (1, CHUNK // DL, CHUNK // DL))
    bcol_s = jnp.tile(b_s[0][:, :DL].T, (CHUNK // DL, 1))

    mk, mv = _memkv(mem_prompt.reshape(B * N_MEM, D_MODEL), gmem, wck_b, wcv_b, tm=512)
    h_p, s_p = _mix_prompt(x_prompt, lb_param, gmix, win_b, hgn, lng, lnb, wmix_p, bcol_p, wout_b,
                           tm=512)
    y_p = _post_prompt(h_p, mk.reshape(B, N_MEM, D_MODEL), mv.reshape(B, N_MEM, D_MODEL), gcr,
                       wcq_b, wco_b, gmlp, wup_b, wdn_b, gfin, tm=512)

    tm_s = 128
    xs = x_sample.reshape(DB * DL // tm_s, tm_s, D_MODEL)
    h_s, s_s, vn_s = _mix_sample(xs, state_hgrn[0], lb_param, gmix, win_b, hgn, lng, lnb, wmix_s,
                                 bcol_s, wout_b, tm=tm_s)
    ca_s = _xattn_sample(h_s.reshape(DB * DL, D_MODEL), cache_mem_k[0].reshape(DB, N_MEM, D_MODEL),
                         cache_mem_v[0].reshape(DB, N_MEM, D_MODEL), gcr, wcq_b, nreq=8)
    y_s = _post_sample(h_s.reshape(1, DB * DL, D_MODEL), ca_s, wco_b, gmlp, wup_b, wdn_b, gfin)

    return (y_p, y_s.reshape(DB, DL, D_MODEL), s_p[None], s_s[None],
            mk.reshape(1, B, N_MEM, X_HEADS, X_HEAD_DIM), mv.reshape(1, B, N_MEM, X_HEADS, X_HEAD_DIM),
            vn_s.reshape(1, DB, DL, CM_WIDTH))
```

```python
import functools

import jax
import jax.numpy as jnp
from jax import lax
from jax.experimental import pallas as pl
from jax.experimental.pallas import tpu as pltpu

F32 = jnp.float32
BF16 = jnp.bfloat16

D_MODEL = 1024
HG_WIDTH = 512
HG_HEADS = 4
HG_DK = 128
CM_WIDTH = 512
CM_GROUPS = 4
CM_GROUP_DIM = 128
IN_WIDTH = 4 * HG_WIDTH + 2 * CM_WIDTH
N_MEM = 256
X_HEADS = 4
X_HEAD_DIM = 256
D_FF = 4096
EPS = 1e-6

CHUNK = 128
SUB = 16
VMEM_LIMIT_BYTES = 56 * 1024 * 1024


def _bdot(a, b):
    return jnp.dot(a.astype(BF16), b.astype(BF16), preferred_element_type=F32)


def _bdot_nt(a, b):
    return lax.dot_general(a.astype(BF16), b.astype(BF16), (((1,), (1,)), ((), ())),
                           preferred_element_type=F32)


def _rmsnorm(x, g):
    ms = jnp.mean(x * x, axis=-1, keepdims=True)
    return x * lax.rsqrt(ms + EPS) * g


def _sigmoid(x):
    return 0.5 * jnp.tanh(0.5 * x) + 0.5


def _gelu(x):
    return 0.5 * x * (1.0 + jnp.tanh(0.7978845608028654 * (x + 0.044715 * (x * x * x))))


def _seg_cumsum(x, seg):
    n = x.shape[0]
    pos = lax.broadcasted_iota(jnp.int32, x.shape, 0) % seg
    s = 1
    while s < seg:
        x = x + jnp.where(pos >= s, pltpu.roll(x, s, 0), 0.0)
        s *= 2
    del n
    return x


def _mix_kernel(*refs, sample, tm):
    if sample:
        (x_ref, s0_ref, lb_ref, gmix_ref, win_ref, hgn_ref, lng_ref, lnb_ref, wmix_ref, bcol_ref,
         wout_ref, h_ref, sout_ref, vn_ref, z_ref, ocat_ref) = refs
        st_ref = None
    else:
        (x_ref, lb_ref, gmix_ref, win_ref, hgn_ref, lng_ref, lnb_ref, wmix_ref, bcol_ref,
         wout_ref, h_ref, sout_ref, z_ref, ocat_ref, st_ref) = refs
        s0_ref = vn_ref = None

    x = x_ref[0]
    xn = _rmsnorm(x, gmix_ref[...]).astype(BF16)
    z_ref[...] = jnp.dot(xn, win_ref[...], preferred_element_type=F32)

    lbp = lb_ref[...]
    lbe = jnp.exp(lbp - jnp.max(lbp, axis=0, keepdims=True))
    lb_all = lbe[0:1] / jnp.sum(lbe, axis=0, keepdims=True)

    if not sample:
        @pl.when(pl.program_id(1) == 0)
        def _():
            st_ref[...] = jnp.zeros_like(st_ref)

    row = lax.broadcasted_iota(jnp.int32, (CHUNK, CHUNK), 0)
    col = lax.broadcasted_iota(jnp.int32, (CHUNK, CHUNK), 1)
    if sample:
        mixmask = (row // 4 == col // 4) & (row >= col)
    else:
        mixmask = row >= col
    eye = row == col
    r16 = lax.broadcasted_iota(jnp.int32, (SUB, HG_DK), 0)
    r8 = lax.broadcasted_iota(jnp.int32, (8, HG_DK), 0)

    def chunk_body(c, carry):
        r0 = pl.multiple_of(c * CHUNK, CHUNK)
        rows = pl.ds(r0, CHUNK)
        for h in range(HG_HEADS):
            cs = slice(h * HG_DK, (h + 1) * HG_DK)
            zq = z_ref[rows, cs]
            zf = z_ref[rows, HG_WIDTH + h * HG_DK:HG_WIDTH + (h + 1) * HG_DK]
            v = z_ref[rows, 2 * HG_WIDTH + h * HG_DK:2 * HG_WIDTH + (h + 1) * HG_DK]
            zg = z_ref[rows, 3 * HG_WIDTH + h * HG_DK:3 * HG_WIDTH + (h + 1) * HG_DK]
            lb = lb_all[:, cs]
            q = zq * _sigmoid(zq)
            th = jnp.tanh(0.5 * zf)
            logf = jnp.log(lb + (1.0 - lb) * (0.5 + 0.5 * th))
            k = (1.0 - lb) * (0.5 - 0.5 * th)
            bcum = _seg_cumsum(logf, 4 if sample else CHUNK)
            eb = jnp.exp(bcum)
            qd = q * eb

            o_parts = []
            if sample:
                for gb in range(CHUNK // SUB):
                    blk = slice(gb * SUB, (gb + 1) * SUB)
                    qd_b, v_b, eb_b = qd[blk], v[blk], eb[blk]
                    k_b, b_b = k[blk], bcum[blk]
                    inter = jnp.zeros((SUB, HG_DK), F32)
                    for j in range(4):
                        req = c * (CHUNK // 4) + gb * 4 + j
                        s0 = s0_ref[req, h]
                        last = 4 * j + 3
                        res = _bdot(qd_b, s0)
                        inter = jnp.where(r16 // 4 == j, res, inter)
                        kd = jnp.where(r16 // 4 == j, k_b * jnp.exp(b_b[last:last + 1] - b_b), 0.0)
                        upd = lax.dot_general(kd.astype(BF16), v_b.astype(BF16),
                                              (((0,), (0,)), ((), ())), preferred_element_type=F32)
                        dcol = jnp.sum(jnp.where(eye, eb_b[last:last + 1], 0.0), axis=-1, keepdims=True)
                        sout_ref[req, h] = dcol * s0 + upd
                    o_parts.append(inter)
            else:
                blast = bcum[CHUNK - 1:CHUNK]
                kd = k * jnp.exp(blast - bcum)
                st = st_ref[h]
                inter = _bdot_nt(qd, st)
                st_ref[h] = st * eb[CHUNK - 1:CHUNK] + _bdot(v.T, kd)
                a_rows = [jnp.zeros((SUB, CHUNK), F32)]
                for i in range(1, CHUNK // SUB):
                    n = i * SUB
                    ref_b = bcum[n - 1:n]
                    qt = q[n:n + SUB] * jnp.exp(bcum[n:n + SUB] - ref_b)
                    kt = k[:n] * jnp.exp(ref_b - bcum[:n])
                    kt = jnp.concatenate([kt, jnp.zeros((CHUNK - n, HG_DK), F32)], axis=0)
                    a_rows.append(_bdot_nt(qt, kt))
                a_off = jnp.concatenate(a_rows, axis=0)
                inter = inter + _bdot(a_off, v)
                for gb in range(CHUNK // SUB):
                    o_parts.append(inter[gb * SUB:(gb + 1) * SUB])

            o_rows = []
            for gb in range(CHUNK // SUB):
                blk = slice(gb * SUB, (gb + 1) * SUB)
                q_b, k_b, v_b, b_b = q[blk], k[blk], v[blk], bcum[blk]
                acc_lo = o_parts[gb][:8]
                acc_hi = o_parts[gb][8:]
                for s in range(SUB):
                    b_s, k_s, v_s = b_b[s:s + 1], k_b[s:s + 1], v_b[s:s + 1]
                    if s < 8:
                        if sample:
                            m = (r8 >= s) & (r8 // 4 == s // 4)
                        else:
                            m = r8 >= s
                        p = q_b[:8] * jnp.exp(jnp.where(m, b_b[:8] - b_s, -jnp.inf)) * k_s
                        acc_lo = acc_lo + jnp.sum(p, axis=-1, keepdims=True) * v_s
                    if (not sample) or s >= 8:
                        if s < 8:
                            p = q_b[8:] * jnp.exp(b_b[8:] - b_s) * k_s
                        else:
                            if sample:
                                m = (r8 + 8 >= s) & ((r8 + 8) // 4 == s // 4)
                            else:
                                m = r8 + 8 >= s
                            p = q_b[8:] * jnp.exp(jnp.where(m, b_b[8:] - b_s, -jnp.inf)) * k_s
                        acc_hi = acc_hi + jnp.sum(p, axis=-1, keepdims=True) * v_s
                o_rows.append(acc_lo)
                o_rows.append(acc_hi)
            o = jnp.concatenate(o_rows, axis=0)
            o = o * lax.rsqrt(jnp.mean(o * o, axis=-1, keepdims=True) + EPS) * hgn_ref[:, cs]
            ocat_ref[rows, cs] = (o * _sigmoid(zg)).astype(BF16)

        u = _gelu(z_ref[rows, 4 * HG_WIDTH:4 * HG_WIDTH + CM_WIDTH])
        gv = _gelu(z_ref[rows, 4 * HG_WIDTH + CM_WIDTH:IN_WIDTH])
        mu = jnp.mean(gv, axis=-1, keepdims=True)
        dv = gv - mu
        var = jnp.mean(dv * dv, axis=-1, keepdims=True)
        vn = dv * lax.rsqrt(var + EPS) * lng_ref[...] + lnb_ref[...]
        if sample:
            vn_ref[rows, :] = vn
        for g in range(CM_GROUPS):
            gs = slice(g * CM_GROUP_DIM, (g + 1) * CM_GROUP_DIM)
            wm = jnp.where(mixmask, wmix_ref[g], 0.0)
            mixed = _bdot(wm, vn[:, gs]) + bcol_ref[:, g:g + 1]
            ocat_ref[rows, HG_WIDTH + g * CM_GROUP_DIM:HG_WIDTH + (g + 1) * CM_GROUP_DIM] = (
                u[:, gs] * mixed).astype(BF16)
        return carry

    lax.fori_loop(0, tm // CHUNK, chunk_body, 0)

    h_ref[0] = x + jnp.dot(ocat_ref[...], wout_ref[...], preferred_element_type=F32)

    if not sample:
        @pl.when(pl.program_id(1) == pl.num_programs(1) - 1)
        def _():
            for h in range(HG_HEADS):
                sout_ref[0, h] = st_ref[h].T


def _const_spec(shape):
    nd = len(shape)
    return pl.BlockSpec(shape, lambda *_: (0,) * nd, pipeline_mode=pl.Buffered(1))


def _mix_prompt(x, lb_param, g_mix, w_in, hg_norm_g, ln_v_g, ln_v_b, wmix, bcol, w_out, tm):
    B, L, _ = x.shape
    nt = L // tm
    in_specs = [
        pl.BlockSpec((1, tm, D_MODEL), lambda b, i: (b, i, 0)),
        _const_spec(lb_param.shape), _const_spec(g_mix.shape), _const_spec(w_in.shape),
        _const_spec(hg_norm_g.shape), _const_spec(ln_v_g.shape), _const_spec(ln_v_b.shape),
        _const_spec(wmix.shape), _const_spec(bcol.shape), _const_spec(w_out.shape),
    ]
    out_specs = [
        pl.BlockSpec((1, tm, D_MODEL), lambda b, i: (b, i, 0)),
        pl.BlockSpec((1, HG_HEADS, HG_DK, HG_DK), lambda b, i: (b, 0, 0, 0)),
    ]
    return pl.pallas_call(
        functools.partial(_mix_kernel, sample=False, tm=tm),
        grid=(B, nt),
        in_specs=in_specs,
        out_specs=out_specs,
        out_shape=[jax.ShapeDtypeStruct((B, L, D_MODEL), F32),
                   jax.ShapeDtypeStruct((B, HG_HEADS, HG_DK, HG_DK), F32)],
        scratch_shapes=[pltpu.VMEM((tm, IN_WIDTH), F32), pltpu.VMEM((tm, D_MODEL), BF16),
                        pltpu.VMEM((HG_HEADS, HG_DK, HG_DK), F32)],
        compiler_params=pltpu.CompilerParams(dimension_semantics=("arbitrary", "arbitrary"),
                                             vmem_limit_bytes=VMEM_LIMIT_BYTES),
        name="mix_prompt",
    )(x, lb_param, g_mix, w_in, hg_norm_g, ln_v_g, ln_v_b, wmix, bcol, w_out)


def _mix_sample(x, s0, lb_param, g_mix, w_in, hg_norm_g, ln_v_g, ln_v_b, wmix, bcol, w_out, tm):
    nt, _, _ = x.shape
    nreq = tm // 4
    in_specs = [
        pl.BlockSpec((1, tm, D_MODEL), lambda i: (i, 0, 0)),
        pl.BlockSpec((nreq, HG_HEADS, HG_DK, HG_DK), lambda i: (i, 0, 0, 0)),
        _const_spec(lb_param.shape), _const_spec(g_mix.shape), _const_spec(w_in.shape),
        _const_spec(hg_norm_g.shape), _const_spec(ln_v_g.shape), _const_spec(ln_v_b.shape),
        _const_spec(wmix.shape), _const_spec(bcol.shape), _const_spec(w_out.shape),
    ]
    out_specs = [
        pl.BlockSpec((1, tm, D_MODEL), lambda i: (i, 0, 0)),
        pl.BlockSpec((nreq, HG_HEADS, HG_DK, HG_DK), lambda i: (i, 0, 0, 0)),
        pl.BlockSpec((tm, CM_WIDTH), lambda i: (i, 0)),
    ]
    return pl.pallas_call(
        functools.partial(_mix_kernel, sample=True, tm=tm),
        grid=(nt,),
        in_specs=in_specs,
        out_specs=out_specs,
        out_shape=[jax.ShapeDtypeStruct((nt, tm, D_MODEL), F32),
                   jax.ShapeDtypeStruct(s0.shape, F32),
                   jax.ShapeDtypeStruct((nt * tm, CM_WIDTH), F32)],
        scratch_shapes=[pltpu.VMEM((tm, IN_WIDTH), F32), pltpu.VMEM((tm, D_MODEL), BF16)],
        compiler_params=pltpu.CompilerParams(dimension_semantics=("arbitrary",),
                                             vmem_limit_bytes=VMEM_LIMIT_BYTES),
        name="mix_sample",
    )(x, s0, lb_param, g_mix, w_in, hg_norm_g, ln_v_g, ln_v_b, wmix, bcol, w_out)


def _memkv_kernel(m_ref, g_ref, wk_ref, wv_ref, k_ref, v_ref):
    mn = _rmsnorm(m_ref[...], g_ref[...]).astype(BF16)
    k_ref[...] = jnp.dot(mn, wk_ref[...], preferred_element_type=F32)
    v_ref[...] = jnp.dot(mn, wv_ref[...], preferred_element_type=F32)


def _memkv(mem, g_mem, w_ck, w_cv, tm):
    n = mem.shape[0]
    return pl.pallas_call(
        _memkv_kernel,
        grid=(n // tm,),
        in_specs=[pl.BlockSpec((tm, D_MODEL), lambda i: (i, 0)), _const_spec(g_mem.shape),
                  _const_spec(w_ck.shape), _const_spec(w_cv.shape)],
        out_specs=[pl.BlockSpec((tm, D_MODEL), lambda i: (i, 0)),
                   pl.BlockSpec((tm, D_MODEL), lambda i: (i, 0))],
        out_shape=[jax.ShapeDtypeStruct((n, D_MODEL), F32), jax.ShapeDtypeStruct((n, D_MODEL), F32)],
        compiler_params=pltpu.CompilerParams(dimension_semantics=("arbitrary",),
                                             vmem_limit_bytes=VMEM_LIMIT_BYTES),
        name="memkv",
    )(mem, g_mem, w_ck, w_cv)


def _softmax_rows(s):
    m = jnp.max(s, axis=-1, keepdims=True)
    e = jnp.exp(s - m)
    return e / jnp.sum(e, axis=-1, keepdims=True)


def _post_kernel(*refs, with_attn):
    if with_attn:
        (x_ref, mk_ref, mv_ref, gc_ref, wcq_ref, wco_ref, gm_ref, wup_ref, wdn_ref, gf_ref,
         y_ref, ca_s, hm_s) = refs
        x = x_ref[0]
        hq = jnp.dot(_rmsnorm(x, gc_ref[...]).astype(BF16), wcq_ref[...], preferred_element_type=F32)
        scale = X_HEAD_DIM ** -0.5
        for h in range(X_HEADS):
            hs = slice(h * X_HEAD_DIM, (h + 1) * X_HEAD_DIM)
            s = _bdot_nt(hq[:, hs], mk_ref[0, :, hs]) * scale
            p = _softmax_rows(s)
            ca_s[:, hs] = _bdot(p, mv_ref[0, :, hs]).astype(BF16)
        ca = ca_s[...]
    else:
        (x_ref, ca_ref, wco_ref, gm_ref, wup_ref, wdn_ref, gf_ref, y_ref, hm_s) = refs
        x = x_ref[0]
        ca = ca_ref[...]
    x = x + jnp.dot(ca, wco_ref[...], preferred_element_type=F32)
    hn = _rmsnorm(x, gm_ref[...]).astype(BF16)
    nc = D_FF // D_MODEL
    for c in range(nc):
        fs = slice(c * D_MODEL, (c + 1) * D_MODEL)
        hm = jnp.maximum(jnp.dot(hn, wup_ref[:, fs], preferred_element_type=F32), 0.0)
        hm_s[:, fs] = (hm * hm).astype(BF16)
    x = x + jnp.dot(hm_s[...], wdn_ref[...], preferred_element_type=F32)
    y_ref[0] = _rmsnorm(x, gf_ref[...])


def _post_prompt(x, mk, mv, g_cross, w_cq, w_co, g_mlp, w_up, w_down, g_final, tm):
    B, L, _ = x.shape
    in_specs = [
        pl.BlockSpec((1, tm, D_MODEL), lambda b, i: (b, i, 0)),
        pl.BlockSpec((1, N_MEM, D_MODEL), lambda b, i: (b, 0, 0)),
        pl.BlockSpec((1, N_MEM, D_MODEL), lambda b, i: (b, 0, 0)),
        _const_spec(g_cross.shape), _const_spec(w_cq.shape), _const_spec(w_co.shape),
        _const_spec(g_mlp.shape), _const_spec(w_up.shape), _const_spec(w_down.shape),
        _const_spec(g_final.shape),
    ]
    return pl.pallas_call(
        functools.partial(_post_kernel, with_attn=True),
        grid=(B, L // tm),
        in_specs=in_specs,
        out_specs=pl.BlockSpec((1, tm, D_MODEL), lambda b, i: (b, i, 0)),
        out_shape=jax.ShapeDtypeStruct((B, L, D_MODEL), F32),
        scratch_shapes=[pltpu.VMEM((tm, D_MODEL), BF16), pltpu.VMEM((tm, D_FF), BF16)],
        compiler_params=pltpu.CompilerParams(dimension_semantics=("arbitrary", "arbitrary"),
                                             vmem_limit_bytes=VMEM_LIMIT_BYTES),
        name="post_prompt",
    )(x, mk, mv, g_cross, w_cq, w_co, g_mlp, w_up, w_down, g_final)


def _post_sample(x, ca, w_co, g_mlp, w_up, w_down, g_final):
    nt, tm, _ = x.shape
    in_specs = [
        pl.BlockSpec((1, tm, D_MODEL), lambda i: (i, 0, 0)),
        pl.BlockSpec((tm, D_MODEL), lambda i: (i, 0)),
        _const_spec(w_co.shape), _const_spec(g_mlp.shape), _const_spec(w_up.shape),
        _const_spec(w_down.shape), _const_spec(g_final.shape),
    ]
    return pl.pallas_call(
        functools.partial(_post_kernel, with_attn=False),
        grid=(nt,),
        in_specs=in_specs,
        out_specs=pl.BlockSpec((1, tm, D_MODEL), lambda i: (i, 0, 0)),
        out_shape=jax.ShapeDtypeStruct((nt, tm, D_MODEL), F32),
        scratch_shapes=[pltpu.VMEM((tm, D_FF), BF16)],
        compiler_params=pltpu.CompilerParams(dimension_semantics=("arbitrary",),
                                             vmem_limit_bytes=VMEM_LIMIT_BYTES),
        name="post_sample",
    )(x, ca, w_co, g_mlp, w_up, w_down, g_final)


def _xattn_sample_kernel(x_ref, k_ref, v_ref, gc_ref, wcq_ref, ca_ref, *, nreq):
    hq = jnp.dot(_rmsnorm(x_ref[...], gc_ref[...]).astype(BF16), wcq_ref[...],
                 preferred_element_type=F32)
    scale = X_HEAD_DIM ** -0.5
    nrow = 8 * X_HEADS
    rh = lax.broadcasted_iota(jnp.int32, (nrow, N_MEM * X_HEADS), 0) // 8
    ch = lax.broadcasted_iota(jnp.int32, (nrow, N_MEM * X_HEADS), 1) % X_HEADS
    same_head = rh == ch
    first = (lax.broadcasted_iota(jnp.int32, (nrow, X_HEAD_DIM), 0) % 8) < 4
    for pr in range(nreq // 2):
        hq8 = hq[8 * pr:8 * pr + 8]
        q = jnp.concatenate([hq8[:, h * X_HEAD_DIM:(h + 1) * X_HEAD_DIM] for h in range(X_HEADS)],
                            axis=0).astype(BF16)
        outs = []
        for r in (2 * pr, 2 * pr + 1):
            k2 = k_ref[r].reshape(N_MEM * X_HEADS, X_HEAD_DIM)
            v2 = v_ref[r].reshape(N_MEM * X_HEADS, X_HEAD_DIM)
            s = jnp.where(same_head, _bdot_nt(q, k2) * scale, -jnp.inf)
            outs.append(_bdot(_softmax_rows(s), v2))
        o = jnp.where(first, outs[0], outs[1]).astype(BF16)
        for h in range(X_HEADS):
            ca_ref[8 * pr:8 * pr + 8, h * X_HEAD_DIM:(h + 1) * X_HEAD_DIM] = o[8 * h:8 * h + 8]


def _xattn_sample(x, ck, cv, g_cross, w_cq, nreq):
    n = x.shape[0]
    nb = ck.shape[0]
    kv_spec = pl.BlockSpec((nreq, N_MEM, X_HEADS, X_HEAD_DIM), lambda i: (i, 0, 0, 0))
    return pl.pallas_call(
        functools.partial(_xattn_sample_kernel, nreq=nreq),
        grid=(nb // nreq,),
        in_specs=[pl.BlockSpec((4 * nreq, D_MODEL), lambda i: (i, 0)), kv_spec, kv_spec,
                  _const_spec(g_cross.shape), _const_spec(w_cq.shape)],
        out_specs=pl.BlockSpec((4 * nreq, D_MODEL), lambda i: (i, 0)),
        out_shape=jax.ShapeDtypeStruct((n, D_MODEL), BF16),
        compiler_params=pltpu.CompilerParams(dimension_semantics=("arbitrary",),
                                             vmem_limit_bytes=VMEM_LIMIT_BYTES),
        name="xattn_sample",
    )(x, ck, cv, g_cross, w_cq)


def kernel(x_prompt, x_sample, mem_prompt, state_hgrn, cache_mem_k, cache_mem_v, lb_param, g_mix,
           w_in, hg_norm_g, ln_v_g, ln_v_b, w_s, b_s, w_out, g_cross, g_mem, w_cq, w_ck, w_cv, w_co,
           g_mlp, w_up, w_down, g_final):
    B, L, _ = x_prompt.shape
    DB, DL, _ = x_sample.shape
    assert DL == 4 and g_mix.shape[0] == 1

    row = lambda a: a.reshape(1, -1)
    win_b, wout_b = w_in[0].astype(BF16), w_out[0].astype(BF16)
    wcq_b, wco_b = w_cq[0].astype(BF16), w_co[0].astype(BF16)
    wck_b, wcv_b = w_ck[0].astype(BF16), w_cv[0].astype(BF16)
    wup_b, wdn_b = w_up[0].astype(BF16), w_down[0].astype(BF16)
    gmix, hgn, lng, lnb = row(g_mix[0]), row(hg_norm_g[0]), row(ln_v_g[0]), row(ln_v_b[0])
    gcr, gmem, gmlp, gfin = row(g_cross[0]), row(g_mem[0]), row(g_mlp[0]), row(g_final)
    wmix_p = w_s[0]
    bcol_p = b_s[0].T
    wmix_s = jnp.tile(w_s[0][:, :DL, :DL], (1, CHUNK // DL, CHUNK // DL))
    bcol_s = jnp.tile(b_s[0][:, :DL].T, (CHUNK // DL, 1))

    mk, mv = _memkv(mem_prompt.reshape(B * N_MEM, D_MODEL), gmem, wck_b, wcv_b, tm=512)
    h_p, s_p = _mix_prompt(x_prompt, lb_param, gmix, win_b, hgn, lng, lnb, wmix_p, bcol_p, wout_b,
                           tm=512)
    y_p = _post_prompt(h_p, mk.reshape(B, N_MEM, D_MODEL), mv.reshape(B, N_MEM, D_MODEL), gcr,
                       wcq_b, wco_b, gmlp, wup_b, wdn_b, gfin, tm=512)

    tm_s = 128
    xs = x_sample.reshape(DB * DL // tm_s, tm_s, D_MODEL)
    h_s, s_s, vn_s = _mix_sample(xs, state_hgrn[0], lb_param, gmix, win_b, hgn, lng, lnb, wmix_s,
                                 bcol_s, wout_b, tm=tm_s)
    ca_s = _xattn_sample(h_s.reshape(DB * DL, D_MODEL), cache_mem_k[0], cache_mem_v[0], gcr, wcq_b,
                         nreq=8)
    y_s = _post_sample(h_s.reshape(1, DB * DL, D_MODEL), ca_s, wco_b, gmlp, wup_b, wdn_b, gfin)

    return (y_p, y_s.reshape(DB, DL, D_MODEL), s_p[None], s_s[None],
            mk.reshape(1, B, N_MEM, X_HEADS, X_HEAD_DIM), mv.reshape(1, B, N_MEM, X_HEADS, X_HEAD_DIM),
            vn_s.reshape(1, DB, DL, CM_WIDTH))
```

```python
import functools

import jax
import jax.numpy as jnp
from jax import lax
from jax.experimental import pallas as pl
from jax.experimental.pallas import tpu as pltpu

F32 = jnp.float32
BF16 = jnp.bfloat16

D_MODEL = 1024
HG_WIDTH = 512
HG_HEADS = 4
HG_DK = 128
CM_WIDTH = 512
CM_GROUPS = 4
CM_GROUP_DIM = 128
IN_WIDTH = 4 * HG_WIDTH + 2 * CM_WIDTH
N_MEM = 256
X_HEADS = 4
X_HEAD_DIM = 256
D_FF = 4096
EPS = 1e-6

CHUNK = 128
SUB = 16
VMEM_LIMIT_BYTES = 56 * 1024 * 1024
LOG2E = 1.4426950408889634
FAST_BLOCK = 32
FAST_MAX_DECAY = 80.0


def _bdot(a, b):
    return jnp.dot(a.astype(BF16), b.astype(BF16), preferred_element_type=F32)


def _bdot_nt(a, b):
    return lax.dot_general(a.astype(BF16), b.astype(BF16), (((1,), (1,)), ((), ())),
                           preferred_element_type=F32)


def _rmsnorm(x, g):
    ms = jnp.mean(x * x, axis=-1, keepdims=True)
    return x * lax.rsqrt(ms + EPS) * g


def _sigmoid(x):
    return 0.5 * jnp.tanh(0.5 * x) + 0.5


def _gelu(x):
    return 0.5 * x * (1.0 + jnp.tanh(0.7978845608028654 * (x + 0.044715 * (x * x * x))))


def _seg_cumsum(x, seg):
    pos = lax.broadcasted_iota(jnp.int32, x.shape, 0) % seg
    s = 1
    while s < seg:
        x = x + jnp.where(pos >= s, pltpu.roll(x, s, 0), 0.0)
        s *= 2
    return x


def _mix_kernel(*refs, sample, tm):
    if sample:
        (x_ref, s0_ref, lb_ref, gmix_ref, win_ref, hgn_ref, lng_ref, lnb_ref, wmix_ref, bcol_ref,
         wout_ref, h_ref, sout_ref, vn_ref, z_ref, ocat_ref, q_s, k_s, b_s) = refs
        st_ref = None
    else:
        (x_ref, lb_ref, gmix_ref, win_ref, hgn_ref, lng_ref, lnb_ref, wmix_ref, bcol_ref,
         wout_ref, h_ref, sout_ref, z_ref, ocat_ref, q_s, k_s, b_s, st_ref) = refs
        s0_ref = vn_ref = None
    seg = 4 if sample else CHUNK

    x = x_ref[0]
    xn = _rmsnorm(x, gmix_ref[...]).astype(BF16)
    z_ref[...] = jnp.dot(xn, win_ref[...], preferred_element_type=F32)

    lbp = lb_ref[...]
    lbe = jnp.exp(lbp - jnp.max(lbp, axis=0, keepdims=True))
    lb_all = lbe[0:1] / jnp.sum(lbe, axis=0, keepdims=True)

    if not sample:
        @pl.when(pl.program_id(1) == 0)
        def _():
            st_ref[...] = jnp.zeros_like(st_ref)

    row = lax.broadcasted_iota(jnp.int32, (CHUNK, CHUNK), 0)
    col = lax.broadcasted_iota(jnp.int32, (CHUNK, CHUNK), 1)
    if sample:
        mixmask = (row // 4 == col // 4) & (row >= col)
    else:
        mixmask = row >= col
    eye = row == col
    r16 = lax.broadcasted_iota(jnp.int32, (SUB, HG_DK), 0)
    col8 = lax.broadcasted_iota(jnp.int32, (8, CHUNK), 1)

    def gate_body(c, dmax):
        rows = pl.ds(pl.multiple_of(c * CHUNK, CHUNK), CHUNK)
        zq = z_ref[rows, 0:HG_WIDTH]
        zf = z_ref[rows, HG_WIDTH:2 * HG_WIDTH]
        th = jnp.tanh(0.5 * zf)
        logf = jnp.log(lb_all + (1.0 - lb_all) * (0.5 + 0.5 * th))
        bcum = _seg_cumsum(logf, seg)
        q_s[rows, :] = zq * _sigmoid(zq)
        k_s[rows, :] = (1.0 - lb_all) * (0.5 - 0.5 * th)
        b_s[rows, :] = bcum
        if sample:
            return jnp.maximum(dmax, jnp.max(-bcum, axis=0, keepdims=True))
        for i in range(CHUNK // FAST_BLOCK):
            n0, n1 = i * FAST_BLOCK, (i + 1) * FAST_BLOCK
            start = bcum[n0 - 1:n0] if i else 0.0
            dmax = jnp.maximum(dmax, start - bcum[n1 - 1:n1])
        return dmax

    dmax = lax.fori_loop(0, tm // CHUNK, gate_body, jnp.zeros((1, HG_WIDTH), F32))
    fast_ok = jnp.max(dmax) < FAST_MAX_DECAY

    def scores_fast(q, k, bcum):
        if sample:
            return _bdot_nt(q * jnp.exp(bcum), k * jnp.exp(-bcum))
        a_rows = []
        for i in range(CHUNK // FAST_BLOCK):
            n0, n1 = i * FAST_BLOCK, (i + 1) * FAST_BLOCK
            ref_b = bcum[n0 - 1:n0] if i else jnp.zeros((1, HG_DK), F32)
            qt = q[n0:n1] * jnp.exp(bcum[n0:n1] - ref_b)
            kt = k[:n1] * jnp.exp(ref_b - bcum[:n1])
            if n1 < CHUNK:
                kt = jnp.concatenate([kt, jnp.zeros((CHUNK - n1, HG_DK), F32)], axis=0)
            a_rows.append(_bdot_nt(qt, kt))
        return jnp.concatenate(a_rows, axis=0)

    def scores_exact(q, k, bcum):
        b2 = bcum * LOG2E
        cexp = b2 - jnp.log2(k)
        a_rows = []
        for gb in range(CHUNK // SUB):
            lo = slice(gb * SUB, gb * SUB + 8)
            hi = slice(gb * SUB + 8, (gb + 1) * SUB)
            a_lo = jnp.zeros((8, CHUNK), F32)
            a_hi = jnp.zeros((8, CHUNK), F32)
            for s in range(SUB):
                sg = gb * SUB + s
                c_s = cexp[sg:sg + 1]
                if s < 8:
                    p = q[lo] * jnp.exp2(b2[lo] - c_s)
                    a_lo = jnp.where(col8 == sg, jnp.sum(p, axis=-1, keepdims=True), a_lo)
                if (not sample) or s >= 8:
                    p = q[hi] * jnp.exp2(b2[hi] - c_s)
                    a_hi = jnp.where(col8 == sg, jnp.sum(p, axis=-1, keepdims=True), a_hi)
            a_rows.append(a_lo)
            a_rows.append(a_hi)
        a = jnp.concatenate(a_rows, axis=0)
        if sample:
            return a
        o_rows = [jnp.zeros((SUB, CHUNK), F32)]
        for i in range(1, CHUNK // SUB):
            n = i * SUB
            ref_b = bcum[n - 1:n]
            qt = q[n:n + SUB] * jnp.exp(bcum[n:n + SUB] - ref_b)
            kt = k[:n] * jnp.exp(ref_b - bcum[:n])
            kt = jnp.concatenate([kt, jnp.zeros((CHUNK - n, HG_DK), F32)], axis=0)
            o_rows.append(_bdot_nt(qt, kt))
        return jnp.where(mixmask, a, 0.0) + jnp.concatenate(o_rows, axis=0)

    def chunk_body(c, carry, *, fast):
        rows = pl.ds(pl.multiple_of(c * CHUNK, CHUNK), CHUNK)
        for h in range(HG_HEADS):
            cs = slice(h * HG_DK, (h + 1) * HG_DK)
            q, k, bcum = q_s[rows, cs], k_s[rows, cs], b_s[rows, cs]
            v = z_ref[rows, 2 * HG_WIDTH + h * HG_DK:2 * HG_WIDTH + (h + 1) * HG_DK]
            zg = z_ref[rows, 3 * HG_WIDTH + h * HG_DK:3 * HG_WIDTH + (h + 1) * HG_DK]
            a = scores_fast(q, k, bcum) if fast else scores_exact(q, k, bcum)
            o = _bdot(jnp.where(mixmask, a, 0.0), v)
            eb = jnp.exp(bcum)
            qd = q * eb
            if sample:
                o_parts = []
                for gb in range(CHUNK // SUB):
                    blk = slice(gb * SUB, (gb + 1) * SUB)
                    qd_b, v_b, eb_b, k_b, b_b = qd[blk], v[blk], eb[blk], k[blk], bcum[blk]
                    inter = jnp.zeros((SUB, HG_DK), F32)
                    for j in range(4):
                        req = c * (CHUNK // 4) + gb * 4 + j
                        s0 = s0_ref[req, h]
                        last = 4 * j + 3
                        inter = jnp.where(r16 // 4 == j, _bdot(qd_b, s0), inter)
                        kd = jnp.where(r16 // 4 == j, k_b * jnp.exp(b_b[last:last + 1] - b_b), 0.0)
                        upd = lax.dot_general(kd.astype(BF16), v_b.astype(BF16),
                                              (((0,), (0,)), ((), ())), preferred_element_type=F32)
                        dcol = jnp.sum(jnp.where(eye, eb_b[last:last + 1], 0.0), axis=-1, keepdims=True)
                        sout_ref[req, h] = dcol * s0 + upd
                    o_parts.append(inter)
                o = o + jnp.concatenate(o_parts, axis=0)
            else:
                kd = k * jnp.exp(bcum[CHUNK - 1:CHUNK] - bcum)
                st = st_ref[h]
                o = o + _bdot_nt(qd, st)
                st_ref[h] = st * eb[CHUNK - 1:CHUNK] + _bdot(v.T, kd)
            o = o * lax.rsqrt(jnp.mean(o * o, axis=-1, keepdims=True) + EPS) * hgn_ref[:, cs]
            ocat_ref[rows, cs] = (o * _sigmoid(zg)).astype(BF16)

        u = _gelu(z_ref[rows, 4 * HG_WIDTH:4 * HG_WIDTH + CM_WIDTH])
        gv = _gelu(z_ref[rows, 4 * HG_WIDTH + CM_WIDTH:IN_WIDTH])
        mu = jnp.mean(gv, axis=-1, keepdims=True)
        dv = gv - mu
        var = jnp.mean(dv * dv, axis=-1, keepdims=True)
        vn = dv * lax.rsqrt(var + EPS) * lng_ref[...] + lnb_ref[...]
        if sample:
            vn_ref[rows, :] = vn
        for g in range(CM_GROUPS):
            gs = slice(g * CM_GROUP_DIM, (g + 1) * CM_GROUP_DIM)
            wm = jnp.where(mixmask, wmix_ref[g], 0.0)
            mixed = _bdot(wm, vn[:, gs]) + bcol_ref[:, g:g + 1]
            ocat_ref[rows, HG_WIDTH + g * CM_GROUP_DIM:HG_WIDTH + (g + 1) * CM_GROUP_DIM] = (
                u[:, gs] * mixed).astype(BF16)
        return carry

    @pl.when(fast_ok)
    def _():
        lax.fori_loop(0, tm // CHUNK, functools.partial(chunk_body, fast=True), 0)

    @pl.when(jnp.logical_not(fast_ok))
    def _():
        lax.fori_loop(0, tm // CHUNK, functools.partial(chunk_body, fast=False), 0)

    h_ref[0] = x + jnp.dot(ocat_ref[...], wout_ref[...], preferred_element_type=F32)

    if not sample:
        @pl.when(pl.program_id(1) == pl.num_programs(1) - 1)
        def _():
            for h in range(HG_HEADS):
                sout_ref[0, h] = st_ref[h].T


def _const_spec(shape):
    nd = len(shape)
    return pl.BlockSpec(shape, lambda *_: (0,) * nd, pipeline_mode=pl.Buffered(1))


def _mix_prompt(x, lb_param, g_mix, w_in, hg_norm_g, ln_v_g, ln_v_b, wmix, bcol, w_out, tm):
    B, L, _ = x.shape
    nt = L // tm
    in_specs = [
        pl.BlockSpec((1, tm, D_MODEL), lambda b, i: (b, i, 0)),
        _const_spec(lb_param.shape), _const_spec(g_mix.shape), _const_spec(w_in.shape),
        _const_spec(hg_norm_g.shape), _const_spec(ln_v_g.shape), _const_spec(ln_v_b.shape),
        _const_spec(wmix.shape), _const_spec(bcol.shape), _const_spec(w_out.shape),
    ]
    out_specs = [
        pl.BlockSpec((1, tm, D_MODEL), lambda b, i: (b, i, 0)),
        pl.BlockSpec((1, HG_HEADS, HG_DK, HG_DK), lambda b, i: (b, 0, 0, 0)),
    ]
    return pl.pallas_call(
        functools.partial(_mix_kernel, sample=False, tm=tm),
        grid=(B, nt),
        in_specs=in_specs,
        out_specs=out_specs,
        out_shape=[jax.ShapeDtypeStruct((B, L, D_MODEL), F32),
                   jax.ShapeDtypeStruct((B, HG_HEADS, HG_DK, HG_DK), F32)],
        scratch_shapes=[pltpu.VMEM((tm, IN_WIDTH), F32), pltpu.VMEM((tm, D_MODEL), BF16)]
        + [pltpu.VMEM((tm, HG_WIDTH), F32)] * 3
        + [pltpu.VMEM((HG_HEADS, HG_DK, HG_DK), F32)],
        compiler_params=pltpu.CompilerParams(dimension_semantics=("arbitrary", "arbitrary"),
                                             vmem_limit_bytes=VMEM_LIMIT_BYTES),
        name="mix_prompt",
    )(x, lb_param, g_mix, w_in, hg_norm_g, ln_v_g, ln_v_b, wmix, bcol, w_out)


def _mix_sample(x, s0, lb_param, g_mix, w_in, hg_norm_g, ln_v_g, ln_v_b, wmix, bcol, w_out, tm):
    nt, _, _ = x.shape
    nreq = tm // 4
    in_specs = [
        pl.BlockSpec((1, tm, D_MODEL), lambda i: (i, 0, 0)),
        pl.BlockSpec((nreq, HG_HEADS, HG_DK, HG_DK), lambda i: (i, 0, 0, 0)),
        _const_spec(lb_param.shape), _const_spec(g_mix.shape), _const_spec(w_in.shape),
        _const_spec(hg_norm_g.shape), _const_spec(ln_v_g.shape), _const_spec(ln_v_b.shape),
        _const_spec(wmix.shape), _const_spec(bcol.shape), _const_spec(w_out.shape),
    ]
    out_specs = [
        pl.BlockSpec((1, tm, D_MODEL), lambda i: (i, 0, 0)),
        pl.BlockSpec((nreq, HG_HEADS, HG_DK, HG_DK), lambda i: (i, 0, 0, 0)),
        pl.BlockSpec((tm, CM_WIDTH), lambda i: (i, 0)),
    ]
    return pl.pallas_call(
        functools.partial(_mix_kernel, sample=True, tm=tm),
        grid=(nt,),
        in_specs=in_specs,
        out_specs=out_specs,
        out_shape=[jax.ShapeDtypeStruct((nt, tm, D_MODEL), F32),
                   jax.ShapeDtypeStruct(s0.shape, F32),
                   jax.ShapeDtypeStruct((nt * tm, CM_WIDTH), F32)],
        scratch_shapes=[pltpu.VMEM((tm, IN_WIDTH), F32), pltpu.VMEM((tm, D_MODEL), BF16)]
        + [pltpu.VMEM((tm, HG_WIDTH), F32)] * 3,
        compiler_params=pltpu.CompilerParams(dimension_semantics=("arbitrary",),
                                             vmem_limit_bytes=VMEM_LIMIT_BYTES),
        name="mix_sample",
    )(x, s0, lb_param, g_mix, w_in, hg_norm_g, ln_v_g, ln_v_b, wmix, bcol, w_out)


def _memkv_kernel(m_ref, g_ref, wk_ref, wv_ref, k_ref, v_ref):
    mn = _rmsnorm(m_ref[...], g_ref[...]).astype(BF16)
    k_ref[...] = jnp.dot(mn, wk_ref[...], preferred_element_type=F32)
    v_ref[...] = jnp.dot(mn, wv_ref[...], preferred_element_type=F32)


def _memkv(mem, g_mem, w_ck, w_cv, tm):
    n = mem.shape[0]
    return pl.pallas_call(
        _memkv_kernel,
        grid=(n // tm,),
        in_specs=[pl.BlockSpec((tm, D_MODEL), lambda i: (i, 0)), _const_spec(g_mem.shape),
                  _const_spec(w_ck.shape), _const_spec(w_cv.shape)],
        out_specs=[pl.BlockSpec((tm, D_MODEL), lambda i: (i, 0)),
                   pl.BlockSpec((tm, D_MODEL), lambda i: (i, 0))],
        out_shape=[jax.ShapeDtypeStruct((n, D_MODEL), F32), jax.ShapeDtypeStruct((n, D_MODEL), F32)],
        compiler_params=pltpu.CompilerParams(dimension_semantics=("arbitrary",),
                                             vmem_limit_bytes=VMEM_LIMIT_BYTES),
        name="memkv",
    )(mem, g_mem, w_ck, w_cv)


def _softmax_rows(s):
    m = jnp.max(s, axis=-1, keepdims=True)
    e = jnp.exp(s - m)
    return e / jnp.sum(e, axis=-1, keepdims=True)


def _post_kernel(*refs, with_attn):
    if with_attn:
        (x_ref, mk_ref, mv_ref, gc_ref, wcq_ref, wco_ref, gm_ref, wup_ref, wdn_ref, gf_ref,
         y_ref, ca_s, hm_s) = refs
        x = x_ref[0]
        hq = jnp.dot(_rmsnorm(x, gc_ref[...]).astype(BF16), wcq_ref[...], preferred_element_type=F32)
        scale = X_HEAD_DIM ** -0.5
        for h in range(X_HEADS):
            hs = slice(h * X_HEAD_DIM, (h + 1) * X_HEAD_DIM)
            s = _bdot_nt(hq[:, hs], mk_ref[0, :, hs]) * scale
            p = _softmax_rows(s)
            ca_s[:, hs] = _bdot(p, mv_ref[0, :, hs]).astype(BF16)
        ca = ca_s[...]
    else:
        (x_ref, ca_ref, wco_ref, gm_ref, wup_ref, wdn_ref, gf_ref, y_ref, hm_s) = refs
        x = x_ref[0]
        ca = ca_ref[...]
    x = x + jnp.dot(ca, wco_ref[...], preferred_element_type=F32)
    hn = _rmsnorm(x, gm_ref[...]).astype(BF16)
    nc = D_FF // D_MODEL
    for c in range(nc):
        fs = slice(c * D_MODEL, (c + 1) * D_MODEL)
        hm = jnp.maximum(jnp.dot(hn, wup_ref[:, fs], preferred_element_type=F32), 0.0)
        hm_s[:, fs] = (hm * hm).astype(BF16)
    x = x + jnp.dot(hm_s[...], wdn_ref[...], preferred_element_type=F32)
    y_ref[0] = _rmsnorm(x, gf_ref[...])


def _post_prompt(x, mk, mv, g_cross, w_cq, w_co, g_mlp, w_up, w_down, g_final, tm):
    B, L, _ = x.shape
    in_specs = [
        pl.BlockSpec((1, tm, D_MODEL), lambda b, i: (b, i, 0)),
        pl.BlockSpec((1, N_MEM, D_MODEL), lambda b, i: (b, 0, 0)),
        pl.BlockSpec((1, N_MEM, D_MODEL), lambda b, i: (b, 0, 0)),
        _const_spec(g_cross.shape), _const_spec(w_cq.shape), _const_spec(w_co.shape),
        _const_spec(g_mlp.shape), _const_spec(w_up.shape), _const_spec(w_down.shape),
        _const_spec(g_final.shape),
    ]
    return pl.pallas_call(
        functools.partial(_post_kernel, with_attn=True),
        grid=(B, L // tm),
        in_specs=in_specs,
        out_specs=pl.BlockSpec((1, tm, D_MODEL), lambda b, i: (b, i, 0)),
        out_shape=jax.ShapeDtypeStruct((B, L, D_MODEL), F32),
        scratch_shapes=[pltpu.VMEM((tm, D_MODEL), BF16), pltpu.VMEM((tm, D_FF), BF16)],
        compiler_params=pltpu.CompilerParams(dimension_semantics=("arbitrary", "arbitrary"),
                                             vmem_limit_bytes=VMEM_LIMIT_BYTES),
        name="post_prompt",
    )(x, mk, mv, g_cross, w_cq, w_co, g_mlp, w_up, w_down, g_final)


def _post_sample(x, ca, w_co, g_mlp, w_up, w_down, g_final):
    nt, tm, _ = x.shape
    in_specs = [
        pl.BlockSpec((1, tm, D_MODEL), lambda i: (i, 0, 0)),
        pl.BlockSpec((tm, D_MODEL), lambda i: (i, 0)),
        _const_spec(w_co.shape), _const_spec(g_mlp.shape), _const_spec(w_up.shape),
        _const_spec(w_down.shape), _const_spec(g_final.shape),
    ]
    return pl.pallas_call(
        functools.partial(_post_kernel, with_attn=False),
        grid=(nt,),
        in_specs=in_specs,
        out_specs=pl.BlockSpec((1, tm, D_MODEL), lambda i: (i, 0, 0)),
        out_shape=jax.ShapeDtypeStruct((nt, tm, D_MODEL), F32),
        scratch_shapes=[pltpu.VMEM((tm, D_FF), BF16)],
        compiler_params=pltpu.CompilerParams(dimension_semantics=("arbitrary",),
                                             vmem_limit_bytes=VMEM_LIMIT_BYTES),
        name="post_sample",
    )(x, ca, w_co, g_mlp, w_up, w_down, g_final)


def _xattn_sample_kernel(x_ref, k_ref, v_ref, gc_ref, wcq_ref, ca_ref, *, nreq):
    hq = jnp.dot(_rmsnorm(x_ref[...], gc_ref[...]).astype(BF16), wcq_ref[...],
                 preferred_element_type=F32)
    scale = X_HEAD_DIM ** -0.5
    nrow = 8 * X_HEADS
    rh = lax.broadcasted_iota(jnp.int32, (nrow, N_MEM * X_HEADS), 0) // 8
    ch = lax.broadcasted_iota(jnp.int32, (nrow, N_MEM * X_HEADS), 1) % X_HEADS
    same_head = rh == ch
    first = (lax.broadcasted_iota(jnp.int32, (nrow, X_HEAD_DIM), 0) % 8) < 4
    for pr in range(nreq // 2):
        hq8 = hq[8 * pr:8 * pr + 8]
        q = jnp.concatenate([hq8[:, h * X_HEAD_DIM:(h + 1) * X_HEAD_DIM] for h in range(X_HEADS)],
                            axis=0).astype(BF16)
        outs = []
        for r in (2 * pr, 2 * pr + 1):
            k2 = k_ref[r].reshape(N_MEM * X_HEADS, X_HEAD_DIM)
            v2 = v_ref[r].reshape(N_MEM * X_HEADS, X_HEAD_DIM)
            s = jnp.where(same_head, _bdot_nt(q, k2) * scale, -jnp.inf)
            outs.append(_bdot(_softmax_rows(s), v2))
        o = jnp.where(first, outs[0], outs[1]).astype(BF16)
        for h in range(X_HEADS):
            ca_ref[8 * pr:8 * pr + 8, h * X_HEAD_DIM:(h + 1) * X_HEAD_DIM] = o[8 * h:8 * h + 8]


def _xattn_sample(x, ck, cv, g_cross, w_cq, nreq):
    n = x.shape[0]
    nb = ck.shape[0]
    kv_spec = pl.BlockSpec((nreq, N_MEM, X_HEADS, X_HEAD_DIM), lambda i: (i, 0, 0, 0))
    return pl.pallas_call(
        functools.partial(_xattn_sample_kernel, nreq=nreq),
        grid=(nb // nreq,),
        in_specs=[pl.BlockSpec((4 * nreq, D_MODEL), lambda i: (i, 0)), kv_spec, kv_spec,
                  _const_spec(g_cross.shape), _const_spec(w_cq.shape)],
        out_specs=pl.BlockSpec((4 * nreq, D_MODEL), lambda i: (i, 0)),
        out_shape=jax.ShapeDtypeStruct((n, D_MODEL), BF16),
        compiler_params=pltpu.CompilerParams(dimension_semantics=("arbitrary",),
                                             vmem_limit_bytes=VMEM_LIMIT_BYTES),
        name="xattn_sample",
    )(x, ck, cv, g_cross, w_cq)


def kernel(x_prompt, x_sample, mem_prompt, state_hgrn, cache_mem_k, cache_mem_v, lb_param, g_mix,
           w_in, hg_norm_g, ln_v_g, ln_v_b, w_s, b_s, w_out, g_cross, g_mem, w_cq, w_ck, w_cv, w_co,
           g_mlp, w_up, w_down, g_final):
    B, L, _ = x_prompt.shape
    DB, DL, _ = x_sample.shape
    assert DL == 4 and g_mix.shape[0] == 1

    row = lambda a: a.reshape(1, -1)
    win_b, wout_b = w_in[0].astype(BF16), w_out[0].astype(BF16)
    wcq_b, wco_b = w_cq[0].astype(BF16), w_co[0].astype(BF16)
    wck_b, wcv_b = w_ck[0].astype(BF16), w_cv[0].astype(BF16)
    wup_b, wdn_b = w_up[0].astype(BF16), w_down[0].astype(BF16)
    gmix, hgn, lng, lnb = row(g_mix[0]), row(hg_norm_g[0]), row(ln_v_g[0]), row(ln_v_b[0])
    gcr, gmem, gmlp, gfin = row(g_cross[0]), row(g_mem[0]), row(g_mlp[0]), row(g_final)
    wmix_p = w_s[0]
    bcol_p = b_s[0].T
    wmix_s = jnp.tile(w_s[0][:, :DL, :DL], (1, CHUNK // DL, CHUNK // DL))
    bcol_s = jnp.tile(b_s[0][:, :DL].T, (CHUNK // DL, 1))

    mk, mv = _memkv(mem_prompt.reshape(B * N_MEM, D_MODEL), gmem, wck_b, wcv_b, tm=512)
    h_p, s_p = _mix_prompt(x_prompt, lb_param, gmix, win_b, hgn, lng, lnb, wmix_p, bcol_p, wout_b,
                           tm=512)
    y_p = _post_prompt(h_p, mk.reshape(B, N_MEM, D_MODEL), mv.reshape(B, N_MEM, D_MODEL), gcr,
                       wcq_b, wco_b, gmlp, wup_b, wdn_b, gfin, tm=512)

    tm_s = 128
    xs = x_sample.reshape(DB * DL // tm_s, tm_s, D_MODEL)
    h_s, s_s, vn_s = _mix_sample(xs, state_hgrn[0], lb_param, gmix, win_b, hgn, lng, lnb, wmix_s,
                                 bcol_s, wout_b, tm=tm_s)
    ca_s = _xattn_sample(h_s.reshape(DB * DL, D_MODEL), cache_mem_k[0], cache_mem_v[0], gcr, wcq_b,
                         nreq=8)
    y_s = _post_sample(h_s.reshape(1, DB * DL, D_MODEL), ca_s, wco_b, gmlp, wup_b, wdn_b, gfin)

    return (y_p, y_s.reshape(DB, DL, D_MODEL), s_p[None], s_s[None],
            mk.reshape(1, B, N_MEM, X_HEADS, X_HEAD_DIM), mv.reshape(1, B, N_MEM, X_HEADS, X_HEAD_DIM),
            vn_s.reshape(1, DB, DL, CM_WIDTH))
```

```python
import functools

import jax
import jax.numpy as jnp
from jax import lax
from jax.experimental import pallas as pl
from jax.experimental.pallas import tpu as pltpu

F32 = jnp.float32
BF16 = jnp.bfloat16

D_MODEL = 1024
HG_WIDTH = 512
HG_HEADS = 4
HG_DK = 128
CM_WIDTH = 512
CM_GROUPS = 4
CM_GROUP_DIM = 128
IN_WIDTH = 4 * HG_WIDTH + 2 * CM_WIDTH
N_MEM = 256
X_HEADS = 4
X_HEAD_DIM = 256
D_FF = 4096
EPS = 1e-6

CHUNK = 128
SUB = 16
VMEM_LIMIT_BYTES = 58 * 1024 * 1024
LOG2E = 1.4426950408889634
FAST_BLOCK = 32
FAST_MAX_DECAY = 80.0


def _bdot(a, b):
    return jnp.dot(a.astype(BF16), b.astype(BF16), preferred_element_type=F32)


def _bdot_nt(a, b):
    return lax.dot_general(a.astype(BF16), b.astype(BF16), (((1,), (1,)), ((), ())),
                           preferred_element_type=F32)


def _rmsnorm(x, g):
    ms = jnp.mean(x * x, axis=-1, keepdims=True)
    return x * lax.rsqrt(ms + EPS) * g


def _sigmoid(x):
    return 0.5 * jnp.tanh(0.5 * x) + 0.5


def _gelu(x):
    return 0.5 * x * (1.0 + jnp.tanh(0.7978845608028654 * (x + 0.044715 * (x * x * x))))


def _softmax_rows(s):
    m = jnp.max(s, axis=-1, keepdims=True)
    e = jnp.exp(s - m)
    return e / jnp.sum(e, axis=-1, keepdims=True)


def _seg_cumsum(x, seg):
    pos = lax.broadcasted_iota(jnp.int32, x.shape, 0) % seg
    s = 1
    while s < seg:
        x = x + jnp.where(pos >= s, pltpu.roll(x, s, 0), 0.0)
        s *= 2
    return x


def _lower_bound(lb_ref):
    lbp = lb_ref[...]
    lbe = jnp.exp(lbp - jnp.max(lbp, axis=0, keepdims=True))
    return lbe[0:1] / jnp.sum(lbe, axis=0, keepdims=True)


def _gates(zq, zf, lb, seg):
    th = jnp.tanh(0.5 * zf)
    logf = jnp.log(lb + (1.0 - lb) * (0.5 + 0.5 * th))
    return zq * _sigmoid(zq), (1.0 - lb) * (0.5 - 0.5 * th), _seg_cumsum(logf, seg)


def _block_decay(bcum, sample):
    if sample:
        return jnp.max(-bcum, axis=0, keepdims=True)
    d = -bcum[FAST_BLOCK - 1:FAST_BLOCK]
    for i in range(1, CHUNK // FAST_BLOCK):
        n0, n1 = i * FAST_BLOCK, (i + 1) * FAST_BLOCK
        d = jnp.maximum(d, bcum[n0 - 1:n0] - bcum[n1 - 1:n1])
    return d


def _chunk_masks(sample):
    row = lax.broadcasted_iota(jnp.int32, (CHUNK, CHUNK), 0)
    col = lax.broadcasted_iota(jnp.int32, (CHUNK, CHUNK), 1)
    if sample:
        return (row // 4 == col // 4) & (row >= col)
    return row >= col


def _scores_fast(q, k, bcum, sample):
    if sample:
        return _bdot_nt(q * jnp.exp(bcum), k * jnp.exp(-bcum))
    a_rows = []
    for i in range(CHUNK // FAST_BLOCK):
        n0, n1 = i * FAST_BLOCK, (i + 1) * FAST_BLOCK
        ref_b = bcum[n0 - 1:n0] if i else jnp.zeros((1, HG_DK), F32)
        qt = q[n0:n1] * jnp.exp(bcum[n0:n1] - ref_b)
        kt = k[:n1] * jnp.exp(ref_b - bcum[:n1])
        if n1 < CHUNK:
            kt = jnp.concatenate([kt, jnp.zeros((CHUNK - n1, HG_DK), F32)], axis=0)
        a_rows.append(_bdot_nt(qt, kt))
    return jnp.concatenate(a_rows, axis=0)


def _scores_exact(q, k, bcum, sample, mask):
    col8 = lax.broadcasted_iota(jnp.int32, (8, CHUNK), 1)
    b2 = bcum * LOG2E
    cexp = b2 - jnp.log2(k)
    a_rows = []
    for gb in range(CHUNK // SUB):
        lo = slice(gb * SUB, gb * SUB + 8)
        hi = slice(gb * SUB + 8, (gb + 1) * SUB)
        a_lo = jnp.zeros((8, CHUNK), F32)
        a_hi = jnp.zeros((8, CHUNK), F32)
        for s in range(SUB):
            sg = gb * SUB + s
            c_s = cexp[sg:sg + 1]
            if s < 8:
                p = q[lo] * jnp.exp2(b2[lo] - c_s)
                a_lo = jnp.where(col8 == sg, jnp.sum(p, axis=-1, keepdims=True), a_lo)
            if (not sample) or s >= 8:
                p = q[hi] * jnp.exp2(b2[hi] - c_s)
                a_hi = jnp.where(col8 == sg, jnp.sum(p, axis=-1, keepdims=True), a_hi)
        a_rows.append(a_lo)
        a_rows.append(a_hi)
    a = jnp.concatenate(a_rows, axis=0)
    if sample:
        return a
    o_rows = [jnp.zeros((SUB, CHUNK), F32)]
    for i in range(1, CHUNK // SUB):
        n = i * SUB
        ref_b = bcum[n - 1:n]
        qt = q[n:n + SUB] * jnp.exp(bcum[n:n + SUB] - ref_b)
        kt = k[:n] * jnp.exp(ref_b - bcum[:n])
        kt = jnp.concatenate([kt, jnp.zeros((CHUNK - n, HG_DK), F32)], axis=0)
        o_rows.append(_bdot_nt(qt, kt))
    return jnp.where(mask, a, 0.0) + jnp.concatenate(o_rows, axis=0)


def _hgrn_out(o, zg, g):
    o = o * lax.rsqrt(jnp.mean(o * o, axis=-1, keepdims=True) + EPS) * g
    return (o * _sigmoid(zg)).astype(BF16)


def _gmlp_chunk(z_ref, rows, lng_ref, lnb_ref, wmix_ref, bcol_ref, mask, ocat_ref, vn_ref):
    u = _gelu(z_ref[rows, 4 * HG_WIDTH:4 * HG_WIDTH + CM_WIDTH])
    gv = _gelu(z_ref[rows, 4 * HG_WIDTH + CM_WIDTH:IN_WIDTH])
    mu = jnp.mean(gv, axis=-1, keepdims=True)
    dv = gv - mu
    var = jnp.mean(dv * dv, axis=-1, keepdims=True)
    vn = dv * lax.rsqrt(var + EPS) * lng_ref[...] + lnb_ref[...]
    if vn_ref is not None:
        vn_ref[rows, :] = vn
    for g in range(CM_GROUPS):
        gs = slice(g * CM_GROUP_DIM, (g + 1) * CM_GROUP_DIM)
        wm = jnp.where(mask, wmix_ref[g], 0.0)
        mixed = _bdot(wm, vn[:, gs]) + bcol_ref[:, g:g + 1]
        ocat_ref[rows, HG_WIDTH + g * CM_GROUP_DIM:HG_WIDTH + (g + 1) * CM_GROUP_DIM] = (
            u[:, gs] * mixed).astype(BF16)


def _attention_shared(hq, mk_ref, mv_ref, ca_s):
    scale = X_HEAD_DIM ** -0.5
    for h in range(X_HEADS):
        hs = slice(h * X_HEAD_DIM, (h + 1) * X_HEAD_DIM)
        s = _bdot_nt(hq[:, hs], mk_ref[0, :, hs]) * scale
        ca_s[:, hs] = _bdot(_softmax_rows(s), mv_ref[0, :, hs]).astype(BF16)


def _mlp(hn_bf16, wup_ref, wdn_ref, hm_s):
    for c in range(D_FF // D_MODEL):
        fs = slice(c * D_MODEL, (c + 1) * D_MODEL)
        hm = jnp.maximum(jnp.dot(hn_bf16, wup_ref[:, fs], preferred_element_type=F32), 0.0)
        hm_s[:, fs] = (hm * hm).astype(BF16)
    return jnp.dot(hm_s[...], wdn_ref[...], preferred_element_type=F32)


def _const_spec(shape):
    nd = len(shape)
    return pl.BlockSpec(shape, lambda *_: (0,) * nd, pipeline_mode=pl.Buffered(1))


def _prompt_kernel(x_ref, mk_ref, mv_ref, lb_ref, gmix_ref, win_ref, hgn_ref, lng_ref, lnb_ref,
                   wmix_ref, bcol_ref, wout_ref, gc_ref, wcq_ref, wco_ref, gm_ref, wup_ref, wdn_ref,
                   gf_ref, y_ref, sout_ref,
                   z_ref, ocat_ref, q_s, k_s, b_s, a_s, st_ref, h1_s, x2_s, hn_s, ca_s, hm_s,
                   *, tm, nt):
    t = pl.program_id(0)
    i = lax.rem(t, nt)
    nch = tm // CHUNK
    mask = _chunk_masks(False)

    @pl.when(t == 0)
    def _():
        h1_s[...] = jnp.zeros_like(h1_s)

    @pl.when(i == 0)
    def _():
        st_ref[...] = jnp.zeros_like(st_ref)


    scale = X_HEAD_DIM ** -0.5
    h1 = h1_s[...]
    hq = jnp.dot(_rmsnorm(h1, gc_ref[...]).astype(BF16), wcq_ref[...], preferred_element_type=F32)
    probs = []
    for h in range(X_HEADS):
        hs = slice(h * X_HEAD_DIM, (h + 1) * X_HEAD_DIM)
        probs.append(_softmax_rows(_bdot_nt(hq[:, hs], mk_ref[0, :, hs]) * scale))

    x = x_ref[0]
    z_ref[...] = jnp.dot(_rmsnorm(x, gmix_ref[...]).astype(BF16), win_ref[...],
                         preferred_element_type=F32)

    for h in range(X_HEADS):
        hs = slice(h * X_HEAD_DIM, (h + 1) * X_HEAD_DIM)
        ca_s[:, hs] = _bdot(probs[h], mv_ref[0, :, hs]).astype(BF16)

    lb_all = _lower_bound(lb_ref)
    dmax = jnp.zeros((1, HG_WIDTH), F32)
    for c in range(nch):
        rows = slice(c * CHUNK, (c + 1) * CHUNK)
        q, k, bcum = _gates(z_ref[rows, 0:HG_WIDTH], z_ref[rows, HG_WIDTH:2 * HG_WIDTH], lb_all, CHUNK)
        q_s[rows, :] = q
        k_s[rows, :] = k
        b_s[rows, :] = bcum
        dmax = jnp.maximum(dmax, _block_decay(bcum, False))
    fast_ok = jnp.max(dmax) < FAST_MAX_DECAY

    x2 = h1 + jnp.dot(ca_s[...], wco_ref[...], preferred_element_type=F32)
    x2_s[...] = x2
    hn_s[...] = _rmsnorm(x2, gm_ref[...]).astype(BF16)

    def fill_scores(j, fast):
        c, h = divmod(j, HG_HEADS)
        rows = slice(c * CHUNK, (c + 1) * CHUNK)
        cs = slice(h * HG_DK, (h + 1) * HG_DK)
        q, k, bcum = q_s[rows, cs], k_s[rows, cs], b_s[rows, cs]
        a = _scores_fast(q, k, bcum, False) if fast else _scores_exact(q, k, bcum, False, mask)
        a_s[j] = jnp.where(mask, a, 0.0).astype(BF16)

    def head_chunk(j):
        c, h = divmod(j, HG_HEADS)
        rows = slice(c * CHUNK, (c + 1) * CHUNK)
        cs = slice(h * HG_DK, (h + 1) * HG_DK)
        q, k, bcum = q_s[rows, cs], k_s[rows, cs], b_s[rows, cs]
        v = z_ref[rows, 2 * HG_WIDTH + h * HG_DK:2 * HG_WIDTH + (h + 1) * HG_DK]
        zg = z_ref[rows, 3 * HG_WIDTH + h * HG_DK:3 * HG_WIDTH + (h + 1) * HG_DK]
        eb = jnp.exp(bcum)
        kd = k * jnp.exp(bcum[CHUNK - 1:CHUNK] - bcum)
        st = st_ref[h]
        o = jnp.dot(a_s[j], v.astype(BF16), preferred_element_type=F32)
        o = o + _bdot_nt(q * eb, st)
        st_ref[h] = st * eb[CHUNK - 1:CHUNK] + _bdot(v.T, kd)
        ocat_ref[rows, cs] = _hgrn_out(o, zg, hgn_ref[:, cs])
        if h == HG_HEADS - 1:
            _gmlp_chunk(z_ref, rows, lng_ref, lnb_ref, wmix_ref, bcol_ref, mask, ocat_ref, None)

    def region2(fast):
        nsl = D_FF // D_MODEL
        nj = nch * HG_HEADS
        hn = hn_s[...]
        acc = x2_s[...]
        for p in range(nsl):
            fs = slice(p * D_MODEL, (p + 1) * D_MODEL)
            hm = jnp.maximum(jnp.dot(hn, wup_ref[:, fs], preferred_element_type=F32), 0.0)
            hm_s[:, fs] = (hm * hm).astype(BF16)
            for j in range(p * nj // nsl, (p + 1) * nj // nsl):
                fill_scores(j, fast)
        for p in range(nsl):
            fs = slice(p * D_MODEL, (p + 1) * D_MODEL)
            acc = acc + jnp.dot(hm_s[:, fs], wdn_ref[fs, :], preferred_element_type=F32)
            for j in range(p * nj // nsl, (p + 1) * nj // nsl):
                head_chunk(j)
        y_ref[0] = _rmsnorm(acc, gf_ref[...])
        h1_s[...] = x_ref[0] + jnp.dot(ocat_ref[...], wout_ref[...], preferred_element_type=F32)

    @pl.when(fast_ok)
    def _():
        region2(True)

    @pl.when(jnp.logical_not(fast_ok))
    def _():
        region2(False)

    @pl.when(i == nt - 1)
    def _():
        for h in range(HG_HEADS):
            sout_ref[0, h] = st_ref[h].T


def _prompt_layer(x, mk, mv, lb_param, g_mix, w_in, hg_norm_g, ln_v_g, ln_v_b, wmix, bcol, w_out,
                  g_cross, w_cq, w_co, g_mlp, w_up, w_down, g_final, tm):
    B, L, _ = x.shape
    nt = L // tm
    T = B * nt

    def cur(t):
        return jnp.minimum(t, T - 1)

    def prev(t):
        return jnp.maximum(t - 1, 0)

    consts = (lb_param, g_mix, w_in, hg_norm_g, ln_v_g, ln_v_b, wmix, bcol, w_out,
              g_cross, w_cq, w_co, g_mlp, w_up, w_down, g_final)
    in_specs = [
        pl.BlockSpec((1, tm, D_MODEL), lambda t: (cur(t) // nt, cur(t) % nt, 0)),
        pl.BlockSpec((1, N_MEM, D_MODEL), lambda t: (prev(t) // nt, 0, 0)),
        pl.BlockSpec((1, N_MEM, D_MODEL), lambda t: (prev(t) // nt, 0, 0)),
    ] + [_const_spec(a.shape) for a in consts]
    out_specs = [
        pl.BlockSpec((1, tm, D_MODEL), lambda t: (prev(t) // nt, prev(t) % nt, 0)),
        pl.BlockSpec((1, HG_HEADS, HG_DK, HG_DK), lambda t: (cur(t) // nt, 0, 0, 0)),
    ]
    scratch = [
        pltpu.VMEM((tm, IN_WIDTH), F32),
        pltpu.VMEM((tm, D_MODEL), BF16),
        pltpu.VMEM((tm, HG_WIDTH), F32), pltpu.VMEM((tm, HG_WIDTH), F32),
        pltpu.VMEM((tm, HG_WIDTH), F32),
        pltpu.VMEM((tm // CHUNK * HG_HEADS, CHUNK, CHUNK), BF16),
        pltpu.VMEM((HG_HEADS, HG_DK, HG_DK), F32),
        pltpu.VMEM((tm, D_MODEL), F32),
        pltpu.VMEM((tm, D_MODEL), F32),
        pltpu.VMEM((tm, D_MODEL), BF16),
        pltpu.VMEM((tm, D_MODEL), BF16),
        pltpu.VMEM((tm, D_FF), BF16),
    ]
    return pl.pallas_call(
        functools.partial(_prompt_kernel, tm=tm, nt=nt),
        grid=(T + 1,),
        in_specs=in_specs,
        out_specs=out_specs,
        out_shape=[jax.ShapeDtypeStruct((B, L, D_MODEL), F32),
                   jax.ShapeDtypeStruct((B, HG_HEADS, HG_DK, HG_DK), F32)],
        scratch_shapes=scratch,
        compiler_params=pltpu.CompilerParams(dimension_semantics=("arbitrary",),
                                             vmem_limit_bytes=VMEM_LIMIT_BYTES),
        name="prompt_layer",
    )(x, mk, mv, *consts)


def _mix_sample_kernel(x_ref, s0_ref, lb_ref, gmix_ref, win_ref, hgn_ref, lng_ref, lnb_ref, wmix_ref,
                       bcol_ref, wout_ref, h_ref, sout_ref, vn_ref, z_ref, ocat_ref, q_s, k_s, b_s,
                       *, tm):
    x = x_ref[0]
    z_ref[...] = jnp.dot(_rmsnorm(x, gmix_ref[...]).astype(BF16), win_ref[...],
                         preferred_element_type=F32)
    lb_all = _lower_bound(lb_ref)
    mask = _chunk_masks(True)
    eye = (lax.broadcasted_iota(jnp.int32, (CHUNK, CHUNK), 0)
           == lax.broadcasted_iota(jnp.int32, (CHUNK, CHUNK), 1))
    r16 = lax.broadcasted_iota(jnp.int32, (SUB, HG_DK), 0)

    def gate_body(c, dmax):
        rows = pl.ds(pl.multiple_of(c * CHUNK, CHUNK), CHUNK)
        q, k, bcum = _gates(z_ref[rows, 0:HG_WIDTH], z_ref[rows, HG_WIDTH:2 * HG_WIDTH], lb_all, 4)
        q_s[rows, :] = q
        k_s[rows, :] = k
        b_s[rows, :] = bcum
        return jnp.maximum(dmax, _block_decay(bcum, True))

    dmax = lax.fori_loop(0, tm // CHUNK, gate_body, jnp.zeros((1, HG_WIDTH), F32))
    fast_ok = jnp.max(dmax) < FAST_MAX_DECAY

    def chunk_body(c, carry, *, fast):
        rows = pl.ds(pl.multiple_of(c * CHUNK, CHUNK), CHUNK)
        for h in range(HG_HEADS):
            cs = slice(h * HG_DK, (h + 1) * HG_DK)
            q, k, bcum = q_s[rows, cs], k_s[rows, cs], b_s[rows, cs]
            v = z_ref[rows, 2 * HG_WIDTH + h * HG_DK:2 * HG_WIDTH + (h + 1) * HG_DK]
            zg = z_ref[rows, 3 * HG_WIDTH + h * HG_DK:3 * HG_WIDTH + (h + 1) * HG_DK]
            a = _scores_fast(q, k, bcum, True) if fast else _scores_exact(q, k, bcum, True, mask)
            o = _bdot(jnp.where(mask, a, 0.0), v)
            eb = jnp.exp(bcum)
            qd = q * eb
            o_parts = []
            for gb in range(CHUNK // SUB):
                blk = slice(gb * SUB, (gb + 1) * SUB)
                qd_b, v_b, eb_b, k_b, b_b = qd[blk], v[blk], eb[blk], k[blk], bcum[blk]
                inter = jnp.zeros((SUB, HG_DK), F32)
                for j in range(4):
                    req = c * (CHUNK // 4) + gb * 4 + j
                    s0 = s0_ref[req, h]
                    last = 4 * j + 3
                    inter = jnp.where(r16 // 4 == j, _bdot(qd_b, s0), inter)
                    kd = jnp.where(r16 // 4 == j, k_b * jnp.exp(b_b[last:last + 1] - b_b), 0.0)
                    upd = lax.dot_general(kd.astype(BF16), v_b.astype(BF16),
                                          (((0,), (0,)), ((), ())), preferred_element_type=F32)
                    dcol = jnp.sum(jnp.where(eye, eb_b[last:last + 1], 0.0), axis=-1, keepdims=True)
                    sout_ref[req, h] = dcol * s0 + upd
                o_parts.append(inter)
            o = o + jnp.concatenate(o_parts, axis=0)
            ocat_ref[rows, cs] = _hgrn_out(o, zg, hgn_ref[:, cs])
        _gmlp_chunk(z_ref, rows, lng_ref, lnb_ref, wmix_ref, bcol_ref, mask, ocat_ref, vn_ref)
        return carry

    @pl.when(fast_ok)
    def _():
        lax.fori_loop(0, tm // CHUNK, functools.partial(chunk_body, fast=True), 0)

    @pl.when(jnp.logical_not(fast_ok))
    def _():
        lax.fori_loop(0, tm // CHUNK, functools.partial(chunk_body, fast=False), 0)

    h_ref[0] = x + jnp.dot(ocat_ref[...], wout_ref[...], preferred_element_type=F32)


def _mix_sample(x, s0, lb_param, g_mix, w_in, hg_norm_g, ln_v_g, ln_v_b, wmix, bcol, w_out, tm):
    nt, _, _ = x.shape
    nreq = tm // 4
    consts = (lb_param, g_mix, w_in, hg_norm_g, ln_v_g, ln_v_b, wmix, bcol, w_out)
    in_specs = [
        pl.BlockSpec((1, tm, D_MODEL), lambda i: (i, 0, 0)),
        pl.BlockSpec((nreq, HG_HEADS, HG_DK, HG_DK), lambda i: (i, 0, 0, 0)),
    ] + [_const_spec(a.shape) for a in consts]
    out_specs = [
        pl.BlockSpec((1, tm, D_MODEL), lambda i: (i, 0, 0)),
        pl.BlockSpec((nreq, HG_HEADS, HG_DK, HG_DK), lambda i: (i, 0, 0, 0)),
        pl.BlockSpec((tm, CM_WIDTH), lambda i: (i, 0)),
    ]
    return pl.pallas_call(
        functools.partial(_mix_sample_kernel, tm=tm),
        grid=(nt,),
        in_specs=in_specs,
        out_specs=out_specs,
        out_shape=[jax.ShapeDtypeStruct((nt, tm, D_MODEL), F32),
                   jax.ShapeDtypeStruct(s0.shape, F32),
                   jax.ShapeDtypeStruct((nt * tm, CM_WIDTH), F32)],
        scratch_shapes=[pltpu.VMEM((tm, IN_WIDTH), F32), pltpu.VMEM((tm, D_MODEL), BF16)]
        + [pltpu.VMEM((tm, HG_WIDTH), F32)] * 3,
        compiler_params=pltpu.CompilerParams(dimension_semantics=("arbitrary",),
                                             vmem_limit_bytes=VMEM_LIMIT_BYTES),
        name="mix_sample",
    )(x, s0, *consts)


def _memkv_kernel(m_ref, g_ref, wk_ref, wv_ref, k_ref, v_ref):
    mn = _rmsnorm(m_ref[...], g_ref[...]).astype(BF16)
    k_ref[...] = jnp.dot(mn, wk_ref[...], preferred_element_type=F32)
    v_ref[...] = jnp.dot(mn, wv_ref[...], preferred_element_type=F32)


def _memkv(mem, g_mem, w_ck, w_cv, tm):
    n = mem.shape[0]
    return pl.pallas_call(
        _memkv_kernel,
        grid=(n // tm,),
        in_specs=[pl.BlockSpec((tm, D_MODEL), lambda i: (i, 0)), _const_spec(g_mem.shape),
                  _const_spec(w_ck.shape), _const_spec(w_cv.shape)],
        out_specs=[pl.BlockSpec((tm, D_MODEL), lambda i: (i, 0)),
                   pl.BlockSpec((tm, D_MODEL), lambda i: (i, 0))],
        out_shape=[jax.ShapeDtypeStruct((n, D_MODEL), F32), jax.ShapeDtypeStruct((n, D_MODEL), F32)],
        compiler_params=pltpu.CompilerParams(dimension_semantics=("arbitrary",),
                                             vmem_limit_bytes=VMEM_LIMIT_BYTES),
        name="memkv",
    )(mem, g_mem, w_ck, w_cv)


def _post_sample_kernel(x_ref, ca_ref, wco_ref, gm_ref, wup_ref, wdn_ref, gf_ref, y_ref, hm_s):
    x = x_ref[0] + jnp.dot(ca_ref[...], wco_ref[...], preferred_element_type=F32)
    x = x + _mlp(_rmsnorm(x, gm_ref[...]).astype(BF16), wup_ref, wdn_ref, hm_s)
    y_ref[0] = _rmsnorm(x, gf_ref[...])


def _post_sample(x, ca, w_co, g_mlp, w_up, w_down, g_final):
    nt, tm, _ = x.shape
    in_specs = [
        pl.BlockSpec((1, tm, D_MODEL), lambda i: (i, 0, 0)),
        pl.BlockSpec((tm, D_MODEL), lambda i: (i, 0)),
        _const_spec(w_co.shape), _const_spec(g_mlp.shape), _const_spec(w_up.shape),
        _const_spec(w_down.shape), _const_spec(g_final.shape),
    ]
    return pl.pallas_call(
        _post_sample_kernel,
        grid=(nt,),
        in_specs=in_specs,
        out_specs=pl.BlockSpec((1, tm, D_MODEL), lambda i: (i, 0, 0)),
        out_shape=jax.ShapeDtypeStruct((nt, tm, D_MODEL), F32),
        scratch_shapes=[pltpu.VMEM((tm, D_FF), BF16)],
        compiler_params=pltpu.CompilerParams(dimension_semantics=("arbitrary",),
                                             vmem_limit_bytes=VMEM_LIMIT_BYTES),
        name="post_sample",
    )(x, ca, w_co, g_mlp, w_up, w_down, g_final)


def _xattn_sample_kernel(x_ref, k_ref, v_ref, gc_ref, wcq_ref, ca_ref, *, nreq):
    hq = jnp.dot(_rmsnorm(x_ref[...], gc_ref[...]).astype(BF16), wcq_ref[...],
                 preferred_element_type=F32)
    scale = X_HEAD_DIM ** -0.5
    nrow = 8 * X_HEADS
    rh = lax.broadcasted_iota(jnp.int32, (nrow, N_MEM * X_HEADS), 0) // 8
    ch = lax.broadcasted_iota(jnp.int32, (nrow, N_MEM * X_HEADS), 1) % X_HEADS
    same_head = rh == ch
    first = (lax.broadcasted_iota(jnp.int32, (nrow, X_HEAD_DIM), 0) % 8) < 4
    for pr in range(nreq // 2):
        hq8 = hq[8 * pr:8 * pr + 8]
        q = jnp.concatenate([hq8[:, h * X_HEAD_DIM:(h + 1) * X_HEAD_DIM] for h in range(X_HEADS)],
                            axis=0).astype(BF16)
        outs = []
        for r in (2 * pr, 2 * pr + 1):
            k2 = k_ref[r].reshape(N_MEM * X_HEADS, X_HEAD_DIM)
            v2 = v_ref[r].reshape(N_MEM * X_HEADS, X_HEAD_DIM)
            s = jnp.where(same_head, _bdot_nt(q, k2) * scale, -jnp.inf)
            outs.append(_bdot(_softmax_rows(s), v2))
        o = jnp.where(first, outs[0], outs[1]).astype(BF16)
        for h in range(X_HEADS):
            ca_ref[8 * pr:8 * pr + 8, h * X_HEAD_DIM:(h + 1) * X_HEAD_DIM] = o[8 * h:8 * h + 8]


def _xattn_sample(x, ck, cv, g_cross, w_cq, nreq):
    n = x.shape[0]
    nb = ck.shape[0]
    kv_spec = pl.BlockSpec((nreq, N_MEM, X_HEADS, X_HEAD_DIM), lambda i: (i, 0, 0, 0))
    return pl.pallas_call(
        functools.partial(_xattn_sample_kernel, nreq=nreq),
        grid=(nb // nreq,),
        in_specs=[pl.BlockSpec((4 * nreq, D_MODEL), lambda i: (i, 0)), kv_spec, kv_spec,
                  _const_spec(g_cross.shape), _const_spec(w_cq.shape)],
        out_specs=pl.BlockSpec((4 * nreq, D_MODEL), lambda i: (i, 0)),
        out_shape=jax.ShapeDtypeStruct((n, D_MODEL), BF16),
        compiler_params=pltpu.CompilerParams(dimension_semantics=("arbitrary",),
                                             vmem_limit_bytes=VMEM_LIMIT_BYTES),
        name="xattn_sample",
    )(x, ck, cv, g_cross, w_cq)


def kernel(x_prompt, x_sample, mem_prompt, state_hgrn, cache_mem_k, cache_mem_v, lb_param, g_mix,
           w_in, hg_norm_g, ln_v_g, ln_v_b, w_s, b_s, w_out, g_cross, g_mem, w_cq, w_ck, w_cv, w_co,
           g_mlp, w_up, w_down, g_final):
    B, L, _ = x_prompt.shape
    DB, DL, _ = x_sample.shape
    assert DL == 4 and g_mix.shape[0] == 1

    row = lambda a: a.reshape(1, -1)
    win_b, wout_b = w_in[0].astype(BF16), w_out[0].astype(BF16)
    wcq_b, wco_b = w_cq[0].astype(BF16), w_co[0].astype(BF16)
    wck_b, wcv_b = w_ck[0].astype(BF16), w_cv[0].astype(BF16)
    wup_b, wdn_b = w_up[0].astype(BF16), w_down[0].astype(BF16)
    gmix, hgn, lng, lnb = row(g_mix[0]), row(hg_norm_g[0]), row(ln_v_g[0]), row(ln_v_b[0])
    gcr, gmem, gmlp, gfin = row(g_cross[0]), row(g_mem[0]), row(g_mlp[0]), row(g_final)
    wmix_p = w_s[0]
    bcol_p = b_s[0].T
    wmix_s = jnp.tile(w_s[0][:, :DL, :DL], (1, CHUNK // DL, CHUNK // DL))
    bcol_s = jnp.tile(b_s[0][:, :DL].T, (CHUNK // DL, 1))

    mk, mv = _memkv(mem_prompt.reshape(B * N_MEM, D_MODEL), gmem, wck_b, wcv_b, tm=512)
    y_p, s_p = _prompt_layer(x_prompt, mk.reshape(B, N_MEM, D_MODEL), mv.reshape(B, N_MEM, D_MODEL),
                             lb_param, gmix, win_b, hgn, lng, lnb, wmix_p, bcol_p, wout_b,
                             gcr, wcq_b, wco_b, gmlp, wup_b, wdn_b, gfin, tm=256)

    tm_s = 128
    xs = x_sample.reshape(DB * DL // tm_s, tm_s, D_MODEL)
    h_s, s_s, vn_s = _mix_sample(xs, state_hgrn[0], lb_param, gmix, win_b, hgn, lng, lnb, wmix_s,
                                 bcol_s, wout_b, tm=tm_s)
    ca_s = _xattn_sample(h_s.reshape(DB * DL, D_MODEL), cache_mem_k[0], cache_mem_v[0], gcr, wcq_b,
                         nreq=8)
    y_s = _post_sample(h_s.reshape(1, DB * DL, D_MODEL), ca_s, wco_b, gmlp, wup_b, wdn_b, gfin)

    return (y_p, y_s.reshape(DB, DL, D_MODEL), s_p[None], s_s[None],
            mk.reshape(1, B, N_MEM, X_HEADS, X_HEAD_DIM), mv.reshape(1, B, N_MEM, X_HEADS, X_HEAD_DIM),
            vn_s.reshape(1, DB, DL, CM_WIDTH))
```

```python
import functools

import jax
import jax.numpy as jnp
from jax import lax
from jax.experimental import pallas as pl
from jax.experimental.pallas import tpu as pltpu

F32 = jnp.float32
BF16 = jnp.bfloat16

D_MODEL = 1024
HG_WIDTH = 512
HG_HEADS = 4
HG_DK = 128
CM_WIDTH = 512
CM_GROUPS = 4
CM_GROUP_DIM = 128
IN_WIDTH = 4 * HG_WIDTH + 2 * CM_WIDTH
N_MEM = 256
X_HEADS = 4
X_HEAD_DIM = 256
D_FF = 4096
EPS = 1e-6

CHUNK = 128
SUB = 16
VMEM_LIMIT_BYTES = 62 * 1024 * 1024
LOG2E = 1.4426950408889634
FAST_BLOCK = 32
FAST_MAX_DECAY = 80.0


def _bdot(a, b):
    return jnp.dot(a.astype(BF16), b.astype(BF16), preferred_element_type=F32)


def _bdot_nt(a, b):
    return lax.dot_general(a.astype(BF16), b.astype(BF16), (((1,), (1,)), ((), ())),
                           preferred_element_type=F32)


def _rmsnorm(x, g):
    ms = jnp.mean(x * x, axis=-1, keepdims=True)
    return x * lax.rsqrt(ms + EPS) * g


def _sigmoid(x):
    return 0.5 * jnp.tanh(0.5 * x) + 0.5


def _gelu(x):
    return 0.5 * x * (1.0 + jnp.tanh(0.7978845608028654 * (x + 0.044715 * (x * x * x))))


def _softmax_rows(s):
    m = jnp.max(s, axis=-1, keepdims=True)
    e = jnp.exp(s - m)
    return e / jnp.sum(e, axis=-1, keepdims=True)


def _seg_cumsum(x, seg):
    pos = lax.broadcasted_iota(jnp.int32, x.shape, 0) % seg
    s = 1
    while s < seg:
        x = x + jnp.where(pos >= s, pltpu.roll(x, s, 0), 0.0)
        s *= 2
    return x


def _lower_bound(lb_ref):
    lbp = lb_ref[...]
    lbe = jnp.exp(lbp - jnp.max(lbp, axis=0, keepdims=True))
    return lbe[0:1] / jnp.sum(lbe, axis=0, keepdims=True)


def _gates(zq, zf, lb, seg):
    th = jnp.tanh(0.5 * zf)
    logf = jnp.log(lb + (1.0 - lb) * (0.5 + 0.5 * th))
    return zq * _sigmoid(zq), (1.0 - lb) * (0.5 - 0.5 * th), _seg_cumsum(logf, seg)


def _block_decay(bcum, sample):
    if sample:
        return jnp.max(-bcum, axis=0, keepdims=True)
    d = -bcum[FAST_BLOCK - 1:FAST_BLOCK]
    for i in range(1, CHUNK // FAST_BLOCK):
        n0, n1 = i * FAST_BLOCK, (i + 1) * FAST_BLOCK
        d = jnp.maximum(d, bcum[n0 - 1:n0] - bcum[n1 - 1:n1])
    return d


def _chunk_masks(sample):
    row = lax.broadcasted_iota(jnp.int32, (CHUNK, CHUNK), 0)
    col = lax.broadcasted_iota(jnp.int32, (CHUNK, CHUNK), 1)
    if sample:
        return (row // 4 == col // 4) & (row >= col)
    return row >= col


def _scores_fast(q, k, bcum, sample):
    if sample:
        return _bdot_nt(q * jnp.exp(bcum), k * jnp.exp(-bcum))
    a_rows = []
    for i in range(CHUNK // FAST_BLOCK):
        n0, n1 = i * FAST_BLOCK, (i + 1) * FAST_BLOCK
        ref_b = bcum[n0 - 1:n0] if i else jnp.zeros((1, HG_DK), F32)
        qt = q[n0:n1] * jnp.exp(bcum[n0:n1] - ref_b)
        kt = k[:n1] * jnp.exp(ref_b - bcum[:n1])
        if n1 < CHUNK:
            kt = jnp.concatenate([kt, jnp.zeros((CHUNK - n1, HG_DK), F32)], axis=0)
        a_rows.append(_bdot_nt(qt, kt))
    return jnp.concatenate(a_rows, axis=0)


def _scores_exact(q, k, bcum, sample, mask):
    col8 = lax.broadcasted_iota(jnp.int32, (8, CHUNK), 1)
    b2 = bcum * LOG2E
    cexp = b2 - jnp.log2(k)
    a_rows = []
    for gb in range(CHUNK // SUB):
        lo = slice(gb * SUB, gb * SUB + 8)
        hi = slice(gb * SUB + 8, (gb + 1) * SUB)
        a_lo = jnp.zeros((8, CHUNK), F32)
        a_hi = jnp.zeros((8, CHUNK), F32)
        for s in range(SUB):
            sg = gb * SUB + s
            c_s = cexp[sg:sg + 1]
            if s < 8:
                p = q[lo] * jnp.exp2(b2[lo] - c_s)
                a_lo = jnp.where(col8 == sg, jnp.sum(p, axis=-1, keepdims=True), a_lo)
            if (not sample) or s >= 8:
                p = q[hi] * jnp.exp2(b2[hi] - c_s)
                a_hi = jnp.where(col8 == sg, jnp.sum(p, axis=-1, keepdims=True), a_hi)
        a_rows.append(a_lo)
        a_rows.append(a_hi)
    a = jnp.concatenate(a_rows, axis=0)
    if sample:
        return a
    o_rows = [jnp.zeros((SUB, CHUNK), F32)]
    for i in range(1, CHUNK // SUB):
        n = i * SUB
        ref_b = bcum[n - 1:n]
        qt = q[n:n + SUB] * jnp.exp(bcum[n:n + SUB] - ref_b)
        kt = k[:n] * jnp.exp(ref_b - bcum[:n])
        kt = jnp.concatenate([kt, jnp.zeros((CHUNK - n, HG_DK), F32)], axis=0)
        o_rows.append(_bdot_nt(qt, kt))
    return jnp.where(mask, a, 0.0) + jnp.concatenate(o_rows, axis=0)


def _hgrn_out(o, zg, g):
    o = o * lax.rsqrt(jnp.mean(o * o, axis=-1, keepdims=True) + EPS) * g
    return (o * _sigmoid(zg)).astype(BF16)


def _gmlp_chunk(z_ref, rows, lng_ref, lnb_ref, wmix_ref, bcol_ref, mask, ocat_ref, vn_ref):
    u = _gelu(z_ref[rows, 4 * HG_WIDTH:4 * HG_WIDTH + CM_WIDTH])
    gv = _gelu(z_ref[rows, 4 * HG_WIDTH + CM_WIDTH:IN_WIDTH])
    mu = jnp.mean(gv, axis=-1, keepdims=True)
    dv = gv - mu
    var = jnp.mean(dv * dv, axis=-1, keepdims=True)
    vn = dv * lax.rsqrt(var + EPS) * lng_ref[...] + lnb_ref[...]
    if vn_ref is not None:
        vn_ref[rows, :] = vn
    for g in range(CM_GROUPS):
        gs = slice(g * CM_GROUP_DIM, (g + 1) * CM_GROUP_DIM)
        wm = jnp.where(mask, wmix_ref[g], 0.0)
        mixed = _bdot(wm, vn[:, gs]) + bcol_ref[:, g:g + 1]
        ocat_ref[rows, HG_WIDTH + g * CM_GROUP_DIM:HG_WIDTH + (g + 1) * CM_GROUP_DIM] = (
            u[:, gs] * mixed).astype(BF16)


def _sample_pair_probs(hq8, ck_ref):
    nrow = 8 * X_HEADS
    rh = lax.broadcasted_iota(jnp.int32, (nrow, N_MEM * X_HEADS), 0) // 8
    ch = lax.broadcasted_iota(jnp.int32, (nrow, N_MEM * X_HEADS), 1) % X_HEADS
    q = jnp.concatenate([hq8[:, h * X_HEAD_DIM:(h + 1) * X_HEAD_DIM] for h in range(X_HEADS)],
                        axis=0).astype(BF16)
    probs = []
    for r in range(2):
        k2 = ck_ref[r].reshape(N_MEM * X_HEADS, X_HEAD_DIM)
        s = jnp.where(rh == ch, _bdot_nt(q, k2) * (X_HEAD_DIM ** -0.5), -jnp.inf)
        probs.append(_softmax_rows(s))
    return probs


def _sample_pair_context(probs, cv_ref, ca_ref):
    first = (lax.broadcasted_iota(jnp.int32, (8 * X_HEADS, X_HEAD_DIM), 0) % 8) < 4
    outs = [_bdot(probs[r], cv_ref[r].reshape(N_MEM * X_HEADS, X_HEAD_DIM)) for r in range(2)]
    o = jnp.where(first, outs[0], outs[1])
    for h in range(X_HEADS):
        ca_ref[:, h * X_HEAD_DIM:(h + 1) * X_HEAD_DIM] = o[8 * h:8 * h + 8]


def _mlp(hn_bf16, wup_ref, wdn_ref, hm_s):
    for c in range(D_FF // D_MODEL):
        fs = slice(c * D_MODEL, (c + 1) * D_MODEL)
        hm = jnp.maximum(jnp.dot(hn_bf16, wup_ref[:, fs], preferred_element_type=F32), 0.0)
        hm_s[:, fs] = (hm * hm).astype(BF16)
    return jnp.dot(hm_s[...], wdn_ref[...], preferred_element_type=F32)


def _const_spec(shape):
    nd = len(shape)
    return pl.BlockSpec(shape, lambda *_: (0,) * nd, pipeline_mode=pl.Buffered(1))


def _prompt_kernel(x_ref, mk_ref, mv_ref, hqs_ref, ck_ref, cv_ref, lb_ref, gmix_ref, win_ref, hgn_ref,
                   lng_ref, lnb_ref, wmix_ref, bcol_ref, wout_ref, gc_ref, wcq_ref, wco_ref, gm_ref,
                   wup_ref, wdn_ref, gf_ref, y_ref, sout_ref, cas_ref,
                   z_ref, ocat_ref, q_s, k_s, b_s, a_s, st_ref, h1_s, x2_s, hn_s, ca_s, hm_s,
                   *, tm, nt):
    t = pl.program_id(0)
    i = lax.rem(t, nt)
    nch = tm // CHUNK
    mask = _chunk_masks(False)

    @pl.when(t == 0)
    def _():
        h1_s[...] = jnp.zeros_like(h1_s)

    @pl.when(i == 0)
    def _():
        st_ref[...] = jnp.zeros_like(st_ref)


    scale = X_HEAD_DIM ** -0.5
    h1 = h1_s[...]
    hq = jnp.dot(_rmsnorm(h1, gc_ref[...]).astype(BF16), wcq_ref[...], preferred_element_type=F32)
    probs = []
    for h in range(X_HEADS):
        hs = slice(h * X_HEAD_DIM, (h + 1) * X_HEAD_DIM)
        probs.append(_softmax_rows(_bdot_nt(hq[:, hs], mk_ref[0, :, hs]) * scale))

    x = x_ref[0]
    z_ref[...] = jnp.dot(_rmsnorm(x, gmix_ref[...]).astype(BF16), win_ref[...],
                         preferred_element_type=F32)

    for h in range(X_HEADS):
        hs = slice(h * X_HEAD_DIM, (h + 1) * X_HEAD_DIM)
        ca_s[:, hs] = _bdot(probs[h], mv_ref[0, :, hs]).astype(BF16)

    lb_all = _lower_bound(lb_ref)
    dmax = jnp.zeros((1, HG_WIDTH), F32)
    for c in range(nch):
        rows = slice(c * CHUNK, (c + 1) * CHUNK)
        q, k, bcum = _gates(z_ref[rows, 0:HG_WIDTH], z_ref[rows, HG_WIDTH:2 * HG_WIDTH], lb_all, CHUNK)
        q_s[rows, :] = q
        k_s[rows, :] = k
        b_s[rows, :] = bcum
        dmax = jnp.maximum(dmax, _block_decay(bcum, False))
    fast_ok = jnp.max(dmax) < FAST_MAX_DECAY

    x2 = h1 + jnp.dot(ca_s[...], wco_ref[...], preferred_element_type=F32)
    x2_s[...] = x2
    hn_s[...] = _rmsnorm(x2, gm_ref[...]).astype(BF16)

    def fill_scores(j, fast):
        c, h = divmod(j, HG_HEADS)
        rows = slice(c * CHUNK, (c + 1) * CHUNK)
        cs = slice(h * HG_DK, (h + 1) * HG_DK)
        q, k, bcum = q_s[rows, cs], k_s[rows, cs], b_s[rows, cs]
        a = _scores_fast(q, k, bcum, False) if fast else _scores_exact(q, k, bcum, False, mask)
        a_s[j] = jnp.where(mask, a, 0.0).astype(BF16)

    def head_chunk(j):
        c, h = divmod(j, HG_HEADS)
        rows = slice(c * CHUNK, (c + 1) * CHUNK)
        cs = slice(h * HG_DK, (h + 1) * HG_DK)
        q, k, bcum = q_s[rows, cs], k_s[rows, cs], b_s[rows, cs]
        v = z_ref[rows, 2 * HG_WIDTH + h * HG_DK:2 * HG_WIDTH + (h + 1) * HG_DK]
        zg = z_ref[rows, 3 * HG_WIDTH + h * HG_DK:3 * HG_WIDTH + (h + 1) * HG_DK]
        eb = jnp.exp(bcum)
        kd = k * jnp.exp(bcum[CHUNK - 1:CHUNK] - bcum)
        st = st_ref[h]
        o = jnp.dot(a_s[j], v.astype(BF16), preferred_element_type=F32)
        o = o + _bdot_nt(q * eb, st)
        st_ref[h] = st * eb[CHUNK - 1:CHUNK] + _bdot(v.T, kd)
        ocat_ref[rows, cs] = _hgrn_out(o, zg, hgn_ref[:, cs])
        if h == HG_HEADS - 1:
            _gmlp_chunk(z_ref, rows, lng_ref, lnb_ref, wmix_ref, bcol_ref, mask, ocat_ref, None)

    def region2(fast):
        nsl = D_FF // D_MODEL
        nj = nch * HG_HEADS
        hn = hn_s[...]
        acc = x2_s[...]
        for p in range(nsl):
            fs = slice(p * D_MODEL, (p + 1) * D_MODEL)
            hm = jnp.maximum(jnp.dot(hn, wup_ref[:, fs], preferred_element_type=F32), 0.0)
            hm_s[:, fs] = (hm * hm).astype(BF16)
            for j in range(p * nj // nsl, (p + 1) * nj // nsl):
                fill_scores(j, fast)
            if p == 0:
                probs = _sample_pair_probs(hqs_ref[...], ck_ref)
            if p == nsl - 1:
                _sample_pair_context(probs, cv_ref, cas_ref)
        for p in range(nsl):
            fs = slice(p * D_MODEL, (p + 1) * D_MODEL)
            acc = acc + jnp.dot(hm_s[:, fs], wdn_ref[fs, :], preferred_element_type=F32)
            for j in range(p * nj // nsl, (p + 1) * nj // nsl):
                head_chunk(j)
        y_ref[0] = _rmsnorm(acc, gf_ref[...])
        h1_s[...] = x_ref[0] + jnp.dot(ocat_ref[...], wout_ref[...], preferred_element_type=F32)

    @pl.when(fast_ok)
    def _():
        region2(True)

    @pl.when(jnp.logical_not(fast_ok))
    def _():
        region2(False)

    @pl.when(i == nt - 1)
    def _():
        for h in range(HG_HEADS):
            sout_ref[0, h] = st_ref[h].T


def _prompt_layer(x, mk, mv, hqs, ck, cv, lb_param, g_mix, w_in, hg_norm_g, ln_v_g, ln_v_b, wmix, bcol,
                  w_out, g_cross, w_cq, w_co, g_mlp, w_up, w_down, g_final, tm):
    B, L, _ = x.shape
    nt = L // tm
    T = B * nt
    npair = ck.shape[0] // 2
    assert npair <= T + 1

    def cur(t):
        return jnp.minimum(t, T - 1)

    def prev(t):
        return jnp.maximum(t - 1, 0)

    def pair(t):
        return jnp.minimum(t, npair - 1)

    consts = (lb_param, g_mix, w_in, hg_norm_g, ln_v_g, ln_v_b, wmix, bcol, w_out,
              g_cross, w_cq, w_co, g_mlp, w_up, w_down, g_final)
    in_specs = [
        pl.BlockSpec((1, tm, D_MODEL), lambda t: (cur(t) // nt, cur(t) % nt, 0)),
        pl.BlockSpec((1, N_MEM, D_MODEL), lambda t: (prev(t) // nt, 0, 0)),
        pl.BlockSpec((1, N_MEM, D_MODEL), lambda t: (prev(t) // nt, 0, 0)),
        pl.BlockSpec((8, D_MODEL), lambda t: (pair(t), 0)),
        pl.BlockSpec((2, N_MEM, X_HEADS, X_HEAD_DIM), lambda t: (pair(t), 0, 0, 0)),
        pl.BlockSpec((2, N_MEM, X_HEADS, X_HEAD_DIM), lambda t: (pair(t), 0, 0, 0)),
    ] + [_const_spec(a.shape) for a in consts]
    out_specs = [
        pl.BlockSpec((1, tm, D_MODEL), lambda t: (prev(t) // nt, prev(t) % nt, 0)),
        pl.BlockSpec((1, HG_HEADS, HG_DK, HG_DK), lambda t: (cur(t) // nt, 0, 0, 0)),
        pl.BlockSpec((8, D_MODEL), lambda t: (pair(t), 0)),
    ]
    scratch = [
        pltpu.VMEM((tm, IN_WIDTH), F32),
        pltpu.VMEM((tm, D_MODEL), BF16),
        pltpu.VMEM((tm, HG_WIDTH), F32), pltpu.VMEM((tm, HG_WIDTH), F32),
        pltpu.VMEM((tm, HG_WIDTH), F32),
        pltpu.VMEM((tm // CHUNK * HG_HEADS, CHUNK, CHUNK), BF16),
        pltpu.VMEM((HG_HEADS, HG_DK, HG_DK), F32),
        pltpu.VMEM((tm, D_MODEL), F32),
        pltpu.VMEM((tm, D_MODEL), F32),
        pltpu.VMEM((tm, D_MODEL), BF16),
        pltpu.VMEM((tm, D_MODEL), BF16),
        pltpu.VMEM((tm, D_FF), BF16),
    ]
    return pl.pallas_call(
        functools.partial(_prompt_kernel, tm=tm, nt=nt),
        grid=(T + 1,),
        in_specs=in_specs,
        out_specs=out_specs,
        out_shape=[jax.ShapeDtypeStruct((B, L, D_MODEL), F32),
                   jax.ShapeDtypeStruct((B, HG_HEADS, HG_DK, HG_DK), F32),
                   jax.ShapeDtypeStruct(hqs.shape, F32)],
        scratch_shapes=scratch,
        compiler_params=pltpu.CompilerParams(dimension_semantics=("arbitrary",),
                                             vmem_limit_bytes=VMEM_LIMIT_BYTES),
        name="prompt_layer",
    )(x, mk, mv, hqs, ck, cv, *consts)


def _mix_sample_kernel(x_ref, s0_ref, lb_ref, gmix_ref, win_ref, hgn_ref, lng_ref, lnb_ref, wmix_ref,
                       bcol_ref, wout_ref, gc_ref, wcq_ref, h_ref, sout_ref, vn_ref, hq_ref,
                       z_ref, ocat_ref, q_s, k_s, b_s, *, tm):
    x = x_ref[0]
    z_ref[...] = jnp.dot(_rmsnorm(x, gmix_ref[...]).astype(BF16), win_ref[...],
                         preferred_element_type=F32)
    lb_all = _lower_bound(lb_ref)
    mask = _chunk_masks(True)
    eye = (lax.broadcasted_iota(jnp.int32, (CHUNK, CHUNK), 0)
           == lax.broadcasted_iota(jnp.int32, (CHUNK, CHUNK), 1))
    r16 = lax.broadcasted_iota(jnp.int32, (SUB, HG_DK), 0)

    def gate_body(c, dmax):
        rows = pl.ds(pl.multiple_of(c * CHUNK, CHUNK), CHUNK)
        q, k, bcum = _gates(z_ref[rows, 0:HG_WIDTH], z_ref[rows, HG_WIDTH:2 * HG_WIDTH], lb_all, 4)
        q_s[rows, :] = q
        k_s[rows, :] = k
        b_s[rows, :] = bcum
        return jnp.maximum(dmax, _block_decay(bcum, True))

    dmax = lax.fori_loop(0, tm // CHUNK, gate_body, jnp.zeros((1, HG_WIDTH), F32))
    fast_ok = jnp.max(dmax) < FAST_MAX_DECAY

    def chunk_body(c, carry, *, fast):
        rows = pl.ds(pl.multiple_of(c * CHUNK, CHUNK), CHUNK)
        for h in range(HG_HEADS):
            cs = slice(h * HG_DK, (h + 1) * HG_DK)
            q, k, bcum = q_s[rows, cs], k_s[rows, cs], b_s[rows, cs]
            v = z_ref[rows, 2 * HG_WIDTH + h * HG_DK:2 * HG_WIDTH + (h + 1) * HG_DK]
            zg = z_ref[rows, 3 * HG_WIDTH + h * HG_DK:3 * HG_WIDTH + (h + 1) * HG_DK]
            a = _scores_fast(q, k, bcum, True) if fast else _scores_exact(q, k, bcum, True, mask)
            o = _bdot(jnp.where(mask, a, 0.0), v)
            eb = jnp.exp(bcum)
            qd = q * eb
            o_parts = []
            for gb in range(CHUNK // SUB):
                blk = slice(gb * SUB, (gb + 1) * SUB)
                qd_b, v_b, eb_b, k_b, b_b = qd[blk], v[blk], eb[blk], k[blk], bcum[blk]
                inter = jnp.zeros((SUB, HG_DK), F32)
                for j in range(4):
                    req = c * (CHUNK // 4) + gb * 4 + j
                    s0 = s0_ref[req, h]
                    last = 4 * j + 3
                    inter = jnp.where(r16 // 4 == j, _bdot(qd_b, s0), inter)
                    kd = jnp.where(r16 // 4 == j, k_b * jnp.exp(b_b[last:last + 1] - b_b), 0.0)
                    upd = lax.dot_general(kd.astype(BF16), v_b.astype(BF16),
                                          (((0,), (0,)), ((), ())), preferred_element_type=F32)
                    dcol = jnp.sum(jnp.where(eye, eb_b[last:last + 1], 0.0), axis=-1, keepdims=True)
                    sout_ref[req, h] = dcol * s0 + upd
                o_parts.append(inter)
            o = o + jnp.concatenate(o_parts, axis=0)
            ocat_ref[rows, cs] = _hgrn_out(o, zg, hgn_ref[:, cs])
        _gmlp_chunk(z_ref, rows, lng_ref, lnb_ref, wmix_ref, bcol_ref, mask, ocat_ref, vn_ref)
        return carry

    @pl.when(fast_ok)
    def _():
        lax.fori_loop(0, tm // CHUNK, functools.partial(chunk_body, fast=True), 0)

    @pl.when(jnp.logical_not(fast_ok))
    def _():
        lax.fori_loop(0, tm // CHUNK, functools.partial(chunk_body, fast=False), 0)

    h = x + jnp.dot(ocat_ref[...], wout_ref[...], preferred_element_type=F32)
    h_ref[0] = h
    hq_ref[...] = jnp.dot(_rmsnorm(h, gc_ref[...]).astype(BF16), wcq_ref[...], preferred_element_type=F32)


def _mix_sample(x, s0, lb_param, g_mix, w_in, hg_norm_g, ln_v_g, ln_v_b, wmix, bcol, w_out, g_cross,
                w_cq, tm):
    nt, _, _ = x.shape
    nreq = tm // 4
    consts = (lb_param, g_mix, w_in, hg_norm_g, ln_v_g, ln_v_b, wmix, bcol, w_out, g_cross, w_cq)
    in_specs = [
        pl.BlockSpec((1, tm, D_MODEL), lambda i: (i, 0, 0)),
        pl.BlockSpec((nreq, HG_HEADS, HG_DK, HG_DK), lambda i: (i, 0, 0, 0)),
    ] + [_const_spec(a.shape) for a in consts]
    out_specs = [
        pl.BlockSpec((1, tm, D_MODEL), lambda i: (i, 0, 0)),
        pl.BlockSpec((nreq, HG_HEADS, HG_DK, HG_DK), lambda i: (i, 0, 0, 0)),
        pl.BlockSpec((tm, CM_WIDTH), lambda i: (i, 0)),
        pl.BlockSpec((tm, D_MODEL), lambda i: (i, 0)),
    ]
    return pl.pallas_call(
        functools.partial(_mix_sample_kernel, tm=tm),
        grid=(nt,),
        in_specs=in_specs,
        out_specs=out_specs,
        out_shape=[jax.ShapeDtypeStruct((nt, tm, D_MODEL), F32),
                   jax.ShapeDtypeStruct(s0.shape, F32),
                   jax.ShapeDtypeStruct((nt * tm, CM_WIDTH), F32),
                   jax.ShapeDtypeStruct((nt * tm, D_MODEL), F32)],
        scratch_shapes=[pltpu.VMEM((tm, IN_WIDTH), F32), pltpu.VMEM((tm, D_MODEL), BF16)]
        + [pltpu.VMEM((tm, HG_WIDTH), F32)] * 3,
        compiler_params=pltpu.CompilerParams(dimension_semantics=("arbitrary",),
                                             vmem_limit_bytes=VMEM_LIMIT_BYTES),
        name="mix_sample",
    )(x, s0, *consts)


def _memkv_kernel(m_ref, g_ref, wk_ref, wv_ref, k_ref, v_ref):
    mn = _rmsnorm(m_ref[...], g_ref[...]).astype(BF16)
    k_ref[...] = jnp.dot(mn, wk_ref[...], preferred_element_type=F32)
    v_ref[...] = jnp.dot(mn, wv_ref[...], preferred_element_type=F32)


def _memkv(mem, g_mem, w_ck, w_cv, tm):
    n = mem.shape[0]
    return pl.pallas_call(
        _memkv_kernel,
        grid=(n // tm,),
        in_specs=[pl.BlockSpec((tm, D_MODEL), lambda i: (i, 0)), _const_spec(g_mem.shape),
                  _const_spec(w_ck.shape), _const_spec(w_cv.shape)],
        out_specs=[pl.BlockSpec((tm, D_MODEL), lambda i: (i, 0)),
                   pl.BlockSpec((tm, D_MODEL), lambda i: (i, 0))],
        out_shape=[jax.ShapeDtypeStruct((n, D_MODEL), F32), jax.ShapeDtypeStruct((n, D_MODEL), F32)],
        compiler_params=pltpu.CompilerParams(dimension_semantics=("arbitrary",),
                                             vmem_limit_bytes=VMEM_LIMIT_BYTES),
        name="memkv",
    )(mem, g_mem, w_ck, w_cv)


def _post_sample_kernel(x_ref, ca_ref, wco_ref, gm_ref, wup_ref, wdn_ref, gf_ref, y_ref, hm_s):
    x = x_ref[0] + jnp.dot(ca_ref[...].astype(BF16), wco_ref[...], preferred_element_type=F32)
    x = x + _mlp(_rmsnorm(x, gm_ref[...]).astype(BF16), wup_ref, wdn_ref, hm_s)
    y_ref[0] = _rmsnorm(x, gf_ref[...])


def _post_sample(x, ca, w_co, g_mlp, w_up, w_down, g_final):
    nt, tm, _ = x.shape
    in_specs = [
        pl.BlockSpec((1, tm, D_MODEL), lambda i: (i, 0, 0)),
        pl.BlockSpec((tm, D_MODEL), lambda i: (i, 0)),
        _const_spec(w_co.shape), _const_spec(g_mlp.shape), _const_spec(w_up.shape),
        _const_spec(w_down.shape), _const_spec(g_final.shape),
    ]
    return pl.pallas_call(
        _post_sample_kernel,
        grid=(nt,),
        in_specs=in_specs,
        out_specs=pl.BlockSpec((1, tm, D_MODEL), lambda i: (i, 0, 0)),
        out_shape=jax.ShapeDtypeStruct((nt, tm, D_MODEL), F32),
        scratch_shapes=[pltpu.VMEM((tm, D_FF), BF16)],
        compiler_params=pltpu.CompilerParams(dimension_semantics=("arbitrary",),
                                             vmem_limit_bytes=VMEM_LIMIT_BYTES),
        name="post_sample",
    )(x, ca, w_co, g_mlp, w_up, w_down, g_final)


def kernel(x_prompt, x_sample, mem_prompt, state_hgrn, cache_mem_k, cache_mem_v, lb_param, g_mix,
           w_in, hg_norm_g, ln_v_g, ln_v_b, w_s, b_s, w_out, g_cross, g_mem, w_cq, w_ck, w_cv, w_co,
           g_mlp, w_up, w_down, g_final):
    B, L, _ = x_prompt.shape
    DB, DL, _ = x_sample.shape
    assert DL == 4 and g_mix.shape[0] == 1

    row = lambda a: a.reshape(1, -1)
    win_b, wout_b = w_in[0].astype(BF16), w_out[0].astype(BF16)
    wcq_b, wco_b = w_cq[0].astype(BF16), w_co[0].astype(BF16)
    wck_b, wcv_b = w_ck[0].astype(BF16), w_cv[0].astype(BF16)
    wup_b, wdn_b = w_up[0].astype(BF16), w_down[0].astype(BF16)
    gmix, hgn, lng, lnb = row(g_mix[0]), row(hg_norm_g[0]), row(ln_v_g[0]), row(ln_v_b[0])
    gcr, gmem, gmlp, gfin = row(g_cross[0]), row(g_mem[0]), row(g_mlp[0]), row(g_final)
    wmix_p = w_s[0]
    bcol_p = b_s[0].T
    wmix_s = jnp.tile(w_s[0][:, :DL, :DL], (1, CHUNK // DL, CHUNK // DL))
    bcol_s = jnp.tile(b_s[0][:, :DL].T, (CHUNK // DL, 1))

    tm_s = 128
    xs = x_sample.reshape(DB * DL // tm_s, tm_s, D_MODEL)
    h_s, s_s, vn_s, hq_s = _mix_sample(xs, state_hgrn[0], lb_param, gmix, win_b, hgn, lng, lnb, wmix_s,
                                       bcol_s, wout_b, gcr, wcq_b, tm=tm_s)

    mk, mv = _memkv(mem_prompt.reshape(B * N_MEM, D_MODEL), gmem, wck_b, wcv_b, tm=512)
    y_p, s_p, ca_s = _prompt_layer(x_prompt, mk.reshape(B, N_MEM, D_MODEL), mv.reshape(B, N_MEM, D_MODEL),
                                   hq_s, cache_mem_k[0], cache_mem_v[0],
                                   lb_param, gmix, win_b, hgn, lng, lnb, wmix_p, bcol_p, wout_b,
                                   gcr, wcq_b, wco_b, gmlp, wup_b, wdn_b, gfin, tm=256)

    y_s = _post_sample(h_s.reshape(1, DB * DL, D_MODEL), ca_s, wco_b, gmlp, wup_b, wdn_b, gfin)

    return (y_p, y_s.reshape(DB, DL, D_MODEL), s_p[None], s_s[None],
            mk.reshape(1, B, N_MEM, X_HEADS, X_HEAD_DIM), mv.reshape(1, B, N_MEM, X_HEADS, X_HEAD_DIM),
            vn_s.reshape(1, DB, DL, CM_WIDTH))
```

```python
import functools

import jax
import jax.numpy as jnp
from jax import lax
from jax.experimental import pallas as pl
from jax.experimental.pallas import tpu as pltpu

F32 = jnp.float32
BF16 = jnp.bfloat16

D_MODEL = 1024
HG_WIDTH = 512
HG_HEADS = 4
HG_DK = 128
CM_WIDTH = 512
CM_GROUPS = 4
CM_GROUP_DIM = 128
IN_WIDTH = 4 * HG_WIDTH + 2 * CM_WIDTH
N_MEM = 256
X_HEADS = 4
X_HEAD_DIM = 256
D_FF = 4096
EPS = 1e-6

CHUNK = 128
SUB = 16
VMEM_LIMIT_BYTES = 62 * 1024 * 1024
LOG2E = 1.4426950408889634
FAST_BLOCK = 32
FAST_MAX_DECAY = 80.0


def _bdot(a, b):
    return jnp.dot(a.astype(BF16), b.astype(BF16), preferred_element_type=F32)


def _bdot_nt(a, b):
    return lax.dot_general(a.astype(BF16), b.astype(BF16), (((1,), (1,)), ((), ())),
                           preferred_element_type=F32)


def _rmsnorm(x, g):
    ms = jnp.mean(x * x, axis=-1, keepdims=True)
    return x * lax.rsqrt(ms + EPS) * g


def _sigmoid(x):
    return 0.5 * jnp.tanh(0.5 * x) + 0.5


def _gelu(x):
    return 0.5 * x * (1.0 + jnp.tanh(0.7978845608028654 * (x + 0.044715 * (x * x * x))))


def _softmax_rows(s):
    m = jnp.max(s, axis=-1, keepdims=True)
    e = jnp.exp(s - m)
    return e / jnp.sum(e, axis=-1, keepdims=True)


def _seg_cumsum(x, seg):
    n = x.shape[0]
    pos = lax.broadcasted_iota(jnp.int32, x.shape, 0) % seg
    s = 1
    while s < min(seg, 8):
        x = x + jnp.where(pos >= s, pltpu.roll(x, s, 0), 0.0)
        s *= 2
    while s < seg:
        parts = []
        for r0 in range(0, n, seg):
            parts.append(x[r0:r0 + s])
            parts.append(x[r0 + s:r0 + seg] + x[r0:r0 + seg - s])
        x = jnp.concatenate(parts, axis=0)
        s *= 2
    return x


def _lower_bound(lb_ref):
    lbp = lb_ref[...]
    lbe = jnp.exp(lbp - jnp.max(lbp, axis=0, keepdims=True))
    return lbe[0:1] / jnp.sum(lbe, axis=0, keepdims=True)


def _gates(zq, zf, lb, seg):
    th = jnp.tanh(0.5 * zf)
    logf = jnp.log(lb + (1.0 - lb) * (0.5 + 0.5 * th))
    return zq * _sigmoid(zq), (1.0 - lb) * (0.5 - 0.5 * th), _seg_cumsum(logf, seg)


def _block_decay(bcum, sample):
    if sample:
        return jnp.max(-bcum, axis=0, keepdims=True)
    d = -bcum[FAST_BLOCK - 1:FAST_BLOCK]
    for i in range(1, CHUNK // FAST_BLOCK):
        n0, n1 = i * FAST_BLOCK, (i + 1) * FAST_BLOCK
        d = jnp.maximum(d, bcum[n0 - 1:n0] - bcum[n1 - 1:n1])
    return d


def _chunk_masks(sample):
    row = lax.broadcasted_iota(jnp.int32, (CHUNK, CHUNK), 0)
    col = lax.broadcasted_iota(jnp.int32, (CHUNK, CHUNK), 1)
    if sample:
        return (row // 4 == col // 4) & (row >= col)
    return row >= col


def _scores_fast(q, k, bcum, sample):
    if sample:
        return _bdot_nt(q * jnp.exp(bcum), k * jnp.exp(-bcum))
    a_rows = []
    for i in range(CHUNK // FAST_BLOCK):
        n0, n1 = i * FAST_BLOCK, (i + 1) * FAST_BLOCK
        ref_b = bcum[n0 - 1:n0] if i else jnp.zeros((1, HG_DK), F32)
        qt = q[n0:n1] * jnp.exp(bcum[n0:n1] - ref_b)
        kt = k[:n1] * jnp.exp(ref_b - bcum[:n1])
        if n1 < CHUNK:
            kt = jnp.concatenate([kt, jnp.zeros((CHUNK - n1, HG_DK), F32)], axis=0)
        a_rows.append(_bdot_nt(qt, kt))
    return jnp.concatenate(a_rows, axis=0)


def _scores_exact(q, k, bcum, sample, mask):
    col8 = lax.broadcasted_iota(jnp.int32, (8, CHUNK), 1)
    b2 = bcum * LOG2E
    cexp = b2 - jnp.log2(k)
    a_rows = []
    for gb in range(CHUNK // SUB):
        lo = slice(gb * SUB, gb * SUB + 8)
        hi = slice(gb * SUB + 8, (gb + 1) * SUB)
        a_lo = jnp.zeros((8, CHUNK), F32)
        a_hi = jnp.zeros((8, CHUNK), F32)
        for s in range(SUB):
            sg = gb * SUB + s
            c_s = cexp[sg:sg + 1]
            if s < 8:
                p = q[lo] * jnp.exp2(b2[lo] - c_s)
                a_lo = jnp.where(col8 == sg, jnp.sum(p, axis=-1, keepdims=True), a_lo)
            if (not sample) or s >= 8:
                p = q[hi] * jnp.exp2(b2[hi] - c_s)
                a_hi = jnp.where(col8 == sg, jnp.sum(p, axis=-1, keepdims=True), a_hi)
        a_rows.append(a_lo)
        a_rows.append(a_hi)
    a = jnp.concatenate(a_rows, axis=0)
    if sample:
        return a
    o_rows = [jnp.zeros((SUB, CHUNK), F32)]
    for i in range(1, CHUNK // SUB):
        n = i * SUB
        ref_b = bcum[n - 1:n]
        qt = q[n:n + SUB] * jnp.exp(bcum[n:n + SUB] - ref_b)
        kt = k[:n] * jnp.exp(ref_b - bcum[:n])
        kt = jnp.concatenate([kt, jnp.zeros((CHUNK - n, HG_DK), F32)], axis=0)
        o_rows.append(_bdot_nt(qt, kt))
    return jnp.where(mask, a, 0.0) + jnp.concatenate(o_rows, axis=0)


def _hgrn_out(o, zg, g):
    o = o * lax.rsqrt(jnp.mean(o * o, axis=-1, keepdims=True) + EPS) * g
    return (o * _sigmoid(zg)).astype(BF16)


def _gmlp_chunk(z_ref, c0, rows, lng_ref, lnb_ref, wmix_ref, bcol_ref, mask, ocat_ref, vn_ref):
    u = _gelu(z_ref[rows, c0:c0 + CM_WIDTH])
    gv = _gelu(z_ref[rows, c0 + CM_WIDTH:c0 + 2 * CM_WIDTH])
    mu = jnp.mean(gv, axis=-1, keepdims=True)
    dv = gv - mu
    var = jnp.mean(dv * dv, axis=-1, keepdims=True)
    vn = dv * lax.rsqrt(var + EPS) * lng_ref[...] + lnb_ref[...]
    if vn_ref is not None:
        vn_ref[rows, :] = vn
    for g in range(CM_GROUPS):
        gs = slice(g * CM_GROUP_DIM, (g + 1) * CM_GROUP_DIM)
        wm = jnp.where(mask, wmix_ref[g], 0.0)
        mixed = _bdot(wm, vn[:, gs]) + bcol_ref[:, g:g + 1]
        ocat_ref[rows, HG_WIDTH + g * CM_GROUP_DIM:HG_WIDTH + (g + 1) * CM_GROUP_DIM] = (
            u[:, gs] * mixed).astype(BF16)


def _sample_pair_probs(hq8, ck_ref):
    nrow = 8 * X_HEADS
    rh = lax.broadcasted_iota(jnp.int32, (nrow, N_MEM * X_HEADS), 0) // 8
    ch = lax.broadcasted_iota(jnp.int32, (nrow, N_MEM * X_HEADS), 1) % X_HEADS
    q = jnp.concatenate([hq8[:, h * X_HEAD_DIM:(h + 1) * X_HEAD_DIM] for h in range(X_HEADS)],
                        axis=0).astype(BF16)
    probs = []
    for r in range(2):
        k2 = ck_ref[r].reshape(N_MEM * X_HEADS, X_HEAD_DIM)
        s = jnp.where(rh == ch, _bdot_nt(q, k2) * (X_HEAD_DIM ** -0.5), -jnp.inf)
        probs.append(_softmax_rows(s))
    return probs


def _sample_pair_context(probs, cv_ref, ca_ref):
    first = (lax.broadcasted_iota(jnp.int32, (8 * X_HEADS, X_HEAD_DIM), 0) % 8) < 4
    outs = [_bdot(probs[r], cv_ref[r].reshape(N_MEM * X_HEADS, X_HEAD_DIM)) for r in range(2)]
    o = jnp.where(first, outs[0], outs[1])
    for h in range(X_HEADS):
        ca_ref[:, h * X_HEAD_DIM:(h + 1) * X_HEAD_DIM] = o[8 * h:8 * h + 8]


def _mlp(hn_bf16, wup_ref, wdn_ref, hm_s):
    for c in range(D_FF // D_MODEL):
        fs = slice(c * D_MODEL, (c + 1) * D_MODEL)
        hm = jnp.maximum(jnp.dot(hn_bf16, wup_ref[:, fs], preferred_element_type=F32), 0.0)
        hm_s[:, fs] = (hm * hm).astype(BF16)
    return jnp.dot(hm_s[...], wdn_ref[...], preferred_element_type=F32)


def _const_spec(shape):
    nd = len(shape)
    return pl.BlockSpec(shape, lambda *_: (0,) * nd, pipeline_mode=pl.Buffered(1))


def _prompt_kernel(x_ref, mk_ref, mv_ref, hqs_ref, ck_ref, cv_ref, lb_ref, gmix_ref, win_ref, hgn_ref,
                   lng_ref, lnb_ref, wmix_ref, bcol_ref, wout_ref, gc_ref, wcq_ref, wco_ref, gm_ref,
                   wup_ref, wdn_ref, gf_ref, y_ref, sout_ref, cas_ref,
                   zg_ref, z_ref, ocat_ref, q_s, k_s, b_s, a_s, st0, st1, st2, st3, h1_s, h1n_s, x2_s,
                   hn_s, sc_s, ca_s, hm_s, *, tm, nt):
    t = pl.program_id(0)
    i = lax.rem(t, nt)
    nch = tm // CHUNK
    mask = _chunk_masks(False)
    st_refs = (st0, st1, st2, st3)

    @pl.when(t == 0)
    def _():
        h1_s[...] = jnp.zeros_like(h1_s)
        h1n_s[...] = jnp.zeros_like(h1n_s)

    @pl.when(i == 0)
    def _():
        for st in st_refs:
            st[...] = jnp.zeros_like(st)


    scale = X_HEAD_DIM ** -0.5
    hq = jnp.dot(h1n_s[...], wcq_ref[...], preferred_element_type=F32)

    x = x_ref[0]
    xn = _rmsnorm(x, gmix_ref[...]).astype(BF16)
    ngate = 2 * HG_WIDTH

    def in_proj(dst, c0, c1):
        off = ngate if dst is z_ref else 0
        dst[:, c0 - off:c1 - off] = jnp.dot(xn, win_ref[:, c0:c1], preferred_element_type=F32)

    in_proj(zg_ref, 0, ngate)
    for h in range(X_HEADS):
        hs = slice(h * X_HEAD_DIM, (h + 1) * X_HEAD_DIM)
        sc_s[h] = _bdot_nt(hq[:, hs], mk_ref[0, :, hs]) * scale
    in_proj(z_ref, ngate, ngate + 2 * HG_WIDTH)
    for h in range(X_HEADS):
        hs = slice(h * X_HEAD_DIM, (h + 1) * X_HEAD_DIM)
        ca_s[:, hs] = _bdot(_softmax_rows(sc_s[h]), mv_ref[0, :, hs]).astype(BF16)
    in_proj(z_ref, ngate + 2 * HG_WIDTH, ngate + 2 * HG_WIDTH + CM_WIDTH)

    lb_all = _lower_bound(lb_ref)
    dmax = jnp.zeros((1, HG_WIDTH), F32)
    for c in range(nch):
        rows = slice(c * CHUNK, (c + 1) * CHUNK)
        q, k, bcum = _gates(zg_ref[rows, 0:HG_WIDTH], zg_ref[rows, HG_WIDTH:2 * HG_WIDTH], lb_all, CHUNK)
        q_s[rows, :] = q
        k_s[rows, :] = k
        b_s[rows, :] = bcum
        dmax = jnp.maximum(dmax, _block_decay(bcum, False))
    fast_ok = jnp.max(dmax) < FAST_MAX_DECAY

    x2 = h1_s[...] + jnp.dot(ca_s[...], wco_ref[...], preferred_element_type=F32)
    in_proj(z_ref, ngate + 2 * HG_WIDTH + CM_WIDTH, IN_WIDTH)
    x2_s[...] = x2
    hn_s[...] = _rmsnorm(x2, gm_ref[...]).astype(BF16)

    def fill_scores(j, fast):
        c, h = divmod(j, HG_HEADS)
        rows = slice(c * CHUNK, (c + 1) * CHUNK)
        cs = slice(h * HG_DK, (h + 1) * HG_DK)
        q, k, bcum = q_s[rows, cs], k_s[rows, cs], b_s[rows, cs]
        a = _scores_fast(q, k, bcum, False) if fast else _scores_exact(q, k, bcum, False, mask)
        a_s[j] = jnp.where(mask, a, 0.0).astype(BF16)

    def head_chunk(j):
        c, h = divmod(j, HG_HEADS)
        rows = slice(c * CHUNK, (c + 1) * CHUNK)
        cs = slice(h * HG_DK, (h + 1) * HG_DK)
        q, k, bcum = q_s[rows, cs], k_s[rows, cs], b_s[rows, cs]
        v = z_ref[rows, h * HG_DK:(h + 1) * HG_DK]
        zg = z_ref[rows, HG_WIDTH + h * HG_DK:HG_WIDTH + (h + 1) * HG_DK]
        eb = jnp.exp(bcum)
        kd = k * jnp.exp(bcum[CHUNK - 1:CHUNK] - bcum)
        st = st_refs[h][...]
        o = jnp.dot(a_s[j], v.astype(BF16), preferred_element_type=F32)
        o = o + _bdot_nt(q * eb, st)
        st_refs[h][...] = st * eb[CHUNK - 1:CHUNK] + _bdot(v.T, kd)
        ocat_ref[rows, cs] = _hgrn_out(o, zg, hgn_ref[:, cs])
        if h == HG_HEADS - 1:
            _gmlp_chunk(z_ref, 2 * HG_WIDTH, rows, lng_ref, lnb_ref, wmix_ref, bcol_ref, mask, ocat_ref,
                        None)

    def region2(fast):
        assert nch * HG_HEADS == 8 and D_FF // D_MODEL == 4
        hn = hn_s[...]

        def up(p):
            fs = slice(p * D_MODEL, (p + 1) * D_MODEL)
            hm = jnp.maximum(jnp.dot(hn, wup_ref[:, fs], preferred_element_type=F32), 0.0)
            hm_s[p % 2] = (hm * hm).astype(BF16)

        def down(p, acc):
            fs = slice(p * D_MODEL, (p + 1) * D_MODEL)
            return acc + jnp.dot(hm_s[p % 2], wdn_ref[fs, :], preferred_element_type=F32)

        def out_proj():
            h1 = x_ref[0] + jnp.dot(ocat_ref[...], wout_ref[...], preferred_element_type=F32)
            h1_s[...] = h1
            h1n_s[...] = _rmsnorm(h1, gc_ref[...]).astype(BF16)

        acc = x2_s[...]
        up(0)
        fill_scores(0, fast), fill_scores(1, fast)
        probs = _sample_pair_probs(hqs_ref[...], ck_ref)
        up(1)
        fill_scores(2, fast), fill_scores(3, fast)
        acc = down(0, acc)
        fill_scores(4, fast), fill_scores(5, fast)
        acc = down(1, acc)
        fill_scores(6, fast), fill_scores(7, fast)
        _sample_pair_context(probs, cv_ref, cas_ref)
        up(2)
        head_chunk(0), head_chunk(1), head_chunk(2)
        up(3)
        head_chunk(3), head_chunk(4), head_chunk(5)
        acc = down(2, acc)
        head_chunk(6), head_chunk(7)
        out_proj()
        acc = down(3, acc)
        y_ref[0] = _rmsnorm(acc, gf_ref[...])

    @pl.when(fast_ok)
    def _():
        region2(True)

    @pl.when(jnp.logical_not(fast_ok))
    def _():
        region2(False)

    @pl.when(i == nt - 1)
    def _():
        for h in range(HG_HEADS):
            sout_ref[0, h] = st_refs[h][...].T


def _prompt_layer(x, mk, mv, hqs, ck, cv, lb_param, g_mix, w_in, hg_norm_g, ln_v_g, ln_v_b, wmix, bcol,
                  w_out, g_cross, w_cq, w_co, g_mlp, w_up, w_down, g_final, tm):
    B, L, _ = x.shape
    nt = L // tm
    T = B * nt
    npair = ck.shape[0] // 2
    assert npair <= T + 1

    def cur(t):
        return jnp.minimum(t, T - 1)

    def prev(t):
        return jnp.maximum(t - 1, 0)

    def pair(t):
        return jnp.minimum(t, npair - 1)

    consts = (lb_param, g_mix, w_in, hg_norm_g, ln_v_g, ln_v_b, wmix, bcol, w_out,
              g_cross, w_cq, w_co, g_mlp, w_up, w_down, g_final)
    in_specs = [
        pl.BlockSpec((1, tm, D_MODEL), lambda t: (cur(t) // nt, cur(t) % nt, 0)),
        pl.BlockSpec((1, N_MEM, D_MODEL), lambda t: (prev(t) // nt, 0, 0)),
        pl.BlockSpec((1, N_MEM, D_MODEL), lambda t: (prev(t) // nt, 0, 0)),
        pl.BlockSpec((8, D_MODEL), lambda t: (pair(t), 0)),
        pl.BlockSpec((2, N_MEM, X_HEADS, X_HEAD_DIM), lambda t: (pair(t), 0, 0, 0)),
        pl.BlockSpec((2, N_MEM, X_HEADS, X_HEAD_DIM), lambda t: (pair(t), 0, 0, 0)),
    ] + [_const_spec(a.shape) for a in consts]
    out_specs = [
        pl.BlockSpec((1, tm, D_MODEL), lambda t: (prev(t) // nt, prev(t) % nt, 0)),
        pl.BlockSpec((1, HG_HEADS, HG_DK, HG_DK), lambda t: (cur(t) // nt, 0, 0, 0)),
        pl.BlockSpec((8, D_MODEL), lambda t: (pair(t), 0)),
    ]
    scratch = [
        pltpu.VMEM((tm, 2 * HG_WIDTH), F32),
        pltpu.VMEM((tm, IN_WIDTH - 2 * HG_WIDTH), F32),
        pltpu.VMEM((tm, D_MODEL), BF16),
        pltpu.VMEM((tm, HG_WIDTH), F32), pltpu.VMEM((tm, HG_WIDTH), F32),
        pltpu.VMEM((tm, HG_WIDTH), F32),
        pltpu.VMEM((tm // CHUNK * HG_HEADS, CHUNK, CHUNK), BF16),
        pltpu.VMEM((HG_DK, HG_DK), F32), pltpu.VMEM((HG_DK, HG_DK), F32),
        pltpu.VMEM((HG_DK, HG_DK), F32), pltpu.VMEM((HG_DK, HG_DK), F32),
        pltpu.VMEM((tm, D_MODEL), F32),
        pltpu.VMEM((tm, D_MODEL), BF16),
        pltpu.VMEM((tm, D_MODEL), F32),
        pltpu.VMEM((tm, D_MODEL), BF16),
        pltpu.VMEM((X_HEADS, tm, N_MEM), F32),
        pltpu.VMEM((tm, D_MODEL), BF16),
        pltpu.VMEM((2, tm, D_MODEL), BF16),
    ]
    return pl.pallas_call(
        functools.partial(_prompt_kernel, tm=tm, nt=nt),
        grid=(T + 1,),
        in_specs=in_specs,
        out_specs=out_specs,
        out_shape=[jax.ShapeDtypeStruct((B, L, D_MODEL), F32),
                   jax.ShapeDtypeStruct((B, HG_HEADS, HG_DK, HG_DK), F32),
                   jax.ShapeDtypeStruct(hqs.shape, F32)],
        scratch_shapes=scratch,
        compiler_params=pltpu.CompilerParams(dimension_semantics=("arbitrary",),
                                             vmem_limit_bytes=VMEM_LIMIT_BYTES),
        name="prompt_layer",
    )(x, mk, mv, hqs, ck, cv, *consts)


def _mix_sample_kernel(x_ref, s0_ref, lb_ref, gmix_ref, win_ref, hgn_ref, lng_ref, lnb_ref, wmix_ref,
                       bcol_ref, wout_ref, gc_ref, wcq_ref, h_ref, sout_ref, vn_ref, hq_ref,
                       z_ref, ocat_ref, q_s, k_s, b_s, *, tm):
    x = x_ref[0]
    z_ref[...] = jnp.dot(_rmsnorm(x, gmix_ref[...]).astype(BF16), win_ref[...],
                         preferred_element_type=F32)
    lb_all = _lower_bound(lb_ref)
    mask = _chunk_masks(True)
    eye = (lax.broadcasted_iota(jnp.int32, (CHUNK, CHUNK), 0)
           == lax.broadcasted_iota(jnp.int32, (CHUNK, CHUNK), 1))
    r16 = lax.broadcasted_iota(jnp.int32, (SUB, HG_DK), 0)

    def gate_body(c, dmax):
        rows = pl.ds(pl.multiple_of(c * CHUNK, CHUNK), CHUNK)
        q, k, bcum = _gates(z_ref[rows, 0:HG_WIDTH], z_ref[rows, HG_WIDTH:2 * HG_WIDTH], lb_all, 4)
        q_s[rows, :] = q
        k_s[rows, :] = k
        b_s[rows, :] = bcum
        return jnp.maximum(dmax, _block_decay(bcum, True))

    dmax = lax.fori_loop(0, tm // CHUNK, gate_body, jnp.zeros((1, HG_WIDTH), F32))
    fast_ok = jnp.max(dmax) < FAST_MAX_DECAY

    def chunk_body(c, carry, *, fast):
        rows = pl.ds(pl.multiple_of(c * CHUNK, CHUNK), CHUNK)
        for h in range(HG_HEADS):
            cs = slice(h * HG_DK, (h + 1) * HG_DK)
            q, k, bcum = q_s[rows, cs], k_s[rows, cs], b_s[rows, cs]
            v = z_ref[rows, 2 * HG_WIDTH + h * HG_DK:2 * HG_WIDTH + (h + 1) * HG_DK]
            zg = z_ref[rows, 3 * HG_WIDTH + h * HG_DK:3 * HG_WIDTH + (h + 1) * HG_DK]
            a = _scores_fast(q, k, bcum, True) if fast else _scores_exact(q, k, bcum, True, mask)
            o = _bdot(jnp.where(mask, a, 0.0), v)
            eb = jnp.exp(bcum)
            qd = q * eb
            o_parts = []
            for gb in range(CHUNK // SUB):
                blk = slice(gb * SUB, (gb + 1) * SUB)
                qd_b, v_b, eb_b, k_b, b_b = qd[blk], v[blk], eb[blk], k[blk], bcum[blk]
                inter = jnp.zeros((SUB, HG_DK), F32)
                for j in range(4):
                    req = c * (CHUNK // 4) + gb * 4 + j
                    s0 = s0_ref[req, h]
                    last = 4 * j + 3
                    inter = jnp.where(r16 // 4 == j, _bdot(qd_b, s0), inter)
                    kd = jnp.where(r16 // 4 == j, k_b * jnp.exp(b_b[last:last + 1] - b_b), 0.0)
                    upd = lax.dot_general(kd.astype(BF16), v_b.astype(BF16),
                                          (((0,), (0,)), ((), ())), preferred_element_type=F32)
                    dcol = jnp.sum(jnp.where(eye, eb_b[last:last + 1], 0.0), axis=-1, keepdims=True)
                    sout_ref[req, h] = dcol * s0 + upd
                o_parts.append(inter)
            o = o + jnp.concatenate(o_parts, axis=0)
            ocat_ref[rows, cs] = _hgrn_out(o, zg, hgn_ref[:, cs])
        _gmlp_chunk(z_ref, 4 * HG_WIDTH, rows, lng_ref, lnb_ref, wmix_ref, bcol_ref, mask, ocat_ref,
                    vn_ref)
        return carry

    @pl.when(fast_ok)
    def _():
        lax.fori_loop(0, tm // CHUNK, functools.partial(chunk_body, fast=True), 0)

    @pl.when(jnp.logical_not(fast_ok))
    def _():
        lax.fori_loop(0, tm // CHUNK, functools.partial(chunk_body, fast=False), 0)

    h = x + jnp.dot(ocat_ref[...], wout_ref[...], preferred_element_type=F32)
    h_ref[0] = h
    hq_ref[...] = jnp.dot(_rmsnorm(h, gc_ref[...]).astype(BF16), wcq_ref[...], preferred_element_type=F32)


def _mix_sample(x, s0, lb_param, g_mix, w_in, hg_norm_g, ln_v_g, ln_v_b, wmix, bcol, w_out, g_cross,
                w_cq, tm):
    nt, _, _ = x.shape
    nreq = tm // 4
    consts = (lb_param, g_mix, w_in, hg_norm_g, ln_v_g, ln_v_b, wmix, bcol, w_out, g_cross, w_cq)
    in_specs = [
        pl.BlockSpec((1, tm, D_MODEL), lambda i: (i, 0, 0)),
        pl.BlockSpec((nreq, HG_HEADS, HG_DK, HG_DK), lambda i: (i, 0, 0, 0)),
    ] + [_const_spec(a.shape) for a in consts]
    out_specs = [
        pl.BlockSpec((1, tm, D_MODEL), lambda i: (i, 0, 0)),
        pl.BlockSpec((nreq, HG_HEADS, HG_DK, HG_DK), lambda i: (i, 0, 0, 0)),
        pl.BlockSpec((tm, CM_WIDTH), lambda i: (i, 0)),
        pl.BlockSpec((tm, D_MODEL), lambda i: (i, 0)),
    ]
    return pl.pallas_call(
        functools.partial(_mix_sample_kernel, tm=tm),
        grid=(nt,),
        in_specs=in_specs,
        out_specs=out_specs,
        out_shape=[jax.ShapeDtypeStruct((nt, tm, D_MODEL), F32),
                   jax.ShapeDtypeStruct(s0.shape, F32),
                   jax.ShapeDtypeStruct((nt * tm, CM_WIDTH), F32),
                   jax.ShapeDtypeStruct((nt * tm, D_MODEL), F32)],
        scratch_shapes=[pltpu.VMEM((tm, IN_WIDTH), F32), pltpu.VMEM((tm, D_MODEL), BF16)]
        + [pltpu.VMEM((tm, HG_WIDTH), F32)] * 3,
        compiler_params=pltpu.CompilerParams(dimension_semantics=("arbitrary",),
                                             vmem_limit_bytes=VMEM_LIMIT_BYTES),
        name="mix_sample",
    )(x, s0, *consts)


def _memkv_kernel(m_ref, g_ref, wk_ref, wv_ref, k_ref, v_ref):
    mn = _rmsnorm(m_ref[...], g_ref[...]).astype(BF16)
    k_ref[...] = jnp.dot(mn, wk_ref[...], preferred_element_type=F32)
    v_ref[...] = jnp.dot(mn, wv_ref[...], preferred_element_type=F32)


def _memkv(mem, g_mem, w_ck, w_cv, tm):
    n = mem.shape[0]
    return pl.pallas_call(
        _memkv_kernel,
        grid=(n // tm,),
        in_specs=[pl.BlockSpec((tm, D_MODEL), lambda i: (i, 0)), _const_spec(g_mem.shape),
                  _const_spec(w_ck.shape), _const_spec(w_cv.shape)],
        out_specs=[pl.BlockSpec((tm, D_MODEL), lambda i: (i, 0)),
                   pl.BlockSpec((tm, D_MODEL), lambda i: (i, 0))],
        out_shape=[jax.ShapeDtypeStruct((n, D_MODEL), F32), jax.ShapeDtypeStruct((n, D_MODEL), F32)],
        compiler_params=pltpu.CompilerParams(dimension_semantics=("arbitrary",),
                                             vmem_limit_bytes=VMEM_LIMIT_BYTES),
        name="memkv",
    )(mem, g_mem, w_ck, w_cv)


def _post_sample_kernel(x_ref, ca_ref, wco_ref, gm_ref, wup_ref, wdn_ref, gf_ref, y_ref, hm_s):
    x = x_ref[0] + jnp.dot(ca_ref[...].astype(BF16), wco_ref[...], preferred_element_type=F32)
    x = x + _mlp(_rmsnorm(x, gm_ref[...]).astype(BF16), wup_ref, wdn_ref, hm_s)
    y_ref[0] = _rmsnorm(x, gf_ref[...])


def _post_sample(x, ca, w_co, g_mlp, w_up, w_down, g_final):
    nt, tm, _ = x.shape
    in_specs = [
        pl.BlockSpec((1, tm, D_MODEL), lambda i: (i, 0, 0)),
        pl.BlockSpec((tm, D_MODEL), lambda i: (i, 0)),
        _const_spec(w_co.shape), _const_spec(g_mlp.shape), _const_spec(w_up.shape),
        _const_spec(w_down.shape), _const_spec(g_final.shape),
    ]
    return pl.pallas_call(
        _post_sample_kernel,
        grid=(nt,),
        in_specs=in_specs,
        out_specs=pl.BlockSpec((1, tm, D_MODEL), lambda i: (i, 0, 0)),
        out_shape=jax.ShapeDtypeStruct((nt, tm, D_MODEL), F32),
        scratch_shapes=[pltpu.VMEM((tm, D_FF), BF16)],
        compiler_params=pltpu.CompilerParams(dimension_semantics=("arbitrary",),
                                             vmem_limit_bytes=VMEM_LIMIT_BYTES),
        name="post_sample",
    )(x, ca, w_co, g_mlp, w_up, w_down, g_final)


def _cast_kernel(*refs):
    n = len(refs) // 2
    for src, dst in zip(refs[:n], refs[n:]):
        dst[...] = src[...].astype(BF16)


def _cast_weights(*ws):
    steps = 8
    specs = [pl.BlockSpec((w.shape[0] // steps, w.shape[1]), lambda i: (i, 0)) for w in ws]
    return pl.pallas_call(
        _cast_kernel,
        grid=(steps,),
        in_specs=specs,
        out_specs=specs,
        out_shape=[jax.ShapeDtypeStruct(w.shape, BF16) for w in ws],
        compiler_params=pltpu.CompilerParams(dimension_semantics=("arbitrary",),
                                             vmem_limit_bytes=VMEM_LIMIT_BYTES),
        name="cast_weights",
    )(*ws)


def kernel(x_prompt, x_sample, mem_prompt, state_hgrn, cache_mem_k, cache_mem_v, lb_param, g_mix,
           w_in, hg_norm_g, ln_v_g, ln_v_b, w_s, b_s, w_out, g_cross, g_mem, w_cq, w_ck, w_cv, w_co,
           g_mlp, w_up, w_down, g_final):
    B, L, _ = x_prompt.shape
    DB, DL, _ = x_sample.shape
    assert DL == 4 and g_mix.shape[0] == 1

    row = lambda a: a.reshape(1, -1)
    win_b, wout_b, wcq_b, wco_b, wck_b, wcv_b, wup_b, wdn_b = _cast_weights(
        w_in[0], w_out[0], w_cq[0], w_co[0], w_ck[0], w_cv[0], w_up[0], w_down[0])
    gmix, hgn, lng, lnb = row(g_mix[0]), row(hg_norm_g[0]), row(ln_v_g[0]), row(ln_v_b[0])
    gcr, gmem, gmlp, gfin = row(g_cross[0]), row(g_mem[0]), row(g_mlp[0]), row(g_final)
    wmix_p = w_s[0]
    bcol_p = b_s[0].T
    nrep = CHUNK // DL
    wmix_s = jnp.broadcast_to(w_s[0][:, None, :DL, None, :DL],
                              (CM_GROUPS, nrep, DL, nrep, DL)).reshape(CM_GROUPS, CHUNK, CHUNK)
    bcol_s = jnp.broadcast_to(b_s[0][:, :DL].T[None], (nrep, DL, CM_GROUPS)).reshape(CHUNK, CM_GROUPS)

    tm_s = 128
    xs = x_sample.reshape(DB * DL // tm_s, tm_s, D_MODEL)
    h_s, s_s, vn_s, hq_s = _mix_sample(xs, state_hgrn[0], lb_param, gmix, win_b, hgn, lng, lnb, wmix_s,
                                       bcol_s, wout_b, gcr, wcq_b, tm=tm_s)

    mk, mv = _memkv(mem_prompt.reshape(B * N_MEM, D_MODEL), gmem, wck_b, wcv_b, tm=512)
    y_p, s_p, ca_s = _prompt_layer(x_prompt, mk.reshape(B, N_MEM, D_MODEL), mv.reshape(B, N_MEM, D_MODEL),
                                   hq_s, cache_mem_k[0], cache_mem_v[0],
                                   lb_param, gmix, win_b, hgn, lng, lnb, wmix_p, bcol_p, wout_b,
                                   gcr, wcq_b, wco_b, gmlp, wup_b, wdn_b, gfin, tm=256)

    y_s = _post_sample(h_s.reshape(1, DB * DL, D_MODEL), ca_s, wco_b, gmlp, wup_b, wdn_b, gfin)

    return (y_p, y_s.reshape(DB, DL, D_MODEL), s_p[None], s_s[None],
            mk.reshape(1, B, N_MEM, X_HEADS, X_HEAD_DIM), mv.reshape(1, B, N_MEM, X_HEADS, X_HEAD_DIM),
            vn_s.reshape(1, DB, DL, CM_WIDTH))
```

```python
import functools

import jax
import jax.numpy as jnp
from jax import lax
from jax.experimental import pallas as pl
from jax.experimental.pallas import tpu as pltpu

F32 = jnp.float32
BF16 = jnp.bfloat16

D_MODEL = 1024
HG_WIDTH = 512
HG_HEADS = 4
HG_DK = 128
CM_WIDTH = 512
CM_GROUPS = 4
CM_GROUP_DIM = 128
IN_WIDTH = 4 * HG_WIDTH + 2 * CM_WIDTH
N_MEM = 256
X_HEADS = 4
X_HEAD_DIM = 256
D_FF = 4096
EPS = 1e-6

CHUNK = 128
SUB = 16
VMEM_LIMIT_BYTES = 62 * 1024 * 1024
LOG2E = 1.4426950408889634
FAST_BLOCK = 32
FAST_MAX_DECAY = 80.0


def _bdot(a, b):
    return jnp.dot(a.astype(BF16), b.astype(BF16), preferred_element_type=F32)


def _bdot_nt(a, b):
    return lax.dot_general(a.astype(BF16), b.astype(BF16), (((1,), (1,)), ((), ())),
                           preferred_element_type=F32)


def _rmsnorm(x, g):
    ms = jnp.mean(x * x, axis=-1, keepdims=True)
    return x * lax.rsqrt(ms + EPS) * g


def _sigmoid(x):
    return 0.5 * jnp.tanh(0.5 * x) + 0.5


def _gelu(x):
    return 0.5 * x * (1.0 + jnp.tanh(0.7978845608028654 * (x + 0.044715 * (x * x * x))))


def _softmax_rows(s):
    m = jnp.max(s, axis=-1, keepdims=True)
    e = jnp.exp(s - m)
    return e / jnp.sum(e, axis=-1, keepdims=True)


def _seg_cumsum(x, seg):
    n = x.shape[0]
    pos = lax.broadcasted_iota(jnp.int32, x.shape, 0) % seg
    s = 1
    while s < min(seg, 8):
        x = x + jnp.where(pos >= s, pltpu.roll(x, s, 0), 0.0)
        s *= 2
    while s < seg:
        parts = []
        for r0 in range(0, n, seg):
            parts.append(x[r0:r0 + s])
            parts.append(x[r0 + s:r0 + seg] + x[r0:r0 + seg - s])
        x = jnp.concatenate(parts, axis=0)
        s *= 2
    return x


def _lower_bound(lb_ref):
    lbp = lb_ref[...]
    lbe = jnp.exp(lbp - jnp.max(lbp, axis=0, keepdims=True))
    return lbe[0:1] / jnp.sum(lbe, axis=0, keepdims=True)


def _gates(zq, zf, lb, seg):
    th = jnp.tanh(0.5 * zf)
    logf = jnp.log(lb + (1.0 - lb) * (0.5 + 0.5 * th))
    return zq * _sigmoid(zq), (1.0 - lb) * (0.5 - 0.5 * th), _seg_cumsum(logf, seg)


def _block_decay(bcum, sample):
    if sample:
        return jnp.max(-bcum, axis=0, keepdims=True)
    d = -bcum[FAST_BLOCK - 1:FAST_BLOCK]
    for i in range(1, CHUNK // FAST_BLOCK):
        n0, n1 = i * FAST_BLOCK, (i + 1) * FAST_BLOCK
        d = jnp.maximum(d, bcum[n0 - 1:n0] - bcum[n1 - 1:n1])
    return d


def _chunk_masks(sample):
    row = lax.broadcasted_iota(jnp.int32, (CHUNK, CHUNK), 0)
    col = lax.broadcasted_iota(jnp.int32, (CHUNK, CHUNK), 1)
    if sample:
        return (row // 4 == col // 4) & (row >= col)
    return row >= col


def _scores_fast(q, k, bcum, sample):
    if sample:
        return _bdot_nt(q * jnp.exp(bcum), k * jnp.exp(-bcum))
    a_rows = []
    for i in range(CHUNK // FAST_BLOCK):
        n0, n1 = i * FAST_BLOCK, (i + 1) * FAST_BLOCK
        ref_b = bcum[n0 - 1:n0] if i else jnp.zeros((1, HG_DK), F32)
        qt = q[n0:n1] * jnp.exp(bcum[n0:n1] - ref_b)
        kt = k[:n1] * jnp.exp(ref_b - bcum[:n1])
        if n1 < CHUNK:
            kt = jnp.concatenate([kt, jnp.zeros((CHUNK - n1, HG_DK), F32)], axis=0)
        a_rows.append(_bdot_nt(qt, kt))
    return jnp.concatenate(a_rows, axis=0)


def _scores_exact(q, k, bcum, sample, mask):
    col8 = lax.broadcasted_iota(jnp.int32, (8, CHUNK), 1)
    b2 = bcum * LOG2E
    cexp = b2 - jnp.log2(k)
    a_rows = []
    for gb in range(CHUNK // SUB):
        lo = slice(gb * SUB, gb * SUB + 8)
        hi = slice(gb * SUB + 8, (gb + 1) * SUB)
        a_lo = jnp.zeros((8, CHUNK), F32)
        a_hi = jnp.zeros((8, CHUNK), F32)
        for s in range(SUB):
            sg = gb * SUB + s
            c_s = cexp[sg:sg + 1]
            if s < 8:
                p = q[lo] * jnp.exp2(b2[lo] - c_s)
                a_lo = jnp.where(col8 == sg, jnp.sum(p, axis=-1, keepdims=True), a_lo)
            if (not sample) or s >= 8:
                p = q[hi] * jnp.exp2(b2[hi] - c_s)
                a_hi = jnp.where(col8 == sg, jnp.sum(p, axis=-1, keepdims=True), a_hi)
        a_rows.append(a_lo)
        a_rows.append(a_hi)
    a = jnp.concatenate(a_rows, axis=0)
    if sample:
        return a
    o_rows = [jnp.zeros((SUB, CHUNK), F32)]
    for i in range(1, CHUNK // SUB):
        n = i * SUB
        ref_b = bcum[n - 1:n]
        qt = q[n:n + SUB] * jnp.exp(bcum[n:n + SUB] - ref_b)
        kt = k[:n] * jnp.exp(ref_b - bcum[:n])
        kt = jnp.concatenate([kt, jnp.zeros((CHUNK - n, HG_DK), F32)], axis=0)
        o_rows.append(_bdot_nt(qt, kt))
    return jnp.where(mask, a, 0.0) + jnp.concatenate(o_rows, axis=0)


def _hgrn_out(o, zg, g):
    o = o * lax.rsqrt(jnp.mean(o * o, axis=-1, keepdims=True) + EPS) * g
    return (o * _sigmoid(zg)).astype(BF16)


def _gmlp_chunk(z_ref, c0, rows, lng_ref, lnb_ref, wmix_ref, bcol_ref, mask, ocat_ref, vn_ref):
    u = _gelu(z_ref[rows, c0:c0 + CM_WIDTH])
    gv = _gelu(z_ref[rows, c0 + CM_WIDTH:c0 + 2 * CM_WIDTH])
    mu = jnp.mean(gv, axis=-1, keepdims=True)
    dv = gv - mu
    var = jnp.mean(dv * dv, axis=-1, keepdims=True)
    vn = dv * lax.rsqrt(var + EPS) * lng_ref[...] + lnb_ref[...]
    if vn_ref is not None:
        vn_ref[rows, :] = vn
    for g in range(CM_GROUPS):
        gs = slice(g * CM_GROUP_DIM, (g + 1) * CM_GROUP_DIM)
        wm = jnp.where(mask, wmix_ref[g], 0.0)
        mixed = _bdot(wm, vn[:, gs]) + bcol_ref[:, g:g + 1]
        ocat_ref[rows, HG_WIDTH + g * CM_GROUP_DIM:HG_WIDTH + (g + 1) * CM_GROUP_DIM] = (
            u[:, gs] * mixed).astype(BF16)


def _sample_pair_probs(hq8, ck_ref):
    nrow = 8 * X_HEADS
    rh = lax.broadcasted_iota(jnp.int32, (nrow, N_MEM * X_HEADS), 0) // 8
    ch = lax.broadcasted_iota(jnp.int32, (nrow, N_MEM * X_HEADS), 1) % X_HEADS
    q = jnp.concatenate([hq8[:, h * X_HEAD_DIM:(h + 1) * X_HEAD_DIM] for h in range(X_HEADS)],
                        axis=0).astype(BF16)
    probs = []
    for r in range(2):
        k2 = ck_ref[r].reshape(N_MEM * X_HEADS, X_HEAD_DIM)
        s = jnp.where(rh == ch, _bdot_nt(q, k2) * (X_HEAD_DIM ** -0.5), -jnp.inf)
        probs.append(_softmax_rows(s))
    return probs


def _sample_pair_context(probs, cv_ref, ca_ref):
    first = (lax.broadcasted_iota(jnp.int32, (8 * X_HEADS, X_HEAD_DIM), 0) % 8) < 4
    outs = [_bdot(probs[r], cv_ref[r].reshape(N_MEM * X_HEADS, X_HEAD_DIM)) for r in range(2)]
    o = jnp.where(first, outs[0], outs[1])
    for h in range(X_HEADS):
        ca_ref[:, h * X_HEAD_DIM:(h + 1) * X_HEAD_DIM] = o[8 * h:8 * h + 8]


def _mlp(hn_bf16, wup_ref, wdn_ref, hm_s):
    for c in range(D_FF // D_MODEL):
        fs = slice(c * D_MODEL, (c + 1) * D_MODEL)
        hm = jnp.maximum(jnp.dot(hn_bf16, wup_ref[:, fs], preferred_element_type=F32), 0.0)
        hm_s[:, fs] = (hm * hm).astype(BF16)
    return jnp.dot(hm_s[...], wdn_ref[...], preferred_element_type=F32)


def _const_spec(shape):
    nd = len(shape)
    return pl.BlockSpec(shape, lambda *_: (0,) * nd, pipeline_mode=pl.Buffered(1))


def _prompt_kernel(x_ref, mk_ref, mv_ref, hqs_ref, ck_ref, cv_ref, lb_ref, gmix_ref, win_ref, hgn_ref,
                   lng_ref, lnb_ref, wmix_ref, bcol_ref, wout_ref, gc_ref, wcq_ref, wco_ref, gm_ref,
                   wup_ref, wdn_ref, gf_ref, y_ref, sout_ref, cas_ref,
                   zg_ref, z_ref, ocat_ref, q_s, k_s, b_s, a_s, st0, st1, st2, st3, h1_s, h1n_s, x2_s,
                   hn_s, sc_s, ca_s, hm_s, *, tm, nt):
    t = pl.program_id(0)
    i = lax.rem(t, nt)
    nch = tm // CHUNK
    mask = _chunk_masks(False)
    st_refs = (st0, st1, st2, st3)

    @pl.when(t == 0)
    def _():
        h1_s[...] = jnp.zeros_like(h1_s)
        h1n_s[...] = jnp.zeros_like(h1n_s)

    @pl.when(i == 0)
    def _():
        for st in st_refs:
            st[...] = jnp.zeros_like(st)


    scale = X_HEAD_DIM ** -0.5
    hq = jnp.dot(h1n_s[...], wcq_ref[...], preferred_element_type=F32)

    x = x_ref[0]
    xn = _rmsnorm(x, gmix_ref[...]).astype(BF16)
    ngate = 2 * HG_WIDTH

    def in_proj(dst, c0, c1):
        off = ngate if dst is z_ref else 0
        dst[:, c0 - off:c1 - off] = jnp.dot(xn, win_ref[:, c0:c1], preferred_element_type=F32)

    in_proj(zg_ref, 0, ngate)
    for h in range(X_HEADS):
        hs = slice(h * X_HEAD_DIM, (h + 1) * X_HEAD_DIM)
        sc_s[h] = _bdot_nt(hq[:, hs], mk_ref[0, :, hs]) * scale
    in_proj(z_ref, ngate, ngate + 2 * HG_WIDTH)
    for h in range(X_HEADS):
        hs = slice(h * X_HEAD_DIM, (h + 1) * X_HEAD_DIM)
        ca_s[:, hs] = _bdot(_softmax_rows(sc_s[h]), mv_ref[0, :, hs]).astype(BF16)
    in_proj(z_ref, ngate + 2 * HG_WIDTH, ngate + 2 * HG_WIDTH + CM_WIDTH)

    lb_all = _lower_bound(lb_ref)
    dmax = jnp.zeros((1, HG_WIDTH), F32)
    for c in range(nch):
        rows = slice(c * CHUNK, (c + 1) * CHUNK)
        q, k, bcum = _gates(zg_ref[rows, 0:HG_WIDTH], zg_ref[rows, HG_WIDTH:2 * HG_WIDTH], lb_all, CHUNK)
        q_s[rows, :] = q
        k_s[rows, :] = k
        b_s[rows, :] = bcum
        dmax = jnp.maximum(dmax, _block_decay(bcum, False))
    fast_ok = jnp.max(dmax) < FAST_MAX_DECAY

    x2 = h1_s[...] + jnp.dot(ca_s[...], wco_ref[...], preferred_element_type=F32)
    in_proj(z_ref, ngate + 2 * HG_WIDTH + CM_WIDTH, IN_WIDTH)
    x2_s[...] = x2
    hn_s[...] = _rmsnorm(x2, gm_ref[...]).astype(BF16)

    def fill_scores(j, fast):
        c, h = divmod(j, HG_HEADS)
        rows = slice(c * CHUNK, (c + 1) * CHUNK)
        cs = slice(h * HG_DK, (h + 1) * HG_DK)
        q, k, bcum = q_s[rows, cs], k_s[rows, cs], b_s[rows, cs]
        a = _scores_fast(q, k, bcum, False) if fast else _scores_exact(q, k, bcum, False, mask)
        a_s[j] = jnp.where(mask, a, 0.0).astype(BF16)

    def head_chunk(j):
        c, h = divmod(j, HG_HEADS)
        rows = slice(c * CHUNK, (c + 1) * CHUNK)
        cs = slice(h * HG_DK, (h + 1) * HG_DK)
        q, k, bcum = q_s[rows, cs], k_s[rows, cs], b_s[rows, cs]
        v = z_ref[rows, h * HG_DK:(h + 1) * HG_DK]
        zg = z_ref[rows, HG_WIDTH + h * HG_DK:HG_WIDTH + (h + 1) * HG_DK]
        eb = jnp.exp(bcum)
        kd = k * jnp.exp(bcum[CHUNK - 1:CHUNK] - bcum)
        st = st_refs[h][...]
        o = jnp.dot(a_s[j], v.astype(BF16), preferred_element_type=F32)
        o = o + _bdot_nt(q * eb, st)
        st_refs[h][...] = st * eb[CHUNK - 1:CHUNK] + _bdot(v.T, kd)
        ocat_ref[rows, cs] = _hgrn_out(o, zg, hgn_ref[:, cs])
        if h == HG_HEADS - 1:
            _gmlp_chunk(z_ref, 2 * HG_WIDTH, rows, lng_ref, lnb_ref, wmix_ref, bcol_ref, mask, ocat_ref,
                        None)

    def region2(fast):
        assert nch * HG_HEADS == 8 and D_FF // D_MODEL == 4
        hn = hn_s[...]

        def up(p):
            fs = slice(p * D_MODEL, (p + 1) * D_MODEL)
            hm = jnp.maximum(jnp.dot(hn, wup_ref[:, fs], preferred_element_type=F32), 0.0)
            hm_s[p % 2] = (hm * hm).astype(BF16)

        def down(p, acc):
            fs = slice(p * D_MODEL, (p + 1) * D_MODEL)
            return acc + jnp.dot(hm_s[p % 2], wdn_ref[fs, :], preferred_element_type=F32)

        def out_proj():
            h1 = x_ref[0] + jnp.dot(ocat_ref[...], wout_ref[...], preferred_element_type=F32)
            h1_s[...] = h1
            h1n_s[...] = _rmsnorm(h1, gc_ref[...]).astype(BF16)

        acc = x2_s[...]
        up(0)
        fill_scores(0, fast), fill_scores(1, fast)
        probs = _sample_pair_probs(hqs_ref[...], ck_ref)
        up(1)
        fill_scores(2, fast), fill_scores(3, fast)
        acc = down(0, acc)
        fill_scores(4, fast), fill_scores(5, fast)
        acc = down(1, acc)
        fill_scores(6, fast), fill_scores(7, fast)
        _sample_pair_context(probs, cv_ref, cas_ref)
        up(2)
        head_chunk(0), head_chunk(1), head_chunk(2)
        up(3)
        head_chunk(3), head_chunk(4), head_chunk(5)
        acc = down(2, acc)
        head_chunk(6), head_chunk(7)
        out_proj()
        acc = down(3, acc)
        y_ref[0] = _rmsnorm(acc, gf_ref[...])

    @pl.when(fast_ok)
    def _():
        region2(True)

    @pl.when(jnp.logical_not(fast_ok))
    def _():
        region2(False)

    @pl.when(i == nt - 1)
    def _():
        for h in range(HG_HEADS):
            sout_ref[0, h] = st_refs[h][...].T


def _prompt_layer(x, mk, mv, hqs, ck, cv, lb_param, g_mix, w_in, hg_norm_g, ln_v_g, ln_v_b, wmix, bcol,
                  w_out, g_cross, w_cq, w_co, g_mlp, w_up, w_down, g_final, tm):
    B, L, _ = x.shape
    nt = L // tm
    T = B * nt
    npair = ck.shape[0] // 2
    assert npair <= T + 1

    def cur(t):
        return jnp.minimum(t, T - 1)

    def prev(t):
        return jnp.maximum(t - 1, 0)

    def pair(t):
        return jnp.minimum(t, npair - 1)

    consts = (lb_param, g_mix, w_in, hg_norm_g, ln_v_g, ln_v_b, wmix, bcol, w_out,
              g_cross, w_cq, w_co, g_mlp, w_up, w_down, g_final)
    in_specs = [
        pl.BlockSpec((1, tm, D_MODEL), lambda t: (cur(t) // nt, cur(t) % nt, 0)),
        pl.BlockSpec((1, N_MEM, D_MODEL), lambda t: (prev(t) // nt, 0, 0)),
        pl.BlockSpec((1, N_MEM, D_MODEL), lambda t: (prev(t) // nt, 0, 0)),
        pl.BlockSpec((8, D_MODEL), lambda t: (pair(t), 0)),
        pl.BlockSpec((2, N_MEM, X_HEADS, X_HEAD_DIM), lambda t: (pair(t), 0, 0, 0)),
        pl.BlockSpec((2, N_MEM, X_HEADS, X_HEAD_DIM), lambda t: (pair(t), 0, 0, 0)),
    ] + [_const_spec(a.shape) for a in consts]
    out_specs = [
        pl.BlockSpec((1, tm, D_MODEL), lambda t: (prev(t) // nt, prev(t) % nt, 0)),
        pl.BlockSpec((1, HG_HEADS, HG_DK, HG_DK), lambda t: (cur(t) // nt, 0, 0, 0)),
        pl.BlockSpec((8, D_MODEL), lambda t: (pair(t), 0)),
    ]
    scratch = [
        pltpu.VMEM((tm, 2 * HG_WIDTH), F32),
        pltpu.VMEM((tm, IN_WIDTH - 2 * HG_WIDTH), F32),
        pltpu.VMEM((tm, D_MODEL), BF16),
        pltpu.VMEM((tm, HG_WIDTH), F32), pltpu.VMEM((tm, HG_WIDTH), F32),
        pltpu.VMEM((tm, HG_WIDTH), F32),
        pltpu.VMEM((tm // CHUNK * HG_HEADS, CHUNK, CHUNK), BF16),
        pltpu.VMEM((HG_DK, HG_DK), F32), pltpu.VMEM((HG_DK, HG_DK), F32),
        pltpu.VMEM((HG_DK, HG_DK), F32), pltpu.VMEM((HG_DK, HG_DK), F32),
        pltpu.VMEM((tm, D_MODEL), F32),
        pltpu.VMEM((tm, D_MODEL), BF16),
        pltpu.VMEM((tm, D_MODEL), F32),
        pltpu.VMEM((tm, D_MODEL), BF16),
        pltpu.VMEM((X_HEADS, tm, N_MEM), F32),
        pltpu.VMEM((tm, D_MODEL), BF16),
        pltpu.VMEM((2, tm, D_MODEL), BF16),
    ]
    return pl.pallas_call(
        functools.partial(_prompt_kernel, tm=tm, nt=nt),
        grid=(T + 1,),
        in_specs=in_specs,
        out_specs=out_specs,
        out_shape=[jax.ShapeDtypeStruct((B, L, D_MODEL), F32),
                   jax.ShapeDtypeStruct((B, HG_HEADS, HG_DK, HG_DK), F32),
                   jax.ShapeDtypeStruct(hqs.shape, F32)],
        scratch_shapes=scratch,
        compiler_params=pltpu.CompilerParams(dimension_semantics=("arbitrary",),
                                             vmem_limit_bytes=VMEM_LIMIT_BYTES),
        name="prompt_layer",
    )(x, mk, mv, hqs, ck, cv, *consts)


def _mix_sample_kernel(x_ref, s0_ref, lb_ref, gmix_ref, win_ref, hgn_ref, lng_ref, lnb_ref, wmix_ref,
                       bcol_ref, wout_ref, gc_ref, wcq_ref, h_ref, sout_ref, vn_ref, hq_ref,
                       z_ref, ocat_ref, q_s, k_s, b_s, wm_s, bc_s, *, tm):
    x = x_ref[0]
    z_ref[...] = jnp.dot(_rmsnorm(x, gmix_ref[...]).astype(BF16), win_ref[...],
                         preferred_element_type=F32)
    lb_all = _lower_bound(lb_ref)
    mask = _chunk_masks(True)
    row = lax.broadcasted_iota(jnp.int32, (CHUNK, CHUNK), 0)
    col = lax.broadcasted_iota(jnp.int32, (CHUNK, CHUNK), 1)
    eye = row == col

    @pl.when(pl.program_id(0) == 0)
    def _():
        pick = (row % 4 == col).astype(BF16)
        for g in range(CM_GROUPS):
            corner = jnp.where((row < 4) & (col < 4), wmix_ref[g], 0.0)
            wm_s[g] = _bdot_nt(_bdot(pick, corner), pick)
        pos = lax.broadcasted_iota(jnp.int32, (CHUNK, CM_GROUPS), 0) % 4
        bc = jnp.zeros((CHUNK, CM_GROUPS), F32)
        for t in range(4):
            bc = jnp.where(pos == t, bcol_ref[t:t + 1, :], bc)
        bc_s[...] = bc
    r16 = lax.broadcasted_iota(jnp.int32, (SUB, HG_DK), 0)

    def gate_body(c, dmax):
        rows = pl.ds(pl.multiple_of(c * CHUNK, CHUNK), CHUNK)
        q, k, bcum = _gates(z_ref[rows, 0:HG_WIDTH], z_ref[rows, HG_WIDTH:2 * HG_WIDTH], lb_all, 4)
        q_s[rows, :] = q
        k_s[rows, :] = k
        b_s[rows, :] = bcum
        return jnp.maximum(dmax, _block_decay(bcum, True))

    dmax = lax.fori_loop(0, tm // CHUNK, gate_body, jnp.zeros((1, HG_WIDTH), F32))
    fast_ok = jnp.max(dmax) < FAST_MAX_DECAY

    def chunk_body(c, carry, *, fast):
        rows = pl.ds(pl.multiple_of(c * CHUNK, CHUNK), CHUNK)
        for h in range(HG_HEADS):
            cs = slice(h * HG_DK, (h + 1) * HG_DK)
            q, k, bcum = q_s[rows, cs], k_s[rows, cs], b_s[rows, cs]
            v = z_ref[rows, 2 * HG_WIDTH + h * HG_DK:2 * HG_WIDTH + (h + 1) * HG_DK]
            zg = z_ref[rows, 3 * HG_WIDTH + h * HG_DK:3 * HG_WIDTH + (h + 1) * HG_DK]
            a = _scores_fast(q, k, bcum, True) if fast else _scores_exact(q, k, bcum, True, mask)
            o = _bdot(jnp.where(mask, a, 0.0), v)
            eb = jnp.exp(bcum)
            qd = q * eb
            o_parts = []
            for gb in range(CHUNK // SUB):
                blk = slice(gb * SUB, (gb + 1) * SUB)
                qd_b, v_b, eb_b, k_b, b_b = qd[blk], v[blk], eb[blk], k[blk], bcum[blk]
                inter = jnp.zeros((SUB, HG_DK), F32)
                for j in range(4):
                    req = c * (CHUNK // 4) + gb * 4 + j
                    s0 = s0_ref[req, h]
                    last = 4 * j + 3
                    inter = jnp.where(r16 // 4 == j, _bdot(qd_b, s0), inter)
                    kd = jnp.where(r16 // 4 == j, k_b * jnp.exp(b_b[last:last + 1] - b_b), 0.0)
                    upd = lax.dot_general(kd.astype(BF16), v_b.astype(BF16),
                                          (((0,), (0,)), ((), ())), preferred_element_type=F32)
                    dcol = jnp.sum(jnp.where(eye, eb_b[last:last + 1], 0.0), axis=-1, keepdims=True)
                    sout_ref[req, h] = dcol * s0 + upd
                o_parts.append(inter)
            o = o + jnp.concatenate(o_parts, axis=0)
            ocat_ref[rows, cs] = _hgrn_out(o, zg, hgn_ref[:, cs])
        _gmlp_chunk(z_ref, 4 * HG_WIDTH, rows, lng_ref, lnb_ref, wm_s, bc_s, mask, ocat_ref, vn_ref)
        return carry

    @pl.when(fast_ok)
    def _():
        lax.fori_loop(0, tm // CHUNK, functools.partial(chunk_body, fast=True), 0)

    @pl.when(jnp.logical_not(fast_ok))
    def _():
        lax.fori_loop(0, tm // CHUNK, functools.partial(chunk_body, fast=False), 0)

    h = x + jnp.dot(ocat_ref[...], wout_ref[...], preferred_element_type=F32)
    h_ref[0] = h
    hq_ref[...] = jnp.dot(_rmsnorm(h, gc_ref[...]).astype(BF16), wcq_ref[...], preferred_element_type=F32)


def _mix_sample(x, s0, lb_param, g_mix, w_in, hg_norm_g, ln_v_g, ln_v_b, wmix, bcol, w_out, g_cross,
                w_cq, tm):
    nt, _, _ = x.shape
    nreq = tm // 4
    consts = (lb_param, g_mix, w_in, hg_norm_g, ln_v_g, ln_v_b, wmix, bcol, w_out, g_cross, w_cq)
    in_specs = [
        pl.BlockSpec((1, tm, D_MODEL), lambda i: (i, 0, 0)),
        pl.BlockSpec((nreq, HG_HEADS, HG_DK, HG_DK), lambda i: (i, 0, 0, 0)),
    ] + [_const_spec(a.shape) for a in consts]
    out_specs = [
        pl.BlockSpec((1, tm, D_MODEL), lambda i: (i, 0, 0)),
        pl.BlockSpec((nreq, HG_HEADS, HG_DK, HG_DK), lambda i: (i, 0, 0, 0)),
        pl.BlockSpec((tm, CM_WIDTH), lambda i: (i, 0)),
        pl.BlockSpec((tm, D_MODEL), lambda i: (i, 0)),
    ]
    return pl.pallas_call(
        functools.partial(_mix_sample_kernel, tm=tm),
        grid=(nt,),
        in_specs=in_specs,
        out_specs=out_specs,
        out_shape=[jax.ShapeDtypeStruct((nt, tm, D_MODEL), F32),
                   jax.ShapeDtypeStruct(s0.shape, F32),
                   jax.ShapeDtypeStruct((nt * tm, CM_WIDTH), F32),
                   jax.ShapeDtypeStruct((nt * tm, D_MODEL), F32)],
        scratch_shapes=[pltpu.VMEM((tm, IN_WIDTH), F32), pltpu.VMEM((tm, D_MODEL), BF16)]
        + [pltpu.VMEM((tm, HG_WIDTH), F32)] * 3
        + [pltpu.VMEM((CM_GROUPS, CHUNK, CHUNK), F32), pltpu.VMEM((CHUNK, CM_GROUPS), F32)],
        compiler_params=pltpu.CompilerParams(dimension_semantics=("arbitrary",),
                                             vmem_limit_bytes=VMEM_LIMIT_BYTES),
        name="mix_sample",
    )(x, s0, *consts)


def _memkv_kernel(m_ref, g_ref, wk_ref, wv_ref, k_ref, v_ref, kb_ref, vb_ref):
    mn = _rmsnorm(m_ref[...], g_ref[...]).astype(BF16)
    k = jnp.dot(mn, wk_ref[...], preferred_element_type=F32)
    v = jnp.dot(mn, wv_ref[...], preferred_element_type=F32)
    k_ref[...] = k
    v_ref[...] = v
    kb_ref[...] = k.astype(BF16)
    vb_ref[...] = v.astype(BF16)


def _memkv(mem, g_mem, w_ck, w_cv, tm):
    n = mem.shape[0]
    return pl.pallas_call(
        _memkv_kernel,
        grid=(n // tm,),
        in_specs=[pl.BlockSpec((tm, D_MODEL), lambda i: (i, 0)), _const_spec(g_mem.shape),
                  _const_spec(w_ck.shape), _const_spec(w_cv.shape)],
        out_specs=[pl.BlockSpec((tm, D_MODEL), lambda i: (i, 0))] * 4,
        out_shape=[jax.ShapeDtypeStruct((n, D_MODEL), F32)] * 2 + [jax.ShapeDtypeStruct((n, D_MODEL), BF16)] * 2,
        compiler_params=pltpu.CompilerParams(dimension_semantics=("arbitrary",),
                                             vmem_limit_bytes=VMEM_LIMIT_BYTES),
        name="memkv",
    )(mem, g_mem, w_ck, w_cv)


def _post_sample_kernel(x_ref, ca_ref, wco_ref, gm_ref, wup_ref, wdn_ref, gf_ref, y_ref, hm_s):
    x = x_ref[0] + jnp.dot(ca_ref[...].astype(BF16), wco_ref[...], preferred_element_type=F32)
    x = x + _mlp(_rmsnorm(x, gm_ref[...]).astype(BF16), wup_ref, wdn_ref, hm_s)
    y_ref[0] = _rmsnorm(x, gf_ref[...])


def _post_sample(x, ca, w_co, g_mlp, w_up, w_down, g_final):
    nt, tm, _ = x.shape
    in_specs = [
        pl.BlockSpec((1, tm, D_MODEL), lambda i: (i, 0, 0)),
        pl.BlockSpec((tm, D_MODEL), lambda i: (i, 0)),
        _const_spec(w_co.shape), _const_spec(g_mlp.shape), _const_spec(w_up.shape),
        _const_spec(w_down.shape), _const_spec(g_final.shape),
    ]
    return pl.pallas_call(
        _post_sample_kernel,
        grid=(nt,),
        in_specs=in_specs,
        out_specs=pl.BlockSpec((1, tm, D_MODEL), lambda i: (i, 0, 0)),
        out_shape=jax.ShapeDtypeStruct((nt, tm, D_MODEL), F32),
        scratch_shapes=[pltpu.VMEM((tm, D_FF), BF16)],
        compiler_params=pltpu.CompilerParams(dimension_semantics=("arbitrary",),
                                             vmem_limit_bytes=VMEM_LIMIT_BYTES),
        name="post_sample",
    )(x, ca, w_co, g_mlp, w_up, w_down, g_final)


def _cast_kernel(*refs):
    n = len(refs) // 2
    for src, dst in zip(refs[:n], refs[n:]):
        dst[...] = src[...].astype(BF16)


def _cast_weights(*ws):
    steps = 8
    specs = [pl.BlockSpec((w.shape[0] // steps, w.shape[1]), lambda i: (i, 0)) for w in ws]
    return pl.pallas_call(
        _cast_kernel,
        grid=(steps,),
        in_specs=specs,
        out_specs=specs,
        out_shape=[jax.ShapeDtypeStruct(w.shape, BF16) for w in ws],
        compiler_params=pltpu.CompilerParams(dimension_semantics=("arbitrary",),
                                             vmem_limit_bytes=VMEM_LIMIT_BYTES),
        name="cast_weights",
    )(*ws)


def kernel(x_prompt, x_sample, mem_prompt, state_hgrn, cache_mem_k, cache_mem_v, lb_param, g_mix,
           w_in, hg_norm_g, ln_v_g, ln_v_b, w_s, b_s, w_out, g_cross, g_mem, w_cq, w_ck, w_cv, w_co,
           g_mlp, w_up, w_down, g_final):
    B, L, _ = x_prompt.shape
    DB, DL, _ = x_sample.shape
    assert DL == 4 and g_mix.shape[0] == 1

    row = lambda a: a.reshape(1, -1)
    win_b, wout_b, wcq_b, wco_b, wck_b, wcv_b, wup_b, wdn_b = _cast_weights(
        w_in[0], w_out[0], w_cq[0], w_co[0], w_ck[0], w_cv[0], w_up[0], w_down[0])
    gmix, hgn, lng, lnb = row(g_mix[0]), row(hg_norm_g[0]), row(ln_v_g[0]), row(ln_v_b[0])
    gcr, gmem, gmlp, gfin = row(g_cross[0]), row(g_mem[0]), row(g_mlp[0]), row(g_final)
    wmix = w_s[0]
    bcol = b_s[0].T

    tm_s = 128
    xs = x_sample.reshape(DB * DL // tm_s, tm_s, D_MODEL)
    h_s, s_s, vn_s, hq_s = _mix_sample(xs, state_hgrn[0], lb_param, gmix, win_b, hgn, lng, lnb, wmix,
                                       bcol, wout_b, gcr, wcq_b, tm=tm_s)

    mk, mv, mk_b, mv_b = _memkv(mem_prompt.reshape(B * N_MEM, D_MODEL), gmem, wck_b, wcv_b, tm=512)
    y_p, s_p, ca_s = _prompt_layer(x_prompt, mk_b.reshape(B, N_MEM, D_MODEL), mv_b.reshape(B, N_MEM, D_MODEL),
                                   hq_s, cache_mem_k[0], cache_mem_v[0],
                                   lb_param, gmix, win_b, hgn, lng, lnb, wmix, bcol, wout_b,
                                   gcr, wcq_b, wco_b, gmlp, wup_b, wdn_b, gfin, tm=256)

    y_s = _post_sample(h_s.reshape(1, DB * DL, D_MODEL), ca_s, wco_b, gmlp, wup_b, wdn_b, gfin)

    return (y_p, y_s.reshape(DB, DL, D_MODEL), s_p[None], s_s[None],
            mk.reshape(1, B, N_MEM, X_HEADS, X_HEAD_DIM), mv.reshape(1, B, N_MEM, X_HEADS, X_HEAD_DIM),
            vn_s.reshape(1, DB, DL, CM_WIDTH))
```

```python
import functools

import jax
import jax.numpy as jnp
from jax import lax
from jax.experimental import pallas as pl
from jax.experimental.pallas import tpu as pltpu

F32 = jnp.float32
BF16 = jnp.bfloat16

D_MODEL = 1024
HG_WIDTH = 512
HG_HEADS = 4
HG_DK = 128
CM_WIDTH = 512
CM_GROUPS = 4
CM_GROUP_DIM = 128
IN_WIDTH = 4 * HG_WIDTH + 2 * CM_WIDTH
N_MEM = 256
X_HEADS = 4
X_HEAD_DIM = 256
D_FF = 4096
EPS = 1e-6

CHUNK = 128
SUB = 16
VMEM_LIMIT_BYTES = 62 * 1024 * 1024
LOG2E = 1.4426950408889634
FAST_BLOCK = 32
FAST_MAX_DECAY = 80.0


def _bdot(a, b):
    return jnp.dot(a.astype(BF16), b.astype(BF16), preferred_element_type=F32)


def _bdot_nt(a, b):
    return lax.dot_general(a.astype(BF16), b.astype(BF16), (((1,), (1,)), ((), ())),
                           preferred_element_type=F32)


def _rmsnorm(x, g):
    ms = jnp.mean(x * x, axis=-1, keepdims=True)
    return x * lax.rsqrt(ms + EPS) * g


def _sigmoid(x):
    return 0.5 * jnp.tanh(0.5 * x) + 0.5


def _gelu(x):
    return 0.5 * x * (1.0 + jnp.tanh(0.7978845608028654 * (x + 0.044715 * (x * x * x))))


def _softmax_rows(s):
    m = jnp.max(s, axis=-1, keepdims=True)
    e = jnp.exp(s - m)
    return e / jnp.sum(e, axis=-1, keepdims=True)


def _seg_cumsum(x, seg):
    n = x.shape[0]
    pos = lax.broadcasted_iota(jnp.int32, x.shape, 0) % seg
    s = 1
    while s < min(seg, 8):
        x = x + jnp.where(pos >= s, pltpu.roll(x, s, 0), 0.0)
        s *= 2
    while s < seg:
        parts = []
        for r0 in range(0, n, seg):
            parts.append(x[r0:r0 + s])
            parts.append(x[r0 + s:r0 + seg] + x[r0:r0 + seg - s])
        x = jnp.concatenate(parts, axis=0)
        s *= 2
    return x


def _lower_bound(lb_ref):
    lbp = lb_ref[...]
    lbe = jnp.exp(lbp - jnp.max(lbp, axis=0, keepdims=True))
    return lbe[0:1] / jnp.sum(lbe, axis=0, keepdims=True)


def _gates(zq, zf, lb, seg):
    th = jnp.tanh(0.5 * zf)
    logf = jnp.log(lb + (1.0 - lb) * (0.5 + 0.5 * th))
    return zq * _sigmoid(zq), (1.0 - lb) * (0.5 - 0.5 * th), _seg_cumsum(logf, seg)


def _block_decay(bcum, sample):
    if sample:
        return jnp.max(-bcum, axis=0, keepdims=True)
    d = -bcum[FAST_BLOCK - 1:FAST_BLOCK]
    for i in range(1, CHUNK // FAST_BLOCK):
        n0, n1 = i * FAST_BLOCK, (i + 1) * FAST_BLOCK
        d = jnp.maximum(d, bcum[n0 - 1:n0] - bcum[n1 - 1:n1])
    return d


def _chunk_masks(sample):
    row = lax.broadcasted_iota(jnp.int32, (CHUNK, CHUNK), 0)
    col = lax.broadcasted_iota(jnp.int32, (CHUNK, CHUNK), 1)
    if sample:
        return (row // 4 == col // 4) & (row >= col)
    return row >= col


def _scores_fast(q, k, bcum, sample):
    if sample:
        return _bdot_nt(q * jnp.exp(bcum), k * jnp.exp(-bcum))
    a_rows = []
    for i in range(CHUNK // FAST_BLOCK):
        n0, n1 = i * FAST_BLOCK, (i + 1) * FAST_BLOCK
        ref_b = bcum[n0 - 1:n0] if i else jnp.zeros((1, HG_DK), F32)
        qt = q[n0:n1] * jnp.exp(bcum[n0:n1] - ref_b)
        kt = k[:n1] * jnp.exp(ref_b - bcum[:n1])
        if n1 < CHUNK:
            kt = jnp.concatenate([kt, jnp.zeros((CHUNK - n1, HG_DK), F32)], axis=0)
        a_rows.append(_bdot_nt(qt, kt))
    return jnp.concatenate(a_rows, axis=0)


def _scores_exact(q, k, bcum, sample, mask):
    col8 = lax.broadcasted_iota(jnp.int32, (8, CHUNK), 1)
    b2 = bcum * LOG2E
    cexp = b2 - jnp.log2(k)
    a_rows = []
    for gb in range(CHUNK // SUB):
        lo = slice(gb * SUB, gb * SUB + 8)
        hi = slice(gb * SUB + 8, (gb + 1) * SUB)
        a_lo = jnp.zeros((8, CHUNK), F32)
        a_hi = jnp.zeros((8, CHUNK), F32)
        for s in range(SUB):
            sg = gb * SUB + s
            c_s = cexp[sg:sg + 1]
            if s < 8:
                p = q[lo] * jnp.exp2(b2[lo] - c_s)
                a_lo = jnp.where(col8 == sg, jnp.sum(p, axis=-1, keepdims=True), a_lo)
            if (not sample) or s >= 8:
                p = q[hi] * jnp.exp2(b2[hi] - c_s)
                a_hi = jnp.where(col8 == sg, jnp.sum(p, axis=-1, keepdims=True), a_hi)
        a_rows.append(a_lo)
        a_rows.append(a_hi)
    a = jnp.concatenate(a_rows, axis=0)
    if sample:
        return a
    o_rows = [jnp.zeros((SUB, CHUNK), F32)]
    for i in range(1, CHUNK // SUB):
        n = i * SUB
        ref_b = bcum[n - 1:n]
        qt = q[n:n + SUB] * jnp.exp(bcum[n:n + SUB] - ref_b)
        kt = k[:n] * jnp.exp(ref_b - bcum[:n])
        kt = jnp.concatenate([kt, jnp.zeros((CHUNK - n, HG_DK), F32)], axis=0)
        o_rows.append(_bdot_nt(qt, kt))
    return jnp.where(mask, a, 0.0) + jnp.concatenate(o_rows, axis=0)


def _hgrn_out(o, zg, g):
    o = o * lax.rsqrt(jnp.mean(o * o, axis=-1, keepdims=True) + EPS) * g
    return (o * _sigmoid(zg)).astype(BF16)


def _gmlp_chunk(z_ref, c0, rows, lng_ref, lnb_ref, wmix_ref, bcol_ref, mask, ocat_ref, vn_ref):
    u = _gelu(z_ref[rows, c0:c0 + CM_WIDTH])
    gv = _gelu(z_ref[rows, c0 + CM_WIDTH:c0 + 2 * CM_WIDTH])
    mu = jnp.mean(gv, axis=-1, keepdims=True)
    dv = gv - mu
    var = jnp.mean(dv * dv, axis=-1, keepdims=True)
    vn = dv * lax.rsqrt(var + EPS) * lng_ref[...] + lnb_ref[...]
    if vn_ref is not None:
        vn_ref[rows, :] = vn
    for g in range(CM_GROUPS):
        gs = slice(g * CM_GROUP_DIM, (g + 1) * CM_GROUP_DIM)
        wm = jnp.where(mask, wmix_ref[g], 0.0)
        mixed = _bdot(wm, vn[:, gs]) + bcol_ref[:, g:g + 1]
        ocat_ref[rows, HG_WIDTH + g * CM_GROUP_DIM:HG_WIDTH + (g + 1) * CM_GROUP_DIM] = (
            u[:, gs] * mixed).astype(BF16)


def _sample_pair_probs(hq8, ck_ref):
    nrow = 8 * X_HEADS
    rh = lax.broadcasted_iota(jnp.int32, (nrow, N_MEM * X_HEADS), 0) // 8
    ch = lax.broadcasted_iota(jnp.int32, (nrow, N_MEM * X_HEADS), 1) % X_HEADS
    q = jnp.concatenate([hq8[:, h * X_HEAD_DIM:(h + 1) * X_HEAD_DIM] for h in range(X_HEADS)],
                        axis=0).astype(BF16)
    probs = []
    for r in range(2):
        k2 = ck_ref[r].reshape(N_MEM * X_HEADS, X_HEAD_DIM)
        s = jnp.where(rh == ch, _bdot_nt(q, k2) * (X_HEAD_DIM ** -0.5), -jnp.inf)
        probs.append(_softmax_rows(s))
    return probs


def _sample_pair_context(probs, cv_ref, ca_ref):
    first = (lax.broadcasted_iota(jnp.int32, (8 * X_HEADS, X_HEAD_DIM), 0) % 8) < 4
    outs = [_bdot(probs[r], cv_ref[r].reshape(N_MEM * X_HEADS, X_HEAD_DIM)) for r in range(2)]
    o = jnp.where(first, outs[0], outs[1])
    for h in range(X_HEADS):
        ca_ref[:, h * X_HEAD_DIM:(h + 1) * X_HEAD_DIM] = o[8 * h:8 * h + 8]


def _mlp(hn_bf16, wup_ref, wdn_ref, hm_s):
    for c in range(D_FF // D_MODEL):
        fs = slice(c * D_MODEL, (c + 1) * D_MODEL)
        hm = jnp.maximum(jnp.dot(hn_bf16, wup_ref[:, fs], preferred_element_type=F32), 0.0)
        hm_s[:, fs] = (hm * hm).astype(BF16)
    return jnp.dot(hm_s[...], wdn_ref[...], preferred_element_type=F32)


def _const_spec(shape):
    nd = len(shape)
    return pl.BlockSpec(shape, lambda *_: (0,) * nd, pipeline_mode=pl.Buffered(1))


def _prompt_kernel(x_ref, mk_ref, mv_ref, hqs_ref, ck_ref, cv_ref, lb_ref, gmix_ref, win_ref, hgn_ref,
                   lng_ref, lnb_ref, wmix_ref, bcol_ref, wout_ref, gc_ref, wcq_ref, wco_ref, gm_ref,
                   wup_ref, wdn_ref, gf_ref, y_ref, sout_ref, cas_ref,
                   zg_ref, z_ref, ocat_ref, q_s, k_s, b_s, a_s, st0, st1, st2, st3, h1_s, h1n_s, x2_s,
                   hn_s, ca_s, hm_s, *, tm, nt):
    t = pl.program_id(0)
    i = lax.rem(t, nt)
    nch = tm // CHUNK
    assert nch * HG_HEADS == 8 and D_FF // D_MODEL == 4
    mask = _chunk_masks(False)
    st_refs = (st0, st1, st2, st3)

    @pl.when(t == 0)
    def _():
        for ref in (h1_s, h1n_s, x2_s, hn_s):
            ref[...] = jnp.zeros_like(ref)

    @pl.when(i == 0)
    def _():
        for st in st_refs:
            st[...] = jnp.zeros_like(st)


    x = x_ref[0]
    xn = _rmsnorm(x, gmix_ref[...]).astype(BF16)
    ngate = 2 * HG_WIDTH

    def in_proj(dst, c0, c1):
        off = ngate if dst is z_ref else 0
        dst[:, c0 - off:c1 - off] = jnp.dot(xn, win_ref[:, c0:c1], preferred_element_type=F32)

    hn = hn_s[...]

    def up(p):
        fs = slice(p * D_MODEL, (p + 1) * D_MODEL)
        hm = jnp.maximum(jnp.dot(hn, wup_ref[:, fs], preferred_element_type=F32), 0.0)
        hm_s[p % 2] = (hm * hm).astype(BF16)

    def down(p, acc):
        fs = slice(p * D_MODEL, (p + 1) * D_MODEL)
        return acc + jnp.dot(hm_s[p % 2], wdn_ref[fs, :], preferred_element_type=F32)

    in_proj(zg_ref, 0, ngate)
    acc = x2_s[...]
    up(0)
    probs = _sample_pair_probs(hqs_ref[...], ck_ref)
    up(1)
    lb_all = _lower_bound(lb_ref)
    dmax = jnp.zeros((1, HG_WIDTH), F32)
    for c in range(nch):
        rows = slice(c * CHUNK, (c + 1) * CHUNK)
        q, k, bcum = _gates(zg_ref[rows, 0:HG_WIDTH], zg_ref[rows, HG_WIDTH:2 * HG_WIDTH], lb_all, CHUNK)
        q_s[rows, :] = q
        k_s[rows, :] = k
        b_s[rows, :] = bcum
        dmax = jnp.maximum(dmax, _block_decay(bcum, False))
    fast_ok = jnp.max(dmax) < FAST_MAX_DECAY
    acc = down(0, acc)
    in_proj(z_ref, ngate, ngate + 2 * HG_WIDTH)
    acc = down(1, acc)
    _sample_pair_context(probs, cv_ref, cas_ref)
    up(2)
    in_proj(z_ref, ngate + 2 * HG_WIDTH, IN_WIDTH)
    up(3)
    acc = down(2, acc)
    acc = down(3, acc)
    y_ref[0] = _rmsnorm(acc, gf_ref[...])

    def fill_scores(j, fast):
        c, h = divmod(j, HG_HEADS)
        rows = slice(c * CHUNK, (c + 1) * CHUNK)
        cs = slice(h * HG_DK, (h + 1) * HG_DK)
        q, k, bcum = q_s[rows, cs], k_s[rows, cs], b_s[rows, cs]
        a = _scores_fast(q, k, bcum, False) if fast else _scores_exact(q, k, bcum, False, mask)
        a_s[j] = jnp.where(mask, a, 0.0).astype(BF16)

    def head_chunk(j):
        c, h = divmod(j, HG_HEADS)
        rows = slice(c * CHUNK, (c + 1) * CHUNK)
        cs = slice(h * HG_DK, (h + 1) * HG_DK)
        q, k, bcum = q_s[rows, cs], k_s[rows, cs], b_s[rows, cs]
        v = z_ref[rows, h * HG_DK:(h + 1) * HG_DK]
        zg = z_ref[rows, HG_WIDTH + h * HG_DK:HG_WIDTH + (h + 1) * HG_DK]
        eb = jnp.exp(bcum)
        kd = k * jnp.exp(bcum[CHUNK - 1:CHUNK] - bcum)
        st = st_refs[h][...]
        o = jnp.dot(a_s[j], v.astype(BF16), preferred_element_type=F32)
        o = o + _bdot_nt(q * eb, st)
        st_refs[h][...] = st * eb[CHUNK - 1:CHUNK] + _bdot(v.T, kd)
        ocat_ref[rows, cs] = _hgrn_out(o, zg, hgn_ref[:, cs])
        if h == HG_HEADS - 1:
            _gmlp_chunk(z_ref, 2 * HG_WIDTH, rows, lng_ref, lnb_ref, wmix_ref, bcol_ref, mask, ocat_ref,
                        None)

    def region2(fast):
        scale = X_HEAD_DIM ** -0.5
        heads = [slice(h * X_HEAD_DIM, (h + 1) * X_HEAD_DIM) for h in range(X_HEADS)]
        hq = jnp.dot(h1n_s[...], wcq_ref[...], preferred_element_type=F32)
        for j in range(0, 4):
            fill_scores(j, fast)
        probs = [_softmax_rows(_bdot_nt(hq[:, hs], mk_ref[0, :, hs]) * scale) for hs in heads]
        for j in range(4, 8):
            fill_scores(j, fast)
        for j in range(0, 4):
            head_chunk(j)
        for h, hs in enumerate(heads):
            ca_s[:, hs] = _bdot(probs[h], mv_ref[0, :, hs]).astype(BF16)
        for j in range(4, 8):
            head_chunk(j)
        x2 = h1_s[...] + jnp.dot(ca_s[...], wco_ref[...], preferred_element_type=F32)
        h1 = x_ref[0] + jnp.dot(ocat_ref[...], wout_ref[...], preferred_element_type=F32)
        x2_s[...] = x2
        hn_s[...] = _rmsnorm(x2, gm_ref[...]).astype(BF16)
        h1_s[...] = h1
        h1n_s[...] = _rmsnorm(h1, gc_ref[...]).astype(BF16)

    @pl.when(fast_ok)
    def _():
        region2(True)

    @pl.when(jnp.logical_not(fast_ok))
    def _():
        region2(False)

    @pl.when(i == nt - 1)
    def _():
        for h in range(HG_HEADS):
            sout_ref[0, h] = st_refs[h][...].T


def _prompt_layer(x, mk, mv, hqs, ck, cv, lb_param, g_mix, w_in, hg_norm_g, ln_v_g, ln_v_b, wmix, bcol,
                  w_out, g_cross, w_cq, w_co, g_mlp, w_up, w_down, g_final, tm):
    B, L, _ = x.shape
    nt = L // tm
    T = B * nt
    npair = ck.shape[0] // 2
    assert npair <= T + 2

    def cur(t):
        return jnp.minimum(t, T - 1)

    def prev(t):
        return jnp.clip(t - 1, 0, T - 1)

    def prev2(t):
        return jnp.clip(t - 2, 0, T - 1)

    def pair(t):
        return jnp.minimum(t, npair - 1)

    consts = (lb_param, g_mix, w_in, hg_norm_g, ln_v_g, ln_v_b, wmix, bcol, w_out,
              g_cross, w_cq, w_co, g_mlp, w_up, w_down, g_final)
    in_specs = [
        pl.BlockSpec((1, tm, D_MODEL), lambda t: (cur(t) // nt, cur(t) % nt, 0)),
        pl.BlockSpec((1, N_MEM, D_MODEL), lambda t: (prev(t) // nt, 0, 0)),
        pl.BlockSpec((1, N_MEM, D_MODEL), lambda t: (prev(t) // nt, 0, 0)),
        pl.BlockSpec((8, D_MODEL), lambda t: (pair(t), 0)),
        pl.BlockSpec((2, N_MEM, X_HEADS, X_HEAD_DIM), lambda t: (pair(t), 0, 0, 0)),
        pl.BlockSpec((2, N_MEM, X_HEADS, X_HEAD_DIM), lambda t: (pair(t), 0, 0, 0)),
    ] + [_const_spec(a.shape) for a in consts]
    out_specs = [
        pl.BlockSpec((1, tm, D_MODEL), lambda t: (prev2(t) // nt, prev2(t) % nt, 0)),
        pl.BlockSpec((1, HG_HEADS, HG_DK, HG_DK), lambda t: (cur(t) // nt, 0, 0, 0)),
        pl.BlockSpec((8, D_MODEL), lambda t: (pair(t), 0)),
    ]
    scratch = [
        pltpu.VMEM((tm, 2 * HG_WIDTH), F32),
        pltpu.VMEM((tm, IN_WIDTH - 2 * HG_WIDTH), F32),
        pltpu.VMEM((tm, D_MODEL), BF16),
        pltpu.VMEM((tm, HG_WIDTH), F32), pltpu.VMEM((tm, HG_WIDTH), F32),
        pltpu.VMEM((tm, HG_WIDTH), F32),
        pltpu.VMEM((tm // CHUNK * HG_HEADS, CHUNK, CHUNK), BF16),
        pltpu.VMEM((HG_DK, HG_DK), F32), pltpu.VMEM((HG_DK, HG_DK), F32),
        pltpu.VMEM((HG_DK, HG_DK), F32), pltpu.VMEM((HG_DK, HG_DK), F32),
        pltpu.VMEM((tm, D_MODEL), F32),
        pltpu.VMEM((tm, D_MODEL), BF16),
        pltpu.VMEM((tm, D_MODEL), F32),
        pltpu.VMEM((tm, D_MODEL), BF16),
        pltpu.VMEM((tm, D_MODEL), BF16),
        pltpu.VMEM((2, tm, D_MODEL), BF16),
    ]
    return pl.pallas_call(
        functools.partial(_prompt_kernel, tm=tm, nt=nt),
        grid=(T + 2,),
        in_specs=in_specs,
        out_specs=out_specs,
        out_shape=[jax.ShapeDtypeStruct((B, L, D_MODEL), F32),
                   jax.ShapeDtypeStruct((B, HG_HEADS, HG_DK, HG_DK), F32),
                   jax.ShapeDtypeStruct(hqs.shape, F32)],
        scratch_shapes=scratch,
        compiler_params=pltpu.CompilerParams(dimension_semantics=("arbitrary",),
                                             vmem_limit_bytes=VMEM_LIMIT_BYTES),
        name="prompt_layer",
    )(x, mk, mv, hqs, ck, cv, *consts)


def _mix_sample_kernel(x_ref, s0_ref, lb_ref, gmix_ref, win_ref, hgn_ref, lng_ref, lnb_ref, wmix_ref,
                       bcol_ref, wout_ref, gc_ref, wcq_ref, h_ref, sout_ref, vn_ref, hq_ref,
                       z_ref, ocat_ref, q_s, k_s, b_s, wm_s, bc_s, *, tm):
    x = x_ref[0]
    z_ref[...] = jnp.dot(_rmsnorm(x, gmix_ref[...]).astype(BF16), win_ref[...],
                         preferred_element_type=F32)
    lb_all = _lower_bound(lb_ref)
    mask = _chunk_masks(True)
    row = lax.broadcasted_iota(jnp.int32, (CHUNK, CHUNK), 0)
    col = lax.broadcasted_iota(jnp.int32, (CHUNK, CHUNK), 1)
    eye = row == col

    @pl.when(pl.program_id(0) == 0)
    def _():
        pick = (row % 4 == col).astype(BF16)
        for g in range(CM_GROUPS):
            corner = jnp.where((row < 4) & (col < 4), wmix_ref[g], 0.0)
            wm_s[g] = _bdot_nt(_bdot(pick, corner), pick)
        pos = lax.broadcasted_iota(jnp.int32, (CHUNK, CM_GROUPS), 0) % 4
        bc = jnp.zeros((CHUNK, CM_GROUPS), F32)
        for t in range(4):
            bc = jnp.where(pos == t, bcol_ref[t:t + 1, :], bc)
        bc_s[...] = bc
    r16 = lax.broadcasted_iota(jnp.int32, (SUB, HG_DK), 0)

    def gate_body(c, dmax):
        rows = pl.ds(pl.multiple_of(c * CHUNK, CHUNK), CHUNK)
        q, k, bcum = _gates(z_ref[rows, 0:HG_WIDTH], z_ref[rows, HG_WIDTH:2 * HG_WIDTH], lb_all, 4)
        q_s[rows, :] = q
        k_s[rows, :] = k
        b_s[rows, :] = bcum
        return jnp.maximum(dmax, _block_decay(bcum, True))

    dmax = lax.fori_loop(0, tm // CHUNK, gate_body, jnp.zeros((1, HG_WIDTH), F32))
    fast_ok = jnp.max(dmax) < FAST_MAX_DECAY

    def chunk_body(c, carry, *, fast):
        rows = pl.ds(pl.multiple_of(c * CHUNK, CHUNK), CHUNK)
        for h in range(HG_HEADS):
            cs = slice(h * HG_DK, (h + 1) * HG_DK)
            q, k, bcum = q_s[rows, cs], k_s[rows, cs], b_s[rows, cs]
            v = z_ref[rows, 2 * HG_WIDTH + h * HG_DK:2 * HG_WIDTH + (h + 1) * HG_DK]
            zg = z_ref[rows, 3 * HG_WIDTH + h * HG_DK:3 * HG_WIDTH + (h + 1) * HG_DK]
            a = _scores_fast(q, k, bcum, True) if fast else _scores_exact(q, k, bcum, True, mask)
            o = _bdot(jnp.where(mask, a, 0.0), v)
            eb = jnp.exp(bcum)
            qd = q * eb
            o_parts = []
            for gb in range(CHUNK // SUB):
                blk = slice(gb * SUB, (gb + 1) * SUB)
                qd_b, v_b, eb_b, k_b, b_b = qd[blk], v[blk], eb[blk], k[blk], bcum[blk]
                inter = jnp.zeros((SUB, HG_DK), F32)
                for j in range(4):
                    req = c * (CHUNK // 4) + gb * 4 + j
                    s0 = s0_ref[req, h]
                    last = 4 * j + 3
                    inter = jnp.where(r16 // 4 == j, _bdot(qd_b, s0), inter)
                    kd = jnp.where(r16 // 4 == j, k_b * jnp.exp(b_b[last:last + 1] - b_b), 0.0)
                    upd = lax.dot_general(kd.astype(BF16), v_b.astype(BF16),
                                          (((0,), (0,)), ((), ())), preferred_element_type=F32)
                    dcol = jnp.sum(jnp.where(eye, eb_b[last:last + 1], 0.0), axis=-1, keepdims=True)
                    sout_ref[req, h] = dcol * s0 + upd
                o_parts.append(inter)
            o = o + jnp.concatenate(o_parts, axis=0)
            ocat_ref[rows, cs] = _hgrn_out(o, zg, hgn_ref[:, cs])
        _gmlp_chunk(z_ref, 4 * HG_WIDTH, rows, lng_ref, lnb_ref, wm_s, bc_s, mask, ocat_ref, vn_ref)
        return carry

    @pl.when(fast_ok)
    def _():
        lax.fori_loop(0, tm // CHUNK, functools.partial(chunk_body, fast=True), 0)

    @pl.when(jnp.logical_not(fast_ok))
    def _():
        lax.fori_loop(0, tm // CHUNK, functools.partial(chunk_body, fast=False), 0)

    h = x + jnp.dot(ocat_ref[...], wout_ref[...], preferred_element_type=F32)
    h_ref[0] = h
    hq_ref[...] = jnp.dot(_rmsnorm(h, gc_ref[...]).astype(BF16), wcq_ref[...], preferred_element_type=F32)


def _mix_sample(x, s0, lb_param, g_mix, w_in, hg_norm_g, ln_v_g, ln_v_b, wmix, bcol, w_out, g_cross,
                w_cq, tm):
    nt, _, _ = x.shape
    nreq = tm // 4
    consts = (lb_param, g_mix, w_in, hg_norm_g, ln_v_g, ln_v_b, wmix, bcol, w_out, g_cross, w_cq)
    in_specs = [
        pl.BlockSpec((1, tm, D_MODEL), lambda i: (i, 0, 0)),
        pl.BlockSpec((nreq, HG_HEADS, HG_DK, HG_DK), lambda i: (i, 0, 0, 0)),
    ] + [_const_spec(a.shape) for a in consts]
    out_specs = [
        pl.BlockSpec((1, tm, D_MODEL), lambda i: (i, 0, 0)),
        pl.BlockSpec((nreq, HG_HEADS, HG_DK, HG_DK), lambda i: (i, 0, 0, 0)),
        pl.BlockSpec((tm, CM_WIDTH), lambda i: (i, 0)),
        pl.BlockSpec((tm, D_MODEL), lambda i: (i, 0)),
    ]
    return pl.pallas_call(
        functools.partial(_mix_sample_kernel, tm=tm),
        grid=(nt,),
        in_specs=in_specs,
        out_specs=out_specs,
        out_shape=[jax.ShapeDtypeStruct((nt, tm, D_MODEL), F32),
                   jax.ShapeDtypeStruct(s0.shape, F32),
                   jax.ShapeDtypeStruct((nt * tm, CM_WIDTH), F32),
                   jax.ShapeDtypeStruct((nt * tm, D_MODEL), F32)],
        scratch_shapes=[pltpu.VMEM((tm, IN_WIDTH), F32), pltpu.VMEM((tm, D_MODEL), BF16)]
        + [pltpu.VMEM((tm, HG_WIDTH), F32)] * 3
        + [pltpu.VMEM((CM_GROUPS, CHUNK, CHUNK), F32), pltpu.VMEM((CHUNK, CM_GROUPS), F32)],
        compiler_params=pltpu.CompilerParams(dimension_semantics=("arbitrary",),
                                             vmem_limit_bytes=VMEM_LIMIT_BYTES),
        name="mix_sample",
    )(x, s0, *consts)


def _memkv_kernel(m_ref, g_ref, wk_ref, wv_ref, k_ref, v_ref, kb_ref, vb_ref):
    mn = _rmsnorm(m_ref[...], g_ref[...]).astype(BF16)
    k = jnp.dot(mn, wk_ref[...], preferred_element_type=F32)
    v = jnp.dot(mn, wv_ref[...], preferred_element_type=F32)
    k_ref[...] = k
    v_ref[...] = v
    kb_ref[...] = k.astype(BF16)
    vb_ref[...] = v.astype(BF16)


def _memkv(mem, g_mem, w_ck, w_cv, tm):
    n = mem.shape[0]
    return pl.pallas_call(
        _memkv_kernel,
        grid=(n // tm,),
        in_specs=[pl.BlockSpec((tm, D_MODEL), lambda i: (i, 0)), _const_spec(g_mem.shape),
                  _const_spec(w_ck.shape), _const_spec(w_cv.shape)],
        out_specs=[pl.BlockSpec((tm, D_MODEL), lambda i: (i, 0))] * 4,
        out_shape=[jax.ShapeDtypeStruct((n, D_MODEL), F32)] * 2 + [jax.ShapeDtypeStruct((n, D_MODEL), BF16)] * 2,
        compiler_params=pltpu.CompilerParams(dimension_semantics=("arbitrary",),
                                             vmem_limit_bytes=VMEM_LIMIT_BYTES),
        name="memkv",
    )(mem, g_mem, w_ck, w_cv)


def _post_sample_kernel(x_ref, ca_ref, wco_ref, gm_ref, wup_ref, wdn_ref, gf_ref, y_ref, hm_s):
    x = x_ref[0] + jnp.dot(ca_ref[...].astype(BF16), wco_ref[...], preferred_element_type=F32)
    x = x + _mlp(_rmsnorm(x, gm_ref[...]).astype(BF16), wup_ref, wdn_ref, hm_s)
    y_ref[0] = _rmsnorm(x, gf_ref[...])


def _post_sample(x, ca, w_co, g_mlp, w_up, w_down, g_final):
    nt, tm, _ = x.shape
    in_specs = [
        pl.BlockSpec((1, tm, D_MODEL), lambda i: (i, 0, 0)),
        pl.BlockSpec((tm, D_MODEL), lambda i: (i, 0)),
        _const_spec(w_co.shape), _const_spec(g_mlp.shape), _const_spec(w_up.shape),
        _const_spec(w_down.shape), _const_spec(g_final.shape),
    ]
    return pl.pallas_call(
        _post_sample_kernel,
        grid=(nt,),
        in_specs=in_specs,
        out_specs=pl.BlockSpec((1, tm, D_MODEL), lambda i: (i, 0, 0)),
        out_shape=jax.ShapeDtypeStruct((nt, tm, D_MODEL), F32),
        scratch_shapes=[pltpu.VMEM((tm, D_FF), BF16)],
        compiler_params=pltpu.CompilerParams(dimension_semantics=("arbitrary",),
                                             vmem_limit_bytes=VMEM_LIMIT_BYTES),
        name="post_sample",
    )(x, ca, w_co, g_mlp, w_up, w_down, g_final)


def _cast_kernel(*refs):
    n = len(refs) // 2
    for src, dst in zip(refs[:n], refs[n:]):
        dst[...] = src[...].astype(BF16)


def _cast_weights(*ws):
    steps = 8
    specs = [pl.BlockSpec((w.shape[0] // steps, w.shape[1]), lambda i: (i, 0)) for w in ws]
    return pl.pallas_call(
        _cast_kernel,
        grid=(steps,),
        in_specs=specs,
        out_specs=specs,
        out_shape=[jax.ShapeDtypeStruct(w.shape, BF16) for w in ws],
        compiler_params=pltpu.CompilerParams(dimension_semantics=("arbitrary",),
                                             vmem_limit_bytes=VMEM_LIMIT_BYTES),
        name="cast_weights",
    )(*ws)


def kernel(x_prompt, x_sample, mem_prompt, state_hgrn, cache_mem_k, cache_mem_v, lb_param, g_mix,
           w_in, hg_norm_g, ln_v_g, ln_v_b, w_s, b_s, w_out, g_cross, g_mem, w_cq, w_ck, w_cv, w_co,
           g_mlp, w_up, w_down, g_final):
    B, L, _ = x_prompt.shape
    DB, DL, _ = x_sample.shape
    assert DL == 4 and g_mix.shape[0] == 1

    row = lambda a: a.reshape(1, -1)
    win_b, wout_b, wcq_b, wco_b, wck_b, wcv_b, wup_b, wdn_b = _cast_weights(
        w_in[0], w_out[0], w_cq[0], w_co[0], w_ck[0], w_cv[0], w_up[0], w_down[0])
    gmix, hgn, lng, lnb = row(g_mix[0]), row(hg_norm_g[0]), row(ln_v_g[0]), row(ln_v_b[0])
    gcr, gmem, gmlp, gfin = row(g_cross[0]), row(g_mem[0]), row(g_mlp[0]), row(g_final)
    wmix = w_s[0]
    bcol = b_s[0].T

    tm_s = 128
    xs = x_sample.reshape(DB * DL // tm_s, tm_s, D_MODEL)
    h_s, s_s, vn_s, hq_s = _mix_sample(xs, state_hgrn[0], lb_param, gmix, win_b, hgn, lng, lnb, wmix,
                                       bcol, wout_b, gcr, wcq_b, tm=tm_s)

    mk, mv, mk_b, mv_b = _memkv(mem_prompt.reshape(B * N_MEM, D_MODEL), gmem, wck_b, wcv_b, tm=512)
    y_p, s_p, ca_s = _prompt_layer(x_prompt, mk_b.reshape(B, N_MEM, D_MODEL), mv_b.reshape(B, N_MEM, D_MODEL),
                                   hq_s, cache_mem_k[0], cache_mem_v[0],
                                   lb_param, gmix, win_b, hgn, lng, lnb, wmix, bcol, wout_b,
                                   gcr, wcq_b, wco_b, gmlp, wup_b, wdn_b, gfin, tm=256)

    y_s = _post_sample(h_s.reshape(1, DB * DL, D_MODEL), ca_s, wco_b, gmlp, wup_b, wdn_b, gfin)

    return (y_p, y_s.reshape(DB, DL, D_MODEL), s_p[None], s_s[None],
            mk.reshape(1, B, N_MEM, X_HEADS, X_HEAD_DIM), mv.reshape(1, B, N_MEM, X_HEADS, X_HEAD_DIM),
            vn_s.reshape(1, DB, DL, CM_WIDTH))
```

```python
import functools

import jax
import jax.numpy as jnp
from jax import lax
from jax.experimental import pallas as pl
from jax.experimental.pallas import tpu as pltpu

F32 = jnp.float32
BF16 = jnp.bfloat16

D_MODEL = 1024
HG_WIDTH = 512
HG_HEADS = 4
HG_DK = 128
CM_WIDTH = 512
CM_GROUPS = 4
CM_GROUP_DIM = 128
IN_WIDTH = 4 * HG_WIDTH + 2 * CM_WIDTH
N_MEM = 256
X_HEADS = 4
X_HEAD_DIM = 256
D_FF = 4096
EPS = 1e-6

CHUNK = 128
SUB = 16
VMEM_LIMIT_BYTES = 62 * 1024 * 1024
LOG2E = 1.4426950408889634
FAST_BLOCK = 32
FAST_MAX_DECAY = 80.0


def _bdot(a, b):
    return jnp.dot(a.astype(BF16), b.astype(BF16), preferred_element_type=F32)


def _bdot_nt(a, b):
    return lax.dot_general(a.astype(BF16), b.astype(BF16), (((1,), (1,)), ((), ())),
                           preferred_element_type=F32)


def _rmsnorm(x, g):
    ms = jnp.mean(x * x, axis=-1, keepdims=True)
    return x * lax.rsqrt(ms + EPS) * g


def _sigmoid(x):
    return 0.5 * jnp.tanh(0.5 * x) + 0.5


def _gelu(x):
    return 0.5 * x * (1.0 + jnp.tanh(0.7978845608028654 * (x + 0.044715 * (x * x * x))))


def _softmax_rows(s):
    m = jnp.max(s, axis=-1, keepdims=True)
    e = jnp.exp(s - m)
    return e / jnp.sum(e, axis=-1, keepdims=True)


def _seg_cumsum(x, seg):
    n = x.shape[0]
    pos = lax.broadcasted_iota(jnp.int32, x.shape, 0) % seg
    s = 1
    while s < min(seg, 8):
        x = x + jnp.where(pos >= s, pltpu.roll(x, s, 0), 0.0)
        s *= 2
    while s < seg:
        parts = []
        for r0 in range(0, n, seg):
            parts.append(x[r0:r0 + s])
            parts.append(x[r0 + s:r0 + seg] + x[r0:r0 + seg - s])
        x = jnp.concatenate(parts, axis=0)
        s *= 2
    return x


def _lower_bound(lb_ref):
    lbp = lb_ref[...]
    lbe = jnp.exp(lbp - jnp.max(lbp, axis=0, keepdims=True))
    return lbe[0:1] / jnp.sum(lbe, axis=0, keepdims=True)


def _gates(zq, zf, lb, seg):
    th = jnp.tanh(0.5 * zf)
    logf = jnp.log(lb + (1.0 - lb) * (0.5 + 0.5 * th))
    return zq * _sigmoid(zq), (1.0 - lb) * (0.5 - 0.5 * th), _seg_cumsum(logf, seg)


def _block_decay(bcum, sample):
    if sample:
        return jnp.max(-bcum, axis=0, keepdims=True)
    d = -bcum[FAST_BLOCK - 1:FAST_BLOCK]
    for i in range(1, CHUNK // FAST_BLOCK):
        n0, n1 = i * FAST_BLOCK, (i + 1) * FAST_BLOCK
        d = jnp.maximum(d, bcum[n0 - 1:n0] - bcum[n1 - 1:n1])
    return d


def _chunk_masks(sample):
    row = lax.broadcasted_iota(jnp.int32, (CHUNK, CHUNK), 0)
    col = lax.broadcasted_iota(jnp.int32, (CHUNK, CHUNK), 1)
    if sample:
        return (row // 4 == col // 4) & (row >= col)
    return row >= col


def _scores_fast(q, k, bcum, sample):
    if sample:
        return _bdot_nt(q * jnp.exp(bcum), k * jnp.exp(-bcum))
    a_rows = []
    for i in range(CHUNK // FAST_BLOCK):
        n0, n1 = i * FAST_BLOCK, (i + 1) * FAST_BLOCK
        ref_b = bcum[n0 - 1:n0] if i else jnp.zeros((1, HG_DK), F32)
        qt = q[n0:n1] * jnp.exp(bcum[n0:n1] - ref_b)
        kt = k[:n1] * jnp.exp(ref_b - bcum[:n1])
        if n1 < CHUNK:
            kt = jnp.concatenate([kt, jnp.zeros((CHUNK - n1, HG_DK), F32)], axis=0)
        a_rows.append(_bdot_nt(qt, kt))
    return jnp.concatenate(a_rows, axis=0)


def _scores_exact(q, k, bcum, sample, mask):
    col8 = lax.broadcasted_iota(jnp.int32, (8, CHUNK), 1)
    b2 = bcum * LOG2E
    cexp = b2 - jnp.log2(k)
    a_rows = []
    for gb in range(CHUNK // SUB):
        lo = slice(gb * SUB, gb * SUB + 8)
        hi = slice(gb * SUB + 8, (gb + 1) * SUB)
        a_lo = jnp.zeros((8, CHUNK), F32)
        a_hi = jnp.zeros((8, CHUNK), F32)
        for s in range(SUB):
            sg = gb * SUB + s
            c_s = cexp[sg:sg + 1]
            if s < 8:
                p = q[lo] * jnp.exp2(b2[lo] - c_s)
                a_lo = jnp.where(col8 == sg, jnp.sum(p, axis=-1, keepdims=True), a_lo)
            if (not sample) or s >= 8:
                p = q[hi] * jnp.exp2(b2[hi] - c_s)
                a_hi = jnp.where(col8 == sg, jnp.sum(p, axis=-1, keepdims=True), a_hi)
        a_rows.append(a_lo)
        a_rows.append(a_hi)
    a = jnp.concatenate(a_rows, axis=0)
    if sample:
        return a
    o_rows = [jnp.zeros((SUB, CHUNK), F32)]
    for i in range(1, CHUNK // SUB):
        n = i * SUB
        ref_b = bcum[n - 1:n]
        qt = q[n:n + SUB] * jnp.exp(bcum[n:n + SUB] - ref_b)
        kt = k[:n] * jnp.exp(ref_b - bcum[:n])
        kt = jnp.concatenate([kt, jnp.zeros((CHUNK - n, HG_DK), F32)], axis=0)
        o_rows.append(_bdot_nt(qt, kt))
    return jnp.where(mask, a, 0.0) + jnp.concatenate(o_rows, axis=0)


def _hgrn_out(o, zg, g):
    o = o * lax.rsqrt(jnp.mean(o * o, axis=-1, keepdims=True) + EPS) * g
    return (o * _sigmoid(zg)).astype(BF16)


def _gmlp_chunk(z_ref, c0, rows, lng_ref, lnb_ref, wmix_ref, bcol_ref, mask, ocat_ref, vn_ref):
    u = _gelu(z_ref[rows, c0:c0 + CM_WIDTH])
    gv = _gelu(z_ref[rows, c0 + CM_WIDTH:c0 + 2 * CM_WIDTH])
    mu = jnp.mean(gv, axis=-1, keepdims=True)
    dv = gv - mu
    var = jnp.mean(dv * dv, axis=-1, keepdims=True)
    vn = dv * lax.rsqrt(var + EPS) * lng_ref[...] + lnb_ref[...]
    if vn_ref is not None:
        vn_ref[rows, :] = vn
    for g in range(CM_GROUPS):
        gs = slice(g * CM_GROUP_DIM, (g + 1) * CM_GROUP_DIM)
        wm = jnp.where(mask, wmix_ref[g], 0.0)
        mixed = _bdot(wm, vn[:, gs]) + bcol_ref[:, g:g + 1]
        ocat_ref[rows, HG_WIDTH + g * CM_GROUP_DIM:HG_WIDTH + (g + 1) * CM_GROUP_DIM] = (
            u[:, gs] * mixed).astype(BF16)


def _sample_pair_probs(hq8, ck_ref):
    nrow = 8 * X_HEADS
    rh = lax.broadcasted_iota(jnp.int32, (nrow, N_MEM * X_HEADS), 0) // 8
    ch = lax.broadcasted_iota(jnp.int32, (nrow, N_MEM * X_HEADS), 1) % X_HEADS
    q = jnp.concatenate([hq8[:, h * X_HEAD_DIM:(h + 1) * X_HEAD_DIM] for h in range(X_HEADS)],
                        axis=0).astype(BF16)
    probs = []
    for r in range(2):
        k2 = ck_ref[r].reshape(N_MEM * X_HEADS, X_HEAD_DIM)
        s = jnp.where(rh == ch, _bdot_nt(q, k2) * (X_HEAD_DIM ** -0.5), -jnp.inf)
        probs.append(_softmax_rows(s))
    return probs


def _sample_pair_context(probs, cv_ref, ca_ref):
    first = (lax.broadcasted_iota(jnp.int32, (8 * X_HEADS, X_HEAD_DIM), 0) % 8) < 4
    outs = [_bdot(probs[r], cv_ref[r].reshape(N_MEM * X_HEADS, X_HEAD_DIM)) for r in range(2)]
    o = jnp.where(first, outs[0], outs[1])
    for h in range(X_HEADS):
        ca_ref[:, h * X_HEAD_DIM:(h + 1) * X_HEAD_DIM] = o[8 * h:8 * h + 8]


def _mlp(hn_bf16, wup_ref, wdn_ref, hm_s):
    for c in range(D_FF // D_MODEL):
        fs = slice(c * D_MODEL, (c + 1) * D_MODEL)
        hm = jnp.maximum(jnp.dot(hn_bf16, wup_ref[:, fs], preferred_element_type=F32), 0.0)
        hm_s[:, fs] = (hm * hm).astype(BF16)
    return jnp.dot(hm_s[...], wdn_ref[...], preferred_element_type=F32)


def _const_spec(shape):
    nd = len(shape)
    return pl.BlockSpec(shape, lambda *_: (0,) * nd, pipeline_mode=pl.Buffered(1))


def _prompt_kernel(x_ref, mk_ref, mv_ref, hqs_ref, ck_ref, cv_ref, lb_ref, gmix_ref, win_ref, hgn_ref,
                   lng_ref, lnb_ref, wmix_ref, bcol_ref, wout_ref, gc_ref, wcq_ref, wco_ref, gm_ref,
                   wup_ref, wdn_ref, gf_ref, y_ref, sout_ref, cas_ref,
                   zg_ref, z_ref, ocat_ref, q_s, k_s, b_s, a_s, st0, st1, st2, st3, h1_s, h1n_s, x2_s,
                   hn_s, ca_s, hm_s, *, tm, nt):
    t = pl.program_id(0)
    i = lax.rem(t, nt)
    nch = tm // CHUNK
    assert nch * HG_HEADS == 8 and D_FF // D_MODEL == 4
    mask = _chunk_masks(False)
    st_refs = (st0, st1, st2, st3)

    @pl.when(t == 0)
    def _():
        for ref in (h1_s, h1n_s, x2_s, hn_s):
            ref[...] = jnp.zeros_like(ref)

    @pl.when(i == 0)
    def _():
        for st in st_refs:
            st[...] = jnp.zeros_like(st)


    x = x_ref[0]
    xn = _rmsnorm(x, gmix_ref[...]).astype(BF16)
    ngate = 2 * HG_WIDTH

    def in_proj(dst, c0, c1):
        off = ngate if dst is z_ref else 0
        dst[:, c0 - off:c1 - off] = jnp.dot(xn, win_ref[:, c0:c1], preferred_element_type=F32)

    hn = hn_s[...]

    def up(p):
        fs = slice(p * D_MODEL, (p + 1) * D_MODEL)
        hm = jnp.maximum(jnp.dot(hn, wup_ref[:, fs], preferred_element_type=F32), 0.0)
        hm_s[p % 2] = (hm * hm).astype(BF16)

    def down(p, acc):
        fs = slice(p * D_MODEL, (p + 1) * D_MODEL)
        return acc + jnp.dot(hm_s[p % 2], wdn_ref[fs, :], preferred_element_type=F32)

    in_proj(zg_ref, 0, ngate)
    acc = x2_s[...]
    up(0)
    probs = _sample_pair_probs(hqs_ref[...], ck_ref)
    up(1)
    lb_all = _lower_bound(lb_ref)
    dmax = jnp.zeros((1, HG_WIDTH), F32)
    for c in range(nch):
        rows = slice(c * CHUNK, (c + 1) * CHUNK)
        q, k, bcum = _gates(zg_ref[rows, 0:HG_WIDTH], zg_ref[rows, HG_WIDTH:2 * HG_WIDTH], lb_all, CHUNK)
        q_s[rows, :] = q
        k_s[rows, :] = k
        b_s[rows, :] = bcum
        dmax = jnp.maximum(dmax, _block_decay(bcum, False))
    fast_ok = jnp.max(dmax) < FAST_MAX_DECAY
    acc = down(0, acc)
    in_proj(z_ref, ngate, ngate + 2 * HG_WIDTH)
    acc = down(1, acc)
    _sample_pair_context(probs, cv_ref, cas_ref)
    up(2)
    in_proj(z_ref, ngate + 2 * HG_WIDTH, IN_WIDTH)
    up(3)
    acc = down(2, acc)
    acc = down(3, acc)
    y_ref[0] = _rmsnorm(acc, gf_ref[...])

    def fill_scores(j, fast):
        c, h = divmod(j, HG_HEADS)
        rows = slice(c * CHUNK, (c + 1) * CHUNK)
        cs = slice(h * HG_DK, (h + 1) * HG_DK)
        q, k, bcum = q_s[rows, cs], k_s[rows, cs], b_s[rows, cs]
        a = _scores_fast(q, k, bcum, False) if fast else _scores_exact(q, k, bcum, False, mask)
        a_s[j] = jnp.where(mask, a, 0.0).astype(BF16)

    def head_chunk(j):
        c, h = divmod(j, HG_HEADS)
        rows = slice(c * CHUNK, (c + 1) * CHUNK)
        cs = slice(h * HG_DK, (h + 1) * HG_DK)
        q, k, bcum = q_s[rows, cs], k_s[rows, cs], b_s[rows, cs]
        v = z_ref[rows, h * HG_DK:(h + 1) * HG_DK]
        zg = z_ref[rows, HG_WIDTH + h * HG_DK:HG_WIDTH + (h + 1) * HG_DK]
        eb = jnp.exp(bcum)
        kd = k * jnp.exp(bcum[CHUNK - 1:CHUNK] - bcum)
        st = st_refs[h][...]
        o = jnp.dot(a_s[j], v.astype(BF16), preferred_element_type=F32)
        o = o + _bdot_nt(q * eb, st)
        st_refs[h][...] = st * eb[CHUNK - 1:CHUNK] + _bdot(v.T, kd)
        ocat_ref[rows, cs] = _hgrn_out(o, zg, hgn_ref[:, cs])
        if h == HG_HEADS - 1:
            _gmlp_chunk(z_ref, 2 * HG_WIDTH, rows, lng_ref, lnb_ref, wmix_ref, bcol_ref, mask, ocat_ref,
                        None)

    def region2(fast):
        scale = X_HEAD_DIM ** -0.5
        heads = [slice(h * X_HEAD_DIM, (h + 1) * X_HEAD_DIM) for h in range(X_HEADS)]
        hq = jnp.dot(h1n_s[...], wcq_ref[...], preferred_element_type=F32)
        for j in range(0, 4):
            fill_scores(j, fast)
        probs = [_softmax_rows(_bdot_nt(hq[:, hs], mk_ref[0, :, hs]) * scale) for hs in heads]
        for j in range(4, 8):
            fill_scores(j, fast)
        for j in range(0, 4):
            head_chunk(j)
        for h, hs in enumerate(heads):
            ca_s[:, hs] = _bdot(probs[h], mv_ref[0, :, hs]).astype(BF16)
        for j in range(4, 8):
            head_chunk(j)
        x2 = h1_s[...] + jnp.dot(ca_s[...], wco_ref[...], preferred_element_type=F32)
        h1 = x_ref[0] + jnp.dot(ocat_ref[...], wout_ref[...], preferred_element_type=F32)
        x2_s[...] = x2
        hn_s[...] = _rmsnorm(x2, gm_ref[...]).astype(BF16)
        h1_s[...] = h1
        h1n_s[...] = _rmsnorm(h1, gc_ref[...]).astype(BF16)

    @pl.when(fast_ok)
    def _():
        region2(True)

    @pl.when(jnp.logical_not(fast_ok))
    def _():
        region2(False)

    @pl.when(i == nt - 1)
    def _():
        for h in range(HG_HEADS):
            sout_ref[0, h] = st_refs[h][...].T


def _prompt_layer(x, mk, mv, hqs, ck, cv, lb_param, g_mix, w_in, hg_norm_g, ln_v_g, ln_v_b, wmix, bcol,
                  w_out, g_cross, w_cq, w_co, g_mlp, w_up, w_down, g_final, tm):
    B, L, _ = x.shape
    nt = L // tm
    T = B * nt
    npair = ck.shape[0] // 2
    assert npair <= T + 2

    def cur(t):
        return jnp.minimum(t, T - 1)

    def prev(t):
        return jnp.clip(t - 1, 0, T - 1)

    def prev2(t):
        return jnp.clip(t - 2, 0, T - 1)

    def pair(t):
        return jnp.minimum(t, npair - 1)

    consts = (lb_param, g_mix, w_in, hg_norm_g, ln_v_g, ln_v_b, wmix, bcol, w_out,
              g_cross, w_cq, w_co, g_mlp, w_up, w_down, g_final)
    in_specs = [
        pl.BlockSpec((1, tm, D_MODEL), lambda t: (cur(t) // nt, cur(t) % nt, 0)),
        pl.BlockSpec((1, N_MEM, D_MODEL), lambda t: (prev(t) // nt, 0, 0)),
        pl.BlockSpec((1, N_MEM, D_MODEL), lambda t: (prev(t) // nt, 0, 0)),
        pl.BlockSpec((8, D_MODEL), lambda t: (pair(t), 0)),
        pl.BlockSpec((2, N_MEM, X_HEADS, X_HEAD_DIM), lambda t: (pair(t), 0, 0, 0)),
        pl.BlockSpec((2, N_MEM, X_HEADS, X_HEAD_DIM), lambda t: (pair(t), 0, 0, 0)),
    ] + [_const_spec(a.shape) for a in consts]
    out_specs = [
        pl.BlockSpec((1, tm, D_MODEL), lambda t: (prev2(t) // nt, prev2(t) % nt, 0)),
        pl.BlockSpec((1, HG_HEADS, HG_DK, HG_DK), lambda t: (cur(t) // nt, 0, 0, 0)),
        pl.BlockSpec((8, D_MODEL), lambda t: (pair(t), 0)),
    ]
    scratch = [
        pltpu.VMEM((tm, 2 * HG_WIDTH), F32),
        pltpu.VMEM((tm, IN_WIDTH - 2 * HG_WIDTH), F32),
        pltpu.VMEM((tm, D_MODEL), BF16),
        pltpu.VMEM((tm, HG_WIDTH), F32), pltpu.VMEM((tm, HG_WIDTH), F32),
        pltpu.VMEM((tm, HG_WIDTH), F32),
        pltpu.VMEM((tm // CHUNK * HG_HEADS, CHUNK, CHUNK), BF16),
        pltpu.VMEM((HG_DK, HG_DK), F32), pltpu.VMEM((HG_DK, HG_DK), F32),
        pltpu.VMEM((HG_DK, HG_DK), F32), pltpu.VMEM((HG_DK, HG_DK), F32),
        pltpu.VMEM((tm, D_MODEL), F32),
        pltpu.VMEM((tm, D_MODEL), BF16),
        pltpu.VMEM((tm, D_MODEL), F32),
        pltpu.VMEM((tm, D_MODEL), BF16),
        pltpu.VMEM((tm, D_MODEL), BF16),
        pltpu.VMEM((2, tm, D_MODEL), BF16),
    ]
    return pl.pallas_call(
        functools.partial(_prompt_kernel, tm=tm, nt=nt),
        grid=(T + 2,),
        in_specs=in_specs,
        out_specs=out_specs,
        out_shape=[jax.ShapeDtypeStruct((B, L, D_MODEL), F32),
                   jax.ShapeDtypeStruct((B, HG_HEADS, HG_DK, HG_DK), F32),
                   jax.ShapeDtypeStruct(hqs.shape, F32)],
        scratch_shapes=scratch,
        compiler_params=pltpu.CompilerParams(dimension_semantics=("arbitrary",),
                                             vmem_limit_bytes=VMEM_LIMIT_BYTES),
        name="prompt_layer",
    )(x, mk, mv, hqs, ck, cv, *consts)


def _mix_sample_kernel(x_ref, s0_ref, lb_ref, gmix_ref, win_ref, hgn_ref, lng_ref, lnb_ref, wmix_ref,
                       bcol_ref, wout_ref, gc_ref, wcq_ref, h_ref, sout_ref, vn_ref, hq_ref,
                       z_ref, ocat_ref, q_s, k_s, b_s, wm_s, bc_s, *, tm):
    x = x_ref[0]
    z_ref[...] = jnp.dot(_rmsnorm(x, gmix_ref[...]).astype(BF16), win_ref[...],
                         preferred_element_type=F32)
    lb_all = _lower_bound(lb_ref)
    mask = _chunk_masks(True)
    row = lax.broadcasted_iota(jnp.int32, (CHUNK, CHUNK), 0)
    col = lax.broadcasted_iota(jnp.int32, (CHUNK, CHUNK), 1)
    eye = row == col

    @pl.when(pl.program_id(0) == 0)
    def _():
        pick = (row % 4 == col).astype(BF16)
        for g in range(CM_GROUPS):
            corner = jnp.where((row < 4) & (col < 4), wmix_ref[g], 0.0)
            wm_s[g] = _bdot_nt(_bdot(pick, corner), pick)
        pos = lax.broadcasted_iota(jnp.int32, (CHUNK, CM_GROUPS), 0) % 4
        bc = jnp.zeros((CHUNK, CM_GROUPS), F32)
        for t in range(4):
            bc = jnp.where(pos == t, bcol_ref[t:t + 1, :], bc)
        bc_s[...] = bc
    r16 = lax.broadcasted_iota(jnp.int32, (SUB, HG_DK), 0)

    def gate_body(c, dmax):
        rows = pl.ds(pl.multiple_of(c * CHUNK, CHUNK), CHUNK)
        q, k, bcum = _gates(z_ref[rows, 0:HG_WIDTH], z_ref[rows, HG_WIDTH:2 * HG_WIDTH], lb_all, 4)
        q_s[rows, :] = q
        k_s[rows, :] = k
        b_s[rows, :] = bcum
        return jnp.maximum(dmax, _block_decay(bcum, True))

    dmax = lax.fori_loop(0, tm // CHUNK, gate_body, jnp.zeros((1, HG_WIDTH), F32))
    fast_ok = jnp.max(dmax) < FAST_MAX_DECAY

    def chunk_body(c, carry, *, fast):
        rows = pl.ds(pl.multiple_of(c * CHUNK, CHUNK), CHUNK)
        for h in range(HG_HEADS):
            cs = slice(h * HG_DK, (h + 1) * HG_DK)
            q, k, bcum = q_s[rows, cs], k_s[rows, cs], b_s[rows, cs]
            v = z_ref[rows, 2 * HG_WIDTH + h * HG_DK:2 * HG_WIDTH + (h + 1) * HG_DK]
            zg = z_ref[rows, 3 * HG_WIDTH + h * HG_DK:3 * HG_WIDTH + (h + 1) * HG_DK]
            a = _scores_fast(q, k, bcum, True) if fast else _scores_exact(q, k, bcum, True, mask)
            o = _bdot(jnp.where(mask, a, 0.0), v)
            eb = jnp.exp(bcum)
            qd = q * eb
            o_parts = []
            for gb in range(CHUNK // SUB):
                blk = slice(gb * SUB, (gb + 1) * SUB)
                qd_b, v_b, eb_b, k_b, b_b = qd[blk], v[blk], eb[blk], k[blk], bcum[blk]
                inter = jnp.zeros((SUB, HG_DK), F32)
                for j in range(4):
                    req = c * (CHUNK // 4) + gb * 4 + j
                    s0 = s0_ref[req, h]
                    last = 4 * j + 3
                    inter = jnp.where(r16 // 4 == j, _bdot(qd_b, s0), inter)
                    kd = jnp.where(r16 // 4 == j, k_b * jnp.exp(b_b[last:last + 1] - b_b), 0.0)
                    upd = lax.dot_general(kd.astype(BF16), v_b.astype(BF16),
                                          (((0,), (0,)), ((), ())), preferred_element_type=F32)
                    dcol = jnp.sum(jnp.where(eye, eb_b[last:last + 1], 0.0), axis=-1, keepdims=True)
                    sout_ref[req, h] = dcol * s0 + upd
                o_parts.append(inter)
            o = o + jnp.concatenate(o_parts, axis=0)
            ocat_ref[rows, cs] = _hgrn_out(o, zg, hgn_ref[:, cs])
        _gmlp_chunk(z_ref, 4 * HG_WIDTH, rows, lng_ref, lnb_ref, wm_s, bc_s, mask, ocat_ref, vn_ref)
        return carry

    @pl.when(fast_ok)
    def _():
        lax.fori_loop(0, tm // CHUNK, functools.partial(chunk_body, fast=True), 0)

    @pl.when(jnp.logical_not(fast_ok))
    def _():
        lax.fori_loop(0, tm // CHUNK, functools.partial(chunk_body, fast=False), 0)

    h = x + jnp.dot(ocat_ref[...], wout_ref[...], preferred_element_type=F32)
    h_ref[0] = h
    hq_ref[...] = jnp.dot(_rmsnorm(h, gc_ref[...]).astype(BF16), wcq_ref[...], preferred_element_type=F32)


def _mix_sample(x, s0, lb_param, g_mix, w_in, hg_norm_g, ln_v_g, ln_v_b, wmix, bcol, w_out, g_cross,
                w_cq, tm):
    nt, _, _ = x.shape
    nreq = tm // 4
    consts = (lb_param, g_mix, w_in, hg_norm_g, ln_v_g, ln_v_b, wmix, bcol, w_out, g_cross, w_cq)
    in_specs = [
        pl.BlockSpec((1, tm, D_MODEL), lambda i: (i, 0, 0)),
        pl.BlockSpec((nreq, HG_HEADS, HG_DK, HG_DK), lambda i: (i, 0, 0, 0)),
    ] + [_const_spec(a.shape) for a in consts]
    out_specs = [
        pl.BlockSpec((1, tm, D_MODEL), lambda i: (i, 0, 0)),
        pl.BlockSpec((nreq, HG_HEADS, HG_DK, HG_DK), lambda i: (i, 0, 0, 0)),
        pl.BlockSpec((tm, CM_WIDTH), lambda i: (i, 0)),
        pl.BlockSpec((tm, D_MODEL), lambda i: (i, 0)),
    ]
    return pl.pallas_call(
        functools.partial(_mix_sample_kernel, tm=tm),
        grid=(nt,),
        in_specs=in_specs,
        out_specs=out_specs,
        out_shape=[jax.ShapeDtypeStruct((nt, tm, D_MODEL), F32),
                   jax.ShapeDtypeStruct(s0.shape, F32),
                   jax.ShapeDtypeStruct((nt * tm, CM_WIDTH), F32),
                   jax.ShapeDtypeStruct((nt * tm, D_MODEL), F32)],
        scratch_shapes=[pltpu.VMEM((tm, IN_WIDTH), F32), pltpu.VMEM((tm, D_MODEL), BF16)]
        + [pltpu.VMEM((tm, HG_WIDTH), F32)] * 3
        + [pltpu.VMEM((CM_GROUPS, CHUNK, CHUNK), F32), pltpu.VMEM((CHUNK, CM_GROUPS), F32)],
        compiler_params=pltpu.CompilerParams(dimension_semantics=("arbitrary",),
                                             vmem_limit_bytes=VMEM_LIMIT_BYTES),
        name="mix_sample",
    )(x, s0, *consts)


def _memkv_kernel(m_ref, g_ref, wk_ref, wv_ref, k_ref, v_ref, kb_ref, vb_ref):
    mn = _rmsnorm(m_ref[...], g_ref[...]).astype(BF16)
    k = jnp.dot(mn, wk_ref[...], preferred_element_type=F32)
    v = jnp.dot(mn, wv_ref[...], preferred_element_type=F32)
    kb_ref[...] = k.astype(BF16)
    vb_ref[...] = v.astype(BF16)
    for src, dst in ((k, k_ref), (v, v_ref)):
        for r in range(dst.shape[0]):
            rows = src[r * N_MEM:(r + 1) * N_MEM]
            dst[r] = jnp.stack([rows[:, h * X_HEAD_DIM:(h + 1) * X_HEAD_DIM] for h in range(X_HEADS)],
                               axis=1)


def _memkv(mem, g_mem, w_ck, w_cv, tm):
    n = mem.shape[0]
    return pl.pallas_call(
        _memkv_kernel,
        grid=(n // tm,),
        in_specs=[pl.BlockSpec((tm, D_MODEL), lambda i: (i, 0)), _const_spec(g_mem.shape),
                  _const_spec(w_ck.shape), _const_spec(w_cv.shape)],
        out_specs=[pl.BlockSpec((tm // N_MEM, N_MEM, X_HEADS, X_HEAD_DIM), lambda i: (i, 0, 0, 0))] * 2
        + [pl.BlockSpec((tm, D_MODEL), lambda i: (i, 0))] * 2,
        out_shape=[jax.ShapeDtypeStruct((n // N_MEM, N_MEM, X_HEADS, X_HEAD_DIM), F32)] * 2
        + [jax.ShapeDtypeStruct((n, D_MODEL), BF16)] * 2,
        compiler_params=pltpu.CompilerParams(dimension_semantics=("arbitrary",),
                                             vmem_limit_bytes=VMEM_LIMIT_BYTES),
        name="memkv",
    )(mem, g_mem, w_ck, w_cv)


def _post_sample_kernel(x_ref, ca_ref, wco_ref, gm_ref, wup_ref, wdn_ref, gf_ref, y_ref, hm_s):
    x = x_ref[0] + jnp.dot(ca_ref[...].astype(BF16), wco_ref[...], preferred_element_type=F32)
    x = x + _mlp(_rmsnorm(x, gm_ref[...]).astype(BF16), wup_ref, wdn_ref, hm_s)
    y_ref[0] = _rmsnorm(x, gf_ref[...])


def _post_sample(x, ca, w_co, g_mlp, w_up, w_down, g_final):
    nt, tm, _ = x.shape
    in_specs = [
        pl.BlockSpec((1, tm, D_MODEL), lambda i: (i, 0, 0)),
        pl.BlockSpec((tm, D_MODEL), lambda i: (i, 0)),
        _const_spec(w_co.shape), _const_spec(g_mlp.shape), _const_spec(w_up.shape),
        _const_spec(w_down.shape), _const_spec(g_final.shape),
    ]
    return pl.pallas_call(
        _post_sample_kernel,
        grid=(nt,),
        in_specs=in_specs,
        out_specs=pl.BlockSpec((1, tm, D_MODEL), lambda i: (i, 0, 0)),
        out_shape=jax.ShapeDtypeStruct((nt, tm, D_MODEL), F32),
        scratch_shapes=[pltpu.VMEM((tm, D_FF), BF16)],
        compiler_params=pltpu.CompilerParams(dimension_semantics=("arbitrary",),
                                             vmem_limit_bytes=VMEM_LIMIT_BYTES),
        name="post_sample",
    )(x, ca, w_co, g_mlp, w_up, w_down, g_final)


def _cast_kernel(*refs):
    n = len(refs) // 2
    for src, dst in zip(refs[:n], refs[n:]):
        dst[...] = src[...].astype(BF16)


def _cast_weights(*ws):
    steps = 8
    specs = [pl.BlockSpec((w.shape[0] // steps, w.shape[1]), lambda i: (i, 0)) for w in ws]
    return pl.pallas_call(
        _cast_kernel,
        grid=(steps,),
        in_specs=specs,
        out_specs=specs,
        out_shape=[jax.ShapeDtypeStruct(w.shape, BF16) for w in ws],
        compiler_params=pltpu.CompilerParams(dimension_semantics=("arbitrary",),
                                             vmem_limit_bytes=VMEM_LIMIT_BYTES),
        name="cast_weights",
    )(*ws)


def kernel(x_prompt, x_sample, mem_prompt, state_hgrn, cache_mem_k, cache_mem_v, lb_param, g_mix,
           w_in, hg_norm_g, ln_v_g, ln_v_b, w_s, b_s, w_out, g_cross, g_mem, w_cq, w_ck, w_cv, w_co,
           g_mlp, w_up, w_down, g_final):
    B, L, _ = x_prompt.shape
    DB, DL, _ = x_sample.shape
    assert DL == 4 and g_mix.shape[0] == 1

    row = lambda a: a.reshape(1, -1)
    win_b, wout_b, wcq_b, wco_b, wck_b, wcv_b, wup_b, wdn_b = _cast_weights(
        w_in[0], w_out[0], w_cq[0], w_co[0], w_ck[0], w_cv[0], w_up[0], w_down[0])
    gmix, hgn, lng, lnb = row(g_mix[0]), row(hg_norm_g[0]), row(ln_v_g[0]), row(ln_v_b[0])
    gcr, gmem, gmlp, gfin = row(g_cross[0]), row(g_mem[0]), row(g_mlp[0]), row(g_final)
    wmix = w_s[0]
    bcol = b_s[0].T

    tm_s = 128
    xs = x_sample.reshape(DB * DL // tm_s, tm_s, D_MODEL)
    h_s, s_s, vn_s, hq_s = _mix_sample(xs, state_hgrn[0], lb_param, gmix, win_b, hgn, lng, lnb, wmix,
                                       bcol, wout_b, gcr, wcq_b, tm=tm_s)

    mk, mv, mk_b, mv_b = _memkv(mem_prompt.reshape(B * N_MEM, D_MODEL), gmem, wck_b, wcv_b, tm=512)
    y_p, s_p, ca_s = _prompt_layer(x_prompt, mk_b.reshape(B, N_MEM, D_MODEL), mv_b.reshape(B, N_MEM, D_MODEL),
                                   hq_s, cache_mem_k[0], cache_mem_v[0],
                                   lb_param, gmix, win_b, hgn, lng, lnb, wmix, bcol, wout_b,
                                   gcr, wcq_b, wco_b, gmlp, wup_b, wdn_b, gfin, tm=256)

    y_s = _post_sample(h_s.reshape(1, DB * DL, D_MODEL), ca_s, wco_b, gmlp, wup_b, wdn_b, gfin)

    return (y_p, y_s.reshape(DB, DL, D_MODEL), s_p[None], s_s[None],
            mk[None], mv[None],
            vn_s.reshape(1, DB, DL, CM_WIDTH))
```

```python
import functools

import jax
import jax.numpy as jnp
from jax import lax
from jax.experimental import pallas as pl
from jax.experimental.pallas import tpu as pltpu

F32 = jnp.float32
BF16 = jnp.bfloat16

D_MODEL = 1024
HG_WIDTH = 512
HG_HEADS = 4
HG_DK = 128
CM_WIDTH = 512
CM_GROUPS = 4
CM_GROUP_DIM = 128
IN_WIDTH = 4 * HG_WIDTH + 2 * CM_WIDTH
N_MEM = 256
X_HEADS = 4
X_HEAD_DIM = 256
D_FF = 4096
EPS = 1e-6

CHUNK = 128
SUB = 16
VMEM_LIMIT_BYTES = 62 * 1024 * 1024
LOG2E = 1.4426950408889634
FAST_BLOCK = 32
FAST_MAX_DECAY = 80.0


def _bdot(a, b):
    return jnp.dot(a.astype(BF16), b.astype(BF16), preferred_element_type=F32)


def _bdot_nt(a, b):
    return lax.dot_general(a.astype(BF16), b.astype(BF16), (((1,), (1,)), ((), ())),
                           preferred_element_type=F32)


def _rmsnorm(x, g):
    ms = jnp.mean(x * x, axis=-1, keepdims=True)
    return x * lax.rsqrt(ms + EPS) * g


def _sigmoid(x):
    return 0.5 * jnp.tanh(0.5 * x) + 0.5


def _gelu(x):
    return 0.5 * x * (1.0 + jnp.tanh(0.7978845608028654 * (x + 0.044715 * (x * x * x))))


def _softmax_rows(s):
    m = jnp.max(s, axis=-1, keepdims=True)
    e = jnp.exp(s - m)
    return e / jnp.sum(e, axis=-1, keepdims=True)


def _seg_cumsum(x, seg):
    n = x.shape[0]
    pos = lax.broadcasted_iota(jnp.int32, x.shape, 0) % seg
    s = 1
    while s < min(seg, 8):
        x = x + jnp.where(pos >= s, pltpu.roll(x, s, 0), 0.0)
        s *= 2
    while s < seg:
        parts = []
        for r0 in range(0, n, seg):
            parts.append(x[r0:r0 + s])
            parts.append(x[r0 + s:r0 + seg] + x[r0:r0 + seg - s])
        x = jnp.concatenate(parts, axis=0)
        s *= 2
    return x


def _lower_bound(lb_ref):
    lbp = lb_ref[...]
    lbe = jnp.exp(lbp - jnp.max(lbp, axis=0, keepdims=True))
    return lbe[0:1] / jnp.sum(lbe, axis=0, keepdims=True)


def _gates(zq, zf, lb, seg):
    th = jnp.tanh(0.5 * zf)
    logf = jnp.log(lb + (1.0 - lb) * (0.5 + 0.5 * th))
    return zq * _sigmoid(zq), (1.0 - lb) * (0.5 - 0.5 * th), _seg_cumsum(logf, seg)


def _block_decay(bcum, sample):
    if sample:
        return jnp.max(-bcum, axis=0, keepdims=True)
    d = -bcum[FAST_BLOCK - 1:FAST_BLOCK]
    for i in range(1, CHUNK // FAST_BLOCK):
        n0, n1 = i * FAST_BLOCK, (i + 1) * FAST_BLOCK
        d = jnp.maximum(d, bcum[n0 - 1:n0] - bcum[n1 - 1:n1])
    return d


def _chunk_masks(sample):
    row = lax.broadcasted_iota(jnp.int32, (CHUNK, CHUNK), 0)
    col = lax.broadcasted_iota(jnp.int32, (CHUNK, CHUNK), 1)
    if sample:
        return (row // 4 == col // 4) & (row >= col)
    return row >= col


def _scores_fast(q, k, bcum, sample):
    if sample:
        return _bdot_nt(q * jnp.exp(bcum), k * jnp.exp(-bcum))
    a_rows = []
    for i in range(CHUNK // FAST_BLOCK):
        n0, n1 = i * FAST_BLOCK, (i + 1) * FAST_BLOCK
        ref_b = bcum[n0 - 1:n0] if i else jnp.zeros((1, HG_DK), F32)
        qt = q[n0:n1] * jnp.exp(bcum[n0:n1] - ref_b)
        kt = k[:n1] * jnp.exp(ref_b - bcum[:n1])
        if n1 < CHUNK:
            kt = jnp.concatenate([kt, jnp.zeros((CHUNK - n1, HG_DK), F32)], axis=0)
        a_rows.append(_bdot_nt(qt, kt))
    return jnp.concatenate(a_rows, axis=0)


def _scores_exact(q, k, bcum, sample, mask):
    col8 = lax.broadcasted_iota(jnp.int32, (8, CHUNK), 1)
    b2 = bcum * LOG2E
    cexp = b2 - jnp.log2(k)
    a_rows = []
    for gb in range(CHUNK // SUB):
        lo = slice(gb * SUB, gb * SUB + 8)
        hi = slice(gb * SUB + 8, (gb + 1) * SUB)
        a_lo = jnp.zeros((8, CHUNK), F32)
        a_hi = jnp.zeros((8, CHUNK), F32)
        for s in range(SUB):
            sg = gb * SUB + s
            c_s = cexp[sg:sg + 1]
            if s < 8:
                p = q[lo] * jnp.exp2(b2[lo] - c_s)
                a_lo = jnp.where(col8 == sg, jnp.sum(p, axis=-1, keepdims=True), a_lo)
            if (not sample) or s >= 8:
                p = q[hi] * jnp.exp2(b2[hi] - c_s)
                a_hi = jnp.where(col8 == sg, jnp.sum(p, axis=-1, keepdims=True), a_hi)
        a_rows.append(a_lo)
        a_rows.append(a_hi)
    a = jnp.concatenate(a_rows, axis=0)
    if sample:
        return a
    o_rows = [jnp.zeros((SUB, CHUNK), F32)]
    for i in range(1, CHUNK // SUB):
        n = i * SUB
        ref_b = bcum[n - 1:n]
        qt = q[n:n + SUB] * jnp.exp(bcum[n:n + SUB] - ref_b)
        kt = k[:n] * jnp.exp(ref_b - bcum[:n])
        kt = jnp.concatenate([kt, jnp.zeros((CHUNK - n, HG_DK), F32)], axis=0)
        o_rows.append(_bdot_nt(qt, kt))
    return jnp.where(mask, a, 0.0) + jnp.concatenate(o_rows, axis=0)


def _hgrn_out(o, zg, g):
    o = o * lax.rsqrt(jnp.mean(o * o, axis=-1, keepdims=True) + EPS) * g
    return (o * _sigmoid(zg)).astype(BF16)


def _gmlp_chunk(z_ref, c0, rows, lng_ref, lnb_ref, wmix_ref, bcol_ref, mask, ocat_ref, vn_ref):
    u = _gelu(z_ref[rows, c0:c0 + CM_WIDTH])
    gv = _gelu(z_ref[rows, c0 + CM_WIDTH:c0 + 2 * CM_WIDTH])
    mu = jnp.mean(gv, axis=-1, keepdims=True)
    dv = gv - mu
    var = jnp.mean(dv * dv, axis=-1, keepdims=True)
    vn = dv * lax.rsqrt(var + EPS) * lng_ref[...] + lnb_ref[...]
    if vn_ref is not None:
        vn_ref[rows, :] = vn
    for g in range(CM_GROUPS):
        gs = slice(g * CM_GROUP_DIM, (g + 1) * CM_GROUP_DIM)
        wm = jnp.where(mask, wmix_ref[g], 0.0)
        mixed = _bdot(wm, vn[:, gs]) + bcol_ref[:, g:g + 1]
        ocat_ref[rows, HG_WIDTH + g * CM_GROUP_DIM:HG_WIDTH + (g + 1) * CM_GROUP_DIM] = (
            u[:, gs] * mixed).astype(BF16)


def _sample_pair_probs(hq8, ck_ref):
    nrow = 8 * X_HEADS
    rh = lax.broadcasted_iota(jnp.int32, (nrow, N_MEM * X_HEADS), 0) // 8
    ch = lax.broadcasted_iota(jnp.int32, (nrow, N_MEM * X_HEADS), 1) % X_HEADS
    q = jnp.concatenate([hq8[:, h * X_HEAD_DIM:(h + 1) * X_HEAD_DIM] for h in range(X_HEADS)],
                        axis=0).astype(BF16)
    probs = []
    for r in range(2):
        k2 = ck_ref[r].reshape(N_MEM * X_HEADS, X_HEAD_DIM)
        s = jnp.where(rh == ch, _bdot_nt(q, k2) * (X_HEAD_DIM ** -0.5), -jnp.inf)
        probs.append(_softmax_rows(s))
    return probs


def _sample_pair_context(probs, cv_ref, ca_ref):
    first = (lax.broadcasted_iota(jnp.int32, (8 * X_HEADS, X_HEAD_DIM), 0) % 8) < 4
    outs = [_bdot(probs[r], cv_ref[r].reshape(N_MEM * X_HEADS, X_HEAD_DIM)) for r in range(2)]
    o = jnp.where(first, outs[0], outs[1])
    for h in range(X_HEADS):
        ca_ref[:, h * X_HEAD_DIM:(h + 1) * X_HEAD_DIM] = o[8 * h:8 * h + 8]


def _mlp(hn_bf16, wup_ref, wdn_ref, hm_s):
    for c in range(D_FF // D_MODEL):
        fs = slice(c * D_MODEL, (c + 1) * D_MODEL)
        hm = jnp.maximum(jnp.dot(hn_bf16, wup_ref[:, fs], preferred_element_type=F32), 0.0)
        hm_s[:, fs] = (hm * hm).astype(BF16)
    return jnp.dot(hm_s[...], wdn_ref[...], preferred_element_type=F32)


def _const_spec(shape):
    nd = len(shape)
    return pl.BlockSpec(shape, lambda *_: (0,) * nd, pipeline_mode=pl.Buffered(1))


def _prompt_kernel(x_ref, mk_ref, mv_ref, hqs_ref, ck_ref, cv_ref, lb_ref, gmix_ref, win_ref, hgn_ref,
                   lng_ref, lnb_ref, wmix_ref, bcol_ref, wout_ref, gc_ref, wcq_ref, wco_ref, gm_ref,
                   wup_ref, wdn_ref, gf_ref, y_ref, sout_ref, cas_ref,
                   zg_ref, z_ref, ocat_ref, q_s, k_s, b_s, a_s, st0, st1, st2, st3, h1_s, h1n_s, x2_s,
                   hn_s, ca_s, hm_s, *, tm, nt):
    t = pl.program_id(0)
    i = lax.rem(t, nt)
    nch = tm // CHUNK
    assert nch * HG_HEADS == 8 and D_FF // D_MODEL == 4
    mask = _chunk_masks(False)
    st_refs = (st0, st1, st2, st3)

    @pl.when(t == 0)
    def _():
        h1_s[...] = jnp.zeros_like(h1_s)
        h1n_s[...] = jnp.zeros_like(h1n_s)

    @pl.when(i == 0)
    def _():
        for st in st_refs:
            st[...] = jnp.zeros_like(st)


    scale = X_HEAD_DIM ** -0.5
    heads = [slice(h * X_HEAD_DIM, (h + 1) * X_HEAD_DIM) for h in range(X_HEADS)]
    hq = jnp.dot(h1n_s[...], wcq_ref[...], preferred_element_type=F32)

    x = x_ref[0]
    xn = _rmsnorm(x, gmix_ref[...]).astype(BF16)
    ngate = 2 * HG_WIDTH

    def in_proj(dst, c0, c1):
        off = ngate if dst is z_ref else 0
        dst[:, c0 - off:c1 - off] = jnp.dot(xn, win_ref[:, c0:c1], preferred_element_type=F32)

    in_proj(zg_ref, 0, ngate)
    probs = [_softmax_rows(_bdot_nt(hq[:, hs], mk_ref[0, :, hs]) * scale) for hs in heads]
    in_proj(z_ref, ngate, ngate + 2 * HG_WIDTH)
    for h, hs in enumerate(heads):
        ca_s[:, hs] = _bdot(probs[h], mv_ref[0, :, hs]).astype(BF16)
    in_proj(z_ref, ngate + 2 * HG_WIDTH, ngate + 2 * HG_WIDTH + CM_WIDTH)

    lb_all = _lower_bound(lb_ref)
    dmax = jnp.zeros((1, HG_WIDTH), F32)
    for c in range(nch):
        rows = slice(c * CHUNK, (c + 1) * CHUNK)
        q, k, bcum = _gates(zg_ref[rows, 0:HG_WIDTH], zg_ref[rows, HG_WIDTH:2 * HG_WIDTH], lb_all, CHUNK)
        q_s[rows, :] = q
        k_s[rows, :] = k
        b_s[rows, :] = bcum
        dmax = jnp.maximum(dmax, _block_decay(bcum, False))
    fast_ok = jnp.max(dmax) < FAST_MAX_DECAY

    x2 = h1_s[...] + jnp.dot(ca_s[...], wco_ref[...], preferred_element_type=F32)
    in_proj(z_ref, ngate + 2 * HG_WIDTH + CM_WIDTH, IN_WIDTH)
    x2_s[...] = x2
    hn_s[...] = _rmsnorm(x2, gm_ref[...]).astype(BF16)

    def fill_scores(j, fast):
        c, h = divmod(j, HG_HEADS)
        rows = slice(c * CHUNK, (c + 1) * CHUNK)
        cs = slice(h * HG_DK, (h + 1) * HG_DK)
        q, k, bcum = q_s[rows, cs], k_s[rows, cs], b_s[rows, cs]
        a = _scores_fast(q, k, bcum, False) if fast else _scores_exact(q, k, bcum, False, mask)
        a_s[j] = jnp.where(mask, a, 0.0).astype(BF16)

    def head_chunk(j):
        c, h = divmod(j, HG_HEADS)
        rows = slice(c * CHUNK, (c + 1) * CHUNK)
        cs = slice(h * HG_DK, (h + 1) * HG_DK)
        q, k, bcum = q_s[rows, cs], k_s[rows, cs], b_s[rows, cs]
        v = z_ref[rows, h * HG_DK:(h + 1) * HG_DK]
        zg = z_ref[rows, HG_WIDTH + h * HG_DK:HG_WIDTH + (h + 1) * HG_DK]
        eb = jnp.exp(bcum)
        kd = k * jnp.exp(bcum[CHUNK - 1:CHUNK] - bcum)
        st = st_refs[h][...]
        o = jnp.dot(a_s[j], v.astype(BF16), preferred_element_type=F32)
        o = o + _bdot_nt(q * eb, st)
        st_refs[h][...] = st * eb[CHUNK - 1:CHUNK] + _bdot(v.T, kd)
        ocat_ref[rows, cs] = _hgrn_out(o, zg, hgn_ref[:, cs])
        if h == HG_HEADS - 1:
            _gmlp_chunk(z_ref, 2 * HG_WIDTH, rows, lng_ref, lnb_ref, wmix_ref, bcol_ref, mask, ocat_ref,
                        None)

    def region2(fast):
        hn = hn_s[...]

        def up(p):
            fs = slice(p * D_MODEL, (p + 1) * D_MODEL)
            hm = jnp.maximum(jnp.dot(hn, wup_ref[:, fs], preferred_element_type=F32), 0.0)
            hm_s[p % 2] = (hm * hm).astype(BF16)

        def down(p, acc):
            fs = slice(p * D_MODEL, (p + 1) * D_MODEL)
            return acc + jnp.dot(hm_s[p % 2], wdn_ref[fs, :], preferred_element_type=F32)

        def out_proj():
            h1 = x_ref[0] + jnp.dot(ocat_ref[...], wout_ref[...], preferred_element_type=F32)
            h1_s[...] = h1
            h1n_s[...] = _rmsnorm(h1, gc_ref[...]).astype(BF16)

        acc = x2_s[...]
        up(0)
        fill_scores(0, fast), fill_scores(1, fast)
        probs = _sample_pair_probs(hqs_ref[...], ck_ref)
        up(1)
        fill_scores(2, fast), fill_scores(3, fast)
        acc = down(0, acc)
        fill_scores(4, fast), fill_scores(5, fast)
        acc = down(1, acc)
        fill_scores(6, fast), fill_scores(7, fast)
        _sample_pair_context(probs, cv_ref, cas_ref)
        up(2)
        head_chunk(0), head_chunk(1), head_chunk(2)
        up(3)
        head_chunk(3), head_chunk(4), head_chunk(5)
        acc = down(2, acc)
        head_chunk(6), head_chunk(7)
        out_proj()
        acc = down(3, acc)
        y_ref[0] = _rmsnorm(acc, gf_ref[...])

    @pl.when(fast_ok)
    def _():
        region2(True)

    @pl.when(jnp.logical_not(fast_ok))
    def _():
        region2(False)

    @pl.when(i == nt - 1)
    def _():
        for h in range(HG_HEADS):
            sout_ref[0, h] = st_refs[h][...].T


def _prompt_layer(x, mk, mv, hqs, ck, cv, lb_param, g_mix, w_in, hg_norm_g, ln_v_g, ln_v_b, wmix, bcol,
                  w_out, g_cross, w_cq, w_co, g_mlp, w_up, w_down, g_final, tm):
    B, L, _ = x.shape
    nt = L // tm
    T = B * nt
    npair = ck.shape[0] // 2
    assert npair <= T + 1

    def cur(t):
        return jnp.minimum(t, T - 1)

    def prev(t):
        return jnp.maximum(t - 1, 0)

    def pair(t):
        return jnp.minimum(t, npair - 1)

    consts = (lb_param, g_mix, w_in, hg_norm_g, ln_v_g, ln_v_b, wmix, bcol, w_out,
              g_cross, w_cq, w_co, g_mlp, w_up, w_down, g_final)
    in_specs = [
        pl.BlockSpec((1, tm, D_MODEL), lambda t: (cur(t) // nt, cur(t) % nt, 0)),
        pl.BlockSpec((1, N_MEM, D_MODEL), lambda t: (prev(t) // nt, 0, 0)),
        pl.BlockSpec((1, N_MEM, D_MODEL), lambda t: (prev(t) // nt, 0, 0)),
        pl.BlockSpec((8, D_MODEL), lambda t: (pair(t), 0)),
        pl.BlockSpec((2, N_MEM, X_HEADS, X_HEAD_DIM), lambda t: (pair(t), 0, 0, 0)),
        pl.BlockSpec((2, N_MEM, X_HEADS, X_HEAD_DIM), lambda t: (pair(t), 0, 0, 0)),
    ] + [_const_spec(a.shape) for a in consts]
    out_specs = [
        pl.BlockSpec((1, tm, D_MODEL), lambda t: (prev(t) // nt, prev(t) % nt, 0)),
        pl.BlockSpec((1, HG_HEADS, HG_DK, HG_DK), lambda t: (cur(t) // nt, 0, 0, 0)),
        pl.BlockSpec((8, D_MODEL), lambda t: (pair(t), 0)),
    ]
    scratch = [
        pltpu.VMEM((tm, 2 * HG_WIDTH), F32),
        pltpu.VMEM((tm, IN_WIDTH - 2 * HG_WIDTH), F32),
        pltpu.VMEM((tm, D_MODEL), BF16),
        pltpu.VMEM((tm, HG_WIDTH), F32), pltpu.VMEM((tm, HG_WIDTH), F32),
        pltpu.VMEM((tm, HG_WIDTH), F32),
        pltpu.VMEM((tm // CHUNK * HG_HEADS, CHUNK, CHUNK), BF16),
        pltpu.VMEM((HG_DK, HG_DK), F32), pltpu.VMEM((HG_DK, HG_DK), F32),
        pltpu.VMEM((HG_DK, HG_DK), F32), pltpu.VMEM((HG_DK, HG_DK), F32),
        pltpu.VMEM((tm, D_MODEL), F32),
        pltpu.VMEM((tm, D_MODEL), BF16),
        pltpu.VMEM((tm, D_MODEL), F32),
        pltpu.VMEM((tm, D_MODEL), BF16),
        pltpu.VMEM((tm, D_MODEL), BF16),
        pltpu.VMEM((2, tm, D_MODEL), BF16),
    ]
    return pl.pallas_call(
        functools.partial(_prompt_kernel, tm=tm, nt=nt),
        grid=(T + 1,),
        in_specs=in_specs,
        out_specs=out_specs,
        out_shape=[jax.ShapeDtypeStruct((B, L, D_MODEL), F32),
                   jax.ShapeDtypeStruct((B, HG_HEADS, HG_DK, HG_DK), F32),
                   jax.ShapeDtypeStruct(hqs.shape, F32)],
        scratch_shapes=scratch,
        compiler_params=pltpu.CompilerParams(dimension_semantics=("arbitrary",),
                                             vmem_limit_bytes=VMEM_LIMIT_BYTES),
        name="prompt_layer",
    )(x, mk, mv, hqs, ck, cv, *consts)


def _mix_sample_kernel(x_ref, s0_ref, lb_ref, gmix_ref, win_ref, hgn_ref, lng_ref, lnb_ref, wmix_ref,
                       bcol_ref, wout_ref, gc_ref, wcq_ref, h_ref, sout_ref, vn_ref, hq_ref,
                       z_ref, ocat_ref, q_s, k_s, b_s, wm_s, bc_s, *, tm):
    x = x_ref[0]
    z_ref[...] = jnp.dot(_rmsnorm(x, gmix_ref[...]).astype(BF16), win_ref[...],
                         preferred_element_type=F32)
    lb_all = _lower_bound(lb_ref)
    mask = _chunk_masks(True)
    row = lax.broadcasted_iota(jnp.int32, (CHUNK, CHUNK), 0)
    col = lax.broadcasted_iota(jnp.int32, (CHUNK, CHUNK), 1)
    eye = row == col

    @pl.when(pl.program_id(0) == 0)
    def _():
        pick = (row % 4 == col).astype(BF16)
        for g in range(CM_GROUPS):
            corner = jnp.where((row < 4) & (col < 4), wmix_ref[g], 0.0)
            wm_s[g] = _bdot_nt(_bdot(pick, corner), pick)
        pos = lax.broadcasted_iota(jnp.int32, (CHUNK, CM_GROUPS), 0) % 4
        bc = jnp.zeros((CHUNK, CM_GROUPS), F32)
        for t in range(4):
            bc = jnp.where(pos == t, bcol_ref[t:t + 1, :], bc)
        bc_s[...] = bc
    r16 = lax.broadcasted_iota(jnp.int32, (SUB, HG_DK), 0)

    def gate_body(c, dmax):
        rows = pl.ds(pl.multiple_of(c * CHUNK, CHUNK), CHUNK)
        q, k, bcum = _gates(z_ref[rows, 0:HG_WIDTH], z_ref[rows, HG_WIDTH:2 * HG_WIDTH], lb_all, 4)
        q_s[rows, :] = q
        k_s[rows, :] = k
        b_s[rows, :] = bcum
        return jnp.maximum(dmax, _block_decay(bcum, True))

    dmax = lax.fori_loop(0, tm // CHUNK, gate_body, jnp.zeros((1, HG_WIDTH), F32))
    fast_ok = jnp.max(dmax) < FAST_MAX_DECAY

    def chunk_body(c, carry, *, fast):
        rows = pl.ds(pl.multiple_of(c * CHUNK, CHUNK), CHUNK)
        for h in range(HG_HEADS):
            cs = slice(h * HG_DK, (h + 1) * HG_DK)
            q, k, bcum = q_s[rows, cs], k_s[rows, cs], b_s[rows, cs]
            v = z_ref[rows, 2 * HG_WIDTH + h * HG_DK:2 * HG_WIDTH + (h + 1) * HG_DK]
            zg = z_ref[rows, 3 * HG_WIDTH + h * HG_DK:3 * HG_WIDTH + (h + 1) * HG_DK]
            a = _scores_fast(q, k, bcum, True) if fast else _scores_exact(q, k, bcum, True, mask)
            o = _bdot(jnp.where(mask, a, 0.0), v)
            eb = jnp.exp(bcum)
            qd = q * eb
            o_parts = []
            for gb in range(CHUNK // SUB):
                blk = slice(gb * SUB, (gb + 1) * SUB)
                qd_b, v_b, eb_b, k_b, b_b = qd[blk], v[blk], eb[blk], k[blk], bcum[blk]
                inter = jnp.zeros((SUB, HG_DK), F32)
                for j in range(4):
                    req = c * (CHUNK // 4) + gb * 4 + j
                    s0 = s0_ref[req, h]
                    last = 4 * j + 3
                    inter = jnp.where(r16 // 4 == j, _bdot(qd_b, s0), inter)
                    kd = jnp.where(r16 // 4 == j, k_b * jnp.exp(b_b[last:last + 1] - b_b), 0.0)
                    upd = lax.dot_general(kd.astype(BF16), v_b.astype(BF16),
                                          (((0,), (0,)), ((), ())), preferred_element_type=F32)
                    dcol = jnp.sum(jnp.where(eye, eb_b[last:last + 1], 0.0), axis=-1, keepdims=True)
                    sout_ref[req, h] = dcol * s0 + upd
                o_parts.append(inter)
            o = o + jnp.concatenate(o_parts, axis=0)
            ocat_ref[rows, cs] = _hgrn_out(o, zg, hgn_ref[:, cs])
        _gmlp_chunk(z_ref, 4 * HG_WIDTH, rows, lng_ref, lnb_ref, wm_s, bc_s, mask, ocat_ref, vn_ref)
        return carry

    @pl.when(fast_ok)
    def _():
        lax.fori_loop(0, tm // CHUNK, functools.partial(chunk_body, fast=True), 0)

    @pl.when(jnp.logical_not(fast_ok))
    def _():
        lax.fori_loop(0, tm // CHUNK, functools.partial(chunk_body, fast=False), 0)

    h = x + jnp.dot(ocat_ref[...], wout_ref[...], preferred_element_type=F32)
    h_ref[0] = h
    hq_ref[...] = jnp.dot(_rmsnorm(h, gc_ref[...]).astype(BF16), wcq_ref[...], preferred_element_type=F32)


def _mix_sample(x, s0, lb_param, g_mix, w_in, hg_norm_g, ln_v_g, ln_v_b, wmix, bcol, w_out, g_cross,
                w_cq, tm):
    nt, _, _ = x.shape
    nreq = tm // 4
    consts = (lb_param, g_mix, w_in, hg_norm_g, ln_v_g, ln_v_b, wmix, bcol, w_out, g_cross, w_cq)
    in_specs = [
        pl.BlockSpec((1, tm, D_MODEL), lambda i: (i, 0, 0)),
        pl.BlockSpec((nreq, HG_HEADS, HG_DK, HG_DK), lambda i: (i, 0, 0, 0)),
    ] + [_const_spec(a.shape) for a in consts]
    out_specs = [
        pl.BlockSpec((1, tm, D_MODEL), lambda i: (i, 0, 0)),
        pl.BlockSpec((nreq, HG_HEADS, HG_DK, HG_DK), lambda i: (i, 0, 0, 0)),
        pl.BlockSpec((tm, CM_WIDTH), lambda i: (i, 0)),
        pl.BlockSpec((tm, D_MODEL), lambda i: (i, 0)),
    ]
    return pl.pallas_call(
        functools.partial(_mix_sample_kernel, tm=tm),
        grid=(nt,),
        in_specs=in_specs,
        out_specs=out_specs,
        out_shape=[jax.ShapeDtypeStruct((nt, tm, D_MODEL), F32),
                   jax.ShapeDtypeStruct(s0.shape, F32),
                   jax.ShapeDtypeStruct((nt * tm, CM_WIDTH), F32),
                   jax.ShapeDtypeStruct((nt * tm, D_MODEL), F32)],
        scratch_shapes=[pltpu.VMEM((tm, IN_WIDTH), F32), pltpu.VMEM((tm, D_MODEL), BF16)]
        + [pltpu.VMEM((tm, HG_WIDTH), F32)] * 3
        + [pltpu.VMEM((CM_GROUPS, CHUNK, CHUNK), F32), pltpu.VMEM((CHUNK, CM_GROUPS), F32)],
        compiler_params=pltpu.CompilerParams(dimension_semantics=("arbitrary",),
                                             vmem_limit_bytes=VMEM_LIMIT_BYTES),
        name="mix_sample",
    )(x, s0, *consts)


def _memkv_kernel(m_ref, g_ref, wk_ref, wv_ref, k_ref, v_ref, kb_ref, vb_ref):
    mn = _rmsnorm(m_ref[...], g_ref[...]).astype(BF16)
    k = jnp.dot(mn, wk_ref[...], preferred_element_type=F32)
    v = jnp.dot(mn, wv_ref[...], preferred_element_type=F32)
    kb_ref[...] = k.astype(BF16)
    vb_ref[...] = v.astype(BF16)
    for src, dst in ((k, k_ref), (v, v_ref)):
        for r in range(dst.shape[0]):
            rows = src[r * N_MEM:(r + 1) * N_MEM]
            dst[r] = jnp.stack([rows[:, h * X_HEAD_DIM:(h + 1) * X_HEAD_DIM] for h in range(X_HEADS)],
                               axis=1)


def _memkv(mem, g_mem, w_ck, w_cv, tm):
    n = mem.shape[0]
    return pl.pallas_call(
        _memkv_kernel,
        grid=(n // tm,),
        in_specs=[pl.BlockSpec((tm, D_MODEL), lambda i: (i, 0)), _const_spec(g_mem.shape),
                  _const_spec(w_ck.shape), _const_spec(w_cv.shape)],
        out_specs=[pl.BlockSpec((tm // N_MEM, N_MEM, X_HEADS, X_HEAD_DIM), lambda i: (i, 0, 0, 0))] * 2
        + [pl.BlockSpec((tm, D_MODEL), lambda i: (i, 0))] * 2,
        out_shape=[jax.ShapeDtypeStruct((n // N_MEM, N_MEM, X_HEADS, X_HEAD_DIM), F32)] * 2
        + [jax.ShapeDtypeStruct((n, D_MODEL), BF16)] * 2,
        compiler_params=pltpu.CompilerParams(dimension_semantics=("arbitrary",),
                                             vmem_limit_bytes=VMEM_LIMIT_BYTES),
        name="memkv",
    )(mem, g_mem, w_ck, w_cv)


def _post_sample_kernel(x_ref, ca_ref, wco_ref, gm_ref, wup_ref, wdn_ref, gf_ref, y_ref, hm_s):
    x = x_ref[0] + jnp.dot(ca_ref[...].astype(BF16), wco_ref[...], preferred_element_type=F32)
    x = x + _mlp(_rmsnorm(x, gm_ref[...]).astype(BF16), wup_ref, wdn_ref, hm_s)
    y_ref[0] = _rmsnorm(x, gf_ref[...])


def _post_sample(x, ca, w_co, g_mlp, w_up, w_down, g_final):
    nt, tm, _ = x.shape
    in_specs = [
        pl.BlockSpec((1, tm, D_MODEL), lambda i: (i, 0, 0)),
        pl.BlockSpec((tm, D_MODEL), lambda i: (i, 0)),
        _const_spec(w_co.shape), _const_spec(g_mlp.shape), _const_spec(w_up.shape),
        _const_spec(w_down.shape), _const_spec(g_final.shape),
    ]
    return pl.pallas_call(
        _post_sample_kernel,
        grid=(nt,),
        in_specs=in_specs,
        out_specs=pl.BlockSpec((1, tm, D_MODEL), lambda i: (i, 0, 0)),
        out_shape=jax.ShapeDtypeStruct((nt, tm, D_MODEL), F32),
        scratch_shapes=[pltpu.VMEM((tm, D_FF), BF16)],
        compiler_params=pltpu.CompilerParams(dimension_semantics=("arbitrary",),
                                             vmem_limit_bytes=VMEM_LIMIT_BYTES),
        name="post_sample",
    )(x, ca, w_co, g_mlp, w_up, w_down, g_final)


def _cast_kernel(*refs):
    n = len(refs) // 2
    for src, dst in zip(refs[:n], refs[n:]):
        dst[...] = src[...].astype(BF16)


def _cast_weights(*ws):
    steps = 8
    specs = [pl.BlockSpec((w.shape[0] // steps, w.shape[1]), lambda i: (i, 0)) for w in ws]
    return pl.pallas_call(
        _cast_kernel,
        grid=(steps,),
        in_specs=specs,
        out_specs=specs,
        out_shape=[jax.ShapeDtypeStruct(w.shape, BF16) for w in ws],
        compiler_params=pltpu.CompilerParams(dimension_semantics=("arbitrary",),
                                             vmem_limit_bytes=VMEM_LIMIT_BYTES),
        name="cast_weights",
    )(*ws)


def kernel(x_prompt, x_sample, mem_prompt, state_hgrn, cache_mem_k, cache_mem_v, lb_param, g_mix,
           w_in, hg_norm_g, ln_v_g, ln_v_b, w_s, b_s, w_out, g_cross, g_mem, w_cq, w_ck, w_cv, w_co,
           g_mlp, w_up, w_down, g_final):
    B, L, _ = x_prompt.shape
    DB, DL, _ = x_sample.shape
    assert DL == 4 and g_mix.shape[0] == 1

    row = lambda a: a.reshape(1, -1)
    win_b, wout_b, wcq_b, wco_b, wck_b, wcv_b, wup_b, wdn_b = _cast_weights(
        w_in[0], w_out[0], w_cq[0], w_co[0], w_ck[0], w_cv[0], w_up[0], w_down[0])
    gmix, hgn, lng, lnb = row(g_mix[0]), row(hg_norm_g[0]), row(ln_v_g[0]), row(ln_v_b[0])
    gcr, gmem, gmlp, gfin = row(g_cross[0]), row(g_mem[0]), row(g_mlp[0]), row(g_final)
    wmix = w_s[0]
    bcol = b_s[0].T

    tm_s = 128
    xs = x_sample.reshape(DB * DL // tm_s, tm_s, D_MODEL)
    h_s, s_s, vn_s, hq_s = _mix_sample(xs, state_hgrn[0], lb_param, gmix, win_b, hgn, lng, lnb, wmix,
                                       bcol, wout_b, gcr, wcq_b, tm=tm_s)

    mk, mv, mk_b, mv_b = _memkv(mem_prompt.reshape(B * N_MEM, D_MODEL), gmem, wck_b, wcv_b, tm=512)
    y_p, s_p, ca_s = _prompt_layer(x_prompt, mk_b.reshape(B, N_MEM, D_MODEL), mv_b.reshape(B, N_MEM, D_MODEL),
                                   hq_s, cache_mem_k[0], cache_mem_v[0],
                                   lb_param, gmix, win_b, hgn, lng, lnb, wmix, bcol, wout_b,
                                   gcr, wcq_b, wco_b, gmlp, wup_b, wdn_b, gfin, tm=256)

    y_s = _post_sample(h_s.reshape(1, DB * DL, D_MODEL), ca_s, wco_b, gmlp, wup_b, wdn_b, gfin)

    return (y_p, y_s.reshape(DB, DL, D_MODEL), s_p[None], s_s[None],
            mk[None], mv[None],
            vn_s.reshape(1, DB, DL, CM_WIDTH))
```

```python
import functools

import jax
import jax.numpy as jnp
from jax import lax
from jax.experimental import pallas as pl
from jax.experimental.pallas import tpu as pltpu

F32 = jnp.float32
BF16 = jnp.bfloat16

D_MODEL = 1024
HG_WIDTH = 512
HG_HEADS = 4
HG_DK = 128
CM_WIDTH = 512
CM_GROUPS = 4
CM_GROUP_DIM = 128
IN_WIDTH = 4 * HG_WIDTH + 2 * CM_WIDTH
N_MEM = 256
X_HEADS = 4
X_HEAD_DIM = 256
D_FF = 4096
EPS = 1e-6

CHUNK = 128
SUB = 16
VMEM_LIMIT_BYTES = 62 * 1024 * 1024
LOG2E = 1.4426950408889634
FAST_BLOCK = 32
FAST_MAX_DECAY = 80.0


def _bdot(a, b):
    return jnp.dot(a.astype(BF16), b.astype(BF16), preferred_element_type=F32)


def _bdot_nt(a, b):
    return lax.dot_general(a.astype(BF16), b.astype(BF16), (((1,), (1,)), ((), ())),
                           preferred_element_type=F32)


def _rmsnorm(x, g):
    ms = jnp.mean(x * x, axis=-1, keepdims=True)
    return x * lax.rsqrt(ms + EPS) * g


def _sigmoid(x):
    return 0.5 * jnp.tanh(0.5 * x) + 0.5


def _gelu(x):
    return 0.5 * x * (1.0 + jnp.tanh(0.7978845608028654 * (x + 0.044715 * (x * x * x))))


def _softmax_rows(s):
    m = jnp.max(s, axis=-1, keepdims=True)
    e = jnp.exp(s - m)
    return e / jnp.sum(e, axis=-1, keepdims=True)


def _seg_cumsum(x, seg):
    n = x.shape[0]
    pos = lax.broadcasted_iota(jnp.int32, x.shape, 0) % seg
    s = 1
    while s < min(seg, 8):
        x = x + jnp.where(pos >= s, pltpu.roll(x, s, 0), 0.0)
        s *= 2
    while s < seg:
        parts = []
        for r0 in range(0, n, seg):
            parts.append(x[r0:r0 + s])
            parts.append(x[r0 + s:r0 + seg] + x[r0:r0 + seg - s])
        x = jnp.concatenate(parts, axis=0)
        s *= 2
    return x


def _lower_bound(lb_ref):
    lbp = lb_ref[...]
    lbe = jnp.exp(lbp - jnp.max(lbp, axis=0, keepdims=True))
    return lbe[0:1] / jnp.sum(lbe, axis=0, keepdims=True)


def _gates(zq, zf, lb, seg):
    th = jnp.tanh(0.5 * zf)
    logf = jnp.log(lb + (1.0 - lb) * (0.5 + 0.5 * th))
    return zq * _sigmoid(zq), (1.0 - lb) * (0.5 - 0.5 * th), _seg_cumsum(logf, seg)


def _block_decay(bcum, sample):
    if sample:
        return jnp.max(-bcum, axis=0, keepdims=True)
    d = -bcum[FAST_BLOCK - 1:FAST_BLOCK]
    for i in range(1, CHUNK // FAST_BLOCK):
        n0, n1 = i * FAST_BLOCK, (i + 1) * FAST_BLOCK
        d = jnp.maximum(d, bcum[n0 - 1:n0] - bcum[n1 - 1:n1])
    return d


def _chunk_masks(sample):
    row = lax.broadcasted_iota(jnp.int32, (CHUNK, CHUNK), 0)
    col = lax.broadcasted_iota(jnp.int32, (CHUNK, CHUNK), 1)
    if sample:
        return (row // 4 == col // 4) & (row >= col)
    return row >= col


def _scores_fast(q, k, bcum, sample):
    if sample:
        return _bdot_nt(q * jnp.exp(bcum), k * jnp.exp(-bcum))
    a_rows = []
    for i in range(CHUNK // FAST_BLOCK):
        n0, n1 = i * FAST_BLOCK, (i + 1) * FAST_BLOCK
        ref_b = bcum[n0 - 1:n0] if i else jnp.zeros((1, HG_DK), F32)
        qt = q[n0:n1] * jnp.exp(bcum[n0:n1] - ref_b)
        kt = k[:n1] * jnp.exp(ref_b - bcum[:n1])
        if n1 < CHUNK:
            kt = jnp.concatenate([kt, jnp.zeros((CHUNK - n1, HG_DK), F32)], axis=0)
        a_rows.append(_bdot_nt(qt, kt))
    return jnp.concatenate(a_rows, axis=0)


def _scores_exact(q, k, bcum, sample, mask):
    col8 = lax.broadcasted_iota(jnp.int32, (8, CHUNK), 1)
    b2 = bcum * LOG2E
    cexp = b2 - jnp.log2(k)
    a_rows = []
    for gb in range(CHUNK // SUB):
        lo = slice(gb * SUB, gb * SUB + 8)
        hi = slice(gb * SUB + 8, (gb + 1) * SUB)
        a_lo = jnp.zeros((8, CHUNK), F32)
        a_hi = jnp.zeros((8, CHUNK), F32)
        for s in range(SUB):
            sg = gb * SUB + s
            c_s = cexp[sg:sg + 1]
            if s < 8:
                p = q[lo] * jnp.exp2(b2[lo] - c_s)
                a_lo = jnp.where(col8 == sg, jnp.sum(p, axis=-1, keepdims=True), a_lo)
            if (not sample) or s >= 8:
                p = q[hi] * jnp.exp2(b2[hi] - c_s)
                a_hi = jnp.where(col8 == sg, jnp.sum(p, axis=-1, keepdims=True), a_hi)
        a_rows.append(a_lo)
        a_rows.append(a_hi)
    a = jnp.concatenate(a_rows, axis=0)
    if sample:
        return a
    o_rows = [jnp.zeros((SUB, CHUNK), F32)]
    for i in range(1, CHUNK // SUB):
        n = i * SUB
        ref_b = bcum[n - 1:n]
        qt = q[n:n + SUB] * jnp.exp(bcum[n:n + SUB] - ref_b)
        kt = k[:n] * jnp.exp(ref_b - bcum[:n])
        kt = jnp.concatenate([kt, jnp.zeros((CHUNK - n, HG_DK), F32)], axis=0)
        o_rows.append(_bdot_nt(qt, kt))
    return jnp.where(mask, a, 0.0) + jnp.concatenate(o_rows, axis=0)


def _hgrn_out(o, zg, g):
    o = o * lax.rsqrt(jnp.mean(o * o, axis=-1, keepdims=True) + EPS) * g
    return (o * _sigmoid(zg)).astype(BF16)


def _gmlp_chunk(z_ref, c0, rows, lng_ref, lnb_ref, wmix_ref, bcol_ref, mask, ocat_ref):
    u = _gelu(z_ref[rows, c0:c0 + CM_WIDTH])
    gv = _gelu(z_ref[rows, c0 + CM_WIDTH:c0 + 2 * CM_WIDTH])
    mu = jnp.mean(gv, axis=-1, keepdims=True)
    dv = gv - mu
    var = jnp.mean(dv * dv, axis=-1, keepdims=True)
    vn = dv * lax.rsqrt(var + EPS) * lng_ref[...] + lnb_ref[...]
    for g in range(CM_GROUPS):
        gs = slice(g * CM_GROUP_DIM, (g + 1) * CM_GROUP_DIM)
        wm = jnp.where(mask, wmix_ref[g], 0.0)
        mixed = _bdot(wm, vn[:, gs]) + bcol_ref[:, g:g + 1]
        ocat_ref[rows, HG_WIDTH + g * CM_GROUP_DIM:HG_WIDTH + (g + 1) * CM_GROUP_DIM] = (
            u[:, gs] * mixed).astype(BF16)
    return vn


def _sample_pair_probs(hq8, ck_ref):
    nrow = 8 * X_HEADS
    rh = lax.broadcasted_iota(jnp.int32, (nrow, N_MEM * X_HEADS), 0) // 8
    ch = lax.broadcasted_iota(jnp.int32, (nrow, N_MEM * X_HEADS), 1) % X_HEADS
    q = jnp.concatenate([hq8[:, h * X_HEAD_DIM:(h + 1) * X_HEAD_DIM] for h in range(X_HEADS)],
                        axis=0).astype(BF16)
    probs = []
    for r in range(2):
        k2 = ck_ref[r].reshape(N_MEM * X_HEADS, X_HEAD_DIM)
        s = jnp.where(rh == ch, _bdot_nt(q, k2) * (X_HEAD_DIM ** -0.5), -jnp.inf)
        probs.append(_softmax_rows(s))
    return probs


def _sample_pair_context(probs, cv_ref, ca_ref):
    first = (lax.broadcasted_iota(jnp.int32, (8 * X_HEADS, X_HEAD_DIM), 0) % 8) < 4
    outs = [_bdot(probs[r], cv_ref[r].reshape(N_MEM * X_HEADS, X_HEAD_DIM)) for r in range(2)]
    o = jnp.where(first, outs[0], outs[1])
    for h in range(X_HEADS):
        ca_ref[:, h * X_HEAD_DIM:(h + 1) * X_HEAD_DIM] = o[8 * h:8 * h + 8]


def _mlp(hn_bf16, wup_ref, wdn_ref, hm_s):
    for c in range(D_FF // D_MODEL):
        fs = slice(c * D_MODEL, (c + 1) * D_MODEL)
        hm = jnp.maximum(jnp.dot(hn_bf16, wup_ref[:, fs], preferred_element_type=F32), 0.0)
        hm_s[:, fs] = (hm * hm).astype(BF16)
    return jnp.dot(hm_s[...], wdn_ref[...], preferred_element_type=F32)


def _const_spec(shape):
    nd = len(shape)
    return pl.BlockSpec(shape, lambda *_: (0,) * nd, pipeline_mode=pl.Buffered(1))


def _prompt_kernel(x_ref, mk_ref, mv_ref, hqs_ref, ck_ref, cv_ref, lb_ref, gmix_ref, win_ref, hgn_ref,
                   lng_ref, lnb_ref, wmix_ref, bcol_ref, wout_ref, gc_ref, wcq_ref, wco_ref, gm_ref,
                   wup_ref, wdn_ref, gf_ref, y_ref, sout_ref, cas_ref,
                   zg_ref, z_ref, ocat_ref, q_s, k_s, b_s, a_s, st0, st1, st2, st3, h1_s, h1n_s, x2_s,
                   hn_s, ca_s, hm_s, *, tm, nt):
    t = pl.program_id(0)
    i = lax.rem(t, nt)
    nch = tm // CHUNK
    assert nch * HG_HEADS == 8 and D_FF // D_MODEL == 4
    mask = _chunk_masks(False)
    st_refs = (st0, st1, st2, st3)

    @pl.when(t == 0)
    def _():
        h1_s[...] = jnp.zeros_like(h1_s)
        h1n_s[...] = jnp.zeros_like(h1n_s)

    @pl.when(i == 0)
    def _():
        for st in st_refs:
            st[...] = jnp.zeros_like(st)


    scale = X_HEAD_DIM ** -0.5
    heads = [slice(h * X_HEAD_DIM, (h + 1) * X_HEAD_DIM) for h in range(X_HEADS)]
    hq = jnp.dot(h1n_s[...], wcq_ref[...], preferred_element_type=F32)

    x = x_ref[0]
    xn = _rmsnorm(x, gmix_ref[...]).astype(BF16)
    ngate = 2 * HG_WIDTH

    def in_proj(dst, c0, c1):
        off = ngate if dst is z_ref else 0
        dst[:, c0 - off:c1 - off] = jnp.dot(xn, win_ref[:, c0:c1], preferred_element_type=F32)

    in_proj(zg_ref, 0, ngate)
    probs = [_softmax_rows(_bdot_nt(hq[:, hs], mk_ref[0, :, hs]) * scale) for hs in heads]
    in_proj(z_ref, ngate, ngate + 2 * HG_WIDTH)
    for h, hs in enumerate(heads):
        ca_s[:, hs] = _bdot(probs[h], mv_ref[0, :, hs]).astype(BF16)
    in_proj(z_ref, ngate + 2 * HG_WIDTH, ngate + 2 * HG_WIDTH + CM_WIDTH)

    lb_all = _lower_bound(lb_ref)
    dmax = jnp.zeros((1, HG_WIDTH), F32)
    for c in range(nch):
        rows = slice(c * CHUNK, (c + 1) * CHUNK)
        q, k, bcum = _gates(zg_ref[rows, 0:HG_WIDTH], zg_ref[rows, HG_WIDTH:2 * HG_WIDTH], lb_all, CHUNK)
        q_s[rows, :] = q
        k_s[rows, :] = k
        b_s[rows, :] = bcum
        dmax = jnp.maximum(dmax, _block_decay(bcum, False))
    fast_ok = jnp.max(dmax) < FAST_MAX_DECAY

    x2 = h1_s[...] + jnp.dot(ca_s[...], wco_ref[...], preferred_element_type=F32)
    in_proj(z_ref, ngate + 2 * HG_WIDTH + CM_WIDTH, IN_WIDTH)
    x2_s[...] = x2
    hn_s[...] = _rmsnorm(x2, gm_ref[...]).astype(BF16)

    def fill_scores(j, fast):
        c, h = divmod(j, HG_HEADS)
        rows = slice(c * CHUNK, (c + 1) * CHUNK)
        cs = slice(h * HG_DK, (h + 1) * HG_DK)
        q, k, bcum = q_s[rows, cs], k_s[rows, cs], b_s[rows, cs]
        a = _scores_fast(q, k, bcum, False) if fast else _scores_exact(q, k, bcum, False, mask)
        a_s[j] = jnp.where(mask, a, 0.0).astype(BF16)

    def head_chunk(j):
        c, h = divmod(j, HG_HEADS)
        rows = slice(c * CHUNK, (c + 1) * CHUNK)
        cs = slice(h * HG_DK, (h + 1) * HG_DK)
        q, k, bcum = q_s[rows, cs], k_s[rows, cs], b_s[rows, cs]
        v = z_ref[rows, h * HG_DK:(h + 1) * HG_DK]
        zg = z_ref[rows, HG_WIDTH + h * HG_DK:HG_WIDTH + (h + 1) * HG_DK]
        eb = jnp.exp(bcum)
        kd = k * jnp.exp(bcum[CHUNK - 1:CHUNK] - bcum)
        st = st_refs[h][...]
        o = jnp.dot(a_s[j], v.astype(BF16), preferred_element_type=F32)
        o = o + _bdot_nt(q * eb, st)
        st_refs[h][...] = st * eb[CHUNK - 1:CHUNK] + _bdot(v.T, kd)
        ocat_ref[rows, cs] = _hgrn_out(o, zg, hgn_ref[:, cs])
        if h == HG_HEADS - 1:
            _gmlp_chunk(z_ref, 2 * HG_WIDTH, rows, lng_ref, lnb_ref, wmix_ref, bcol_ref, mask, ocat_ref)

    def region2(fast):
        hn = hn_s[...]

        def up(p):
            fs = slice(p * D_MODEL, (p + 1) * D_MODEL)
            hm = jnp.maximum(jnp.dot(hn, wup_ref[:, fs], preferred_element_type=F32), 0.0)
            hm_s[p % 2] = (hm * hm).astype(BF16)

        def down(p, acc):
            fs = slice(p * D_MODEL, (p + 1) * D_MODEL)
            return acc + jnp.dot(hm_s[p % 2], wdn_ref[fs, :], preferred_element_type=F32)

        def out_proj():
            h1 = x_ref[0] + jnp.dot(ocat_ref[...], wout_ref[...], preferred_element_type=F32)
            h1_s[...] = h1
            h1n_s[...] = _rmsnorm(h1, gc_ref[...]).astype(BF16)

        acc = x2_s[...]
        up(0)
        fill_scores(0, fast), fill_scores(1, fast)
        probs = _sample_pair_probs(hqs_ref[...], ck_ref)
        up(1)
        fill_scores(2, fast), fill_scores(3, fast)
        acc = down(0, acc)
        fill_scores(4, fast), fill_scores(5, fast)
        acc = down(1, acc)
        fill_scores(6, fast), fill_scores(7, fast)
        _sample_pair_context(probs, cv_ref, cas_ref)
        up(2)
        head_chunk(0), head_chunk(1), head_chunk(2)
        up(3)
        head_chunk(3), head_chunk(4), head_chunk(5)
        acc = down(2, acc)
        head_chunk(6), head_chunk(7)
        out_proj()
        acc = down(3, acc)
        y_ref[0] = _rmsnorm(acc, gf_ref[...])

    @pl.when(fast_ok)
    def _():
        region2(True)

    @pl.when(jnp.logical_not(fast_ok))
    def _():
        region2(False)

    @pl.when(i == nt - 1)
    def _():
        for h in range(HG_HEADS):
            sout_ref[0, h] = st_refs[h][...].T


def _prompt_layer(x, mk, mv, hqs, ck, cv, lb_param, g_mix, w_in, hg_norm_g, ln_v_g, ln_v_b, wmix, bcol,
                  w_out, g_cross, w_cq, w_co, g_mlp, w_up, w_down, g_final, tm):
    B, L, _ = x.shape
    nt = L // tm
    T = B * nt
    npair = ck.shape[0] // 2
    assert npair <= T + 1

    def cur(t):
        return jnp.minimum(t, T - 1)

    def prev(t):
        return jnp.maximum(t - 1, 0)

    def pair(t):
        return jnp.minimum(t, npair - 1)

    consts = (lb_param, g_mix, w_in, hg_norm_g, ln_v_g, ln_v_b, wmix, bcol, w_out,
              g_cross, w_cq, w_co, g_mlp, w_up, w_down, g_final)
    in_specs = [
        pl.BlockSpec((1, tm, D_MODEL), lambda t: (cur(t) // nt, cur(t) % nt, 0)),
        pl.BlockSpec((1, N_MEM, D_MODEL), lambda t: (prev(t) // nt, 0, 0)),
        pl.BlockSpec((1, N_MEM, D_MODEL), lambda t: (prev(t) // nt, 0, 0)),
        pl.BlockSpec((8, D_MODEL), lambda t: (pair(t), 0)),
        pl.BlockSpec((2, N_MEM, X_HEADS, X_HEAD_DIM), lambda t: (pair(t), 0, 0, 0)),
        pl.BlockSpec((2, N_MEM, X_HEADS, X_HEAD_DIM), lambda t: (pair(t), 0, 0, 0)),
    ] + [_const_spec(a.shape) for a in consts]
    out_specs = [
        pl.BlockSpec((1, tm, D_MODEL), lambda t: (prev(t) // nt, prev(t) % nt, 0)),
        pl.BlockSpec((1, HG_HEADS, HG_DK, HG_DK), lambda t: (cur(t) // nt, 0, 0, 0)),
        pl.BlockSpec((8, D_MODEL), lambda t: (pair(t), 0)),
    ]
    scratch = [
        pltpu.VMEM((tm, 2 * HG_WIDTH), F32),
        pltpu.VMEM((tm, IN_WIDTH - 2 * HG_WIDTH), F32),
        pltpu.VMEM((tm, D_MODEL), BF16),
        pltpu.VMEM((tm, HG_WIDTH), F32), pltpu.VMEM((tm, HG_WIDTH), F32),
        pltpu.VMEM((tm, HG_WIDTH), F32),
        pltpu.VMEM((tm // CHUNK * HG_HEADS, CHUNK, CHUNK), BF16),
        pltpu.VMEM((HG_DK, HG_DK), F32), pltpu.VMEM((HG_DK, HG_DK), F32),
        pltpu.VMEM((HG_DK, HG_DK), F32), pltpu.VMEM((HG_DK, HG_DK), F32),
        pltpu.VMEM((tm, D_MODEL), F32),
        pltpu.VMEM((tm, D_MODEL), BF16),
        pltpu.VMEM((tm, D_MODEL), F32),
        pltpu.VMEM((tm, D_MODEL), BF16),
        pltpu.VMEM((tm, D_MODEL), BF16),
        pltpu.VMEM((2, tm, D_MODEL), BF16),
    ]
    return pl.pallas_call(
        functools.partial(_prompt_kernel, tm=tm, nt=nt),
        grid=(T + 1,),
        in_specs=in_specs,
        out_specs=out_specs,
        out_shape=[jax.ShapeDtypeStruct((B, L, D_MODEL), F32),
                   jax.ShapeDtypeStruct((B, HG_HEADS, HG_DK, HG_DK), F32),
                   jax.ShapeDtypeStruct(hqs.shape, F32)],
        scratch_shapes=scratch,
        compiler_params=pltpu.CompilerParams(dimension_semantics=("arbitrary",),
                                             vmem_limit_bytes=VMEM_LIMIT_BYTES),
        name="prompt_layer",
    )(x, mk, mv, hqs, ck, cv, *consts)


def _mix_sample_kernel(x_ref, s0_ref, lb_ref, gmix_ref, win_ref, hgn_ref, lng_ref, lnb_ref, wmix_ref,
                       bcol_ref, wout_ref, gc_ref, wcq_ref, h_ref, sout_ref, vn_ref, hq_ref,
                       z_ref, ocat_ref, q_s, k_s, b_s, wm_s, bc_s, *, tm):
    x = x_ref[...].reshape(tm, D_MODEL)
    z_ref[...] = jnp.dot(_rmsnorm(x, gmix_ref[...]).astype(BF16), win_ref[...],
                         preferred_element_type=F32)
    lb_all = _lower_bound(lb_ref)
    mask = _chunk_masks(True)
    row = lax.broadcasted_iota(jnp.int32, (CHUNK, CHUNK), 0)
    col = lax.broadcasted_iota(jnp.int32, (CHUNK, CHUNK), 1)
    eye = row == col

    @pl.when(pl.program_id(0) == 0)
    def _():
        pick = (row % 4 == col).astype(BF16)
        for g in range(CM_GROUPS):
            corner = jnp.where((row < 4) & (col < 4), wmix_ref[g], 0.0)
            wm_s[g] = _bdot_nt(_bdot(pick, corner), pick)
        pos = lax.broadcasted_iota(jnp.int32, (CHUNK, CM_GROUPS), 0) % 4
        bc = jnp.zeros((CHUNK, CM_GROUPS), F32)
        for t in range(4):
            bc = jnp.where(pos == t, bcol_ref[t:t + 1, :], bc)
        bc_s[...] = bc
    r16 = lax.broadcasted_iota(jnp.int32, (SUB, HG_DK), 0)

    def gate_body(c, dmax):
        rows = pl.ds(pl.multiple_of(c * CHUNK, CHUNK), CHUNK)
        q, k, bcum = _gates(z_ref[rows, 0:HG_WIDTH], z_ref[rows, HG_WIDTH:2 * HG_WIDTH], lb_all, 4)
        q_s[rows, :] = q
        k_s[rows, :] = k
        b_s[rows, :] = bcum
        return jnp.maximum(dmax, _block_decay(bcum, True))

    dmax = lax.fori_loop(0, tm // CHUNK, gate_body, jnp.zeros((1, HG_WIDTH), F32))
    fast_ok = jnp.max(dmax) < FAST_MAX_DECAY

    def chunk_body(c, carry, *, fast):
        rows = pl.ds(pl.multiple_of(c * CHUNK, CHUNK), CHUNK)
        for h in range(HG_HEADS):
            cs = slice(h * HG_DK, (h + 1) * HG_DK)
            q, k, bcum = q_s[rows, cs], k_s[rows, cs], b_s[rows, cs]
            v = z_ref[rows, 2 * HG_WIDTH + h * HG_DK:2 * HG_WIDTH + (h + 1) * HG_DK]
            zg = z_ref[rows, 3 * HG_WIDTH + h * HG_DK:3 * HG_WIDTH + (h + 1) * HG_DK]
            a = _scores_fast(q, k, bcum, True) if fast else _scores_exact(q, k, bcum, True, mask)
            o = _bdot(jnp.where(mask, a, 0.0), v)
            eb = jnp.exp(bcum)
            qd = q * eb
            o_parts = []
            for gb in range(CHUNK // SUB):
                blk = slice(gb * SUB, (gb + 1) * SUB)
                qd_b, v_b, eb_b, k_b, b_b = qd[blk], v[blk], eb[blk], k[blk], bcum[blk]
                inter = jnp.zeros((SUB, HG_DK), F32)
                for j in range(4):
                    req = c * (CHUNK // 4) + gb * 4 + j
                    s0 = s0_ref[req, h]
                    last = 4 * j + 3
                    inter = jnp.where(r16 // 4 == j, _bdot(qd_b, s0), inter)
                    kd = jnp.where(r16 // 4 == j, k_b * jnp.exp(b_b[last:last + 1] - b_b), 0.0)
                    upd = lax.dot_general(kd.astype(BF16), v_b.astype(BF16),
                                          (((0,), (0,)), ((), ())), preferred_element_type=F32)
                    dcol = jnp.sum(jnp.where(eye, eb_b[last:last + 1], 0.0), axis=-1, keepdims=True)
                    sout_ref[req, h] = dcol * s0 + upd
                o_parts.append(inter)
            o = o + jnp.concatenate(o_parts, axis=0)
            ocat_ref[rows, cs] = _hgrn_out(o, zg, hgn_ref[:, cs])
        vn = _gmlp_chunk(z_ref, 4 * HG_WIDTH, rows, lng_ref, lnb_ref, wm_s, bc_s, mask, ocat_ref)
        vn_ref[pl.ds(pl.multiple_of(c * (CHUNK // 4), CHUNK // 4), CHUNK // 4)] = vn.reshape(
            CHUNK // 4, 4, CM_WIDTH)
        return carry

    @pl.when(fast_ok)
    def _():
        lax.fori_loop(0, tm // CHUNK, functools.partial(chunk_body, fast=True), 0)

    @pl.when(jnp.logical_not(fast_ok))
    def _():
        lax.fori_loop(0, tm // CHUNK, functools.partial(chunk_body, fast=False), 0)

    h = x + jnp.dot(ocat_ref[...], wout_ref[...], preferred_element_type=F32)
    h_ref[0] = h
    hq_ref[...] = jnp.dot(_rmsnorm(h, gc_ref[...]).astype(BF16), wcq_ref[...], preferred_element_type=F32)


def _mix_sample(x, s0, lb_param, g_mix, w_in, hg_norm_g, ln_v_g, ln_v_b, wmix, bcol, w_out, g_cross,
                w_cq, tm):
    nreq = tm // 4
    nt = x.shape[0] // nreq
    consts = (lb_param, g_mix, w_in, hg_norm_g, ln_v_g, ln_v_b, wmix, bcol, w_out, g_cross, w_cq)
    in_specs = [
        pl.BlockSpec((nreq, 4, D_MODEL), lambda i: (i, 0, 0)),
        pl.BlockSpec((nreq, HG_HEADS, HG_DK, HG_DK), lambda i: (i, 0, 0, 0)),
    ] + [_const_spec(a.shape) for a in consts]
    out_specs = [
        pl.BlockSpec((1, tm, D_MODEL), lambda i: (i, 0, 0)),
        pl.BlockSpec((nreq, HG_HEADS, HG_DK, HG_DK), lambda i: (i, 0, 0, 0)),
        pl.BlockSpec((nreq, 4, CM_WIDTH), lambda i: (i, 0, 0)),
        pl.BlockSpec((tm, D_MODEL), lambda i: (i, 0)),
    ]
    return pl.pallas_call(
        functools.partial(_mix_sample_kernel, tm=tm),
        grid=(nt,),
        in_specs=in_specs,
        out_specs=out_specs,
        out_shape=[jax.ShapeDtypeStruct((nt, tm, D_MODEL), F32),
                   jax.ShapeDtypeStruct(s0.shape, F32),
                   jax.ShapeDtypeStruct((nt * nreq, 4, CM_WIDTH), F32),
                   jax.ShapeDtypeStruct((nt * tm, D_MODEL), F32)],
        scratch_shapes=[pltpu.VMEM((tm, IN_WIDTH), F32), pltpu.VMEM((tm, D_MODEL), BF16)]
        + [pltpu.VMEM((tm, HG_WIDTH), F32)] * 3
        + [pltpu.VMEM((CM_GROUPS, CHUNK, CHUNK), F32), pltpu.VMEM((CHUNK, CM_GROUPS), F32)],
        compiler_params=pltpu.CompilerParams(dimension_semantics=("arbitrary",),
                                             vmem_limit_bytes=VMEM_LIMIT_BYTES),
        name="mix_sample",
    )(x, s0, *consts)


def _memkv_kernel(m_ref, g_ref, wk_ref, wv_ref, k_ref, v_ref, kb_ref, vb_ref):
    mn = _rmsnorm(m_ref[...], g_ref[...]).astype(BF16)
    k = jnp.dot(mn, wk_ref[...], preferred_element_type=F32)
    v = jnp.dot(mn, wv_ref[...], preferred_element_type=F32)
    kb_ref[...] = k.astype(BF16)
    vb_ref[...] = v.astype(BF16)
    for src, dst in ((k, k_ref), (v, v_ref)):
        for r in range(dst.shape[0]):
            rows = src[r * N_MEM:(r + 1) * N_MEM]
            dst[r] = jnp.stack([rows[:, h * X_HEAD_DIM:(h + 1) * X_HEAD_DIM] for h in range(X_HEADS)],
                               axis=1)


def _memkv(mem, g_mem, w_ck, w_cv, tm):
    n = mem.shape[0]
    return pl.pallas_call(
        _memkv_kernel,
        grid=(n // tm,),
        in_specs=[pl.BlockSpec((tm, D_MODEL), lambda i: (i, 0)), _const_spec(g_mem.shape),
                  _const_spec(w_ck.shape), _const_spec(w_cv.shape)],
        out_specs=[pl.BlockSpec((tm // N_MEM, N_MEM, X_HEADS, X_HEAD_DIM), lambda i: (i, 0, 0, 0))] * 2
        + [pl.BlockSpec((tm, D_MODEL), lambda i: (i, 0))] * 2,
        out_shape=[jax.ShapeDtypeStruct((n // N_MEM, N_MEM, X_HEADS, X_HEAD_DIM), F32)] * 2
        + [jax.ShapeDtypeStruct((n, D_MODEL), BF16)] * 2,
        compiler_params=pltpu.CompilerParams(dimension_semantics=("arbitrary",),
                                             vmem_limit_bytes=VMEM_LIMIT_BYTES),
        name="memkv",
    )(mem, g_mem, w_ck, w_cv)


def _post_sample_kernel(x_ref, ca_ref, wco_ref, gm_ref, wup_ref, wdn_ref, gf_ref, y_ref, hm_s):
    x = x_ref[0] + jnp.dot(ca_ref[...].astype(BF16), wco_ref[...], preferred_element_type=F32)
    x = x + _mlp(_rmsnorm(x, gm_ref[...]).astype(BF16), wup_ref, wdn_ref, hm_s)
    y_ref[...] = _rmsnorm(x, gf_ref[...]).reshape(y_ref.shape)


def _post_sample(x, ca, w_co, g_mlp, w_up, w_down, g_final):
    nt, tm, _ = x.shape
    in_specs = [
        pl.BlockSpec((1, tm, D_MODEL), lambda i: (i, 0, 0)),
        pl.BlockSpec((tm, D_MODEL), lambda i: (i, 0)),
        _const_spec(w_co.shape), _const_spec(g_mlp.shape), _const_spec(w_up.shape),
        _const_spec(w_down.shape), _const_spec(g_final.shape),
    ]
    return pl.pallas_call(
        _post_sample_kernel,
        grid=(nt,),
        in_specs=in_specs,
        out_specs=pl.BlockSpec((tm // 4, 4, D_MODEL), lambda i: (i, 0, 0)),
        out_shape=jax.ShapeDtypeStruct((nt * tm // 4, 4, D_MODEL), F32),
        scratch_shapes=[pltpu.VMEM((tm, D_FF), BF16)],
        compiler_params=pltpu.CompilerParams(dimension_semantics=("arbitrary",),
                                             vmem_limit_bytes=VMEM_LIMIT_BYTES),
        name="post_sample",
    )(x, ca, w_co, g_mlp, w_up, w_down, g_final)


def _cast_kernel(*refs):
    n = len(refs) // 2
    for src, dst in zip(refs[:n], refs[n:]):
        dst[...] = src[...].astype(BF16)


def _cast_weights(*ws):
    steps = 8
    specs = [pl.BlockSpec((w.shape[0] // steps, w.shape[1]), lambda i: (i, 0)) for w in ws]
    return pl.pallas_call(
        _cast_kernel,
        grid=(steps,),
        in_specs=specs,
        out_specs=specs,
        out_shape=[jax.ShapeDtypeStruct(w.shape, BF16) for w in ws],
        compiler_params=pltpu.CompilerParams(dimension_semantics=("arbitrary",),
                                             vmem_limit_bytes=VMEM_LIMIT_BYTES),
        name="cast_weights",
    )(*ws)


def kernel(x_prompt, x_sample, mem_prompt, state_hgrn, cache_mem_k, cache_mem_v, lb_param, g_mix,
           w_in, hg_norm_g, ln_v_g, ln_v_b, w_s, b_s, w_out, g_cross, g_mem, w_cq, w_ck, w_cv, w_co,
           g_mlp, w_up, w_down, g_final):
    B, L, _ = x_prompt.shape
    DB, DL, _ = x_sample.shape
    assert DL == 4 and g_mix.shape[0] == 1

    row = lambda a: a.reshape(1, -1)
    win_b, wout_b, wcq_b, wco_b, wck_b, wcv_b, wup_b, wdn_b = _cast_weights(
        w_in[0], w_out[0], w_cq[0], w_co[0], w_ck[0], w_cv[0], w_up[0], w_down[0])
    gmix, hgn, lng, lnb = row(g_mix[0]), row(hg_norm_g[0]), row(ln_v_g[0]), row(ln_v_b[0])
    gcr, gmem, gmlp, gfin = row(g_cross[0]), row(g_mem[0]), row(g_mlp[0]), row(g_final)
    wmix = w_s[0]
    bcol = b_s[0].T

    tm_s = 128
    h_s, s_s, vn_s, hq_s = _mix_sample(x_sample, state_hgrn[0], lb_param, gmix, win_b, hgn, lng, lnb, wmix,
                                       bcol, wout_b, gcr, wcq_b, tm=tm_s)

    mk, mv, mk_b, mv_b = _memkv(mem_prompt.reshape(B * N_MEM, D_MODEL), gmem, wck_b, wcv_b, tm=512)
    y_p, s_p, ca_s = _prompt_layer(x_prompt, mk_b.reshape(B, N_MEM, D_MODEL), mv_b.reshape(B, N_MEM, D_MODEL),
                                   hq_s, cache_mem_k[0], cache_mem_v[0],
                                   lb_param, gmix, win_b, hgn, lng, lnb, wmix, bcol, wout_b,
                                   gcr, wcq_b, wco_b, gmlp, wup_b, wdn_b, gfin, tm=256)

    y_s = _post_sample(h_s.reshape(1, DB * DL, D_MODEL), ca_s, wco_b, gmlp, wup_b, wdn_b, gfin)

    return (y_p, y_s, s_p[None], s_s[None],
            mk[None], mv[None],
            vn_s[None])
```

```python
import functools

import jax
import jax.numpy as jnp
from jax import lax
from jax.experimental import pallas as pl
from jax.experimental.pallas import tpu as pltpu

F32 = jnp.float32
BF16 = jnp.bfloat16

D_MODEL = 1024
HG_WIDTH = 512
HG_HEADS = 4
HG_DK = 128
CM_WIDTH = 512
CM_GROUPS = 4
CM_GROUP_DIM = 128
IN_WIDTH = 4 * HG_WIDTH + 2 * CM_WIDTH
N_MEM = 256
X_HEADS = 4
X_HEAD_DIM = 256
D_FF = 4096
EPS = 1e-6

CHUNK = 128
SUB = 16
VMEM_LIMIT_BYTES = 62 * 1024 * 1024
LOG2E = 1.4426950408889634
FAST_BLOCK = 32
FAST_MAX_DECAY = 80.0


def _bdot(a, b):
    return jnp.dot(a.astype(BF16), b.astype(BF16), preferred_element_type=F32)


def _bdot_nt(a, b):
    return lax.dot_general(a.astype(BF16), b.astype(BF16), (((1,), (1,)), ((), ())),
                           preferred_element_type=F32)


def _rmsnorm(x, g):
    ms = jnp.mean(x * x, axis=-1, keepdims=True)
    return x * lax.rsqrt(ms + EPS) * g


def _sigmoid(x):
    return 0.5 * jnp.tanh(0.5 * x) + 0.5


def _gelu(x):
    return 0.5 * x * (1.0 + jnp.tanh(0.7978845608028654 * (x + 0.044715 * (x * x * x))))


def _softmax_rows(s):
    m = jnp.max(s, axis=-1, keepdims=True)
    e = jnp.exp(s - m)
    return e / jnp.sum(e, axis=-1, keepdims=True)


def _seg_cumsum(x, seg):
    n = x.shape[0]
    pos = lax.broadcasted_iota(jnp.int32, x.shape, 0) % seg
    s = 1
    while s < min(seg, 8):
        x = x + jnp.where(pos >= s, pltpu.roll(x, s, 0), 0.0)
        s *= 2
    while s < seg:
        parts = []
        for r0 in range(0, n, seg):
            parts.append(x[r0:r0 + s])
            parts.append(x[r0 + s:r0 + seg] + x[r0:r0 + seg - s])
        x = jnp.concatenate(parts, axis=0)
        s *= 2
    return x


def _lower_bound(lb_ref):
    lbp = lb_ref[...]
    lbe = jnp.exp(lbp - jnp.max(lbp, axis=0, keepdims=True))
    return lbe[0:1] / jnp.sum(lbe, axis=0, keepdims=True)


def _gates(zq, zf, lb, seg):
    th = jnp.tanh(0.5 * zf)
    logf = jnp.log(lb + (1.0 - lb) * (0.5 + 0.5 * th))
    return zq * _sigmoid(zq), (1.0 - lb) * (0.5 - 0.5 * th), _seg_cumsum(logf, seg)


def _block_decay(bcum, sample):
    if sample:
        return jnp.max(-bcum, axis=0, keepdims=True)
    d = -bcum[FAST_BLOCK - 1:FAST_BLOCK]
    for i in range(1, CHUNK // FAST_BLOCK):
        n0, n1 = i * FAST_BLOCK, (i + 1) * FAST_BLOCK
        d = jnp.maximum(d, bcum[n0 - 1:n0] - bcum[n1 - 1:n1])
    return d


def _chunk_masks(sample):
    row = lax.broadcasted_iota(jnp.int32, (CHUNK, CHUNK), 0)
    col = lax.broadcasted_iota(jnp.int32, (CHUNK, CHUNK), 1)
    if sample:
        return (row // 4 == col // 4) & (row >= col)
    return row >= col


def _scores_fast(q, k, bcum, sample):
    if sample:
        return _bdot_nt(q * jnp.exp(bcum), k * jnp.exp(-bcum))
    a_rows = []
    for i in range(CHUNK // FAST_BLOCK):
        n0, n1 = i * FAST_BLOCK, (i + 1) * FAST_BLOCK
        ref_b = bcum[n0 - 1:n0] if i else jnp.zeros((1, HG_DK), F32)
        qt = q[n0:n1] * jnp.exp(bcum[n0:n1] - ref_b)
        kt = k[:n1] * jnp.exp(ref_b - bcum[:n1])
        if n1 < CHUNK:
            kt = jnp.concatenate([kt, jnp.zeros((CHUNK - n1, HG_DK), F32)], axis=0)
        a_rows.append(_bdot_nt(qt, kt))
    return jnp.concatenate(a_rows, axis=0)


def _scores_exact(q, k, bcum, sample, mask):
    col8 = lax.broadcasted_iota(jnp.int32, (8, CHUNK), 1)
    b2 = bcum * LOG2E
    cexp = b2 - jnp.log2(k)
    a_rows = []
    for gb in range(CHUNK // SUB):
        lo = slice(gb * SUB, gb * SUB + 8)
        hi = slice(gb * SUB + 8, (gb + 1) * SUB)
        a_lo = jnp.zeros((8, CHUNK), F32)
        a_hi = jnp.zeros((8, CHUNK), F32)
        for s in range(SUB):
            sg = gb * SUB + s
            c_s = cexp[sg:sg + 1]
            if s < 8:
                p = q[lo] * jnp.exp2(b2[lo] - c_s)
                a_lo = jnp.where(col8 == sg, jnp.sum(p, axis=-1, keepdims=True), a_lo)
            if (not sample) or s >= 8:
                p = q[hi] * jnp.exp2(b2[hi] - c_s)
                a_hi = jnp.where(col8 == sg, jnp.sum(p, axis=-1, keepdims=True), a_hi)
        a_rows.append(a_lo)
        a_rows.append(a_hi)
    a = jnp.concatenate(a_rows, axis=0)
    if sample:
        return a
    o_rows = [jnp.zeros((SUB, CHUNK), F32)]
    for i in range(1, CHUNK // SUB):
        n = i * SUB
        ref_b = bcum[n - 1:n]
        qt = q[n:n + SUB] * jnp.exp(bcum[n:n + SUB] - ref_b)
        kt = k[:n] * jnp.exp(ref_b - bcum[:n])
        kt = jnp.concatenate([kt, jnp.zeros((CHUNK - n, HG_DK), F32)], axis=0)
        o_rows.append(_bdot_nt(qt, kt))
    return jnp.where(mask, a, 0.0) + jnp.concatenate(o_rows, axis=0)


def _hgrn_out(o, zg, g):
    o = o * lax.rsqrt(jnp.mean(o * o, axis=-1, keepdims=True) + EPS) * g
    return (o * _sigmoid(zg)).astype(BF16)


def _gmlp_chunk(z_ref, c0, rows, lng_ref, lnb_ref, wmix_ref, bcol_ref, mask, ocat_ref):
    u = _gelu(z_ref[rows, c0:c0 + CM_WIDTH])
    gv = _gelu(z_ref[rows, c0 + CM_WIDTH:c0 + 2 * CM_WIDTH])
    mu = jnp.mean(gv, axis=-1, keepdims=True)
    dv = gv - mu
    var = jnp.mean(dv * dv, axis=-1, keepdims=True)
    vn = dv * lax.rsqrt(var + EPS) * lng_ref[...] + lnb_ref[...]
    for g in range(CM_GROUPS):
        gs = slice(g * CM_GROUP_DIM, (g + 1) * CM_GROUP_DIM)
        wm = jnp.where(mask, wmix_ref[g], 0.0)
        mixed = _bdot(wm, vn[:, gs]) + bcol_ref[:, g:g + 1]
        ocat_ref[rows, HG_WIDTH + g * CM_GROUP_DIM:HG_WIDTH + (g + 1) * CM_GROUP_DIM] = (
            u[:, gs] * mixed).astype(BF16)
    return vn


def _sample_pair_probs(hq8, ck_ref):
    nrow = 8 * X_HEADS
    rh = lax.broadcasted_iota(jnp.int32, (nrow, N_MEM * X_HEADS), 0) // 8
    ch = lax.broadcasted_iota(jnp.int32, (nrow, N_MEM * X_HEADS), 1) % X_HEADS
    q = jnp.concatenate([hq8[:, h * X_HEAD_DIM:(h + 1) * X_HEAD_DIM] for h in range(X_HEADS)],
                        axis=0).astype(BF16)
    probs = []
    for r in range(2):
        k2 = ck_ref[r].reshape(N_MEM * X_HEADS, X_HEAD_DIM)
        s = jnp.where(rh == ch, _bdot_nt(q, k2) * (X_HEAD_DIM ** -0.5), -jnp.inf)
        probs.append(_softmax_rows(s))
    return probs


def _sample_pair_context(probs, cv_ref, ca_ref):
    first = (lax.broadcasted_iota(jnp.int32, (8 * X_HEADS, X_HEAD_DIM), 0) % 8) < 4
    outs = [_bdot(probs[r], cv_ref[r].reshape(N_MEM * X_HEADS, X_HEAD_DIM)) for r in range(2)]
    o = jnp.where(first, outs[0], outs[1])
    for h in range(X_HEADS):
        ca_ref[:, h * X_HEAD_DIM:(h + 1) * X_HEAD_DIM] = o[8 * h:8 * h + 8]


def _mlp(hn_bf16, wup_ref, wdn_ref, hm_s):
    for c in range(D_FF // D_MODEL):
        fs = slice(c * D_MODEL, (c + 1) * D_MODEL)
        hm = jnp.maximum(jnp.dot(hn_bf16, wup_ref[:, fs], preferred_element_type=F32), 0.0)
        hm_s[:, fs] = (hm * hm).astype(BF16)
    return jnp.dot(hm_s[...], wdn_ref[...], preferred_element_type=F32)


def _const_spec(shape):
    nd = len(shape)
    return pl.BlockSpec(shape, lambda *_: (0,) * nd, pipeline_mode=pl.Buffered(1))


def _mixattn_kernel(x_ref, mk_ref, mv_ref, lb_ref, gmix_ref, win_ref, hgn_ref, lng_ref, lnb_ref,
                    wmix_ref, bcol_ref, wout_ref, gc_ref, wcq_ref, wco_ref, x2_ref, sout_ref,
                    zg_ref, z_ref, ocat_ref, q_s, k_s, b_s, a_s, st0, st1, st2, st3, h1_s, h1n_s, ca_s,
                    *, tm, nt):
    t = pl.program_id(0)
    i = lax.rem(t, nt)
    nch = tm // CHUNK
    nj = nch * HG_HEADS
    mask = _chunk_masks(False)
    st_refs = (st0, st1, st2, st3)

    @pl.when(t == 0)
    def _():
        h1_s[...] = jnp.zeros_like(h1_s)
        h1n_s[...] = jnp.zeros_like(h1n_s)

    @pl.when(i == 0)
    def _():
        for st in st_refs:
            st[...] = jnp.zeros_like(st)

    x = x_ref[0]
    xn = _rmsnorm(x, gmix_ref[...]).astype(BF16)
    ngate = 2 * HG_WIDTH
    zg_ref[...] = jnp.dot(xn, win_ref[:, 0:ngate], preferred_element_type=F32)
    z_ref[...] = jnp.dot(xn, win_ref[:, ngate:IN_WIDTH], preferred_element_type=F32)
    lb_all = _lower_bound(lb_ref)
    dmax = jnp.zeros((1, HG_WIDTH), F32)
    for c in range(nch):
        rows = slice(c * CHUNK, (c + 1) * CHUNK)
        q, k, bcum = _gates(zg_ref[rows, 0:HG_WIDTH], zg_ref[rows, HG_WIDTH:2 * HG_WIDTH], lb_all, CHUNK)
        q_s[rows, :] = q
        k_s[rows, :] = k
        b_s[rows, :] = bcum
        dmax = jnp.maximum(dmax, _block_decay(bcum, False))
    fast_ok = jnp.max(dmax) < FAST_MAX_DECAY

    def fill_scores(j, fast):
        c, h = divmod(j, HG_HEADS)
        rows = slice(c * CHUNK, (c + 1) * CHUNK)
        cs = slice(h * HG_DK, (h + 1) * HG_DK)
        q, k, bcum = q_s[rows, cs], k_s[rows, cs], b_s[rows, cs]
        a = _scores_fast(q, k, bcum, False) if fast else _scores_exact(q, k, bcum, False, mask)
        a_s[j] = jnp.where(mask, a, 0.0).astype(BF16)

    def head_chunk(j):
        c, h = divmod(j, HG_HEADS)
        rows = slice(c * CHUNK, (c + 1) * CHUNK)
        cs = slice(h * HG_DK, (h + 1) * HG_DK)
        q, k, bcum = q_s[rows, cs], k_s[rows, cs], b_s[rows, cs]
        v = z_ref[rows, h * HG_DK:(h + 1) * HG_DK]
        zg = z_ref[rows, HG_WIDTH + h * HG_DK:HG_WIDTH + (h + 1) * HG_DK]
        eb = jnp.exp(bcum)
        kd = k * jnp.exp(bcum[CHUNK - 1:CHUNK] - bcum)
        st = st_refs[h][...]
        o = jnp.dot(a_s[j], v.astype(BF16), preferred_element_type=F32)
        o = o + _bdot_nt(q * eb, st)
        st_refs[h][...] = st * eb[CHUNK - 1:CHUNK] + _bdot(v.T, kd)
        ocat_ref[rows, cs] = _hgrn_out(o, zg, hgn_ref[:, cs])
        if h == HG_HEADS - 1:
            _gmlp_chunk(z_ref, 2 * HG_WIDTH, rows, lng_ref, lnb_ref, wmix_ref, bcol_ref, mask, ocat_ref)

    def region2(fast):
        scale = X_HEAD_DIM ** -0.5
        heads = [slice(h * X_HEAD_DIM, (h + 1) * X_HEAD_DIM) for h in range(X_HEADS)]
        hq = jnp.dot(h1n_s[...], wcq_ref[...], preferred_element_type=F32)
        for j in range(0, nj // 2):
            fill_scores(j, fast)
        probs = [_softmax_rows(_bdot_nt(hq[:, hs], mk_ref[0, :, hs]) * scale) for hs in heads]
        for j in range(nj // 2, nj):
            fill_scores(j, fast)
        for j in range(0, nj // 2):
            head_chunk(j)
        for h, hs in enumerate(heads):
            ca_s[:, hs] = _bdot(probs[h], mv_ref[0, :, hs]).astype(BF16)
        for j in range(nj // 2, nj):
            head_chunk(j)
        x2_ref[0] = h1_s[...] + jnp.dot(ca_s[...], wco_ref[...], preferred_element_type=F32)
        h1 = x_ref[0] + jnp.dot(ocat_ref[...], wout_ref[...], preferred_element_type=F32)
        h1_s[...] = h1
        h1n_s[...] = _rmsnorm(h1, gc_ref[...]).astype(BF16)

    @pl.when(fast_ok)
    def _():
        region2(True)

    @pl.when(jnp.logical_not(fast_ok))
    def _():
        region2(False)

    @pl.when(i == nt - 1)
    def _():
        for h in range(HG_HEADS):
            sout_ref[0, h] = st_refs[h][...].T


def _mixattn_layer(x, mk, mv, lb_param, g_mix, w_in, hg_norm_g, ln_v_g, ln_v_b, wmix, bcol, w_out,
                   g_cross, w_cq, w_co, tm):
    B, L, _ = x.shape
    nt = L // tm
    T = B * nt

    def cur(t):
        return jnp.minimum(t, T - 1)

    def prev(t):
        return jnp.maximum(t - 1, 0)

    consts = (lb_param, g_mix, w_in, hg_norm_g, ln_v_g, ln_v_b, wmix, bcol, w_out, g_cross, w_cq, w_co)
    in_specs = [
        pl.BlockSpec((1, tm, D_MODEL), lambda t: (cur(t) // nt, cur(t) % nt, 0)),
        pl.BlockSpec((1, N_MEM, D_MODEL), lambda t: (prev(t) // nt, 0, 0)),
        pl.BlockSpec((1, N_MEM, D_MODEL), lambda t: (prev(t) // nt, 0, 0)),
    ] + [_const_spec(a.shape) for a in consts]
    out_specs = [
        pl.BlockSpec((1, tm, D_MODEL), lambda t: (prev(t) // nt, prev(t) % nt, 0)),
        pl.BlockSpec((1, HG_HEADS, HG_DK, HG_DK), lambda t: (cur(t) // nt, 0, 0, 0)),
    ]
    scratch = [
        pltpu.VMEM((tm, 2 * HG_WIDTH), F32),
        pltpu.VMEM((tm, IN_WIDTH - 2 * HG_WIDTH), F32),
        pltpu.VMEM((tm, D_MODEL), BF16),
        pltpu.VMEM((tm, HG_WIDTH), F32), pltpu.VMEM((tm, HG_WIDTH), F32),
        pltpu.VMEM((tm, HG_WIDTH), F32),
        pltpu.VMEM((tm // CHUNK * HG_HEADS, CHUNK, CHUNK), BF16),
        pltpu.VMEM((HG_DK, HG_DK), F32), pltpu.VMEM((HG_DK, HG_DK), F32),
        pltpu.VMEM((HG_DK, HG_DK), F32), pltpu.VMEM((HG_DK, HG_DK), F32),
        pltpu.VMEM((tm, D_MODEL), F32),
        pltpu.VMEM((tm, D_MODEL), BF16),
        pltpu.VMEM((tm, D_MODEL), BF16),
    ]
    return pl.pallas_call(
        functools.partial(_mixattn_kernel, tm=tm, nt=nt),
        grid=(T + 1,),
        in_specs=in_specs,
        out_specs=out_specs,
        out_shape=[jax.ShapeDtypeStruct((B, L, D_MODEL), F32),
                   jax.ShapeDtypeStruct((B, HG_HEADS, HG_DK, HG_DK), F32)],
        scratch_shapes=scratch,
        compiler_params=pltpu.CompilerParams(dimension_semantics=("arbitrary",),
                                             vmem_limit_bytes=VMEM_LIMIT_BYTES),
        name="mix_attn",
    )(x, mk, mv, *consts)


def _mlp_kernel(x2_ref, hqs_ref, ck_ref, cv_ref, gm_ref, wup_ref, wdn_ref, gf_ref, y_ref, cas_ref, hm_s,
                *, npairs):
    x2 = x2_ref[0]
    hn = _rmsnorm(x2, gm_ref[...]).astype(BF16)
    nsl = D_FF // D_MODEL
    probs = {}
    for p in range(nsl):
        fs = slice(p * D_MODEL, (p + 1) * D_MODEL)
        hm = jnp.maximum(jnp.dot(hn, wup_ref[:, fs], preferred_element_type=F32), 0.0)
        hm_s[:, fs] = (hm * hm).astype(BF16)
        if p < npairs:
            probs[p] = _sample_pair_probs(hqs_ref[8 * p:8 * p + 8, :], ck_ref.at[2 * p:2 * p + 2])
    acc = x2
    for p in range(nsl):
        fs = slice(p * D_MODEL, (p + 1) * D_MODEL)
        acc = acc + jnp.dot(hm_s[:, fs], wdn_ref[fs, :], preferred_element_type=F32)
        if p < npairs:
            _sample_pair_context(probs[p], cv_ref.at[2 * p:2 * p + 2], cas_ref.at[8 * p:8 * p + 8, :])
    y_ref[0] = _rmsnorm(acc, gf_ref[...])


def _mlp_layer(x2, hqs, ck, cv, g_mlp, w_up, w_down, g_final, tm):
    B, L, _ = x2.shape
    nt = L // tm
    T = B * nt
    npairs = ck.shape[0] // (2 * T)
    assert npairs * 2 * T == ck.shape[0] and npairs <= D_FF // D_MODEL
    consts = (g_mlp, w_up, w_down, g_final)
    kv_spec = pl.BlockSpec((2 * npairs, N_MEM, X_HEADS, X_HEAD_DIM), lambda t: (t, 0, 0, 0))
    return pl.pallas_call(
        functools.partial(_mlp_kernel, npairs=npairs),
        grid=(T,),
        in_specs=[pl.BlockSpec((1, tm, D_MODEL), lambda t: (t // nt, t % nt, 0)),
                  pl.BlockSpec((8 * npairs, D_MODEL), lambda t: (t, 0)), kv_spec, kv_spec]
        + [_const_spec(a.shape) for a in consts],
        out_specs=[pl.BlockSpec((1, tm, D_MODEL), lambda t: (t // nt, t % nt, 0)),
                   pl.BlockSpec((8 * npairs, D_MODEL), lambda t: (t, 0))],
        out_shape=[jax.ShapeDtypeStruct((B, L, D_MODEL), F32), jax.ShapeDtypeStruct(hqs.shape, F32)],
        scratch_shapes=[pltpu.VMEM((tm, D_FF), BF16)],
        compiler_params=pltpu.CompilerParams(dimension_semantics=("arbitrary",),
                                             vmem_limit_bytes=VMEM_LIMIT_BYTES),
        name="mlp_prompt",
    )(x2, hqs, ck, cv, *consts)


def _mix_sample_kernel(x_ref, s0_ref, lb_ref, gmix_ref, win_ref, hgn_ref, lng_ref, lnb_ref, wmix_ref,
                       bcol_ref, wout_ref, gc_ref, wcq_ref, h_ref, sout_ref, vn_ref, hq_ref,
                       z_ref, ocat_ref, q_s, k_s, b_s, wm_s, bc_s, *, tm):
    x = x_ref[...].reshape(tm, D_MODEL)
    z_ref[...] = jnp.dot(_rmsnorm(x, gmix_ref[...]).astype(BF16), win_ref[...],
                         preferred_element_type=F32)
    lb_all = _lower_bound(lb_ref)
    mask = _chunk_masks(True)
    row = lax.broadcasted_iota(jnp.int32, (CHUNK, CHUNK), 0)
    col = lax.broadcasted_iota(jnp.int32, (CHUNK, CHUNK), 1)
    eye = row == col

    @pl.when(pl.program_id(0) == 0)
    def _():
        pick = (row % 4 == col).astype(BF16)
        for g in range(CM_GROUPS):
            corner = jnp.where((row < 4) & (col < 4), wmix_ref[g], 0.0)
            wm_s[g] = _bdot_nt(_bdot(pick, corner), pick)
        pos = lax.broadcasted_iota(jnp.int32, (CHUNK, CM_GROUPS), 0) % 4
        bc = jnp.zeros((CHUNK, CM_GROUPS), F32)
        for t in range(4):
            bc = jnp.where(pos == t, bcol_ref[t:t + 1, :], bc)
        bc_s[...] = bc
    r16 = lax.broadcasted_iota(jnp.int32, (SUB, HG_DK), 0)

    def gate_body(c, dmax):
        rows = pl.ds(pl.multiple_of(c * CHUNK, CHUNK), CHUNK)
        q, k, bcum = _gates(z_ref[rows, 0:HG_WIDTH], z_ref[rows, HG_WIDTH:2 * HG_WIDTH], lb_all, 4)
        q_s[rows, :] = q
        k_s[rows, :] = k
        b_s[rows, :] = bcum
        return jnp.maximum(dmax, _block_decay(bcum, True))

    dmax = lax.fori_loop(0, tm // CHUNK, gate_body, jnp.zeros((1, HG_WIDTH), F32))
    fast_ok = jnp.max(dmax) < FAST_MAX_DECAY

    def chunk_body(c, carry, *, fast):
        rows = pl.ds(pl.multiple_of(c * CHUNK, CHUNK), CHUNK)
        for h in range(HG_HEADS):
            cs = slice(h * HG_DK, (h + 1) * HG_DK)
            q, k, bcum = q_s[rows, cs], k_s[rows, cs], b_s[rows, cs]
            v = z_ref[rows, 2 * HG_WIDTH + h * HG_DK:2 * HG_WIDTH + (h + 1) * HG_DK]
            zg = z_ref[rows, 3 * HG_WIDTH + h * HG_DK:3 * HG_WIDTH + (h + 1) * HG_DK]
            a = _scores_fast(q, k, bcum, True) if fast else _scores_exact(q, k, bcum, True, mask)
            o = _bdot(jnp.where(mask, a, 0.0), v)
            eb = jnp.exp(bcum)
            qd = q * eb
            o_parts = []
            for gb in range(CHUNK // SUB):
                blk = slice(gb * SUB, (gb + 1) * SUB)
                qd_b, v_b, eb_b, k_b, b_b = qd[blk], v[blk], eb[blk], k[blk], bcum[blk]
                inter = jnp.zeros((SUB, HG_DK), F32)
                for j in range(4):
                    req = c * (CHUNK // 4) + gb * 4 + j
                    s0 = s0_ref[req, h]
                    last = 4 * j + 3
                    inter = jnp.where(r16 // 4 == j, _bdot(qd_b, s0), inter)
                    kd = jnp.where(r16 // 4 == j, k_b * jnp.exp(b_b[last:last + 1] - b_b), 0.0)
                    upd = lax.dot_general(kd.astype(BF16), v_b.astype(BF16),
                                          (((0,), (0,)), ((), ())), preferred_element_type=F32)
                    dcol = jnp.sum(jnp.where(eye, eb_b[last:last + 1], 0.0), axis=-1, keepdims=True)
                    sout_ref[req, h] = dcol * s0 + upd
                o_parts.append(inter)
            o = o + jnp.concatenate(o_parts, axis=0)
            ocat_ref[rows, cs] = _hgrn_out(o, zg, hgn_ref[:, cs])
        vn = _gmlp_chunk(z_ref, 4 * HG_WIDTH, rows, lng_ref, lnb_ref, wm_s, bc_s, mask, ocat_ref)
        vn_ref[pl.ds(pl.multiple_of(c * (CHUNK // 4), CHUNK // 4), CHUNK // 4)] = vn.reshape(
            CHUNK // 4, 4, CM_WIDTH)
        return carry

    @pl.when(fast_ok)
    def _():
        lax.fori_loop(0, tm // CHUNK, functools.partial(chunk_body, fast=True), 0)

    @pl.when(jnp.logical_not(fast_ok))
    def _():
        lax.fori_loop(0, tm // CHUNK, functools.partial(chunk_body, fast=False), 0)

    h = x + jnp.dot(ocat_ref[...], wout_ref[...], preferred_element_type=F32)
    h_ref[0] = h
    hq_ref[...] = jnp.dot(_rmsnorm(h, gc_ref[...]).astype(BF16), wcq_ref[...], preferred_element_type=F32)


def _mix_sample(x, s0, lb_param, g_mix, w_in, hg_norm_g, ln_v_g, ln_v_b, wmix, bcol, w_out, g_cross,
                w_cq, tm):
    nreq = tm // 4
    nt = x.shape[0] // nreq
    consts = (lb_param, g_mix, w_in, hg_norm_g, ln_v_g, ln_v_b, wmix, bcol, w_out, g_cross, w_cq)
    in_specs = [
        pl.BlockSpec((nreq, 4, D_MODEL), lambda i: (i, 0, 0)),
        pl.BlockSpec((nreq, HG_HEADS, HG_DK, HG_DK), lambda i: (i, 0, 0, 0)),
    ] + [_const_spec(a.shape) for a in consts]
    out_specs = [
        pl.BlockSpec((1, tm, D_MODEL), lambda i: (i, 0, 0)),
        pl.BlockSpec((nreq, HG_HEADS, HG_DK, HG_DK), lambda i: (i, 0, 0, 0)),
        pl.BlockSpec((nreq, 4, CM_WIDTH), lambda i: (i, 0, 0)),
        pl.BlockSpec((tm, D_MODEL), lambda i: (i, 0)),
    ]
    return pl.pallas_call(
        functools.partial(_mix_sample_kernel, tm=tm),
        grid=(nt,),
        in_specs=in_specs,
        out_specs=out_specs,
        out_shape=[jax.ShapeDtypeStruct((nt, tm, D_MODEL), F32),
                   jax.ShapeDtypeStruct(s0.shape, F32),
                   jax.ShapeDtypeStruct((nt * nreq, 4, CM_WIDTH), F32),
                   jax.ShapeDtypeStruct((nt * tm, D_MODEL), F32)],
        scratch_shapes=[pltpu.VMEM((tm, IN_WIDTH), F32), pltpu.VMEM((tm, D_MODEL), BF16)]
        + [pltpu.VMEM((tm, HG_WIDTH), F32)] * 3
        + [pltpu.VMEM((CM_GROUPS, CHUNK, CHUNK), F32), pltpu.VMEM((CHUNK, CM_GROUPS), F32)],
        compiler_params=pltpu.CompilerParams(dimension_semantics=("arbitrary",),
                                             vmem_limit_bytes=VMEM_LIMIT_BYTES),
        name="mix_sample",
    )(x, s0, *consts)


def _memkv_kernel(m_ref, g_ref, wk_ref, wv_ref, k_ref, v_ref, kb_ref, vb_ref):
    mn = _rmsnorm(m_ref[...], g_ref[...]).astype(BF16)
    k = jnp.dot(mn, wk_ref[...], preferred_element_type=F32)
    v = jnp.dot(mn, wv_ref[...], preferred_element_type=F32)
    kb_ref[...] = k.astype(BF16)
    vb_ref[...] = v.astype(BF16)
    for src, dst in ((k, k_ref), (v, v_ref)):
        for r in range(dst.shape[0]):
            rows = src[r * N_MEM:(r + 1) * N_MEM]
            dst[r] = jnp.stack([rows[:, h * X_HEAD_DIM:(h + 1) * X_HEAD_DIM] for h in range(X_HEADS)],
                               axis=1)


def _memkv(mem, g_mem, w_ck, w_cv, tm):
    n = mem.shape[0]
    return pl.pallas_call(
        _memkv_kernel,
        grid=(n // tm,),
        in_specs=[pl.BlockSpec((tm, D_MODEL), lambda i: (i, 0)), _const_spec(g_mem.shape),
                  _const_spec(w_ck.shape), _const_spec(w_cv.shape)],
        out_specs=[pl.BlockSpec((tm // N_MEM, N_MEM, X_HEADS, X_HEAD_DIM), lambda i: (i, 0, 0, 0))] * 2
        + [pl.BlockSpec((tm, D_MODEL), lambda i: (i, 0))] * 2,
        out_shape=[jax.ShapeDtypeStruct((n // N_MEM, N_MEM, X_HEADS, X_HEAD_DIM), F32)] * 2
        + [jax.ShapeDtypeStruct((n, D_MODEL), BF16)] * 2,
        compiler_params=pltpu.CompilerParams(dimension_semantics=("arbitrary",),
                                             vmem_limit_bytes=VMEM_LIMIT_BYTES),
        name="memkv",
    )(mem, g_mem, w_ck, w_cv)


def _post_sample_kernel(x_ref, ca_ref, wco_ref, gm_ref, wup_ref, wdn_ref, gf_ref, y_ref, hm_s):
    x = x_ref[0] + jnp.dot(ca_ref[...].astype(BF16), wco_ref[...], preferred_element_type=F32)
    x = x + _mlp(_rmsnorm(x, gm_ref[...]).astype(BF16), wup_ref, wdn_ref, hm_s)
    y_ref[...] = _rmsnorm(x, gf_ref[...]).reshape(y_ref.shape)


def _post_sample(x, ca, w_co, g_mlp, w_up, w_down, g_final):
    nt, tm, _ = x.shape
    in_specs = [
        pl.BlockSpec((1, tm, D_MODEL), lambda i: (i, 0, 0)),
        pl.BlockSpec((tm, D_MODEL), lambda i: (i, 0)),
        _const_spec(w_co.shape), _const_spec(g_mlp.shape), _const_spec(w_up.shape),
        _const_spec(w_down.shape), _const_spec(g_final.shape),
    ]
    return pl.pallas_call(
        _post_sample_kernel,
        grid=(nt,),
        in_specs=in_specs,
        out_specs=pl.BlockSpec((tm // 4, 4, D_MODEL), lambda i: (i, 0, 0)),
        out_shape=jax.ShapeDtypeStruct((nt * tm // 4, 4, D_MODEL), F32),
        scratch_shapes=[pltpu.VMEM((tm, D_FF), BF16)],
        compiler_params=pltpu.CompilerParams(dimension_semantics=("arbitrary",),
                                             vmem_limit_bytes=VMEM_LIMIT_BYTES),
        name="post_sample",
    )(x, ca, w_co, g_mlp, w_up, w_down, g_final)


def _cast_kernel(*refs):
    n = len(refs) // 2
    for src, dst in zip(refs[:n], refs[n:]):
        dst[...] = src[...].astype(BF16)


def _cast_weights(*ws):
    steps = 8
    specs = [pl.BlockSpec((w.shape[0] // steps, w.shape[1]), lambda i: (i, 0)) for w in ws]
    return pl.pallas_call(
        _cast_kernel,
        grid=(steps,),
        in_specs=specs,
        out_specs=specs,
        out_shape=[jax.ShapeDtypeStruct(w.shape, BF16) for w in ws],
        compiler_params=pltpu.CompilerParams(dimension_semantics=("arbitrary",),
                                             vmem_limit_bytes=VMEM_LIMIT_BYTES),
        name="cast_weights",
    )(*ws)


def kernel(x_prompt, x_sample, mem_prompt, state_hgrn, cache_mem_k, cache_mem_v, lb_param, g_mix,
           w_in, hg_norm_g, ln_v_g, ln_v_b, w_s, b_s, w_out, g_cross, g_mem, w_cq, w_ck, w_cv, w_co,
           g_mlp, w_up, w_down, g_final):
    B, L, _ = x_prompt.shape
    DB, DL, _ = x_sample.shape
    assert DL == 4 and g_mix.shape[0] == 1

    row = lambda a: a.reshape(1, -1)
    win_b, wout_b, wcq_b, wco_b, wck_b, wcv_b, wup_b, wdn_b = _cast_weights(
        w_in[0], w_out[0], w_cq[0], w_co[0], w_ck[0], w_cv[0], w_up[0], w_down[0])
    gmix, hgn, lng, lnb = row(g_mix[0]), row(hg_norm_g[0]), row(ln_v_g[0]), row(ln_v_b[0])
    gcr, gmem, gmlp, gfin = row(g_cross[0]), row(g_mem[0]), row(g_mlp[0]), row(g_final)
    wmix = w_s[0]
    bcol = b_s[0].T

    tm_s = 128
    h_s, s_s, vn_s, hq_s = _mix_sample(x_sample, state_hgrn[0], lb_param, gmix, win_b, hgn, lng, lnb, wmix,
                                       bcol, wout_b, gcr, wcq_b, tm=tm_s)

    mk, mv, mk_b, mv_b = _memkv(mem_prompt.reshape(B * N_MEM, D_MODEL), gmem, wck_b, wcv_b, tm=512)
    x2_p, s_p = _mixattn_layer(x_prompt, mk_b.reshape(B, N_MEM, D_MODEL), mv_b.reshape(B, N_MEM, D_MODEL),
                               lb_param, gmix, win_b, hgn, lng, lnb, wmix, bcol, wout_b, gcr, wcq_b, wco_b,
                               tm=512)
    y_p, ca_s = _mlp_layer(x2_p, hq_s, cache_mem_k[0], cache_mem_v[0], gmlp, wup_b, wdn_b, gfin, tm=512)

    y_s = _post_sample(h_s.reshape(1, DB * DL, D_MODEL), ca_s, wco_b, gmlp, wup_b, wdn_b, gfin)

    return (y_p, y_s, s_p[None], s_s[None],
            mk[None], mv[None],
            vn_s[None])
```

```python
import functools

import jax
import jax.numpy as jnp
from jax import lax
from jax.experimental import pallas as pl
from jax.experimental.pallas import tpu as pltpu

F32 = jnp.float32
BF16 = jnp.bfloat16

D_MODEL = 1024
HG_WIDTH = 512
HG_HEADS = 4
HG_DK = 128
CM_WIDTH = 512
CM_GROUPS = 4
CM_GROUP_DIM = 128
IN_WIDTH = 4 * HG_WIDTH + 2 * CM_WIDTH
N_MEM = 256
X_HEADS = 4
X_HEAD_DIM = 256
D_FF = 4096
EPS = 1e-6

CHUNK = 128
SUB = 16
VMEM_LIMIT_BYTES = 62 * 1024 * 1024
LOG2E = 1.4426950408889634
FAST_BLOCK = 32
FAST_MAX_DECAY = 80.0


def _bdot(a, b):
    return jnp.dot(a.astype(BF16), b.astype(BF16), preferred_element_type=F32)


def _bdot_nt(a, b):
    return lax.dot_general(a.astype(BF16), b.astype(BF16), (((1,), (1,)), ((), ())),
                           preferred_element_type=F32)


def _rmsnorm(x, g):
    ms = jnp.mean(x * x, axis=-1, keepdims=True)
    return x * lax.rsqrt(ms + EPS) * g


def _sigmoid(x):
    return 0.5 * jnp.tanh(0.5 * x) + 0.5


def _gelu(x):
    return 0.5 * x * (1.0 + jnp.tanh(0.7978845608028654 * (x + 0.044715 * (x * x * x))))


def _softmax_rows(s):
    m = jnp.max(s, axis=-1, keepdims=True)
    e = jnp.exp(s - m)
    return e / jnp.sum(e, axis=-1, keepdims=True)


def _seg_cumsum(x, seg):
    n = x.shape[0]
    pos = lax.broadcasted_iota(jnp.int32, x.shape, 0) % seg
    s = 1
    while s < min(seg, 8):
        x = x + jnp.where(pos >= s, pltpu.roll(x, s, 0), 0.0)
        s *= 2
    while s < seg:
        parts = []
        for r0 in range(0, n, seg):
            parts.append(x[r0:r0 + s])
            parts.append(x[r0 + s:r0 + seg] + x[r0:r0 + seg - s])
        x = jnp.concatenate(parts, axis=0)
        s *= 2
    return x


def _lower_bound(lb_ref):
    lbp = lb_ref[...]
    lbe = jnp.exp(lbp - jnp.max(lbp, axis=0, keepdims=True))
    return lbe[0:1] / jnp.sum(lbe, axis=0, keepdims=True)


def _gates(zq, zf, lb, seg):
    th = jnp.tanh(0.5 * zf)
    logf = jnp.log(lb + (1.0 - lb) * (0.5 + 0.5 * th))
    return zq * _sigmoid(zq), (1.0 - lb) * (0.5 - 0.5 * th), _seg_cumsum(logf, seg)


def _block_decay(bcum, sample):
    if sample:
        return jnp.max(-bcum, axis=0, keepdims=True)
    d = -bcum[FAST_BLOCK - 1:FAST_BLOCK]
    for i in range(1, CHUNK // FAST_BLOCK):
        n0, n1 = i * FAST_BLOCK, (i + 1) * FAST_BLOCK
        d = jnp.maximum(d, bcum[n0 - 1:n0] - bcum[n1 - 1:n1])
    return d


def _chunk_masks(sample):
    row = lax.broadcasted_iota(jnp.int32, (CHUNK, CHUNK), 0)
    col = lax.broadcasted_iota(jnp.int32, (CHUNK, CHUNK), 1)
    if sample:
        return (row // 4 == col // 4) & (row >= col)
    return row >= col


def _scores_fast(q, k, bcum, sample):
    if sample:
        return _bdot_nt(q * jnp.exp(bcum), k * jnp.exp(-bcum))
    a_rows = []
    for i in range(CHUNK // FAST_BLOCK):
        n0, n1 = i * FAST_BLOCK, (i + 1) * FAST_BLOCK
        ref_b = bcum[n0 - 1:n0] if i else jnp.zeros((1, HG_DK), F32)
        qt = q[n0:n1] * jnp.exp(bcum[n0:n1] - ref_b)
        kt = k[:n1] * jnp.exp(ref_b - bcum[:n1])
        if n1 < CHUNK:
            kt = jnp.concatenate([kt, jnp.zeros((CHUNK - n1, HG_DK), F32)], axis=0)
        a_rows.append(_bdot_nt(qt, kt))
    return jnp.concatenate(a_rows, axis=0)


def _scores_exact(q, k, bcum, sample, mask):
    col8 = lax.broadcasted_iota(jnp.int32, (8, CHUNK), 1)
    b2 = bcum * LOG2E
    cexp = b2 - jnp.log2(k)
    a_rows = []
    for gb in range(CHUNK // SUB):
        lo = slice(gb * SUB, gb * SUB + 8)
        hi = slice(gb * SUB + 8, (gb + 1) * SUB)
        a_lo = jnp.zeros((8, CHUNK), F32)
        a_hi = jnp.zeros((8, CHUNK), F32)
        for s in range(SUB):
            sg = gb * SUB + s
            c_s = cexp[sg:sg + 1]
            if s < 8:
                p = q[lo] * jnp.exp2(b2[lo] - c_s)
                a_lo = jnp.where(col8 == sg, jnp.sum(p, axis=-1, keepdims=True), a_lo)
            if (not sample) or s >= 8:
                p = q[hi] * jnp.exp2(b2[hi] - c_s)
                a_hi = jnp.where(col8 == sg, jnp.sum(p, axis=-1, keepdims=True), a_hi)
        a_rows.append(a_lo)
        a_rows.append(a_hi)
    a = jnp.concatenate(a_rows, axis=0)
    if sample:
        return a
    o_rows = [jnp.zeros((SUB, CHUNK), F32)]
    for i in range(1, CHUNK // SUB):
        n = i * SUB
        ref_b = bcum[n - 1:n]
        qt = q[n:n + SUB] * jnp.exp(bcum[n:n + SUB] - ref_b)
        kt = k[:n] * jnp.exp(ref_b - bcum[:n])
        kt = jnp.concatenate([kt, jnp.zeros((CHUNK - n, HG_DK), F32)], axis=0)
        o_rows.append(_bdot_nt(qt, kt))
    return jnp.where(mask, a, 0.0) + jnp.concatenate(o_rows, axis=0)


def _hgrn_out(o, zg, g):
    o = o * lax.rsqrt(jnp.mean(o * o, axis=-1, keepdims=True) + EPS) * g
    return (o * _sigmoid(zg)).astype(BF16)


def _gmlp_chunk(z_ref, c0, rows, lng_ref, lnb_ref, wmix_ref, bcol_ref, mask, ocat_ref):
    u = _gelu(z_ref[rows, c0:c0 + CM_WIDTH])
    gv = _gelu(z_ref[rows, c0 + CM_WIDTH:c0 + 2 * CM_WIDTH])
    mu = jnp.mean(gv, axis=-1, keepdims=True)
    dv = gv - mu
    var = jnp.mean(dv * dv, axis=-1, keepdims=True)
    vn = dv * lax.rsqrt(var + EPS) * lng_ref[...] + lnb_ref[...]
    for g in range(CM_GROUPS):
        gs = slice(g * CM_GROUP_DIM, (g + 1) * CM_GROUP_DIM)
        wm = jnp.where(mask, wmix_ref[g], 0.0)
        mixed = _bdot(wm, vn[:, gs]) + bcol_ref[:, g:g + 1]
        ocat_ref[rows, HG_WIDTH + g * CM_GROUP_DIM:HG_WIDTH + (g + 1) * CM_GROUP_DIM] = (
            u[:, gs] * mixed).astype(BF16)
    return vn


def _sample_pair_probs(hq8, ck_ref):
    nrow = 8 * X_HEADS
    rh = lax.broadcasted_iota(jnp.int32, (nrow, N_MEM * X_HEADS), 0) // 8
    ch = lax.broadcasted_iota(jnp.int32, (nrow, N_MEM * X_HEADS), 1) % X_HEADS
    q = jnp.concatenate([hq8[:, h * X_HEAD_DIM:(h + 1) * X_HEAD_DIM] for h in range(X_HEADS)],
                        axis=0).astype(BF16)
    probs = []
    for r in range(2):
        k2 = ck_ref[r].reshape(N_MEM * X_HEADS, X_HEAD_DIM)
        s = jnp.where(rh == ch, _bdot_nt(q, k2) * (X_HEAD_DIM ** -0.5), -jnp.inf)
        probs.append(_softmax_rows(s))
    return probs


def _sample_pair_context(probs, cv_ref, ca_ref):
    first = (lax.broadcasted_iota(jnp.int32, (8 * X_HEADS, X_HEAD_DIM), 0) % 8) < 4
    outs = [_bdot(probs[r], cv_ref[r].reshape(N_MEM * X_HEADS, X_HEAD_DIM)) for r in range(2)]
    o = jnp.where(first, outs[0], outs[1])
    for h in range(X_HEADS):
        ca_ref[:, h * X_HEAD_DIM:(h + 1) * X_HEAD_DIM] = o[8 * h:8 * h + 8]


def _mlp(hn_bf16, wup_ref, wdn_ref, hm_s):
    for c in range(D_FF // D_MODEL):
        fs = slice(c * D_MODEL, (c + 1) * D_MODEL)
        hm = jnp.maximum(jnp.dot(hn_bf16, wup_ref[:, fs], preferred_element_type=F32), 0.0)
        hm_s[:, fs] = (hm * hm).astype(BF16)
    return jnp.dot(hm_s[...], wdn_ref[...], preferred_element_type=F32)


def _const_spec(shape):
    nd = len(shape)
    return pl.BlockSpec(shape, lambda *_: (0,) * nd, pipeline_mode=pl.Buffered(1))


def _mixattn_kernel(x_ref, mk_ref, mv_ref, hqs_ref, ck_ref, cv_ref, lb_ref, gmix_ref, win_ref, hgn_ref,
                    lng_ref, lnb_ref, wmix_ref, bcol_ref, wout_ref, gc_ref, wcq_ref, wco_ref,
                    x2_ref, sout_ref, cas_ref,
                    zg_ref, z_ref, ocat_ref, q_s, k_s, b_s, a_s, st0, st1, st2, st3, h1_s, h1n_s, ca_s,
                    *, tm, nt, npairs):
    t = pl.program_id(0)
    i = lax.rem(t, nt)
    nch = tm // CHUNK
    nj = nch * HG_HEADS
    mask = _chunk_masks(False)
    st_refs = (st0, st1, st2, st3)

    @pl.when(t == 0)
    def _():
        h1_s[...] = jnp.zeros_like(h1_s)
        h1n_s[...] = jnp.zeros_like(h1n_s)

    @pl.when(i == 0)
    def _():
        for st in st_refs:
            st[...] = jnp.zeros_like(st)

    x = x_ref[0]
    xn = _rmsnorm(x, gmix_ref[...]).astype(BF16)
    ngate = 2 * HG_WIDTH
    zg_ref[...] = jnp.dot(xn, win_ref[:, 0:ngate], preferred_element_type=F32)
    z_ref[...] = jnp.dot(xn, win_ref[:, ngate:IN_WIDTH], preferred_element_type=F32)
    lb_all = _lower_bound(lb_ref)
    dmax = jnp.zeros((1, HG_WIDTH), F32)
    for c in range(nch):
        rows = slice(c * CHUNK, (c + 1) * CHUNK)
        q, k, bcum = _gates(zg_ref[rows, 0:HG_WIDTH], zg_ref[rows, HG_WIDTH:2 * HG_WIDTH], lb_all, CHUNK)
        q_s[rows, :] = q
        k_s[rows, :] = k
        b_s[rows, :] = bcum
        dmax = jnp.maximum(dmax, _block_decay(bcum, False))
    fast_ok = jnp.max(dmax) < FAST_MAX_DECAY

    def fill_scores(j, fast):
        c, h = divmod(j, HG_HEADS)
        rows = slice(c * CHUNK, (c + 1) * CHUNK)
        cs = slice(h * HG_DK, (h + 1) * HG_DK)
        q, k, bcum = q_s[rows, cs], k_s[rows, cs], b_s[rows, cs]
        a = _scores_fast(q, k, bcum, False) if fast else _scores_exact(q, k, bcum, False, mask)
        a_s[j] = jnp.where(mask, a, 0.0).astype(BF16)

    def head_chunk(j):
        c, h = divmod(j, HG_HEADS)
        rows = slice(c * CHUNK, (c + 1) * CHUNK)
        cs = slice(h * HG_DK, (h + 1) * HG_DK)
        q, k, bcum = q_s[rows, cs], k_s[rows, cs], b_s[rows, cs]
        v = z_ref[rows, h * HG_DK:(h + 1) * HG_DK]
        zg = z_ref[rows, HG_WIDTH + h * HG_DK:HG_WIDTH + (h + 1) * HG_DK]
        eb = jnp.exp(bcum)
        kd = k * jnp.exp(bcum[CHUNK - 1:CHUNK] - bcum)
        st = st_refs[h][...]
        o = jnp.dot(a_s[j], v.astype(BF16), preferred_element_type=F32)
        o = o + _bdot_nt(q * eb, st)
        st_refs[h][...] = st * eb[CHUNK - 1:CHUNK] + _bdot(v.T, kd)
        ocat_ref[rows, cs] = _hgrn_out(o, zg, hgn_ref[:, cs])
        if h == HG_HEADS - 1:
            _gmlp_chunk(z_ref, 2 * HG_WIDTH, rows, lng_ref, lnb_ref, wmix_ref, bcol_ref, mask, ocat_ref)

    def region2(fast):
        scale = X_HEAD_DIM ** -0.5
        heads = [slice(h * X_HEAD_DIM, (h + 1) * X_HEAD_DIM) for h in range(X_HEADS)]
        hq = jnp.dot(h1n_s[...], wcq_ref[...], preferred_element_type=F32)
        for j in range(0, nj // 2):
            fill_scores(j, fast)
        probs = [_softmax_rows(_bdot_nt(hq[:, hs], mk_ref[0, :, hs]) * scale) for hs in heads]
        for j in range(nj // 2, nj):
            fill_scores(j, fast)
        sprobs = [_sample_pair_probs(hqs_ref[8 * p:8 * p + 8, :], ck_ref.at[2 * p:2 * p + 2])
                  for p in range(npairs)]
        for j in range(0, nj // 2):
            head_chunk(j)
        for h, hs in enumerate(heads):
            ca_s[:, hs] = _bdot(probs[h], mv_ref[0, :, hs]).astype(BF16)
        for j in range(nj // 2, nj):
            head_chunk(j)
        for p in range(npairs):
            _sample_pair_context(sprobs[p], cv_ref.at[2 * p:2 * p + 2], cas_ref.at[8 * p:8 * p + 8, :])
        x2_ref[0] = h1_s[...] + jnp.dot(ca_s[...], wco_ref[...], preferred_element_type=F32)
        h1 = x_ref[0] + jnp.dot(ocat_ref[...], wout_ref[...], preferred_element_type=F32)
        h1_s[...] = h1
        h1n_s[...] = _rmsnorm(h1, gc_ref[...]).astype(BF16)

    @pl.when(fast_ok)
    def _():
        region2(True)

    @pl.when(jnp.logical_not(fast_ok))
    def _():
        region2(False)

    @pl.when(i == nt - 1)
    def _():
        for h in range(HG_HEADS):
            sout_ref[0, h] = st_refs[h][...].T


def _mixattn_layer(x, mk, mv, hqs, ck, cv, lb_param, g_mix, w_in, hg_norm_g, ln_v_g, ln_v_b, wmix, bcol,
                   w_out, g_cross, w_cq, w_co, tm):
    B, L, _ = x.shape
    nt = L // tm
    T = B * nt
    npairs = ck.shape[0] // (2 * T)
    assert npairs * 2 * T == ck.shape[0]

    def cur(t):
        return jnp.minimum(t, T - 1)

    def prev(t):
        return jnp.maximum(t - 1, 0)

    consts = (lb_param, g_mix, w_in, hg_norm_g, ln_v_g, ln_v_b, wmix, bcol, w_out, g_cross, w_cq, w_co)
    kv_spec = pl.BlockSpec((2 * npairs, N_MEM, X_HEADS, X_HEAD_DIM), lambda t: (cur(t), 0, 0, 0))
    in_specs = [
        pl.BlockSpec((1, tm, D_MODEL), lambda t: (cur(t) // nt, cur(t) % nt, 0)),
        pl.BlockSpec((1, N_MEM, D_MODEL), lambda t: (prev(t) // nt, 0, 0)),
        pl.BlockSpec((1, N_MEM, D_MODEL), lambda t: (prev(t) // nt, 0, 0)),
        pl.BlockSpec((8 * npairs, D_MODEL), lambda t: (cur(t), 0)), kv_spec, kv_spec,
    ] + [_const_spec(a.shape) for a in consts]
    out_specs = [
        pl.BlockSpec((1, tm, D_MODEL), lambda t: (prev(t) // nt, prev(t) % nt, 0)),
        pl.BlockSpec((1, HG_HEADS, HG_DK, HG_DK), lambda t: (cur(t) // nt, 0, 0, 0)),
        pl.BlockSpec((8 * npairs, D_MODEL), lambda t: (cur(t), 0)),
    ]
    scratch = [
        pltpu.VMEM((tm, 2 * HG_WIDTH), F32),
        pltpu.VMEM((tm, IN_WIDTH - 2 * HG_WIDTH), F32),
        pltpu.VMEM((tm, D_MODEL), BF16),
        pltpu.VMEM((tm, HG_WIDTH), F32), pltpu.VMEM((tm, HG_WIDTH), F32),
        pltpu.VMEM((tm, HG_WIDTH), F32),
        pltpu.VMEM((tm // CHUNK * HG_HEADS, CHUNK, CHUNK), BF16),
        pltpu.VMEM((HG_DK, HG_DK), F32), pltpu.VMEM((HG_DK, HG_DK), F32),
        pltpu.VMEM((HG_DK, HG_DK), F32), pltpu.VMEM((HG_DK, HG_DK), F32),
        pltpu.VMEM((tm, D_MODEL), F32),
        pltpu.VMEM((tm, D_MODEL), BF16),
        pltpu.VMEM((tm, D_MODEL), BF16),
    ]
    return pl.pallas_call(
        functools.partial(_mixattn_kernel, tm=tm, nt=nt, npairs=npairs),
        grid=(T + 1,),
        in_specs=in_specs,
        out_specs=out_specs,
        out_shape=[jax.ShapeDtypeStruct((B, L, D_MODEL), F32),
                   jax.ShapeDtypeStruct((B, HG_HEADS, HG_DK, HG_DK), F32),
                   jax.ShapeDtypeStruct(hqs.shape, F32)],
        scratch_shapes=scratch,
        compiler_params=pltpu.CompilerParams(dimension_semantics=("arbitrary",),
                                             vmem_limit_bytes=VMEM_LIMIT_BYTES),
        name="mix_attn",
    )(x, mk, mv, hqs, ck, cv, *consts)


def _mlp_kernel(x2_ref, gm_ref, wup_ref, wdn_ref, gf_ref, y_ref, hm_s):
    x2 = x2_ref[0]
    y_ref[0] = _rmsnorm(x2 + _mlp(_rmsnorm(x2, gm_ref[...]).astype(BF16), wup_ref, wdn_ref, hm_s),
                        gf_ref[...])


def _mlp_layer(x2, g_mlp, w_up, w_down, g_final, tm):
    B, L, _ = x2.shape
    nt = L // tm
    consts = (g_mlp, w_up, w_down, g_final)
    tile = pl.BlockSpec((1, tm, D_MODEL), lambda t: (t // nt, t % nt, 0))
    return pl.pallas_call(
        _mlp_kernel,
        grid=(B * nt,),
        in_specs=[tile] + [_const_spec(a.shape) for a in consts],
        out_specs=tile,
        out_shape=jax.ShapeDtypeStruct((B, L, D_MODEL), F32),
        scratch_shapes=[pltpu.VMEM((tm, D_FF), BF16)],
        compiler_params=pltpu.CompilerParams(dimension_semantics=("arbitrary",),
                                             vmem_limit_bytes=VMEM_LIMIT_BYTES),
        name="mlp_prompt",
    )(x2, *consts)


def _mix_sample_kernel(x_ref, s0_ref, lb_ref, gmix_ref, win_ref, hgn_ref, lng_ref, lnb_ref, wmix_ref,
                       bcol_ref, wout_ref, gc_ref, wcq_ref, h_ref, sout_ref, vn_ref, hq_ref,
                       z_ref, ocat_ref, q_s, k_s, b_s, wm_s, bc_s, *, tm):
    x = x_ref[...].reshape(tm, D_MODEL)
    z_ref[...] = jnp.dot(_rmsnorm(x, gmix_ref[...]).astype(BF16), win_ref[...],
                         preferred_element_type=F32)
    lb_all = _lower_bound(lb_ref)
    mask = _chunk_masks(True)
    row = lax.broadcasted_iota(jnp.int32, (CHUNK, CHUNK), 0)
    col = lax.broadcasted_iota(jnp.int32, (CHUNK, CHUNK), 1)
    eye = row == col

    @pl.when(pl.program_id(0) == 0)
    def _():
        pick = (row % 4 == col).astype(BF16)
        for g in range(CM_GROUPS):
            corner = jnp.where((row < 4) & (col < 4), wmix_ref[g], 0.0)
            wm_s[g] = _bdot_nt(_bdot(pick, corner), pick)
        pos = lax.broadcasted_iota(jnp.int32, (CHUNK, CM_GROUPS), 0) % 4
        bc = jnp.zeros((CHUNK, CM_GROUPS), F32)
        for t in range(4):
            bc = jnp.where(pos == t, bcol_ref[t:t + 1, :], bc)
        bc_s[...] = bc
    r16 = lax.broadcasted_iota(jnp.int32, (SUB, HG_DK), 0)

    def gate_body(c, dmax):
        rows = pl.ds(pl.multiple_of(c * CHUNK, CHUNK), CHUNK)
        q, k, bcum = _gates(z_ref[rows, 0:HG_WIDTH], z_ref[rows, HG_WIDTH:2 * HG_WIDTH], lb_all, 4)
        q_s[rows, :] = q
        k_s[rows, :] = k
        b_s[rows, :] = bcum
        return jnp.maximum(dmax, _block_decay(bcum, True))

    dmax = lax.fori_loop(0, tm // CHUNK, gate_body, jnp.zeros((1, HG_WIDTH), F32))
    fast_ok = jnp.max(dmax) < FAST_MAX_DECAY

    def chunk_body(c, carry, *, fast):
        rows = pl.ds(pl.multiple_of(c * CHUNK, CHUNK), CHUNK)
        for h in range(HG_HEADS):
            cs = slice(h * HG_DK, (h + 1) * HG_DK)
            q, k, bcum = q_s[rows, cs], k_s[rows, cs], b_s[rows, cs]
            v = z_ref[rows, 2 * HG_WIDTH + h * HG_DK:2 * HG_WIDTH + (h + 1) * HG_DK]
            zg = z_ref[rows, 3 * HG_WIDTH + h * HG_DK:3 * HG_WIDTH + (h + 1) * HG_DK]
            a = _scores_fast(q, k, bcum, True) if fast else _scores_exact(q, k, bcum, True, mask)
            o = _bdot(jnp.where(mask, a, 0.0), v)
            eb = jnp.exp(bcum)
            qd = q * eb
            o_parts = []
            for gb in range(CHUNK // SUB):
                blk = slice(gb * SUB, (gb + 1) * SUB)
                qd_b, v_b, eb_b, k_b, b_b = qd[blk], v[blk], eb[blk], k[blk], bcum[blk]
                inter = jnp.zeros((SUB, HG_DK), F32)
                for j in range(4):
                    req = c * (CHUNK // 4) + gb * 4 + j
                    s0 = s0_ref[req, h]
                    last = 4 * j + 3
                    inter = jnp.where(r16 // 4 == j, _bdot(qd_b, s0), inter)
                    kd = jnp.where(r16 // 4 == j, k_b * jnp.exp(b_b[last:last + 1] - b_b), 0.0)
                    upd = lax.dot_general(kd.astype(BF16), v_b.astype(BF16),
                                          (((0,), (0,)), ((), ())), preferred_element_type=F32)
                    dcol = jnp.sum(jnp.where(eye, eb_b[last:last + 1], 0.0), axis=-1, keepdims=True)
                    sout_ref[req, h] = dcol * s0 + upd
                o_parts.append(inter)
            o = o + jnp.concatenate(o_parts, axis=0)
            ocat_ref[rows, cs] = _hgrn_out(o, zg, hgn_ref[:, cs])
        vn = _gmlp_chunk(z_ref, 4 * HG_WIDTH, rows, lng_ref, lnb_ref, wm_s, bc_s, mask, ocat_ref)
        vn_ref[pl.ds(pl.multiple_of(c * (CHUNK // 4), CHUNK // 4), CHUNK // 4)] = vn.reshape(
            CHUNK // 4, 4, CM_WIDTH)
        return carry

    @pl.when(fast_ok)
    def _():
        lax.fori_loop(0, tm // CHUNK, functools.partial(chunk_body, fast=True), 0)

    @pl.when(jnp.logical_not(fast_ok))
    def _():
        lax.fori_loop(0, tm // CHUNK, functools.partial(chunk_body, fast=False), 0)

    h = x + jnp.dot(ocat_ref[...], wout_ref[...], preferred_element_type=F32)
    h_ref[0] = h
    hq_ref[...] = jnp.dot(_rmsnorm(h, gc_ref[...]).astype(BF16), wcq_ref[...], preferred_element_type=F32)


def _mix_sample(x, s0, lb_param, g_mix, w_in, hg_norm_g, ln_v_g, ln_v_b, wmix, bcol, w_out, g_cross,
                w_cq, tm):
    nreq = tm // 4
    nt = x.shape[0] // nreq
    consts = (lb_param, g_mix, w_in, hg_norm_g, ln_v_g, ln_v_b, wmix, bcol, w_out, g_cross, w_cq)
    in_specs = [
        pl.BlockSpec((nreq, 4, D_MODEL), lambda i: (i, 0, 0)),
        pl.BlockSpec((nreq, HG_HEADS, HG_DK, HG_DK), lambda i: (i, 0, 0, 0)),
    ] + [_const_spec(a.shape) for a in consts]
    out_specs = [
        pl.BlockSpec((1, tm, D_MODEL), lambda i: (i, 0, 0)),
        pl.BlockSpec((nreq, HG_HEADS, HG_DK, HG_DK), lambda i: (i, 0, 0, 0)),
        pl.BlockSpec((nreq, 4, CM_WIDTH), lambda i: (i, 0, 0)),
        pl.BlockSpec((tm, D_MODEL), lambda i: (i, 0)),
    ]
    return pl.pallas_call(
        functools.partial(_mix_sample_kernel, tm=tm),
        grid=(nt,),
        in_specs=in_specs,
        out_specs=out_specs,
        out_shape=[jax.ShapeDtypeStruct((nt, tm, D_MODEL), F32),
                   jax.ShapeDtypeStruct(s0.shape, F32),
                   jax.ShapeDtypeStruct((nt * nreq, 4, CM_WIDTH), F32),
                   jax.ShapeDtypeStruct((nt * tm, D_MODEL), F32)],
        scratch_shapes=[pltpu.VMEM((tm, IN_WIDTH), F32), pltpu.VMEM((tm, D_MODEL), BF16)]
        + [pltpu.VMEM((tm, HG_WIDTH), F32)] * 3
        + [pltpu.VMEM((CM_GROUPS, CHUNK, CHUNK), F32), pltpu.VMEM((CHUNK, CM_GROUPS), F32)],
        compiler_params=pltpu.CompilerParams(dimension_semantics=("arbitrary",),
                                             vmem_limit_bytes=VMEM_LIMIT_BYTES),
        name="mix_sample",
    )(x, s0, *consts)


def _memkv_kernel(m_ref, g_ref, wk_ref, wv_ref, k_ref, v_ref, kb_ref, vb_ref):
    mn = _rmsnorm(m_ref[...], g_ref[...]).astype(BF16)
    k = jnp.dot(mn, wk_ref[...], preferred_element_type=F32)
    v = jnp.dot(mn, wv_ref[...], preferred_element_type=F32)
    kb_ref[...] = k.astype(BF16)
    vb_ref[...] = v.astype(BF16)
    for src, dst in ((k, k_ref), (v, v_ref)):
        for r in range(dst.shape[0]):
            rows = src[r * N_MEM:(r + 1) * N_MEM]
            dst[r] = jnp.stack([rows[:, h * X_HEAD_DIM:(h + 1) * X_HEAD_DIM] for h in range(X_HEADS)],
                               axis=1)


def _memkv(mem, g_mem, w_ck, w_cv, tm):
    n = mem.shape[0]
    return pl.pallas_call(
        _memkv_kernel,
        grid=(n // tm,),
        in_specs=[pl.BlockSpec((tm, D_MODEL), lambda i: (i, 0)), _const_spec(g_mem.shape),
                  _const_spec(w_ck.shape), _const_spec(w_cv.shape)],
        out_specs=[pl.BlockSpec((tm // N_MEM, N_MEM, X_HEADS, X_HEAD_DIM), lambda i: (i, 0, 0, 0))] * 2
        + [pl.BlockSpec((tm, D_MODEL), lambda i: (i, 0))] * 2,
        out_shape=[jax.ShapeDtypeStruct((n // N_MEM, N_MEM, X_HEADS, X_HEAD_DIM), F32)] * 2
        + [jax.ShapeDtypeStruct((n, D_MODEL), BF16)] * 2,
        compiler_params=pltpu.CompilerParams(dimension_semantics=("arbitrary",),
                                             vmem_limit_bytes=VMEM_LIMIT_BYTES),
        name="memkv",
    )(mem, g_mem, w_ck, w_cv)


def _post_sample_kernel(x_ref, ca_ref, wco_ref, gm_ref, wup_ref, wdn_ref, gf_ref, y_ref, hm_s):
    x = x_ref[0] + jnp.dot(ca_ref[...].astype(BF16), wco_ref[...], preferred_element_type=F32)
    x = x + _mlp(_rmsnorm(x, gm_ref[...]).astype(BF16), wup_ref, wdn_ref, hm_s)
    y_ref[...] = _rmsnorm(x, gf_ref[...]).reshape(y_ref.shape)


def _post_sample(x, ca, w_co, g_mlp, w_up, w_down, g_final):
    nt, tm, _ = x.shape
    in_specs = [
        pl.BlockSpec((1, tm, D_MODEL), lambda i: (i, 0, 0)),
        pl.BlockSpec((tm, D_MODEL), lambda i: (i, 0)),
        _const_spec(w_co.shape), _const_spec(g_mlp.shape), _const_spec(w_up.shape),
        _const_spec(w_down.shape), _const_spec(g_final.shape),
    ]
    return pl.pallas_call(
        _post_sample_kernel,
        grid=(nt,),
        in_specs=in_specs,
        out_specs=pl.BlockSpec((tm // 4, 4, D_MODEL), lambda i: (i, 0, 0)),
        out_shape=jax.ShapeDtypeStruct((nt * tm // 4, 4, D_MODEL), F32),
        scratch_shapes=[pltpu.VMEM((tm, D_FF), BF16)],
        compiler_params=pltpu.CompilerParams(dimension_semantics=("arbitrary",),
                                             vmem_limit_bytes=VMEM_LIMIT_BYTES),
        name="post_sample",
    )(x, ca, w_co, g_mlp, w_up, w_down, g_final)


def _cast_kernel(*refs):
    n = len(refs) // 2
    for src, dst in zip(refs[:n], refs[n:]):
        dst[...] = src[...].astype(BF16)


def _cast_weights(*ws):
    steps = 8
    specs = [pl.BlockSpec((w.shape[0] // steps, w.shape[1]), lambda i: (i, 0)) for w in ws]
    return pl.pallas_call(
        _cast_kernel,
        grid=(steps,),
        in_specs=specs,
        out_specs=specs,
        out_shape=[jax.ShapeDtypeStruct(w.shape, BF16) for w in ws],
        compiler_params=pltpu.CompilerParams(dimension_semantics=("arbitrary",),
                                             vmem_limit_bytes=VMEM_LIMIT_BYTES),
        name="cast_weights",
    )(*ws)


def kernel(x_prompt, x_sample, mem_prompt, state_hgrn, cache_mem_k, cache_mem_v, lb_param, g_mix,
           w_in, hg_norm_g, ln_v_g, ln_v_b, w_s, b_s, w_out, g_cross, g_mem, w_cq, w_ck, w_cv, w_co,
           g_mlp, w_up, w_down, g_final):
    B, L, _ = x_prompt.shape
    DB, DL, _ = x_sample.shape
    assert DL == 4 and g_mix.shape[0] == 1

    row = lambda a: a.reshape(1, -1)
    win_b, wout_b, wcq_b, wco_b, wck_b, wcv_b, wup_b, wdn_b = _cast_weights(
        w_in[0], w_out[0], w_cq[0], w_co[0], w_ck[0], w_cv[0], w_up[0], w_down[0])
    gmix, hgn, lng, lnb = row(g_mix[0]), row(hg_norm_g[0]), row(ln_v_g[0]), row(ln_v_b[0])
    gcr, gmem, gmlp, gfin = row(g_cross[0]), row(g_mem[0]), row(g_mlp[0]), row(g_final)
    wmix = w_s[0]
    bcol = b_s[0].T

    tm_s = 128
    h_s, s_s, vn_s, hq_s = _mix_sample(x_sample, state_hgrn[0], lb_param, gmix, win_b, hgn, lng, lnb, wmix,
                                       bcol, wout_b, gcr, wcq_b, tm=tm_s)

    mk, mv, mk_b, mv_b = _memkv(mem_prompt.reshape(B * N_MEM, D_MODEL), gmem, wck_b, wcv_b, tm=512)
    x2_p, s_p, ca_s = _mixattn_layer(x_prompt, mk_b.reshape(B, N_MEM, D_MODEL),
                                     mv_b.reshape(B, N_MEM, D_MODEL), hq_s, cache_mem_k[0], cache_mem_v[0],
                                     lb_param, gmix, win_b, hgn, lng, lnb, wmix, bcol, wout_b, gcr, wcq_b,
                                     wco_b, tm=512)
    y_p = _mlp_layer(x2_p, gmlp, wup_b, wdn_b, gfin, tm=1024)

    y_s = _post_sample(h_s.reshape(1, DB * DL, D_MODEL), ca_s, wco_b, gmlp, wup_b, wdn_b, gfin)

    return (y_p, y_s, s_p[None], s_s[None],
            mk[None], mv[None],
            vn_s[None])
```

```python
import functools

import jax
import jax.numpy as jnp
from jax import lax
from jax.experimental import pallas as pl
from jax.experimental.pallas import tpu as pltpu

F32 = jnp.float32
BF16 = jnp.bfloat16

D_MODEL = 1024
HG_WIDTH = 512
HG_HEADS = 4
HG_DK = 128
CM_WIDTH = 512
CM_GROUPS = 4
CM_GROUP_DIM = 128
IN_WIDTH = 4 * HG_WIDTH + 2 * CM_WIDTH
N_MEM = 256
X_HEADS = 4
X_HEAD_DIM = 256
D_FF = 4096
EPS = 1e-6

CHUNK = 128
SUB = 16
VMEM_LIMIT_BYTES = 62 * 1024 * 1024
LOG2E = 1.4426950408889634
FAST_BLOCK = 32
FAST_MAX_DECAY = 80.0


def _bdot(a, b):
    return jnp.dot(a.astype(BF16), b.astype(BF16), preferred_element_type=F32)


def _bdot_nt(a, b):
    return lax.dot_general(a.astype(BF16), b.astype(BF16), (((1,), (1,)), ((), ())),
                           preferred_element_type=F32)


def _rmsnorm(x, g):
    ms = jnp.mean(x * x, axis=-1, keepdims=True)
    return x * lax.rsqrt(ms + EPS) * g


def _sigmoid(x):
    return 0.5 * jnp.tanh(0.5 * x) + 0.5


def _gelu(x):
    return 0.5 * x * (1.0 + jnp.tanh(0.7978845608028654 * (x + 0.044715 * (x * x * x))))


def _softmax_rows(s):
    m = jnp.max(s, axis=-1, keepdims=True)
    e = jnp.exp(s - m)
    return e / jnp.sum(e, axis=-1, keepdims=True)


def _seg_cumsum(x, seg):
    n = x.shape[0]
    pos = lax.broadcasted_iota(jnp.int32, x.shape, 0) % seg
    s = 1
    while s < min(seg, 8):
        x = x + jnp.where(pos >= s, pltpu.roll(x, s, 0), 0.0)
        s *= 2
    while s < seg:
        parts = []
        for r0 in range(0, n, seg):
            parts.append(x[r0:r0 + s])
            parts.append(x[r0 + s:r0 + seg] + x[r0:r0 + seg - s])
        x = jnp.concatenate(parts, axis=0)
        s *= 2
    return x


def _chunk_cumsum(x):
    n = x.shape[0]
    tril = (lax.broadcasted_iota(jnp.int32, (n, n), 0)
            >= lax.broadcasted_iota(jnp.int32, (n, n), 1)).astype(BF16)
    hi = x.astype(BF16)
    r1 = x - hi.astype(F32)
    mid = r1.astype(BF16)
    lo = (r1 - mid.astype(F32)).astype(BF16)
    return (jnp.dot(tril, hi, preferred_element_type=F32) + jnp.dot(tril, mid, preferred_element_type=F32)
            + jnp.dot(tril, lo, preferred_element_type=F32))


def _lower_bound(lb_ref):
    lbp = lb_ref[...]
    lbe = jnp.exp(lbp - jnp.max(lbp, axis=0, keepdims=True))
    return lbe[0:1] / jnp.sum(lbe, axis=0, keepdims=True)


def _gates(zq, zf, lb, seg):
    half = 0.5 - 0.5 * lb
    hth = half * jnp.tanh(0.5 * zf)
    logf = jnp.log((lb + half) + hth)
    bcum = _chunk_cumsum(logf) if seg == logf.shape[0] else _seg_cumsum(logf, seg)
    hq = 0.5 * zq
    return hq * jnp.tanh(hq) + hq, half - hth, bcum


def _block_decay(bcum, sample):
    if sample:
        return jnp.max(-bcum, axis=0, keepdims=True)
    d = -bcum[FAST_BLOCK - 1:FAST_BLOCK]
    for i in range(1, CHUNK // FAST_BLOCK):
        n0, n1 = i * FAST_BLOCK, (i + 1) * FAST_BLOCK
        d = jnp.maximum(d, bcum[n0 - 1:n0] - bcum[n1 - 1:n1])
    return d


def _chunk_masks(sample):
    row = lax.broadcasted_iota(jnp.int32, (CHUNK, CHUNK), 0)
    col = lax.broadcasted_iota(jnp.int32, (CHUNK, CHUNK), 1)
    if sample:
        return (row // 4 == col // 4) & (row >= col)
    return row >= col


def _scores_fast(q, k, bcum, sample):
    if sample:
        return _bdot_nt(q * jnp.exp(bcum), k * jnp.exp(-bcum))
    a_rows = []
    for i in range(CHUNK // FAST_BLOCK):
        n0, n1 = i * FAST_BLOCK, (i + 1) * FAST_BLOCK
        ref_b = bcum[n0 - 1:n0] if i else jnp.zeros((1, HG_DK), F32)
        qt = q[n0:n1] * jnp.exp(bcum[n0:n1] - ref_b)
        kt = k[:n1] * jnp.exp(ref_b - bcum[:n1])
        if n1 < CHUNK:
            kt = jnp.concatenate([kt, jnp.zeros((CHUNK - n1, HG_DK), F32)], axis=0)
        a_rows.append(_bdot_nt(qt, kt))
    return jnp.concatenate(a_rows, axis=0)


def _scores_exact(q, k, bcum, sample, mask):
    col8 = lax.broadcasted_iota(jnp.int32, (8, CHUNK), 1)
    b2 = bcum * LOG2E
    cexp = b2 - jnp.log2(k)
    a_rows = []
    for gb in range(CHUNK // SUB):
        lo = slice(gb * SUB, gb * SUB + 8)
        hi = slice(gb * SUB + 8, (gb + 1) * SUB)
        a_lo = jnp.zeros((8, CHUNK), F32)
        a_hi = jnp.zeros((8, CHUNK), F32)
        for s in range(SUB):
            sg = gb * SUB + s
            c_s = cexp[sg:sg + 1]
            if s < 8:
                p = q[lo] * jnp.exp2(b2[lo] - c_s)
                a_lo = jnp.where(col8 == sg, jnp.sum(p, axis=-1, keepdims=True), a_lo)
            if (not sample) or s >= 8:
                p = q[hi] * jnp.exp2(b2[hi] - c_s)
                a_hi = jnp.where(col8 == sg, jnp.sum(p, axis=-1, keepdims=True), a_hi)
        a_rows.append(a_lo)
        a_rows.append(a_hi)
    a = jnp.concatenate(a_rows, axis=0)
    if sample:
        return a
    o_rows = [jnp.zeros((SUB, CHUNK), F32)]
    for i in range(1, CHUNK // SUB):
        n = i * SUB
        ref_b = bcum[n - 1:n]
        qt = q[n:n + SUB] * jnp.exp(bcum[n:n + SUB] - ref_b)
        kt = k[:n] * jnp.exp(ref_b - bcum[:n])
        kt = jnp.concatenate([kt, jnp.zeros((CHUNK - n, HG_DK), F32)], axis=0)
        o_rows.append(_bdot_nt(qt, kt))
    return jnp.where(mask, a, 0.0) + jnp.concatenate(o_rows, axis=0)


def _hgrn_out(o, zg, g):
    o = o * lax.rsqrt(jnp.mean(o * o, axis=-1, keepdims=True) + EPS) * g
    return (o * _sigmoid(zg)).astype(BF16)


def _gmlp_chunk(z_ref, c0, rows, lng_ref, lnb_ref, wmix_ref, bcol_ref, mask, ocat_ref):
    u = _gelu(z_ref[rows, c0:c0 + CM_WIDTH])
    gv = _gelu(z_ref[rows, c0 + CM_WIDTH:c0 + 2 * CM_WIDTH])
    mu = jnp.mean(gv, axis=-1, keepdims=True)
    dv = gv - mu
    var = jnp.mean(dv * dv, axis=-1, keepdims=True)
    vn = dv * lax.rsqrt(var + EPS) * lng_ref[...] + lnb_ref[...]
    for g in range(CM_GROUPS):
        gs = slice(g * CM_GROUP_DIM, (g + 1) * CM_GROUP_DIM)
        wm = jnp.where(mask, wmix_ref[g], 0.0)
        mixed = _bdot(wm, vn[:, gs]) + bcol_ref[:, g:g + 1]
        ocat_ref[rows, HG_WIDTH + g * CM_GROUP_DIM:HG_WIDTH + (g + 1) * CM_GROUP_DIM] = (
            u[:, gs] * mixed).astype(BF16)
    return vn


def _sample_pair_probs(hq8, ck_ref):
    nrow = 8 * X_HEADS
    rh = lax.broadcasted_iota(jnp.int32, (nrow, N_MEM * X_HEADS), 0) // 8
    ch = lax.broadcasted_iota(jnp.int32, (nrow, N_MEM * X_HEADS), 1) % X_HEADS
    q = jnp.concatenate([hq8[:, h * X_HEAD_DIM:(h + 1) * X_HEAD_DIM] for h in range(X_HEADS)],
                        axis=0).astype(BF16)
    probs = []
    for r in range(2):
        k2 = ck_ref[r].reshape(N_MEM * X_HEADS, X_HEAD_DIM)
        s = jnp.where(rh == ch, _bdot_nt(q, k2) * (X_HEAD_DIM ** -0.5), -jnp.inf)
        probs.append(_softmax_rows(s))
    return probs


def _sample_pair_context(probs, cv_ref, ca_ref):
    first = (lax.broadcasted_iota(jnp.int32, (8 * X_HEADS, X_HEAD_DIM), 0) % 8) < 4
    outs = [_bdot(probs[r], cv_ref[r].reshape(N_MEM * X_HEADS, X_HEAD_DIM)) for r in range(2)]
    o = jnp.where(first, outs[0], outs[1])
    for h in range(X_HEADS):
        ca_ref[:, h * X_HEAD_DIM:(h + 1) * X_HEAD_DIM] = o[8 * h:8 * h + 8]


def _mlp(hn_bf16, wup_ref, wdn_ref, hm_s):
    for c in range(D_FF // D_MODEL):
        fs = slice(c * D_MODEL, (c + 1) * D_MODEL)
        hm = jnp.maximum(jnp.dot(hn_bf16, wup_ref[:, fs], preferred_element_type=F32), 0.0)
        hm_s[:, fs] = (hm * hm).astype(BF16)
    return jnp.dot(hm_s[...], wdn_ref[...], preferred_element_type=F32)


def _const_spec(shape):
    nd = len(shape)
    return pl.BlockSpec(shape, lambda *_: (0,) * nd, pipeline_mode=pl.Buffered(1))


def _mixattn_kernel(x_ref, mk_ref, mv_ref, hqs_ref, ck_ref, cv_ref, lb_ref, gmix_ref, win_ref, hgn_ref,
                    lng_ref, lnb_ref, wmix_ref, bcol_ref, wout_ref, gc_ref, wcq_ref, wco_ref,
                    x2_ref, sout_ref, cas_ref,
                    zg_ref, z_ref, ocat_ref, q_s, k_s, b_s, a_s, st0, st1, st2, st3, h1_s, h1n_s, ca_s,
                    *, tm, nt, npairs):
    t = pl.program_id(0)
    i = lax.rem(t, nt)
    nch = tm // CHUNK
    nj = nch * HG_HEADS
    mask = _chunk_masks(False)
    st_refs = (st0, st1, st2, st3)

    @pl.when(t == 0)
    def _():
        h1_s[...] = jnp.zeros_like(h1_s)
        h1n_s[...] = jnp.zeros_like(h1n_s)

    @pl.when(i == 0)
    def _():
        for st in st_refs:
            st[...] = jnp.zeros_like(st)

    x = x_ref[0]
    xn = _rmsnorm(x, gmix_ref[...]).astype(BF16)
    ngate = 2 * HG_WIDTH
    zg_ref[...] = jnp.dot(xn, win_ref[:, 0:ngate], preferred_element_type=F32)
    z_ref[...] = jnp.dot(xn, win_ref[:, ngate:IN_WIDTH], preferred_element_type=F32)
    lb_all = _lower_bound(lb_ref)
    dmax = jnp.zeros((1, HG_WIDTH), F32)
    for c in range(nch):
        rows = slice(c * CHUNK, (c + 1) * CHUNK)
        q, k, bcum = _gates(zg_ref[rows, 0:HG_WIDTH], zg_ref[rows, HG_WIDTH:2 * HG_WIDTH], lb_all, CHUNK)
        q_s[rows, :] = q
        k_s[rows, :] = k
        b_s[rows, :] = bcum
        dmax = jnp.maximum(dmax, _block_decay(bcum, False))
    fast_ok = jnp.max(dmax) < FAST_MAX_DECAY

    def fill_scores(j, fast):
        c, h = divmod(j, HG_HEADS)
        rows = slice(c * CHUNK, (c + 1) * CHUNK)
        cs = slice(h * HG_DK, (h + 1) * HG_DK)
        q, k, bcum = q_s[rows, cs], k_s[rows, cs], b_s[rows, cs]
        a = _scores_fast(q, k, bcum, False) if fast else _scores_exact(q, k, bcum, False, mask)
        a_s[j] = jnp.where(mask, a, 0.0).astype(BF16)

    def head_chunk(j):
        c, h = divmod(j, HG_HEADS)
        rows = slice(c * CHUNK, (c + 1) * CHUNK)
        cs = slice(h * HG_DK, (h + 1) * HG_DK)
        q, k, bcum = q_s[rows, cs], k_s[rows, cs], b_s[rows, cs]
        v = z_ref[rows, h * HG_DK:(h + 1) * HG_DK]
        zg = z_ref[rows, HG_WIDTH + h * HG_DK:HG_WIDTH + (h + 1) * HG_DK]
        eb = jnp.exp(bcum)
        kd = k * jnp.exp(bcum[CHUNK - 1:CHUNK] - bcum)
        st = st_refs[h][...]
        o = jnp.dot(a_s[j], v.astype(BF16), preferred_element_type=F32)
        o = o + _bdot_nt(q * eb, st)
        st_refs[h][...] = st * eb[CHUNK - 1:CHUNK] + _bdot(v.T, kd)
        ocat_ref[rows, cs] = _hgrn_out(o, zg, hgn_ref[:, cs])
        if h == HG_HEADS - 1:
            _gmlp_chunk(z_ref, 2 * HG_WIDTH, rows, lng_ref, lnb_ref, wmix_ref, bcol_ref, mask, ocat_ref)

    def region2(fast):
        scale = X_HEAD_DIM ** -0.5
        heads = [slice(h * X_HEAD_DIM, (h + 1) * X_HEAD_DIM) for h in range(X_HEADS)]
        hq = jnp.dot(h1n_s[...], wcq_ref[...], preferred_element_type=F32)
        for j in range(0, nj // 2):
            fill_scores(j, fast)
        probs = [_softmax_rows(_bdot_nt(hq[:, hs], mk_ref[0, :, hs]) * scale) for hs in heads]
        for j in range(nj // 2, nj):
            fill_scores(j, fast)
        sprobs = [_sample_pair_probs(hqs_ref[8 * p:8 * p + 8, :], ck_ref.at[2 * p:2 * p + 2])
                  for p in range(npairs)]
        for j in range(0, nj // 2):
            head_chunk(j)
        for h, hs in enumerate(heads):
            ca_s[:, hs] = _bdot(probs[h], mv_ref[0, :, hs]).astype(BF16)
        for j in range(nj // 2, nj):
            head_chunk(j)
        for p in range(npairs):
            _sample_pair_context(sprobs[p], cv_ref.at[2 * p:2 * p + 2], cas_ref.at[8 * p:8 * p + 8, :])
        x2_ref[0] = h1_s[...] + jnp.dot(ca_s[...], wco_ref[...], preferred_element_type=F32)
        h1 = x_ref[0] + jnp.dot(ocat_ref[...], wout_ref[...], preferred_element_type=F32)
        h1_s[...] = h1
        h1n_s[...] = _rmsnorm(h1, gc_ref[...]).astype(BF16)

    @pl.when(fast_ok)
    def _():
        region2(True)

    @pl.when(jnp.logical_not(fast_ok))
    def _():
        region2(False)

    @pl.when(i == nt - 1)
    def _():
        for h in range(HG_HEADS):
            sout_ref[0, h] = st_refs[h][...].T


def _mixattn_layer(x, mk, mv, hqs, ck, cv, lb_param, g_mix, w_in, hg_norm_g, ln_v_g, ln_v_b, wmix, bcol,
                   w_out, g_cross, w_cq, w_co, tm):
    B, L, _ = x.shape
    nt = L // tm
    T = B * nt
    npairs = ck.shape[0] // (2 * T)
    assert npairs * 2 * T == ck.shape[0]

    def cur(t):
        return jnp.minimum(t, T - 1)

    def prev(t):
        return jnp.maximum(t - 1, 0)

    consts = (lb_param, g_mix, w_in, hg_norm_g, ln_v_g, ln_v_b, wmix, bcol, w_out, g_cross, w_cq, w_co)
    kv_spec = pl.BlockSpec((2 * npairs, N_MEM, X_HEADS, X_HEAD_DIM), lambda t: (cur(t), 0, 0, 0))
    in_specs = [
        pl.BlockSpec((1, tm, D_MODEL), lambda t: (cur(t) // nt, cur(t) % nt, 0)),
        pl.BlockSpec((1, N_MEM, D_MODEL), lambda t: (prev(t) // nt, 0, 0)),
        pl.BlockSpec((1, N_MEM, D_MODEL), lambda t: (prev(t) // nt, 0, 0)),
        pl.BlockSpec((8 * npairs, D_MODEL), lambda t: (cur(t), 0)), kv_spec, kv_spec,
    ] + [_const_spec(a.shape) for a in consts]
    out_specs = [
        pl.BlockSpec((1, tm, D_MODEL), lambda t: (prev(t) // nt, prev(t) % nt, 0)),
        pl.BlockSpec((1, HG_HEADS, HG_DK, HG_DK), lambda t: (cur(t) // nt, 0, 0, 0)),
        pl.BlockSpec((8 * npairs, D_MODEL), lambda t: (cur(t), 0)),
    ]
    scratch = [
        pltpu.VMEM((tm, 2 * HG_WIDTH), F32),
        pltpu.VMEM((tm, IN_WIDTH - 2 * HG_WIDTH), F32),
        pltpu.VMEM((tm, D_MODEL), BF16),
        pltpu.VMEM((tm, HG_WIDTH), F32), pltpu.VMEM((tm, HG_WIDTH), F32),
        pltpu.VMEM((tm, HG_WIDTH), F32),
        pltpu.VMEM((tm // CHUNK * HG_HEADS, CHUNK, CHUNK), BF16),
        pltpu.VMEM((HG_DK, HG_DK), F32), pltpu.VMEM((HG_DK, HG_DK), F32),
        pltpu.VMEM((HG_DK, HG_DK), F32), pltpu.VMEM((HG_DK, HG_DK), F32),
        pltpu.VMEM((tm, D_MODEL), F32),
        pltpu.VMEM((tm, D_MODEL), BF16),
        pltpu.VMEM((tm, D_MODEL), BF16),
    ]
    return pl.pallas_call(
        functools.partial(_mixattn_kernel, tm=tm, nt=nt, npairs=npairs),
        grid=(T + 1,),
        in_specs=in_specs,
        out_specs=out_specs,
        out_shape=[jax.ShapeDtypeStruct((B, L, D_MODEL), F32),
                   jax.ShapeDtypeStruct((B, HG_HEADS, HG_DK, HG_DK), F32),
                   jax.ShapeDtypeStruct(hqs.shape, F32)],
        scratch_shapes=scratch,
        compiler_params=pltpu.CompilerParams(dimension_semantics=("arbitrary",),
                                             vmem_limit_bytes=VMEM_LIMIT_BYTES),
        name="mix_attn",
    )(x, mk, mv, hqs, ck, cv, *consts)


def _mlp_kernel(x2_ref, gm_ref, wup_ref, wdn_ref, gf_ref, y_ref, hm_s):
    x2 = x2_ref[0]
    y_ref[0] = _rmsnorm(x2 + _mlp(_rmsnorm(x2, gm_ref[...]).astype(BF16), wup_ref, wdn_ref, hm_s),
                        gf_ref[...])


def _mlp_layer(x2, g_mlp, w_up, w_down, g_final, tm):
    B, L, _ = x2.shape
    nt = L // tm
    consts = (g_mlp, w_up, w_down, g_final)
    tile = pl.BlockSpec((1, tm, D_MODEL), lambda t: (t // nt, t % nt, 0))
    return pl.pallas_call(
        _mlp_kernel,
        grid=(B * nt,),
        in_specs=[tile] + [_const_spec(a.shape) for a in consts],
        out_specs=tile,
        out_shape=jax.ShapeDtypeStruct((B, L, D_MODEL), F32),
        scratch_shapes=[pltpu.VMEM((tm, D_FF), BF16)],
        compiler_params=pltpu.CompilerParams(dimension_semantics=("arbitrary",),
                                             vmem_limit_bytes=VMEM_LIMIT_BYTES),
        name="mlp_prompt",
    )(x2, *consts)


def _mix_sample_kernel(x_ref, s0_ref, lb_ref, gmix_ref, win_ref, hgn_ref, lng_ref, lnb_ref, wmix_ref,
                       bcol_ref, wout_ref, gc_ref, wcq_ref, h_ref, sout_ref, vn_ref, hq_ref,
                       z_ref, ocat_ref, q_s, k_s, b_s, wm_s, bc_s, *, tm):
    x = x_ref[...].reshape(tm, D_MODEL)
    z_ref[...] = jnp.dot(_rmsnorm(x, gmix_ref[...]).astype(BF16), win_ref[...],
                         preferred_element_type=F32)
    lb_all = _lower_bound(lb_ref)
    mask = _chunk_masks(True)
    row = lax.broadcasted_iota(jnp.int32, (CHUNK, CHUNK), 0)
    col = lax.broadcasted_iota(jnp.int32, (CHUNK, CHUNK), 1)
    eye = row == col

    @pl.when(pl.program_id(0) == 0)
    def _():
        pick = (row % 4 == col).astype(BF16)
        for g in range(CM_GROUPS):
            corner = jnp.where((row < 4) & (col < 4), wmix_ref[g], 0.0)
            wm_s[g] = _bdot_nt(_bdot(pick, corner), pick)
        pos = lax.broadcasted_iota(jnp.int32, (CHUNK, CM_GROUPS), 0) % 4
        bc = jnp.zeros((CHUNK, CM_GROUPS), F32)
        for t in range(4):
            bc = jnp.where(pos == t, bcol_ref[t:t + 1, :], bc)
        bc_s[...] = bc
    r16 = lax.broadcasted_iota(jnp.int32, (SUB, HG_DK), 0)

    def gate_body(c, dmax):
        rows = pl.ds(pl.multiple_of(c * CHUNK, CHUNK), CHUNK)
        q, k, bcum = _gates(z_ref[rows, 0:HG_WIDTH], z_ref[rows, HG_WIDTH:2 * HG_WIDTH], lb_all, 4)
        q_s[rows, :] = q
        k_s[rows, :] = k
        b_s[rows, :] = bcum
        return jnp.maximum(dmax, _block_decay(bcum, True))

    dmax = lax.fori_loop(0, tm // CHUNK, gate_body, jnp.zeros((1, HG_WIDTH), F32))
    fast_ok = jnp.max(dmax) < FAST_MAX_DECAY

    def chunk_body(c, carry, *, fast):
        rows = pl.ds(pl.multiple_of(c * CHUNK, CHUNK), CHUNK)
        for h in range(HG_HEADS):
            cs = slice(h * HG_DK, (h + 1) * HG_DK)
            q, k, bcum = q_s[rows, cs], k_s[rows, cs], b_s[rows, cs]
            v = z_ref[rows, 2 * HG_WIDTH + h * HG_DK:2 * HG_WIDTH + (h + 1) * HG_DK]
            zg = z_ref[rows, 3 * HG_WIDTH + h * HG_DK:3 * HG_WIDTH + (h + 1) * HG_DK]
            a = _scores_fast(q, k, bcum, True) if fast else _scores_exact(q, k, bcum, True, mask)
            o = _bdot(jnp.where(mask, a, 0.0), v)
            eb = jnp.exp(bcum)
            qd = q * eb
            o_parts = []
            for gb in range(CHUNK // SUB):
                blk = slice(gb * SUB, (gb + 1) * SUB)
                qd_b, v_b, eb_b, k_b, b_b = qd[blk], v[blk], eb[blk], k[blk], bcum[blk]
                inter = jnp.zeros((SUB, HG_DK), F32)
                for j in range(4):
                    req = c * (CHUNK // 4) + gb * 4 + j
                    s0 = s0_ref[req, h]
                    last = 4 * j + 3
                    inter = jnp.where(r16 // 4 == j, _bdot(qd_b, s0), inter)
                    kd = jnp.where(r16 // 4 == j, k_b * jnp.exp(b_b[last:last + 1] - b_b), 0.0)
                    upd = lax.dot_general(kd.astype(BF16), v_b.astype(BF16),
                                          (((0,), (0,)), ((), ())), preferred_element_type=F32)
                    dcol = jnp.sum(jnp.where(eye, eb_b[last:last + 1], 0.0), axis=-1, keepdims=True)
                    sout_ref[req, h] = dcol * s0 + upd
                o_parts.append(inter)
            o = o + jnp.concatenate(o_parts, axis=0)
            ocat_ref[rows, cs] = _hgrn_out(o, zg, hgn_ref[:, cs])
        vn = _gmlp_chunk(z_ref, 4 * HG_WIDTH, rows, lng_ref, lnb_ref, wm_s, bc_s, mask, ocat_ref)
        vn_ref[pl.ds(pl.multiple_of(c * (CHUNK // 4), CHUNK // 4), CHUNK // 4)] = vn.reshape(
            CHUNK // 4, 4, CM_WIDTH)
        return carry

    @pl.when(fast_ok)
    def _():
        lax.fori_loop(0, tm // CHUNK, functools.partial(chunk_body, fast=True), 0)

    @pl.when(jnp.logical_not(fast_ok))
    def _():
        lax.fori_loop(0, tm // CHUNK, functools.partial(chunk_body, fast=False), 0)

    h = x + jnp.dot(ocat_ref[...], wout_ref[...], preferred_element_type=F32)
    h_ref[0] = h
    hq_ref[...] = jnp.dot(_rmsnorm(h, gc_ref[...]).astype(BF16), wcq_ref[...], preferred_element_type=F32)


def _mix_sample(x, s0, lb_param, g_mix, w_in, hg_norm_g, ln_v_g, ln_v_b, wmix, bcol, w_out, g_cross,
                w_cq, tm):
    nreq = tm // 4
    nt = x.shape[0] // nreq
    consts = (lb_param, g_mix, w_in, hg_norm_g, ln_v_g, ln_v_b, wmix, bcol, w_out, g_cross, w_cq)
    in_specs = [
        pl.BlockSpec((nreq, 4, D_MODEL), lambda i: (i, 0, 0)),
        pl.BlockSpec((nreq, HG_HEADS, HG_DK, HG_DK), lambda i: (i, 0, 0, 0)),
    ] + [_const_spec(a.shape) for a in consts]
    out_specs = [
        pl.BlockSpec((1, tm, D_MODEL), lambda i: (i, 0, 0)),
        pl.BlockSpec((nreq, HG_HEADS, HG_DK, HG_DK), lambda i: (i, 0, 0, 0)),
        pl.BlockSpec((nreq, 4, CM_WIDTH), lambda i: (i, 0, 0)),
        pl.BlockSpec((tm, D_MODEL), lambda i: (i, 0)),
    ]
    return pl.pallas_call(
        functools.partial(_mix_sample_kernel, tm=tm),
        grid=(nt,),
        in_specs=in_specs,
        out_specs=out_specs,
        out_shape=[jax.ShapeDtypeStruct((nt, tm, D_MODEL), F32),
                   jax.ShapeDtypeStruct(s0.shape, F32),
                   jax.ShapeDtypeStruct((nt * nreq, 4, CM_WIDTH), F32),
                   jax.ShapeDtypeStruct((nt * tm, D_MODEL), F32)],
        scratch_shapes=[pltpu.VMEM((tm, IN_WIDTH), F32), pltpu.VMEM((tm, D_MODEL), BF16)]
        + [pltpu.VMEM((tm, HG_WIDTH), F32)] * 3
        + [pltpu.VMEM((CM_GROUPS, CHUNK, CHUNK), F32), pltpu.VMEM((CHUNK, CM_GROUPS), F32)],
        compiler_params=pltpu.CompilerParams(dimension_semantics=("arbitrary",),
                                             vmem_limit_bytes=VMEM_LIMIT_BYTES),
        name="mix_sample",
    )(x, s0, *consts)


def _memkv_kernel(m_ref, g_ref, wk_ref, wv_ref, k_ref, v_ref, kb_ref, vb_ref):
    mn = _rmsnorm(m_ref[...], g_ref[...]).astype(BF16)
    k = jnp.dot(mn, wk_ref[...], preferred_element_type=F32)
    v = jnp.dot(mn, wv_ref[...], preferred_element_type=F32)
    kb_ref[...] = k.astype(BF16)
    vb_ref[...] = v.astype(BF16)
    for src, dst in ((k, k_ref), (v, v_ref)):
        for r in range(dst.shape[0]):
            rows = src[r * N_MEM:(r + 1) * N_MEM]
            dst[r] = jnp.stack([rows[:, h * X_HEAD_DIM:(h + 1) * X_HEAD_DIM] for h in range(X_HEADS)],
                               axis=1)


def _memkv(mem, g_mem, w_ck, w_cv, tm):
    n = mem.shape[0]
    return pl.pallas_call(
        _memkv_kernel,
        grid=(n // tm,),
        in_specs=[pl.BlockSpec((tm, D_MODEL), lambda i: (i, 0)), _const_spec(g_mem.shape),
                  _const_spec(w_ck.shape), _const_spec(w_cv.shape)],
        out_specs=[pl.BlockSpec((tm // N_MEM, N_MEM, X_HEADS, X_HEAD_DIM), lambda i: (i, 0, 0, 0))] * 2
        + [pl.BlockSpec((tm, D_MODEL), lambda i: (i, 0))] * 2,
        out_shape=[jax.ShapeDtypeStruct((n // N_MEM, N_MEM, X_HEADS, X_HEAD_DIM), F32)] * 2
        + [jax.ShapeDtypeStruct((n, D_MODEL), BF16)] * 2,
        compiler_params=pltpu.CompilerParams(dimension_semantics=("arbitrary",),
                                             vmem_limit_bytes=VMEM_LIMIT_BYTES),
        name="memkv",
    )(mem, g_mem, w_ck, w_cv)


def _post_sample_kernel(x_ref, ca_ref, wco_ref, gm_ref, wup_ref, wdn_ref, gf_ref, y_ref, hm_s):
    x = x_ref[0] + jnp.dot(ca_ref[...].astype(BF16), wco_ref[...], preferred_element_type=F32)
    x = x + _mlp(_rmsnorm(x, gm_ref[...]).astype(BF16), wup_ref, wdn_ref, hm_s)
    y_ref[...] = _rmsnorm(x, gf_ref[...]).reshape(y_ref.shape)


def _post_sample(x, ca, w_co, g_mlp, w_up, w_down, g_final):
    nt, tm, _ = x.shape
    in_specs = [
        pl.BlockSpec((1, tm, D_MODEL), lambda i: (i, 0, 0)),
        pl.BlockSpec((tm, D_MODEL), lambda i: (i, 0)),
        _const_spec(w_co.shape), _const_spec(g_mlp.shape), _const_spec(w_up.shape),
        _const_spec(w_down.shape), _const_spec(g_final.shape),
    ]
    return pl.pallas_call(
        _post_sample_kernel,
        grid=(nt,),
        in_specs=in_specs,
        out_specs=pl.BlockSpec((tm // 4, 4, D_MODEL), lambda i: (i, 0, 0)),
        out_shape=jax.ShapeDtypeStruct((nt * tm // 4, 4, D_MODEL), F32),
        scratch_shapes=[pltpu.VMEM((tm, D_FF), BF16)],
        compiler_params=pltpu.CompilerParams(dimension_semantics=("arbitrary",),
                                             vmem_limit_bytes=VMEM_LIMIT_BYTES),
        name="post_sample",
    )(x, ca, w_co, g_mlp, w_up, w_down, g_final)


def _cast_kernel(*refs):
    n = len(refs) // 2
    for src, dst in zip(refs[:n], refs[n:]):
        dst[...] = src[...].astype(BF16)


def _cast_weights(*ws):
    steps = 8
    specs = [pl.BlockSpec((w.shape[0] // steps, w.shape[1]), lambda i: (i, 0)) for w in ws]
    return pl.pallas_call(
        _cast_kernel,
        grid=(steps,),
        in_specs=specs,
        out_specs=specs,
        out_shape=[jax.ShapeDtypeStruct(w.shape, BF16) for w in ws],
        compiler_params=pltpu.CompilerParams(dimension_semantics=("arbitrary",),
                                             vmem_limit_bytes=VMEM_LIMIT_BYTES),
        name="cast_weights",
    )(*ws)


def kernel(x_prompt, x_sample, mem_prompt, state_hgrn, cache_mem_k, cache_mem_v, lb_param, g_mix,
           w_in, hg_norm_g, ln_v_g, ln_v_b, w_s, b_s, w_out, g_cross, g_mem, w_cq, w_ck, w_cv, w_co,
           g_mlp, w_up, w_down, g_final):
    B, L, _ = x_prompt.shape
    DB, DL, _ = x_sample.shape
    assert DL == 4 and g_mix.shape[0] == 1

    row = lambda a: a.reshape(1, -1)
    win_b, wout_b, wcq_b, wco_b, wck_b, wcv_b, wup_b, wdn_b = _cast_weights(
        w_in[0], w_out[0], w_cq[0], w_co[0], w_ck[0], w_cv[0], w_up[0], w_down[0])
    gmix, hgn, lng, lnb = row(g_mix[0]), row(hg_norm_g[0]), row(ln_v_g[0]), row(ln_v_b[0])
    gcr, gmem, gmlp, gfin = row(g_cross[0]), row(g_mem[0]), row(g_mlp[0]), row(g_final)
    wmix = w_s[0]
    bcol = b_s[0].T

    tm_s = 128
    h_s, s_s, vn_s, hq_s = _mix_sample(x_sample, state_hgrn[0], lb_param, gmix, win_b, hgn, lng, lnb, wmix,
                                       bcol, wout_b, gcr, wcq_b, tm=tm_s)

    mk, mv, mk_b, mv_b = _memkv(mem_prompt.reshape(B * N_MEM, D_MODEL), gmem, wck_b, wcv_b, tm=512)
    x2_p, s_p, ca_s = _mixattn_layer(x_prompt, mk_b.reshape(B, N_MEM, D_MODEL),
                                     mv_b.reshape(B, N_MEM, D_MODEL), hq_s, cache_mem_k[0], cache_mem_v[0],
                                     lb_param, gmix, win_b, hgn, lng, lnb, wmix, bcol, wout_b, gcr, wcq_b,
                                     wco_b, tm=512)
    y_p = _mlp_layer(x2_p, gmlp, wup_b, wdn_b, gfin, tm=1024)

    y_s = _post_sample(h_s.reshape(1, DB * DL, D_MODEL), ca_s, wco_b, gmlp, wup_b, wdn_b, gfin)

    return (y_p, y_s, s_p[None], s_s[None],
            mk[None], mv[None],
            vn_s[None])
```

```python
import functools

import jax
import jax.numpy as jnp
from jax import lax
from jax.experimental import pallas as pl
from jax.experimental.pallas import tpu as pltpu

F32 = jnp.float32
BF16 = jnp.bfloat16

D_MODEL = 1024
HG_WIDTH = 512
HG_HEADS = 4
HG_DK = 128
CM_WIDTH = 512
CM_GROUPS = 4
CM_GROUP_DIM = 128
IN_WIDTH = 4 * HG_WIDTH + 2 * CM_WIDTH
N_MEM = 256
X_HEADS = 4
X_HEAD_DIM = 256
D_FF = 4096
EPS = 1e-6

CHUNK = 128
SUB = 16
VMEM_LIMIT_BYTES = 62 * 1024 * 1024
LOG2E = 1.4426950408889634
FAST_BLOCK = 32
FAST_MAX_DECAY = 80.0


def _bdot(a, b):
    return jnp.dot(a.astype(BF16), b.astype(BF16), preferred_element_type=F32)


def _bdot_nt(a, b):
    return lax.dot_general(a.astype(BF16), b.astype(BF16), (((1,), (1,)), ((), ())),
                           preferred_element_type=F32)


def _rmsnorm(x, g):
    ms = jnp.mean(x * x, axis=-1, keepdims=True)
    return x * lax.rsqrt(ms + EPS) * g


def _sigmoid(x):
    return 0.5 * jnp.tanh(0.5 * x) + 0.5


def _gelu(x):
    return 0.5 * x * (1.0 + jnp.tanh(0.7978845608028654 * (x + 0.044715 * (x * x * x))))


def _softmax_rows(s):
    m = jnp.max(s, axis=-1, keepdims=True)
    e = jnp.exp(s - m)
    return e / jnp.sum(e, axis=-1, keepdims=True)


def _seg_cumsum(x, seg):
    n = x.shape[0]
    pos = lax.broadcasted_iota(jnp.int32, x.shape, 0) % seg
    s = 1
    while s < min(seg, 8):
        x = x + jnp.where(pos >= s, pltpu.roll(x, s, 0), 0.0)
        s *= 2
    while s < seg:
        parts = []
        for r0 in range(0, n, seg):
            parts.append(x[r0:r0 + s])
            parts.append(x[r0 + s:r0 + seg] + x[r0:r0 + seg - s])
        x = jnp.concatenate(parts, axis=0)
        s *= 2
    return x


def _chunk_cumsum(x):
    n = x.shape[0]
    tril = (lax.broadcasted_iota(jnp.int32, (n, n), 0)
            >= lax.broadcasted_iota(jnp.int32, (n, n), 1)).astype(BF16)
    hi = x.astype(BF16)
    r1 = x - hi.astype(F32)
    mid = r1.astype(BF16)
    lo = (r1 - mid.astype(F32)).astype(BF16)
    return (jnp.dot(tril, hi, preferred_element_type=F32) + jnp.dot(tril, mid, preferred_element_type=F32)
            + jnp.dot(tril, lo, preferred_element_type=F32))


def _lower_bound(lb_ref):
    lbp = lb_ref[...]
    lbe = jnp.exp(lbp - jnp.max(lbp, axis=0, keepdims=True))
    return lbe[0:1] / jnp.sum(lbe, axis=0, keepdims=True)


def _gates(zq, zf, lb, seg):
    half = 0.5 - 0.5 * lb
    hth = half * jnp.tanh(0.5 * zf)
    logf = jnp.log((lb + half) + hth)
    bcum = _chunk_cumsum(logf) if seg == logf.shape[0] else _seg_cumsum(logf, seg)
    hq = 0.5 * zq
    return hq * jnp.tanh(hq) + hq, half - hth, bcum


def _block_decay(bcum, sample):
    if sample:
        return jnp.max(-bcum, axis=0, keepdims=True)
    d = -bcum[FAST_BLOCK - 1:FAST_BLOCK]
    for i in range(1, CHUNK // FAST_BLOCK):
        n0, n1 = i * FAST_BLOCK, (i + 1) * FAST_BLOCK
        d = jnp.maximum(d, bcum[n0 - 1:n0] - bcum[n1 - 1:n1])
    return d


def _chunk_masks(sample):
    row = lax.broadcasted_iota(jnp.int32, (CHUNK, CHUNK), 0)
    col = lax.broadcasted_iota(jnp.int32, (CHUNK, CHUNK), 1)
    if sample:
        return (row // 4 == col // 4) & (row >= col)
    return row >= col


def _scores_fast(q, k, bcum, sample):
    if sample:
        return _bdot_nt(q * jnp.exp(bcum), k * jnp.exp(-bcum))
    a_rows = []
    zero = jnp.zeros((1, HG_DK), F32)
    kt, prev_ref = None, zero
    for i in range(CHUNK // FAST_BLOCK):
        n0, n1 = i * FAST_BLOCK, (i + 1) * FAST_BLOCK
        ref_b = bcum[n0 - 1:n0] if i else zero
        qt = q[n0:n1] * jnp.exp(bcum[n0:n1] - ref_b)
        kt_blk = k[n0:n1] * jnp.exp(ref_b - bcum[n0:n1])
        kt = kt_blk if kt is None else jnp.concatenate([kt * jnp.exp(ref_b - prev_ref), kt_blk], axis=0)
        prev_ref = ref_b
        kt_b = kt.astype(BF16)
        if n1 < CHUNK:
            kt_b = jnp.concatenate([kt_b, jnp.zeros((CHUNK - n1, HG_DK), BF16)], axis=0)
        a_rows.append(_bdot_nt(qt, kt_b))
    return jnp.concatenate(a_rows, axis=0)


def _scores_exact(q, k, bcum, sample, mask):
    col8 = lax.broadcasted_iota(jnp.int32, (8, CHUNK), 1)
    b2 = bcum * LOG2E
    cexp = b2 - jnp.log2(k)
    a_rows = []
    for gb in range(CHUNK // SUB):
        lo = slice(gb * SUB, gb * SUB + 8)
        hi = slice(gb * SUB + 8, (gb + 1) * SUB)
        a_lo = jnp.zeros((8, CHUNK), F32)
        a_hi = jnp.zeros((8, CHUNK), F32)
        for s in range(SUB):
            sg = gb * SUB + s
            c_s = cexp[sg:sg + 1]
            if s < 8:
                p = q[lo] * jnp.exp2(b2[lo] - c_s)
                a_lo = jnp.where(col8 == sg, jnp.sum(p, axis=-1, keepdims=True), a_lo)
            if (not sample) or s >= 8:
                p = q[hi] * jnp.exp2(b2[hi] - c_s)
                a_hi = jnp.where(col8 == sg, jnp.sum(p, axis=-1, keepdims=True), a_hi)
        a_rows.append(a_lo)
        a_rows.append(a_hi)
    a = jnp.concatenate(a_rows, axis=0)
    if sample:
        return a
    o_rows = [jnp.zeros((SUB, CHUNK), F32)]
    for i in range(1, CHUNK // SUB):
        n = i * SUB
        ref_b = bcum[n - 1:n]
        qt = q[n:n + SUB] * jnp.exp(bcum[n:n + SUB] - ref_b)
        kt = k[:n] * jnp.exp(ref_b - bcum[:n])
        kt = jnp.concatenate([kt, jnp.zeros((CHUNK - n, HG_DK), F32)], axis=0)
        o_rows.append(_bdot_nt(qt, kt))
    return jnp.where(mask, a, 0.0) + jnp.concatenate(o_rows, axis=0)


def _hgrn_out(o, zg, g):
    o = o * lax.rsqrt(jnp.mean(o * o, axis=-1, keepdims=True) + EPS) * g
    return (o * _sigmoid(zg)).astype(BF16)


def _masked_mix_weights(wmix_ref, mask):
    return [jnp.where(mask, wmix_ref[g], 0.0).astype(BF16) for g in range(CM_GROUPS)]


def _gmlp_chunk(z_ref, c0, rows, lng_ref, lnb_ref, wms, bcol_ref, ocat_ref):
    u = _gelu(z_ref[rows, c0:c0 + CM_WIDTH])
    gv = _gelu(z_ref[rows, c0 + CM_WIDTH:c0 + 2 * CM_WIDTH])
    mu = jnp.mean(gv, axis=-1, keepdims=True)
    dv = gv - mu
    var = jnp.mean(dv * dv, axis=-1, keepdims=True)
    vn = dv * lax.rsqrt(var + EPS) * lng_ref[...] + lnb_ref[...]
    for g in range(CM_GROUPS):
        gs = slice(g * CM_GROUP_DIM, (g + 1) * CM_GROUP_DIM)
        mixed = _bdot(wms[g], vn[:, gs]) + bcol_ref[:, g:g + 1]
        ocat_ref[rows, HG_WIDTH + g * CM_GROUP_DIM:HG_WIDTH + (g + 1) * CM_GROUP_DIM] = (
            u[:, gs] * mixed).astype(BF16)
    return vn


def _sample_pair_probs(hq8, ck_ref):
    nrow = 8 * X_HEADS
    rh = lax.broadcasted_iota(jnp.int32, (nrow, N_MEM * X_HEADS), 0) // 8
    ch = lax.broadcasted_iota(jnp.int32, (nrow, N_MEM * X_HEADS), 1) % X_HEADS
    q = jnp.concatenate([hq8[:, h * X_HEAD_DIM:(h + 1) * X_HEAD_DIM] for h in range(X_HEADS)],
                        axis=0).astype(BF16)
    probs = []
    for r in range(2):
        k2 = ck_ref[r].reshape(N_MEM * X_HEADS, X_HEAD_DIM)
        s = jnp.where(rh == ch, _bdot_nt(q, k2) * (X_HEAD_DIM ** -0.5), -jnp.inf)
        probs.append(_softmax_rows(s))
    return probs


def _sample_pair_context(probs, cv_ref, ca_ref):
    first = (lax.broadcasted_iota(jnp.int32, (8 * X_HEADS, X_HEAD_DIM), 0) % 8) < 4
    outs = [_bdot(probs[r], cv_ref[r].reshape(N_MEM * X_HEADS, X_HEAD_DIM)) for r in range(2)]
    o = jnp.where(first, outs[0], outs[1])
    for h in range(X_HEADS):
        ca_ref[:, h * X_HEAD_DIM:(h + 1) * X_HEAD_DIM] = o[8 * h:8 * h + 8]


def _mlp(hn_bf16, wup_ref, wdn_ref, hm_s):
    for c in range(D_FF // D_MODEL):
        fs = slice(c * D_MODEL, (c + 1) * D_MODEL)
        hm = jnp.maximum(jnp.dot(hn_bf16, wup_ref[:, fs], preferred_element_type=F32), 0.0)
        hm_s[:, fs] = (hm * hm).astype(BF16)
    return jnp.dot(hm_s[...], wdn_ref[...], preferred_element_type=F32)


def _const_spec(shape):
    nd = len(shape)
    return pl.BlockSpec(shape, lambda *_: (0,) * nd, pipeline_mode=pl.Buffered(1))


def _mixattn_kernel(x_ref, mk_ref, mv_ref, hqs_ref, ck_ref, cv_ref, lb_ref, gmix_ref, win_ref, hgn_ref,
                    lng_ref, lnb_ref, wmix_ref, bcol_ref, wout_ref, gc_ref, wcq_ref, wco_ref,
                    x2_ref, sout_ref, cas_ref,
                    zg_ref, z_ref, ocat_ref, q_s, k_s, b_s, a_s, st0, st1, st2, st3, h1_s, h1n_s, ca_s,
                    *, tm, nt, npairs):
    t = pl.program_id(0)
    i = lax.rem(t, nt)
    nch = tm // CHUNK
    nj = nch * HG_HEADS
    mask = _chunk_masks(False)
    st_refs = (st0, st1, st2, st3)

    @pl.when(t == 0)
    def _():
        h1_s[...] = jnp.zeros_like(h1_s)
        h1n_s[...] = jnp.zeros_like(h1n_s)

    @pl.when(i == 0)
    def _():
        for st in st_refs:
            st[...] = jnp.zeros_like(st)

    x = x_ref[0]
    xn = _rmsnorm(x, gmix_ref[...]).astype(BF16)
    ngate = 2 * HG_WIDTH
    zg_ref[...] = jnp.dot(xn, win_ref[:, 0:ngate], preferred_element_type=F32)
    z_ref[...] = jnp.dot(xn, win_ref[:, ngate:IN_WIDTH], preferred_element_type=F32)
    lb_all = _lower_bound(lb_ref)
    dmax = jnp.zeros((1, HG_WIDTH), F32)
    for c in range(nch):
        rows = slice(c * CHUNK, (c + 1) * CHUNK)
        q, k, bcum = _gates(zg_ref[rows, 0:HG_WIDTH], zg_ref[rows, HG_WIDTH:2 * HG_WIDTH], lb_all, CHUNK)
        q_s[rows, :] = q
        k_s[rows, :] = k
        b_s[rows, :] = bcum
        dmax = jnp.maximum(dmax, _block_decay(bcum, False))
    fast_ok = jnp.max(dmax) < FAST_MAX_DECAY

    def fill_scores(j, fast):
        c, h = divmod(j, HG_HEADS)
        rows = slice(c * CHUNK, (c + 1) * CHUNK)
        cs = slice(h * HG_DK, (h + 1) * HG_DK)
        q, k, bcum = q_s[rows, cs], k_s[rows, cs], b_s[rows, cs]
        a = _scores_fast(q, k, bcum, False) if fast else _scores_exact(q, k, bcum, False, mask)
        a_s[j] = jnp.where(mask, a, 0.0).astype(BF16)

    def head_chunk(j, wms):
        c, h = divmod(j, HG_HEADS)
        rows = slice(c * CHUNK, (c + 1) * CHUNK)
        cs = slice(h * HG_DK, (h + 1) * HG_DK)
        q, k, bcum = q_s[rows, cs], k_s[rows, cs], b_s[rows, cs]
        v = z_ref[rows, h * HG_DK:(h + 1) * HG_DK]
        zg = z_ref[rows, HG_WIDTH + h * HG_DK:HG_WIDTH + (h + 1) * HG_DK]
        eb = jnp.exp(bcum)
        kd = k * jnp.exp(bcum[CHUNK - 1:CHUNK] - bcum)
        st = st_refs[h][...]
        o = jnp.dot(a_s[j], v.astype(BF16), preferred_element_type=F32)
        o = o + _bdot_nt(q * eb, st)
        st_refs[h][...] = st * eb[CHUNK - 1:CHUNK] + _bdot(v.T, kd)
        ocat_ref[rows, cs] = _hgrn_out(o, zg, hgn_ref[:, cs])
        if h == HG_HEADS - 1:
            _gmlp_chunk(z_ref, 2 * HG_WIDTH, rows, lng_ref, lnb_ref, wms, bcol_ref, ocat_ref)

    def region2(fast):
        wms = _masked_mix_weights(wmix_ref, mask)
        scale = X_HEAD_DIM ** -0.5
        heads = [slice(h * X_HEAD_DIM, (h + 1) * X_HEAD_DIM) for h in range(X_HEADS)]
        hq = jnp.dot(h1n_s[...], wcq_ref[...], preferred_element_type=F32)
        for j in range(0, nj // 2):
            fill_scores(j, fast)
        probs = [_softmax_rows(_bdot_nt(hq[:, hs], mk_ref[0, :, hs]) * scale) for hs in heads]
        for j in range(nj // 2, nj):
            fill_scores(j, fast)
        sprobs = [_sample_pair_probs(hqs_ref[8 * p:8 * p + 8, :], ck_ref.at[2 * p:2 * p + 2])
                  for p in range(npairs)]
        for j in range(0, nj // 2):
            head_chunk(j, wms)
        for h, hs in enumerate(heads):
            ca_s[:, hs] = _bdot(probs[h], mv_ref[0, :, hs]).astype(BF16)
        for j in range(nj // 2, nj):
            head_chunk(j, wms)
        for p in range(npairs):
            _sample_pair_context(sprobs[p], cv_ref.at[2 * p:2 * p + 2], cas_ref.at[8 * p:8 * p + 8, :])
        x2_ref[0] = h1_s[...] + jnp.dot(ca_s[...], wco_ref[...], preferred_element_type=F32)
        h1 = x_ref[0] + jnp.dot(ocat_ref[...], wout_ref[...], preferred_element_type=F32)
        h1_s[...] = h1
        h1n_s[...] = _rmsnorm(h1, gc_ref[...]).astype(BF16)

    @pl.when(fast_ok)
    def _():
        region2(True)

    @pl.when(jnp.logical_not(fast_ok))
    def _():
        region2(False)

    @pl.when(i == nt - 1)
    def _():
        for h in range(HG_HEADS):
            sout_ref[0, h] = st_refs[h][...].T


def _mixattn_layer(x, mk, mv, hqs, ck, cv, lb_param, g_mix, w_in, hg_norm_g, ln_v_g, ln_v_b, wmix, bcol,
                   w_out, g_cross, w_cq, w_co, tm):
    B, L, _ = x.shape
    nt = L // tm
    T = B * nt
    npairs = ck.shape[0] // (2 * T)
    assert npairs * 2 * T == ck.shape[0]

    def cur(t):
        return jnp.minimum(t, T - 1)

    def prev(t):
        return jnp.maximum(t - 1, 0)

    consts = (lb_param, g_mix, w_in, hg_norm_g, ln_v_g, ln_v_b, wmix, bcol, w_out, g_cross, w_cq, w_co)
    kv_spec = pl.BlockSpec((2 * npairs, N_MEM, X_HEADS, X_HEAD_DIM), lambda t: (cur(t), 0, 0, 0))
    in_specs = [
        pl.BlockSpec((1, tm, D_MODEL), lambda t: (cur(t) // nt, cur(t) % nt, 0)),
        pl.BlockSpec((1, N_MEM, D_MODEL), lambda t: (prev(t) // nt, 0, 0)),
        pl.BlockSpec((1, N_MEM, D_MODEL), lambda t: (prev(t) // nt, 0, 0)),
        pl.BlockSpec((8 * npairs, D_MODEL), lambda t: (cur(t), 0)), kv_spec, kv_spec,
    ] + [_const_spec(a.shape) for a in consts]
    out_specs = [
        pl.BlockSpec((1, tm, D_MODEL), lambda t: (prev(t) // nt, prev(t) % nt, 0)),
        pl.BlockSpec((1, HG_HEADS, HG_DK, HG_DK), lambda t: (cur(t) // nt, 0, 0, 0)),
        pl.BlockSpec((8 * npairs, D_MODEL), lambda t: (cur(t), 0)),
    ]
    scratch = [
        pltpu.VMEM((tm, 2 * HG_WIDTH), F32),
        pltpu.VMEM((tm, IN_WIDTH - 2 * HG_WIDTH), F32),
        pltpu.VMEM((tm, D_MODEL), BF16),
        pltpu.VMEM((tm, HG_WIDTH), F32), pltpu.VMEM((tm, HG_WIDTH), F32),
        pltpu.VMEM((tm, HG_WIDTH), F32),
        pltpu.VMEM((tm // CHUNK * HG_HEADS, CHUNK, CHUNK), BF16),
        pltpu.VMEM((HG_DK, HG_DK), F32), pltpu.VMEM((HG_DK, HG_DK), F32),
        pltpu.VMEM((HG_DK, HG_DK), F32), pltpu.VMEM((HG_DK, HG_DK), F32),
        pltpu.VMEM((tm, D_MODEL), F32),
        pltpu.VMEM((tm, D_MODEL), BF16),
        pltpu.VMEM((tm, D_MODEL), BF16),
    ]
    return pl.pallas_call(
        functools.partial(_mixattn_kernel, tm=tm, nt=nt, npairs=npairs),
        grid=(T + 1,),
        in_specs=in_specs,
        out_specs=out_specs,
        out_shape=[jax.ShapeDtypeStruct((B, L, D_MODEL), F32),
                   jax.ShapeDtypeStruct((B, HG_HEADS, HG_DK, HG_DK), F32),
                   jax.ShapeDtypeStruct(hqs.shape, F32)],
        scratch_shapes=scratch,
        compiler_params=pltpu.CompilerParams(dimension_semantics=("arbitrary",),
                                             vmem_limit_bytes=VMEM_LIMIT_BYTES),
        name="mix_attn",
    )(x, mk, mv, hqs, ck, cv, *consts)


def _mlp_kernel(x2_ref, gm_ref, wup_ref, wdn_ref, gf_ref, y_ref, hm_s):
    x2 = x2_ref[0]
    y_ref[0] = _rmsnorm(x2 + _mlp(_rmsnorm(x2, gm_ref[...]).astype(BF16), wup_ref, wdn_ref, hm_s),
                        gf_ref[...])


def _mlp_layer(x2, g_mlp, w_up, w_down, g_final, tm):
    B, L, _ = x2.shape
    nt = L // tm
    consts = (g_mlp, w_up, w_down, g_final)
    tile = pl.BlockSpec((1, tm, D_MODEL), lambda t: (t // nt, t % nt, 0))
    return pl.pallas_call(
        _mlp_kernel,
        grid=(B * nt,),
        in_specs=[tile] + [_const_spec(a.shape) for a in consts],
        out_specs=tile,
        out_shape=jax.ShapeDtypeStruct((B, L, D_MODEL), F32),
        scratch_shapes=[pltpu.VMEM((tm, D_FF), BF16)],
        compiler_params=pltpu.CompilerParams(dimension_semantics=("arbitrary",),
                                             vmem_limit_bytes=VMEM_LIMIT_BYTES),
        name="mlp_prompt",
    )(x2, *consts)


def _mix_sample_kernel(x_ref, s0_ref, lb_ref, gmix_ref, win_ref, hgn_ref, lng_ref, lnb_ref, wmix_ref,
                       bcol_ref, wout_ref, gc_ref, wcq_ref, h_ref, sout_ref, vn_ref, hq_ref,
                       z_ref, ocat_ref, q_s, k_s, b_s, wm_s, bc_s, *, tm):
    x = x_ref[...].reshape(tm, D_MODEL)
    z_ref[...] = jnp.dot(_rmsnorm(x, gmix_ref[...]).astype(BF16), win_ref[...],
                         preferred_element_type=F32)
    lb_all = _lower_bound(lb_ref)
    mask = _chunk_masks(True)
    row = lax.broadcasted_iota(jnp.int32, (CHUNK, CHUNK), 0)
    col = lax.broadcasted_iota(jnp.int32, (CHUNK, CHUNK), 1)
    eye = row == col

    @pl.when(pl.program_id(0) == 0)
    def _():
        pick = (row % 4 == col).astype(BF16)
        for g in range(CM_GROUPS):
            corner = jnp.where((row < 4) & (col < 4), wmix_ref[g], 0.0)
            wm_s[g] = _bdot_nt(_bdot(pick, corner), pick)
        pos = lax.broadcasted_iota(jnp.int32, (CHUNK, CM_GROUPS), 0) % 4
        bc = jnp.zeros((CHUNK, CM_GROUPS), F32)
        for t in range(4):
            bc = jnp.where(pos == t, bcol_ref[t:t + 1, :], bc)
        bc_s[...] = bc
    r16 = lax.broadcasted_iota(jnp.int32, (SUB, HG_DK), 0)

    def gate_body(c, dmax):
        rows = pl.ds(pl.multiple_of(c * CHUNK, CHUNK), CHUNK)
        q, k, bcum = _gates(z_ref[rows, 0:HG_WIDTH], z_ref[rows, HG_WIDTH:2 * HG_WIDTH], lb_all, 4)
        q_s[rows, :] = q
        k_s[rows, :] = k
        b_s[rows, :] = bcum
        return jnp.maximum(dmax, _block_decay(bcum, True))

    dmax = lax.fori_loop(0, tm // CHUNK, gate_body, jnp.zeros((1, HG_WIDTH), F32))
    fast_ok = jnp.max(dmax) < FAST_MAX_DECAY

    def chunk_body(c, carry, *, fast):
        rows = pl.ds(pl.multiple_of(c * CHUNK, CHUNK), CHUNK)
        for h in range(HG_HEADS):
            cs = slice(h * HG_DK, (h + 1) * HG_DK)
            q, k, bcum = q_s[rows, cs], k_s[rows, cs], b_s[rows, cs]
            v = z_ref[rows, 2 * HG_WIDTH + h * HG_DK:2 * HG_WIDTH + (h + 1) * HG_DK]
            zg = z_ref[rows, 3 * HG_WIDTH + h * HG_DK:3 * HG_WIDTH + (h + 1) * HG_DK]
            a = _scores_fast(q, k, bcum, True) if fast else _scores_exact(q, k, bcum, True, mask)
            o = _bdot(jnp.where(mask, a, 0.0), v)
            eb = jnp.exp(bcum)
            qd = q * eb
            o_parts = []
            for gb in range(CHUNK // SUB):
                blk = slice(gb * SUB, (gb + 1) * SUB)
                qd_b, v_b, eb_b, k_b, b_b = qd[blk], v[blk], eb[blk], k[blk], bcum[blk]
                inter = jnp.zeros((SUB, HG_DK), F32)
                for j in range(4):
                    req = c * (CHUNK // 4) + gb * 4 + j
                    s0 = s0_ref[req, h]
                    last = 4 * j + 3
                    inter = jnp.where(r16 // 4 == j, _bdot(qd_b, s0), inter)
                    kd = jnp.where(r16 // 4 == j, k_b * jnp.exp(b_b[last:last + 1] - b_b), 0.0)
                    upd = lax.dot_general(kd.astype(BF16), v_b.astype(BF16),
                                          (((0,), (0,)), ((), ())), preferred_element_type=F32)
                    dcol = jnp.sum(jnp.where(eye, eb_b[last:last + 1], 0.0), axis=-1, keepdims=True)
                    sout_ref[req, h] = dcol * s0 + upd
                o_parts.append(inter)
            o = o + jnp.concatenate(o_parts, axis=0)
            ocat_ref[rows, cs] = _hgrn_out(o, zg, hgn_ref[:, cs])
        vn = _gmlp_chunk(z_ref, 4 * HG_WIDTH, rows, lng_ref, lnb_ref, _masked_mix_weights(wm_s, mask),
                         bc_s, ocat_ref)
        vn_ref[pl.ds(pl.multiple_of(c * (CHUNK // 4), CHUNK // 4), CHUNK // 4)] = vn.reshape(
            CHUNK // 4, 4, CM_WIDTH)
        return carry

    @pl.when(fast_ok)
    def _():
        lax.fori_loop(0, tm // CHUNK, functools.partial(chunk_body, fast=True), 0)

    @pl.when(jnp.logical_not(fast_ok))
    def _():
        lax.fori_loop(0, tm // CHUNK, functools.partial(chunk_body, fast=False), 0)

    h = x + jnp.dot(ocat_ref[...], wout_ref[...], preferred_element_type=F32)
    h_ref[0] = h
    hq_ref[...] = jnp.dot(_rmsnorm(h, gc_ref[...]).astype(BF16), wcq_ref[...], preferred_element_type=F32)


def _mix_sample(x, s0, lb_param, g_mix, w_in, hg_norm_g, ln_v_g, ln_v_b, wmix, bcol, w_out, g_cross,
                w_cq, tm):
    nreq = tm // 4
    nt = x.shape[0] // nreq
    consts = (lb_param, g_mix, w_in, hg_norm_g, ln_v_g, ln_v_b, wmix, bcol, w_out, g_cross, w_cq)
    in_specs = [
        pl.BlockSpec((nreq, 4, D_MODEL), lambda i: (i, 0, 0)),
        pl.BlockSpec((nreq, HG_HEADS, HG_DK, HG_DK), lambda i: (i, 0, 0, 0)),
    ] + [_const_spec(a.shape) for a in consts]
    out_specs = [
        pl.BlockSpec((1, tm, D_MODEL), lambda i: (i, 0, 0)),
        pl.BlockSpec((nreq, HG_HEADS, HG_DK, HG_DK), lambda i: (i, 0, 0, 0)),
        pl.BlockSpec((nreq, 4, CM_WIDTH), lambda i: (i, 0, 0)),
        pl.BlockSpec((tm, D_MODEL), lambda i: (i, 0)),
    ]
    return pl.pallas_call(
        functools.partial(_mix_sample_kernel, tm=tm),
        grid=(nt,),
        in_specs=in_specs,
        out_specs=out_specs,
        out_shape=[jax.ShapeDtypeStruct((nt, tm, D_MODEL), F32),
                   jax.ShapeDtypeStruct(s0.shape, F32),
                   jax.ShapeDtypeStruct((nt * nreq, 4, CM_WIDTH), F32),
                   jax.ShapeDtypeStruct((nt * tm, D_MODEL), F32)],
        scratch_shapes=[pltpu.VMEM((tm, IN_WIDTH), F32), pltpu.VMEM((tm, D_MODEL), BF16)]
        + [pltpu.VMEM((tm, HG_WIDTH), F32)] * 3
        + [pltpu.VMEM((CM_GROUPS, CHUNK, CHUNK), F32), pltpu.VMEM((CHUNK, CM_GROUPS), F32)],
        compiler_params=pltpu.CompilerParams(dimension_semantics=("arbitrary",),
                                             vmem_limit_bytes=VMEM_LIMIT_BYTES),
        name="mix_sample",
    )(x, s0, *consts)


def _memkv_kernel(m_ref, g_ref, wk_ref, wv_ref, k_ref, v_ref, kb_ref, vb_ref):
    mn = _rmsnorm(m_ref[...], g_ref[...]).astype(BF16)
    k = jnp.dot(mn, wk_ref[...], preferred_element_type=F32)
    v = jnp.dot(mn, wv_ref[...], preferred_element_type=F32)
    kb_ref[...] = k.astype(BF16)
    vb_ref[...] = v.astype(BF16)
    for src, dst in ((k, k_ref), (v, v_ref)):
        for r in range(dst.shape[0]):
            rows = src[r * N_MEM:(r + 1) * N_MEM]
            dst[r] = jnp.stack([rows[:, h * X_HEAD_DIM:(h + 1) * X_HEAD_DIM] for h in range(X_HEADS)],
                               axis=1)


def _memkv(mem, g_mem, w_ck, w_cv, tm):
    n = mem.shape[0]
    return pl.pallas_call(
        _memkv_kernel,
        grid=(n // tm,),
        in_specs=[pl.BlockSpec((tm, D_MODEL), lambda i: (i, 0)), _const_spec(g_mem.shape),
                  _const_spec(w_ck.shape), _const_spec(w_cv.shape)],
        out_specs=[pl.BlockSpec((tm // N_MEM, N_MEM, X_HEADS, X_HEAD_DIM), lambda i: (i, 0, 0, 0))] * 2
        + [pl.BlockSpec((tm, D_MODEL), lambda i: (i, 0))] * 2,
        out_shape=[jax.ShapeDtypeStruct((n // N_MEM, N_MEM, X_HEADS, X_HEAD_DIM), F32)] * 2
        + [jax.ShapeDtypeStruct((n, D_MODEL), BF16)] * 2,
        compiler_params=pltpu.CompilerParams(dimension_semantics=("arbitrary",),
                                             vmem_limit_bytes=VMEM_LIMIT_BYTES),
        name="memkv",
    )(mem, g_mem, w_ck, w_cv)


def _post_sample_kernel(x_ref, ca_ref, wco_ref, gm_ref, wup_ref, wdn_ref, gf_ref, y_ref, hm_s):
    x = x_ref[0] + jnp.dot(ca_ref[...].astype(BF16), wco_ref[...], preferred_element_type=F32)
    x = x + _mlp(_rmsnorm(x, gm_ref[...]).astype(BF16), wup_ref, wdn_ref, hm_s)
    y_ref[...] = _rmsnorm(x, gf_ref[...]).reshape(y_ref.shape)


def _post_sample(x, ca, w_co, g_mlp, w_up, w_down, g_final):
    nt, tm, _ = x.shape
    in_specs = [
        pl.BlockSpec((1, tm, D_MODEL), lambda i: (i, 0, 0)),
        pl.BlockSpec((tm, D_MODEL), lambda i: (i, 0)),
        _const_spec(w_co.shape), _const_spec(g_mlp.shape), _const_spec(w_up.shape),
        _const_spec(w_down.shape), _const_spec(g_final.shape),
    ]
    return pl.pallas_call(
        _post_sample_kernel,
        grid=(nt,),
        in_specs=in_specs,
        out_specs=pl.BlockSpec((tm // 4, 4, D_MODEL), lambda i: (i, 0, 0)),
        out_shape=jax.ShapeDtypeStruct((nt * tm // 4, 4, D_MODEL), F32),
        scratch_shapes=[pltpu.VMEM((tm, D_FF), BF16)],
        compiler_params=pltpu.CompilerParams(dimension_semantics=("arbitrary",),
                                             vmem_limit_bytes=VMEM_LIMIT_BYTES),
        name="post_sample",
    )(x, ca, w_co, g_mlp, w_up, w_down, g_final)


def _cast_kernel(*refs):
    n = len(refs) // 2
    for src, dst in zip(refs[:n], refs[n:]):
        dst[...] = src[...].astype(BF16)


def _cast_weights(*ws):
    steps = 8
    specs = [pl.BlockSpec((w.shape[0] // steps, w.shape[1]), lambda i: (i, 0)) for w in ws]
    return pl.pallas_call(
        _cast_kernel,
        grid=(steps,),
        in_specs=specs,
        out_specs=specs,
        out_shape=[jax.ShapeDtypeStruct(w.shape, BF16) for w in ws],
        compiler_params=pltpu.CompilerParams(dimension_semantics=("arbitrary",),
                                             vmem_limit_bytes=VMEM_LIMIT_BYTES),
        name="cast_weights",
    )(*ws)


def kernel(x_prompt, x_sample, mem_prompt, state_hgrn, cache_mem_k, cache_mem_v, lb_param, g_mix,
           w_in, hg_norm_g, ln_v_g, ln_v_b, w_s, b_s, w_out, g_cross, g_mem, w_cq, w_ck, w_cv, w_co,
           g_mlp, w_up, w_down, g_final):
    B, L, _ = x_prompt.shape
    DB, DL, _ = x_sample.shape
    assert DL == 4 and g_mix.shape[0] == 1

    row = lambda a: a.reshape(1, -1)
    win_b, wout_b, wcq_b, wco_b, wck_b, wcv_b, wup_b, wdn_b = _cast_weights(
        w_in[0], w_out[0], w_cq[0], w_co[0], w_ck[0], w_cv[0], w_up[0], w_down[0])
    gmix, hgn, lng, lnb = row(g_mix[0]), row(hg_norm_g[0]), row(ln_v_g[0]), row(ln_v_b[0])
    gcr, gmem, gmlp, gfin = row(g_cross[0]), row(g_mem[0]), row(g_mlp[0]), row(g_final)
    wmix = w_s[0]
    bcol = b_s[0].T

    tm_s = 128
    h_s, s_s, vn_s, hq_s = _mix_sample(x_sample, state_hgrn[0], lb_param, gmix, win_b, hgn, lng, lnb, wmix,
                                       bcol, wout_b, gcr, wcq_b, tm=tm_s)

    mk, mv, mk_b, mv_b = _memkv(mem_prompt.reshape(B * N_MEM, D_MODEL), gmem, wck_b, wcv_b, tm=512)
    x2_p, s_p, ca_s = _mixattn_layer(x_prompt, mk_b.reshape(B, N_MEM, D_MODEL),
                                     mv_b.reshape(B, N_MEM, D_MODEL), hq_s, cache_mem_k[0], cache_mem_v[0],
                                     lb_param, gmix, win_b, hgn, lng, lnb, wmix, bcol, wout_b, gcr, wcq_b,
                                     wco_b, tm=512)
    y_p = _mlp_layer(x2_p, gmlp, wup_b, wdn_b, gfin, tm=1024)

    y_s = _post_sample(h_s.reshape(1, DB * DL, D_MODEL), ca_s, wco_b, gmlp, wup_b, wdn_b, gfin)

    return (y_p, y_s, s_p[None], s_s[None],
            mk[None], mv[None],
            vn_s[None])
```

```python
import functools

import jax
import jax.numpy as jnp
from jax import lax
from jax.experimental import pallas as pl
from jax.experimental.pallas import tpu as pltpu

F32 = jnp.float32
BF16 = jnp.bfloat16

D_MODEL = 1024
HG_WIDTH = 512
HG_HEADS = 4
HG_DK = 128
CM_WIDTH = 512
CM_GROUPS = 4
CM_GROUP_DIM = 128
IN_WIDTH = 4 * HG_WIDTH + 2 * CM_WIDTH
N_MEM = 256
X_HEADS = 4
X_HEAD_DIM = 256
D_FF = 4096
EPS = 1e-6

CHUNK = 128
SUB = 16
VMEM_LIMIT_BYTES = 62 * 1024 * 1024
LOG2E = 1.4426950408889634
FAST_BLOCK = 32
FAST_MAX_DECAY = 80.0


def _bdot(a, b):
    return jnp.dot(a.astype(BF16), b.astype(BF16), preferred_element_type=F32)


def _bdot_nt(a, b):
    return lax.dot_general(a.astype(BF16), b.astype(BF16), (((1,), (1,)), ((), ())),
                           preferred_element_type=F32)


def _rmsnorm(x, g):
    ms = jnp.mean(x * x, axis=-1, keepdims=True)
    return x * lax.rsqrt(ms + EPS) * g


def _sigmoid(x):
    return 0.5 * jnp.tanh(0.5 * x) + 0.5


def _gelu(x):
    return 0.5 * x * (1.0 + jnp.tanh(0.7978845608028654 * (x + 0.044715 * (x * x * x))))


def _softmax_rows(s):
    m = jnp.max(s, axis=-1, keepdims=True)
    e = jnp.exp(s - m)
    return e / jnp.sum(e, axis=-1, keepdims=True)


def _seg_cumsum(x, seg):
    n = x.shape[0]
    pos = lax.broadcasted_iota(jnp.int32, x.shape, 0) % seg
    s = 1
    while s < min(seg, 8):
        x = x + jnp.where(pos >= s, pltpu.roll(x, s, 0), 0.0)
        s *= 2
    while s < seg:
        parts = []
        for r0 in range(0, n, seg):
            parts.append(x[r0:r0 + s])
            parts.append(x[r0 + s:r0 + seg] + x[r0:r0 + seg - s])
        x = jnp.concatenate(parts, axis=0)
        s *= 2
    return x


def _chunk_cumsum(x):
    n = x.shape[0]
    tril = (lax.broadcasted_iota(jnp.int32, (n, n), 0)
            >= lax.broadcasted_iota(jnp.int32, (n, n), 1)).astype(BF16)
    hi = x.astype(BF16)
    r1 = x - hi.astype(F32)
    mid = r1.astype(BF16)
    lo = (r1 - mid.astype(F32)).astype(BF16)
    return (jnp.dot(tril, hi, preferred_element_type=F32) + jnp.dot(tril, mid, preferred_element_type=F32)
            + jnp.dot(tril, lo, preferred_element_type=F32))


def _lower_bound(lb_ref):
    lbp = lb_ref[...]
    lbe = jnp.exp(lbp - jnp.max(lbp, axis=0, keepdims=True))
    return lbe[0:1] / jnp.sum(lbe, axis=0, keepdims=True)


def _gates(zq, zf, lb, seg):
    half = 0.5 - 0.5 * lb
    hth = half * jnp.tanh(0.5 * zf)
    logf = jnp.log((lb + half) + hth)
    bcum = _chunk_cumsum(logf) if seg == logf.shape[0] else _seg_cumsum(logf, seg)
    hq = 0.5 * zq
    return hq * jnp.tanh(hq) + hq, half - hth, bcum


def _block_decay(bcum, sample):
    if sample:
        return jnp.max(-bcum, axis=0, keepdims=True)
    d = -bcum[FAST_BLOCK - 1:FAST_BLOCK]
    for i in range(1, CHUNK // FAST_BLOCK):
        n0, n1 = i * FAST_BLOCK, (i + 1) * FAST_BLOCK
        d = jnp.maximum(d, bcum[n0 - 1:n0] - bcum[n1 - 1:n1])
    return d


def _chunk_masks(sample):
    row = lax.broadcasted_iota(jnp.int32, (CHUNK, CHUNK), 0)
    col = lax.broadcasted_iota(jnp.int32, (CHUNK, CHUNK), 1)
    if sample:
        return (row // 4 == col // 4) & (row >= col)
    return row >= col


def _scores_fast(q, k, bcum, sample):
    if sample:
        return _bdot_nt(q * jnp.exp(bcum), k * jnp.exp(-bcum))
    a_rows = []
    zero = jnp.zeros((1, HG_DK), F32)
    kt, prev_ref = None, zero
    for i in range(CHUNK // FAST_BLOCK):
        n0, n1 = i * FAST_BLOCK, (i + 1) * FAST_BLOCK
        ref_b = bcum[n0 - 1:n0] if i else zero
        qt = q[n0:n1] * jnp.exp(bcum[n0:n1] - ref_b)
        kt_blk = k[n0:n1] * jnp.exp(ref_b - bcum[n0:n1])
        kt = kt_blk if kt is None else jnp.concatenate([kt * jnp.exp(ref_b - prev_ref), kt_blk], axis=0)
        prev_ref = ref_b
        kt_b = kt.astype(BF16)
        if n1 < CHUNK:
            kt_b = jnp.concatenate([kt_b, jnp.zeros((CHUNK - n1, HG_DK), BF16)], axis=0)
        a_rows.append(_bdot_nt(qt, kt_b))
    return jnp.concatenate(a_rows, axis=0)


def _scores_exact(q, k, bcum, sample, mask):
    col8 = lax.broadcasted_iota(jnp.int32, (8, CHUNK), 1)
    b2 = bcum * LOG2E
    cexp = b2 - jnp.log2(k)
    a_rows = []
    for gb in range(CHUNK // SUB):
        lo = slice(gb * SUB, gb * SUB + 8)
        hi = slice(gb * SUB + 8, (gb + 1) * SUB)
        a_lo = jnp.zeros((8, CHUNK), F32)
        a_hi = jnp.zeros((8, CHUNK), F32)
        for s in range(SUB):
            sg = gb * SUB + s
            c_s = cexp[sg:sg + 1]
            if s < 8:
                p = q[lo] * jnp.exp2(b2[lo] - c_s)
                a_lo = jnp.where(col8 == sg, jnp.sum(p, axis=-1, keepdims=True), a_lo)
            if (not sample) or s >= 8:
                p = q[hi] * jnp.exp2(b2[hi] - c_s)
                a_hi = jnp.where(col8 == sg, jnp.sum(p, axis=-1, keepdims=True), a_hi)
        a_rows.append(a_lo)
        a_rows.append(a_hi)
    a = jnp.concatenate(a_rows, axis=0)
    if sample:
        return a
    o_rows = [jnp.zeros((SUB, CHUNK), F32)]
    for i in range(1, CHUNK // SUB):
        n = i * SUB
        ref_b = bcum[n - 1:n]
        qt = q[n:n + SUB] * jnp.exp(bcum[n:n + SUB] - ref_b)
        kt = k[:n] * jnp.exp(ref_b - bcum[:n])
        kt = jnp.concatenate([kt, jnp.zeros((CHUNK - n, HG_DK), F32)], axis=0)
        o_rows.append(_bdot_nt(qt, kt))
    return jnp.where(mask, a, 0.0) + jnp.concatenate(o_rows, axis=0)


def _hgrn_out(o, zg, g):
    o = o * lax.rsqrt(jnp.mean(o * o, axis=-1, keepdims=True) + EPS) * g
    return (o * _sigmoid(zg)).astype(BF16)


def _masked_mix_weights(wmix_ref, mask):
    return [jnp.where(mask, wmix_ref[g], 0.0).astype(BF16) for g in range(CM_GROUPS)]


def _gmlp_chunk(z_ref, c0, rows, lng_ref, lnb_ref, wms, bcol_ref, ocat_ref):
    u = _gelu(z_ref[rows, c0:c0 + CM_WIDTH])
    gv = _gelu(z_ref[rows, c0 + CM_WIDTH:c0 + 2 * CM_WIDTH])
    mu = jnp.mean(gv, axis=-1, keepdims=True)
    dv = gv - mu
    var = jnp.mean(dv * dv, axis=-1, keepdims=True)
    vn = dv * lax.rsqrt(var + EPS) * lng_ref[...] + lnb_ref[...]
    for g in range(CM_GROUPS):
        gs = slice(g * CM_GROUP_DIM, (g + 1) * CM_GROUP_DIM)
        mixed = _bdot(wms[g], vn[:, gs]) + bcol_ref[:, g:g + 1]
        ocat_ref[rows, HG_WIDTH + g * CM_GROUP_DIM:HG_WIDTH + (g + 1) * CM_GROUP_DIM] = (
            u[:, gs] * mixed).astype(BF16)
    return vn


def _sample_pair_probs(hq8, ck_ref):
    nrow = 8 * X_HEADS
    rh = lax.broadcasted_iota(jnp.int32, (nrow, N_MEM * X_HEADS), 0) // 8
    ch = lax.broadcasted_iota(jnp.int32, (nrow, N_MEM * X_HEADS), 1) % X_HEADS
    q = jnp.concatenate([hq8[:, h * X_HEAD_DIM:(h + 1) * X_HEAD_DIM] for h in range(X_HEADS)],
                        axis=0).astype(BF16)
    probs = []
    for r in range(2):
        k2 = ck_ref[r].reshape(N_MEM * X_HEADS, X_HEAD_DIM)
        s = jnp.where(rh == ch, _bdot_nt(q, k2) * (X_HEAD_DIM ** -0.5), -jnp.inf)
        probs.append(_softmax_rows(s))
    return probs


def _sample_pair_context(probs, cv_ref, ca_ref):
    first = (lax.broadcasted_iota(jnp.int32, (8 * X_HEADS, X_HEAD_DIM), 0) % 8) < 4
    outs = [_bdot(probs[r], cv_ref[r].reshape(N_MEM * X_HEADS, X_HEAD_DIM)) for r in range(2)]
    o = jnp.where(first, outs[0], outs[1])
    for h in range(X_HEADS):
        ca_ref[:, h * X_HEAD_DIM:(h + 1) * X_HEAD_DIM] = o[8 * h:8 * h + 8]


def _mlp(hn_bf16, wup_ref, wdn_ref, hm_s):
    for c in range(D_FF // D_MODEL):
        fs = slice(c * D_MODEL, (c + 1) * D_MODEL)
        hm = jnp.maximum(jnp.dot(hn_bf16, wup_ref[:, fs], preferred_element_type=F32), 0.0)
        hm_s[:, fs] = (hm * hm).astype(BF16)
    return jnp.dot(hm_s[...], wdn_ref[...], preferred_element_type=F32)


def _const_spec(shape):
    nd = len(shape)
    return pl.BlockSpec(shape, lambda *_: (0,) * nd, pipeline_mode=pl.Buffered(1))


def _mixattn_kernel(x_ref, mk_ref, mv_ref, hqs_ref, ck_ref, cv_ref, lb_ref, gmix_ref, win_ref, hgn_ref,
                    lng_ref, lnb_ref, wmix_ref, bcol_ref, wout_ref, gc_ref, wcq_ref, wco_ref,
                    x2_ref, sout_ref, cas_ref,
                    zg_ref, z_ref, ocat_ref, q_s, k_s, b_s, a_s, st0, st1, st2, st3, h1_s, h1n_s, ca_s,
                    *, tm, nt, npairs):
    t = pl.program_id(0)
    i = lax.rem(t, nt)
    nch = tm // CHUNK
    nj = nch * HG_HEADS
    mask = _chunk_masks(False)
    st_refs = (st0, st1, st2, st3)

    @pl.when(t == 0)
    def _():
        h1_s[...] = jnp.zeros_like(h1_s)
        h1n_s[...] = jnp.zeros_like(h1n_s)

    @pl.when(i == 0)
    def _():
        for st in st_refs:
            st[...] = jnp.zeros_like(st)

    x = x_ref[0]
    xn = _rmsnorm(x, gmix_ref[...]).astype(BF16)
    ngate = 2 * HG_WIDTH
    zg_ref[...] = jnp.dot(xn, win_ref[:, 0:ngate], preferred_element_type=F32)
    z_ref[...] = jnp.dot(xn, win_ref[:, ngate:IN_WIDTH], preferred_element_type=F32)
    lb_all = _lower_bound(lb_ref)
    dmax = jnp.zeros((1, HG_WIDTH), F32)
    for c in range(nch):
        rows = slice(c * CHUNK, (c + 1) * CHUNK)
        q, k, bcum = _gates(zg_ref[rows, 0:HG_WIDTH], zg_ref[rows, HG_WIDTH:2 * HG_WIDTH], lb_all, CHUNK)
        q_s[rows, :] = q
        k_s[rows, :] = k
        b_s[rows, :] = bcum
        dmax = jnp.maximum(dmax, _block_decay(bcum, False))
    fast_ok = jnp.max(dmax) < FAST_MAX_DECAY

    def fill_scores(j, fast):
        c, h = divmod(j, HG_HEADS)
        rows = slice(c * CHUNK, (c + 1) * CHUNK)
        cs = slice(h * HG_DK, (h + 1) * HG_DK)
        q, k, bcum = q_s[rows, cs], k_s[rows, cs], b_s[rows, cs]
        a = _scores_fast(q, k, bcum, False) if fast else _scores_exact(q, k, bcum, False, mask)
        a_s[j] = jnp.where(mask, a, 0.0).astype(BF16)

    def head_chunk(j, wms):
        c, h = divmod(j, HG_HEADS)
        rows = slice(c * CHUNK, (c + 1) * CHUNK)
        cs = slice(h * HG_DK, (h + 1) * HG_DK)
        q, k, bcum = q_s[rows, cs], k_s[rows, cs], b_s[rows, cs]
        v = z_ref[rows, h * HG_DK:(h + 1) * HG_DK]
        zg = z_ref[rows, HG_WIDTH + h * HG_DK:HG_WIDTH + (h + 1) * HG_DK]
        eb = jnp.exp(bcum)
        kd = k * jnp.exp(bcum[CHUNK - 1:CHUNK] - bcum)
        st = st_refs[h][...]
        o = jnp.dot(a_s[j], v.astype(BF16), preferred_element_type=F32)
        o = o + _bdot_nt(q * eb, st)
        st_refs[h][...] = st * eb[CHUNK - 1:CHUNK] + _bdot(v.T, kd)
        ocat_ref[rows, cs] = _hgrn_out(o, zg, hgn_ref[:, cs])
        if h == HG_HEADS - 1:
            _gmlp_chunk(z_ref, 2 * HG_WIDTH, rows, lng_ref, lnb_ref, wms, bcol_ref, ocat_ref)

    def region2(fast):
        wms = _masked_mix_weights(wmix_ref, mask)
        scale = X_HEAD_DIM ** -0.5
        heads = [slice(h * X_HEAD_DIM, (h + 1) * X_HEAD_DIM) for h in range(X_HEADS)]
        nq = nj // X_HEADS
        h1n = h1n_s[...]
        hq = []
        for h, hs in enumerate(heads):
            hq.append(jnp.dot(h1n, wcq_ref[:, hs], preferred_element_type=F32))
            for j in range(h * nq, (h + 1) * nq):
                fill_scores(j, fast)
        probs = [_softmax_rows(_bdot_nt(hq[h], mk_ref[0, :, hs]) * scale) for h, hs in enumerate(heads)]
        sprobs = [_sample_pair_probs(hqs_ref[8 * p:8 * p + 8, :], ck_ref.at[2 * p:2 * p + 2])
                  for p in range(npairs)]
        for j in range(0, nq):
            head_chunk(j, wms)
        for h, hs in enumerate(heads):
            ca_s[:, hs] = _bdot(probs[h], mv_ref[0, :, hs]).astype(BF16)
        ca = ca_s[...]
        for h, hs in enumerate(heads):
            x2_ref[0, :, hs] = h1_s[:, hs] + jnp.dot(ca, wco_ref[:, hs], preferred_element_type=F32)
            for j in range((h + 1) * nq, min(nj, (h + 2) * nq)):
                head_chunk(j, wms)
        for p in range(npairs):
            _sample_pair_context(sprobs[p], cv_ref.at[2 * p:2 * p + 2], cas_ref.at[8 * p:8 * p + 8, :])
        h1 = x_ref[0] + jnp.dot(ocat_ref[...], wout_ref[...], preferred_element_type=F32)
        h1_s[...] = h1
        h1n_s[...] = _rmsnorm(h1, gc_ref[...]).astype(BF16)

    @pl.when(fast_ok)
    def _():
        region2(True)

    @pl.when(jnp.logical_not(fast_ok))
    def _():
        region2(False)

    @pl.when(i == nt - 1)
    def _():
        for h in range(HG_HEADS):
            sout_ref[0, h] = st_refs[h][...].T


def _mixattn_layer(x, mk, mv, hqs, ck, cv, lb_param, g_mix, w_in, hg_norm_g, ln_v_g, ln_v_b, wmix, bcol,
                   w_out, g_cross, w_cq, w_co, tm):
    B, L, _ = x.shape
    nt = L // tm
    T = B * nt
    npairs = ck.shape[0] // (2 * T)
    assert npairs * 2 * T == ck.shape[0]

    def cur(t):
        return jnp.minimum(t, T - 1)

    def prev(t):
        return jnp.maximum(t - 1, 0)

    consts = (lb_param, g_mix, w_in, hg_norm_g, ln_v_g, ln_v_b, wmix, bcol, w_out, g_cross, w_cq, w_co)
    kv_spec = pl.BlockSpec((2 * npairs, N_MEM, X_HEADS, X_HEAD_DIM), lambda t: (cur(t), 0, 0, 0))
    in_specs = [
        pl.BlockSpec((1, tm, D_MODEL), lambda t: (cur(t) // nt, cur(t) % nt, 0)),
        pl.BlockSpec((1, N_MEM, D_MODEL), lambda t: (prev(t) // nt, 0, 0)),
        pl.BlockSpec((1, N_MEM, D_MODEL), lambda t: (prev(t) // nt, 0, 0)),
        pl.BlockSpec((8 * npairs, D_MODEL), lambda t: (cur(t), 0)), kv_spec, kv_spec,
    ] + [_const_spec(a.shape) for a in consts]
    out_specs = [
        pl.BlockSpec((1, tm, D_MODEL), lambda t: (prev(t) // nt, prev(t) % nt, 0)),
        pl.BlockSpec((1, HG_HEADS, HG_DK, HG_DK), lambda t: (cur(t) // nt, 0, 0, 0)),
        pl.BlockSpec((8 * npairs, D_MODEL), lambda t: (cur(t), 0)),
    ]
    scratch = [
        pltpu.VMEM((tm, 2 * HG_WIDTH), F32),
        pltpu.VMEM((tm, IN_WIDTH - 2 * HG_WIDTH), F32),
        pltpu.VMEM((tm, D_MODEL), BF16),
        pltpu.VMEM((tm, HG_WIDTH), F32), pltpu.VMEM((tm, HG_WIDTH), F32),
        pltpu.VMEM((tm, HG_WIDTH), F32),
        pltpu.VMEM((tm // CHUNK * HG_HEADS, CHUNK, CHUNK), BF16),
        pltpu.VMEM((HG_DK, HG_DK), F32), pltpu.VMEM((HG_DK, HG_DK), F32),
        pltpu.VMEM((HG_DK, HG_DK), F32), pltpu.VMEM((HG_DK, HG_DK), F32),
        pltpu.VMEM((tm, D_MODEL), F32),
        pltpu.VMEM((tm, D_MODEL), BF16),
        pltpu.VMEM((tm, D_MODEL), BF16),
    ]
    return pl.pallas_call(
        functools.partial(_mixattn_kernel, tm=tm, nt=nt, npairs=npairs),
        grid=(T + 1,),
        in_specs=in_specs,
        out_specs=out_specs,
        out_shape=[jax.ShapeDtypeStruct((B, L, D_MODEL), F32),
                   jax.ShapeDtypeStruct((B, HG_HEADS, HG_DK, HG_DK), F32),
                   jax.ShapeDtypeStruct(hqs.shape, F32)],
        scratch_shapes=scratch,
        compiler_params=pltpu.CompilerParams(dimension_semantics=("arbitrary",),
                                             vmem_limit_bytes=VMEM_LIMIT_BYTES),
        name="mix_attn",
    )(x, mk, mv, hqs, ck, cv, *consts)


def _mlp_kernel(x2_ref, gm_ref, wup_ref, wdn_ref, gf_ref, y_ref, hm_s):
    x2 = x2_ref[0]
    y_ref[0] = _rmsnorm(x2 + _mlp(_rmsnorm(x2, gm_ref[...]).astype(BF16), wup_ref, wdn_ref, hm_s),
                        gf_ref[...])


def _mlp_layer(x2, g_mlp, w_up, w_down, g_final, tm):
    B, L, _ = x2.shape
    nt = L // tm
    consts = (g_mlp, w_up, w_down, g_final)
    tile = pl.BlockSpec((1, tm, D_MODEL), lambda t: (t // nt, t % nt, 0))
    return pl.pallas_call(
        _mlp_kernel,
        grid=(B * nt,),
        in_specs=[tile] + [_const_spec(a.shape) for a in consts],
        out_specs=tile,
        out_shape=jax.ShapeDtypeStruct((B, L, D_MODEL), F32),
        scratch_shapes=[pltpu.VMEM((tm, D_FF), BF16)],
        compiler_params=pltpu.CompilerParams(dimension_semantics=("arbitrary",),
                                             vmem_limit_bytes=VMEM_LIMIT_BYTES),
        name="mlp_prompt",
    )(x2, *consts)


def _mix_sample_kernel(x_ref, s0_ref, lb_ref, gmix_ref, win_ref, hgn_ref, lng_ref, lnb_ref, wmix_ref,
                       bcol_ref, wout_ref, gc_ref, wcq_ref, h_ref, sout_ref, vn_ref, hq_ref,
                       z_ref, ocat_ref, q_s, k_s, b_s, wm_s, bc_s, *, tm):
    x = x_ref[...].reshape(tm, D_MODEL)
    z_ref[...] = jnp.dot(_rmsnorm(x, gmix_ref[...]).astype(BF16), win_ref[...],
                         preferred_element_type=F32)
    lb_all = _lower_bound(lb_ref)
    mask = _chunk_masks(True)
    row = lax.broadcasted_iota(jnp.int32, (CHUNK, CHUNK), 0)
    col = lax.broadcasted_iota(jnp.int32, (CHUNK, CHUNK), 1)
    eye = row == col

    @pl.when(pl.program_id(0) == 0)
    def _():
        pick = (row % 4 == col).astype(BF16)
        for g in range(CM_GROUPS):
            corner = jnp.where((row < 4) & (col < 4), wmix_ref[g], 0.0)
            wm_s[g] = _bdot_nt(_bdot(pick, corner), pick)
        pos = lax.broadcasted_iota(jnp.int32, (CHUNK, CM_GROUPS), 0) % 4
        bc = jnp.zeros((CHUNK, CM_GROUPS), F32)
        for t in range(4):
            bc = jnp.where(pos == t, bcol_ref[t:t + 1, :], bc)
        bc_s[...] = bc
    r16 = lax.broadcasted_iota(jnp.int32, (SUB, HG_DK), 0)

    def gate_body(c, dmax):
        rows = pl.ds(pl.multiple_of(c * CHUNK, CHUNK), CHUNK)
        q, k, bcum = _gates(z_ref[rows, 0:HG_WIDTH], z_ref[rows, HG_WIDTH:2 * HG_WIDTH], lb_all, 4)
        q_s[rows, :] = q
        k_s[rows, :] = k
        b_s[rows, :] = bcum
        return jnp.maximum(dmax, _block_decay(bcum, True))

    dmax = lax.fori_loop(0, tm // CHUNK, gate_body, jnp.zeros((1, HG_WIDTH), F32))
    fast_ok = jnp.max(dmax) < FAST_MAX_DECAY

    def chunk_body(c, carry, *, fast):
        rows = pl.ds(pl.multiple_of(c * CHUNK, CHUNK), CHUNK)
        for h in range(HG_HEADS):
            cs = slice(h * HG_DK, (h + 1) * HG_DK)
            q, k, bcum = q_s[rows, cs], k_s[rows, cs], b_s[rows, cs]
            v = z_ref[rows, 2 * HG_WIDTH + h * HG_DK:2 * HG_WIDTH + (h + 1) * HG_DK]
            zg = z_ref[rows, 3 * HG_WIDTH + h * HG_DK:3 * HG_WIDTH + (h + 1) * HG_DK]
            a = _scores_fast(q, k, bcum, True) if fast else _scores_exact(q, k, bcum, True, mask)
            o = _bdot(jnp.where(mask, a, 0.0), v)
            eb = jnp.exp(bcum)
            qd = q * eb
            o_parts = []
            for gb in range(CHUNK // SUB):
                blk = slice(gb * SUB, (gb + 1) * SUB)
                qd_b, v_b, eb_b, k_b, b_b = qd[blk], v[blk], eb[blk], k[blk], bcum[blk]
                inter = jnp.zeros((SUB, HG_DK), F32)
                for j in range(4):
                    req = c * (CHUNK // 4) + gb * 4 + j
                    s0 = s0_ref[req, h]
                    last = 4 * j + 3
                    inter = jnp.where(r16 // 4 == j, _bdot(qd_b, s0), inter)
                    kd = jnp.where(r16 // 4 == j, k_b * jnp.exp(b_b[last:last + 1] - b_b), 0.0)
                    upd = lax.dot_general(kd.astype(BF16), v_b.astype(BF16),
                                          (((0,), (0,)), ((), ())), preferred_element_type=F32)
                    dcol = jnp.sum(jnp.where(eye, eb_b[last:last + 1], 0.0), axis=-1, keepdims=True)
                    sout_ref[req, h] = dcol * s0 + upd
                o_parts.append(inter)
            o = o + jnp.concatenate(o_parts, axis=0)
            ocat_ref[rows, cs] = _hgrn_out(o, zg, hgn_ref[:, cs])
        vn = _gmlp_chunk(z_ref, 4 * HG_WIDTH, rows, lng_ref, lnb_ref, _masked_mix_weights(wm_s, mask),
                         bc_s, ocat_ref)
        vn_ref[pl.ds(pl.multiple_of(c * (CHUNK // 4), CHUNK // 4), CHUNK // 4)] = vn.reshape(
            CHUNK // 4, 4, CM_WIDTH)
        return carry

    @pl.when(fast_ok)
    def _():
        lax.fori_loop(0, tm // CHUNK, functools.partial(chunk_body, fast=True), 0)

    @pl.when(jnp.logical_not(fast_ok))
    def _():
        lax.fori_loop(0, tm // CHUNK, functools.partial(chunk_body, fast=False), 0)

    h = x + jnp.dot(ocat_ref[...], wout_ref[...], preferred_element_type=F32)
    h_ref[0] = h
    hq_ref[...] = jnp.dot(_rmsnorm(h, gc_ref[...]).astype(BF16), wcq_ref[...], preferred_element_type=F32)


def _mix_sample(x, s0, lb_param, g_mix, w_in, hg_norm_g, ln_v_g, ln_v_b, wmix, bcol, w_out, g_cross,
                w_cq, tm):
    nreq = tm // 4
    nt = x.shape[0] // nreq
    consts = (lb_param, g_mix, w_in, hg_norm_g, ln_v_g, ln_v_b, wmix, bcol, w_out, g_cross, w_cq)
    in_specs = [
        pl.BlockSpec((nreq, 4, D_MODEL), lambda i: (i, 0, 0)),
        pl.BlockSpec((nreq, HG_HEADS, HG_DK, HG_DK), lambda i: (i, 0, 0, 0)),
    ] + [_const_spec(a.shape) for a in consts]
    out_specs = [
        pl.BlockSpec((1, tm, D_MODEL), lambda i: (i, 0, 0)),
        pl.BlockSpec((nreq, HG_HEADS, HG_DK, HG_DK), lambda i: (i, 0, 0, 0)),
        pl.BlockSpec((nreq, 4, CM_WIDTH), lambda i: (i, 0, 0)),
        pl.BlockSpec((tm, D_MODEL), lambda i: (i, 0)),
    ]
    return pl.pallas_call(
        functools.partial(_mix_sample_kernel, tm=tm),
        grid=(nt,),
        in_specs=in_specs,
        out_specs=out_specs,
        out_shape=[jax.ShapeDtypeStruct((nt, tm, D_MODEL), F32),
                   jax.ShapeDtypeStruct(s0.shape, F32),
                   jax.ShapeDtypeStruct((nt * nreq, 4, CM_WIDTH), F32),
                   jax.ShapeDtypeStruct((nt * tm, D_MODEL), F32)],
        scratch_shapes=[pltpu.VMEM((tm, IN_WIDTH), F32), pltpu.VMEM((tm, D_MODEL), BF16)]
        + [pltpu.VMEM((tm, HG_WIDTH), F32)] * 3
        + [pltpu.VMEM((CM_GROUPS, CHUNK, CHUNK), F32), pltpu.VMEM((CHUNK, CM_GROUPS), F32)],
        compiler_params=pltpu.CompilerParams(dimension_semantics=("arbitrary",),
                                             vmem_limit_bytes=VMEM_LIMIT_BYTES),
        name="mix_sample",
    )(x, s0, *consts)


def _memkv_kernel(m_ref, g_ref, wk_ref, wv_ref, k_ref, v_ref, kb_ref, vb_ref):
    mn = _rmsnorm(m_ref[...], g_ref[...]).astype(BF16)
    k = jnp.dot(mn, wk_ref[...], preferred_element_type=F32)
    v = jnp.dot(mn, wv_ref[...], preferred_element_type=F32)
    kb_ref[...] = k.astype(BF16)
    vb_ref[...] = v.astype(BF16)
    for src, dst in ((k, k_ref), (v, v_ref)):
        for r in range(dst.shape[0]):
            rows = src[r * N_MEM:(r + 1) * N_MEM]
            dst[r] = jnp.stack([rows[:, h * X_HEAD_DIM:(h + 1) * X_HEAD_DIM] for h in range(X_HEADS)],
                               axis=1)


def _memkv(mem, g_mem, w_ck, w_cv, tm):
    n = mem.shape[0]
    return pl.pallas_call(
        _memkv_kernel,
        grid=(n // tm,),
        in_specs=[pl.BlockSpec((tm, D_MODEL), lambda i: (i, 0)), _const_spec(g_mem.shape),
                  _const_spec(w_ck.shape), _const_spec(w_cv.shape)],
        out_specs=[pl.BlockSpec((tm // N_MEM, N_MEM, X_HEADS, X_HEAD_DIM), lambda i: (i, 0, 0, 0))] * 2
        + [pl.BlockSpec((tm, D_MODEL), lambda i: (i, 0))] * 2,
        out_shape=[jax.ShapeDtypeStruct((n // N_MEM, N_MEM, X_HEADS, X_HEAD_DIM), F32)] * 2
        + [jax.ShapeDtypeStruct((n, D_MODEL), BF16)] * 2,
        compiler_params=pltpu.CompilerParams(dimension_semantics=("arbitrary",),
                                             vmem_limit_bytes=VMEM_LIMIT_BYTES),
        name="memkv",
    )(mem, g_mem, w_ck, w_cv)


def _post_sample_kernel(x_ref, ca_ref, wco_ref, gm_ref, wup_ref, wdn_ref, gf_ref, y_ref, hm_s):
    x = x_ref[0] + jnp.dot(ca_ref[...].astype(BF16), wco_ref[...], preferred_element_type=F32)
    x = x + _mlp(_rmsnorm(x, gm_ref[...]).astype(BF16), wup_ref, wdn_ref, hm_s)
    y_ref[...] = _rmsnorm(x, gf_ref[...]).reshape(y_ref.shape)


def _post_sample(x, ca, w_co, g_mlp, w_up, w_down, g_final):
    nt, tm, _ = x.shape
    in_specs = [
        pl.BlockSpec((1, tm, D_MODEL), lambda i: (i, 0, 0)),
        pl.BlockSpec((tm, D_MODEL), lambda i: (i, 0)),
        _const_spec(w_co.shape), _const_spec(g_mlp.shape), _const_spec(w_up.shape),
        _const_spec(w_down.shape), _const_spec(g_final.shape),
    ]
    return pl.pallas_call(
        _post_sample_kernel,
        grid=(nt,),
        in_specs=in_specs,
        out_specs=pl.BlockSpec((tm // 4, 4, D_MODEL), lambda i: (i, 0, 0)),
        out_shape=jax.ShapeDtypeStruct((nt * tm // 4, 4, D_MODEL), F32),
        scratch_shapes=[pltpu.VMEM((tm, D_FF), BF16)],
        compiler_params=pltpu.CompilerParams(dimension_semantics=("arbitrary",),
                                             vmem_limit_bytes=VMEM_LIMIT_BYTES),
        name="post_sample",
    )(x, ca, w_co, g_mlp, w_up, w_down, g_final)


def _cast_kernel(*refs):
    n = len(refs) // 2
    for src, dst in zip(refs[:n], refs[n:]):
        dst[...] = src[...].astype(BF16)


def _cast_weights(*ws):
    steps = 8
    specs = [pl.BlockSpec((w.shape[0] // steps, w.shape[1]), lambda i: (i, 0)) for w in ws]
    return pl.pallas_call(
        _cast_kernel,
        grid=(steps,),
        in_specs=specs,
        out_specs=specs,
        out_shape=[jax.ShapeDtypeStruct(w.shape, BF16) for w in ws],
        compiler_params=pltpu.CompilerParams(dimension_semantics=("arbitrary",),
                                             vmem_limit_bytes=VMEM_LIMIT_BYTES),
        name="cast_weights",
    )(*ws)


def kernel(x_prompt, x_sample, mem_prompt, state_hgrn, cache_mem_k, cache_mem_v, lb_param, g_mix,
           w_in, hg_norm_g, ln_v_g, ln_v_b, w_s, b_s, w_out, g_cross, g_mem, w_cq, w_ck, w_cv, w_co,
           g_mlp, w_up, w_down, g_final):
    B, L, _ = x_prompt.shape
    DB, DL, _ = x_sample.shape
    assert DL == 4 and g_mix.shape[0] == 1

    row = lambda a: a.reshape(1, -1)
    win_b, wout_b, wcq_b, wco_b, wck_b, wcv_b, wup_b, wdn_b = _cast_weights(
        w_in[0], w_out[0], w_cq[0], w_co[0], w_ck[0], w_cv[0], w_up[0], w_down[0])
    gmix, hgn, lng, lnb = row(g_mix[0]), row(hg_norm_g[0]), row(ln_v_g[0]), row(ln_v_b[0])
    gcr, gmem, gmlp, gfin = row(g_cross[0]), row(g_mem[0]), row(g_mlp[0]), row(g_final)
    wmix = w_s[0]
    bcol = b_s[0].T

    tm_s = 128
    h_s, s_s, vn_s, hq_s = _mix_sample(x_sample, state_hgrn[0], lb_param, gmix, win_b, hgn, lng, lnb, wmix,
                                       bcol, wout_b, gcr, wcq_b, tm=tm_s)

    mk, mv, mk_b, mv_b = _memkv(mem_prompt.reshape(B * N_MEM, D_MODEL), gmem, wck_b, wcv_b, tm=512)
    x2_p, s_p, ca_s = _mixattn_layer(x_prompt, mk_b.reshape(B, N_MEM, D_MODEL),
                                     mv_b.reshape(B, N_MEM, D_MODEL), hq_s, cache_mem_k[0], cache_mem_v[0],
                                     lb_param, gmix, win_b, hgn, lng, lnb, wmix, bcol, wout_b, gcr, wcq_b,
                                     wco_b, tm=512)
    y_p = _mlp_layer(x2_p, gmlp, wup_b, wdn_b, gfin, tm=1024)

    y_s = _post_sample(h_s.reshape(1, DB * DL, D_MODEL), ca_s, wco_b, gmlp, wup_b, wdn_b, gfin)

    return (y_p, y_s, s_p[None], s_s[None],
            mk[None], mv[None],
            vn_s[None])
```

```python
import functools

import jax
import jax.numpy as jnp
from jax import lax
from jax.experimental import pallas as pl
from jax.experimental.pallas import tpu as pltpu

F32 = jnp.float32
BF16 = jnp.bfloat16

D_MODEL = 1024
HG_WIDTH = 512
HG_HEADS = 4
HG_DK = 128
CM_WIDTH = 512
CM_GROUPS = 4
CM_GROUP_DIM = 128
IN_WIDTH = 4 * HG_WIDTH + 2 * CM_WIDTH
N_MEM = 256
X_HEADS = 4
X_HEAD_DIM = 256
D_FF = 4096
EPS = 1e-6

CHUNK = 128
SUBLANES = 8
SUB = 2 * SUBLANES
VMEM_LIMIT_BYTES = 62 * 1024 * 1024
LOG2E = 1.4426950408889634
FAST_BLOCK = 32
FAST_MAX_DECAY = 80.0


def _bdot(a, b):
    return jnp.dot(a.astype(BF16), b.astype(BF16), preferred_element_type=F32)


def _bdot_nt(a, b):
    return lax.dot_general(a.astype(BF16), b.astype(BF16), (((1,), (1,)), ((), ())),
                           preferred_element_type=F32)


def _rmsnorm(x, g):
    ms = jnp.mean(x * x, axis=-1, keepdims=True)
    return x * lax.rsqrt(ms + EPS) * g


def _sigmoid(x):
    return 0.5 * jnp.tanh(0.5 * x) + 0.5


def _gelu(x):
    return 0.5 * x * (1.0 + jnp.tanh(0.7978845608028654 * (x + 0.044715 * (x * x * x))))


def _softmax_rows(s):
    m = jnp.max(s, axis=-1, keepdims=True)
    e = jnp.exp(s - m)
    return e / jnp.sum(e, axis=-1, keepdims=True)


def _seg_cumsum(x, seg):
    n = x.shape[0]
    pos = lax.broadcasted_iota(jnp.int32, x.shape, 0) % seg
    s = 1
    while s < min(seg, SUBLANES):
        x = x + jnp.where(pos >= s, pltpu.roll(x, s, 0), 0.0)
        s *= 2
    while s < seg:
        parts = []
        for r0 in range(0, n, seg):
            parts.append(x[r0:r0 + s])
            parts.append(x[r0 + s:r0 + seg] + x[r0:r0 + seg - s])
        x = jnp.concatenate(parts, axis=0)
        s *= 2
    return x


def _chunk_cumsum(x):
    n = x.shape[0]
    tril = (lax.broadcasted_iota(jnp.int32, (n, n), 0)
            >= lax.broadcasted_iota(jnp.int32, (n, n), 1)).astype(BF16)
    hi = x.astype(BF16)
    r1 = x - hi.astype(F32)
    mid = r1.astype(BF16)
    lo = (r1 - mid.astype(F32)).astype(BF16)
    return (jnp.dot(tril, hi, preferred_element_type=F32) + jnp.dot(tril, mid, preferred_element_type=F32)
            + jnp.dot(tril, lo, preferred_element_type=F32))


def _lower_bound(lb_ref):
    lbp = lb_ref[...]
    lbe = jnp.exp(lbp - jnp.max(lbp, axis=0, keepdims=True))
    return lbe[0:1] / jnp.sum(lbe, axis=0, keepdims=True)


def _gates(zq, zf, lb, seg):
    half = 0.5 - 0.5 * lb
    hth = half * jnp.tanh(0.5 * zf)
    logf = jnp.log((lb + half) + hth)
    bcum = _chunk_cumsum(logf) if seg == logf.shape[0] else _seg_cumsum(logf, seg)
    hq = 0.5 * zq
    return hq * jnp.tanh(hq) + hq, half - hth, bcum


def _block_decay(bcum, sample):
    if sample:
        return jnp.max(-bcum, axis=0, keepdims=True)
    d = -bcum[FAST_BLOCK - 1:FAST_BLOCK]
    for i in range(1, CHUNK // FAST_BLOCK):
        n0, n1 = i * FAST_BLOCK, (i + 1) * FAST_BLOCK
        d = jnp.maximum(d, bcum[n0 - 1:n0] - bcum[n1 - 1:n1])
    return d


def _chunk_masks(sample):
    row = lax.broadcasted_iota(jnp.int32, (CHUNK, CHUNK), 0)
    col = lax.broadcasted_iota(jnp.int32, (CHUNK, CHUNK), 1)
    if sample:
        return (row // 4 == col // 4) & (row >= col)
    return row >= col


def _scores_fast(q, k, bcum, sample):
    if sample:
        return _bdot_nt(q * jnp.exp(bcum), k * jnp.exp(-bcum))
    a_rows = []
    zero = jnp.zeros((1, HG_DK), F32)
    kt, prev_ref = None, zero
    for i in range(CHUNK // FAST_BLOCK):
        n0, n1 = i * FAST_BLOCK, (i + 1) * FAST_BLOCK
        ref_b = bcum[n0 - 1:n0] if i else zero
        qt = q[n0:n1] * jnp.exp(bcum[n0:n1] - ref_b)
        kt_blk = k[n0:n1] * jnp.exp(ref_b - bcum[n0:n1])
        kt = kt_blk if kt is None else jnp.concatenate([kt * jnp.exp(ref_b - prev_ref), kt_blk], axis=0)
        prev_ref = ref_b
        kt_b = kt.astype(BF16)
        if n1 < CHUNK:
            kt_b = jnp.concatenate([kt_b, jnp.zeros((CHUNK - n1, HG_DK), BF16)], axis=0)
        a_rows.append(_bdot_nt(qt, kt_b))
    return jnp.concatenate(a_rows, axis=0)


def _scores_exact(q, k, bcum, sample, mask):
    col8 = lax.broadcasted_iota(jnp.int32, (SUBLANES, CHUNK), 1)
    b2 = bcum * LOG2E
    cexp = b2 - jnp.log2(k)
    a_rows = []
    for gb in range(CHUNK // SUB):
        lo = slice(gb * SUB, gb * SUB + SUBLANES)
        hi = slice(gb * SUB + SUBLANES, (gb + 1) * SUB)
        a_lo = jnp.zeros((SUBLANES, CHUNK), F32)
        a_hi = jnp.zeros((SUBLANES, CHUNK), F32)
        for s in range(SUB):
            sg = gb * SUB + s
            c_s = cexp[sg:sg + 1]
            if s < SUBLANES:
                p = q[lo] * jnp.exp2(b2[lo] - c_s)
                a_lo = jnp.where(col8 == sg, jnp.sum(p, axis=-1, keepdims=True), a_lo)
            if (not sample) or s >= SUBLANES:
                p = q[hi] * jnp.exp2(b2[hi] - c_s)
                a_hi = jnp.where(col8 == sg, jnp.sum(p, axis=-1, keepdims=True), a_hi)
        a_rows.append(a_lo)
        a_rows.append(a_hi)
    a = jnp.concatenate(a_rows, axis=0)
    if sample:
        return a
    o_rows = [jnp.zeros((SUB, CHUNK), F32)]
    for i in range(1, CHUNK // SUB):
        n = i * SUB
        ref_b = bcum[n - 1:n]
        qt = q[n:n + SUB] * jnp.exp(bcum[n:n + SUB] - ref_b)
        kt = k[:n] * jnp.exp(ref_b - bcum[:n])
        kt = jnp.concatenate([kt, jnp.zeros((CHUNK - n, HG_DK), F32)], axis=0)
        o_rows.append(_bdot_nt(qt, kt))
    return jnp.where(mask, a, 0.0) + jnp.concatenate(o_rows, axis=0)


def _hgrn_out(o, zg, g):
    o = o * lax.rsqrt(jnp.mean(o * o, axis=-1, keepdims=True) + EPS) * g
    return (o * _sigmoid(zg)).astype(BF16)


def _masked_mix_weights(wmix_ref, mask):
    return [jnp.where(mask, wmix_ref[g], 0.0).astype(BF16) for g in range(CM_GROUPS)]


def _gmlp_chunk(z_ref, c0, rows, lng_ref, lnb_ref, wms, bcol_ref, ocat_ref):
    u = _gelu(z_ref[rows, c0:c0 + CM_WIDTH])
    gv = _gelu(z_ref[rows, c0 + CM_WIDTH:c0 + 2 * CM_WIDTH])
    mu = jnp.mean(gv, axis=-1, keepdims=True)
    dv = gv - mu
    var = jnp.mean(dv * dv, axis=-1, keepdims=True)
    vn = dv * lax.rsqrt(var + EPS) * lng_ref[...] + lnb_ref[...]
    for g in range(CM_GROUPS):
        gs = slice(g * CM_GROUP_DIM, (g + 1) * CM_GROUP_DIM)
        mixed = _bdot(wms[g], vn[:, gs]) + bcol_ref[:, g:g + 1]
        ocat_ref[rows, HG_WIDTH + g * CM_GROUP_DIM:HG_WIDTH + (g + 1) * CM_GROUP_DIM] = (
            u[:, gs] * mixed).astype(BF16)
    return vn


def _sample_pair_probs(hq8, ck_ref):
    nrow = 8 * X_HEADS
    rh = lax.broadcasted_iota(jnp.int32, (nrow, N_MEM * X_HEADS), 0) // 8
    ch = lax.broadcasted_iota(jnp.int32, (nrow, N_MEM * X_HEADS), 1) % X_HEADS
    q = jnp.concatenate([hq8[:, h * X_HEAD_DIM:(h + 1) * X_HEAD_DIM] for h in range(X_HEADS)],
                        axis=0).astype(BF16)
    probs = []
    for r in range(2):
        k2 = ck_ref[r].reshape(N_MEM * X_HEADS, X_HEAD_DIM)
        s = jnp.where(rh == ch, _bdot_nt(q, k2) * (X_HEAD_DIM ** -0.5), -jnp.inf)
        probs.append(_softmax_rows(s))
    return probs


def _sample_pair_context(probs, cv_ref, ca_ref):
    first = (lax.broadcasted_iota(jnp.int32, (8 * X_HEADS, X_HEAD_DIM), 0) % 8) < 4
    outs = [_bdot(probs[r], cv_ref[r].reshape(N_MEM * X_HEADS, X_HEAD_DIM)) for r in range(2)]
    o = jnp.where(first, outs[0], outs[1])
    for h in range(X_HEADS):
        ca_ref[:, h * X_HEAD_DIM:(h + 1) * X_HEAD_DIM] = o[8 * h:8 * h + 8]


def _mlp(hn_bf16, wup_ref, wdn_ref, hm_s):
    for c in range(D_FF // D_MODEL):
        fs = slice(c * D_MODEL, (c + 1) * D_MODEL)
        hm = jnp.maximum(jnp.dot(hn_bf16, wup_ref[:, fs], preferred_element_type=F32), 0.0)
        hm_s[:, fs] = (hm * hm).astype(BF16)
    return jnp.dot(hm_s[...], wdn_ref[...], preferred_element_type=F32)


def _const_spec(shape):
    nd = len(shape)
    return pl.BlockSpec(shape, lambda *_: (0,) * nd, pipeline_mode=pl.Buffered(1))


def _mixattn_kernel(x_ref, mk_ref, mv_ref, hqs_ref, ck_ref, cv_ref, lb_ref, gmix_ref, win_ref, hgn_ref,
                    lng_ref, lnb_ref, wmix_ref, bcol_ref, wout_ref, gc_ref, wcq_ref, wco_ref,
                    x2_ref, sout_ref, cas_ref,
                    zg_ref, z_ref, ocat_ref, q_s, k_s, b_s, a_s, st0, st1, st2, st3, h1_s, h1n_s, ca_s,
                    *, tm, nt, npairs):
    t = pl.program_id(0)
    i = lax.rem(t, nt)
    nch = tm // CHUNK
    nj = nch * HG_HEADS
    mask = _chunk_masks(False)
    st_refs = (st0, st1, st2, st3)

    @pl.when(t == 0)
    def _():
        h1_s[...] = jnp.zeros_like(h1_s)
        h1n_s[...] = jnp.zeros_like(h1n_s)

    @pl.when(i == 0)
    def _():
        for st in st_refs:
            st[...] = jnp.zeros_like(st)

    x = x_ref[0]
    xn = _rmsnorm(x, gmix_ref[...]).astype(BF16)
    ngate = 2 * HG_WIDTH
    zg_ref[...] = jnp.dot(xn, win_ref[:, 0:ngate], preferred_element_type=F32)
    z_ref[...] = jnp.dot(xn, win_ref[:, ngate:IN_WIDTH], preferred_element_type=F32)
    lb_all = _lower_bound(lb_ref)
    dmax = jnp.zeros((1, HG_WIDTH), F32)
    for c in range(nch):
        rows = slice(c * CHUNK, (c + 1) * CHUNK)
        q, k, bcum = _gates(zg_ref[rows, 0:HG_WIDTH], zg_ref[rows, HG_WIDTH:2 * HG_WIDTH], lb_all, CHUNK)
        q_s[rows, :] = q
        k_s[rows, :] = k
        b_s[rows, :] = bcum
        dmax = jnp.maximum(dmax, _block_decay(bcum, False))
    fast_ok = jnp.max(dmax) < FAST_MAX_DECAY

    def fill_scores(j, fast):
        c, h = divmod(j, HG_HEADS)
        rows = slice(c * CHUNK, (c + 1) * CHUNK)
        cs = slice(h * HG_DK, (h + 1) * HG_DK)
        q, k, bcum = q_s[rows, cs], k_s[rows, cs], b_s[rows, cs]
        a = _scores_fast(q, k, bcum, False) if fast else _scores_exact(q, k, bcum, False, mask)
        a_s[j] = jnp.where(mask, a, 0.0).astype(BF16)

    def head_chunk(j, wms):
        c, h = divmod(j, HG_HEADS)
        rows = slice(c * CHUNK, (c + 1) * CHUNK)
        cs = slice(h * HG_DK, (h + 1) * HG_DK)
        q, k, bcum = q_s[rows, cs], k_s[rows, cs], b_s[rows, cs]
        v = z_ref[rows, h * HG_DK:(h + 1) * HG_DK]
        zg = z_ref[rows, HG_WIDTH + h * HG_DK:HG_WIDTH + (h + 1) * HG_DK]
        eb = jnp.exp(bcum)
        kd = k * jnp.exp(bcum[CHUNK - 1:CHUNK] - bcum)
        st = st_refs[h][...]
        o = jnp.dot(a_s[j], v.astype(BF16), preferred_element_type=F32)
        o = o + _bdot_nt(q * eb, st)
        st_refs[h][...] = st * eb[CHUNK - 1:CHUNK] + _bdot(v.T, kd)
        ocat_ref[rows, cs] = _hgrn_out(o, zg, hgn_ref[:, cs])
        if h == HG_HEADS - 1:
            _gmlp_chunk(z_ref, 2 * HG_WIDTH, rows, lng_ref, lnb_ref, wms, bcol_ref, ocat_ref)

    def region2(fast):
        wms = _masked_mix_weights(wmix_ref, mask)
        scale = X_HEAD_DIM ** -0.5
        heads = [slice(h * X_HEAD_DIM, (h + 1) * X_HEAD_DIM) for h in range(X_HEADS)]
        nq = nj // X_HEADS
        h1n = h1n_s[...]
        hq = []
        for h, hs in enumerate(heads):
            hq.append(jnp.dot(h1n, wcq_ref[:, hs], preferred_element_type=F32))
            for j in range(h * nq, (h + 1) * nq):
                fill_scores(j, fast)
        probs = [_softmax_rows(_bdot_nt(hq[h], mk_ref[0, :, hs]) * scale) for h, hs in enumerate(heads)]
        sprobs = [_sample_pair_probs(hqs_ref[8 * p:8 * p + 8, :], ck_ref.at[2 * p:2 * p + 2])
                  for p in range(npairs)]
        for j in range(0, nq):
            head_chunk(j, wms)
        for h, hs in enumerate(heads):
            ca_s[:, hs] = _bdot(probs[h], mv_ref[0, :, hs]).astype(BF16)
        ca = ca_s[...]
        for h, hs in enumerate(heads):
            x2_ref[0, :, hs] = h1_s[:, hs] + jnp.dot(ca, wco_ref[:, hs], preferred_element_type=F32)
            for j in range((h + 1) * nq, min(nj, (h + 2) * nq)):
                head_chunk(j, wms)
        for p in range(npairs):
            _sample_pair_context(sprobs[p], cv_ref.at[2 * p:2 * p + 2], cas_ref.at[8 * p:8 * p + 8, :])
        h1 = x_ref[0] + jnp.dot(ocat_ref[...], wout_ref[...], preferred_element_type=F32)
        h1_s[...] = h1
        h1n_s[...] = _rmsnorm(h1, gc_ref[...]).astype(BF16)

    @pl.when(fast_ok)
    def _():
        region2(True)

    @pl.when(jnp.logical_not(fast_ok))
    def _():
        region2(False)

    @pl.when(i == nt - 1)
    def _():
        for h in range(HG_HEADS):
            sout_ref[0, h] = st_refs[h][...].T


def _mixattn_layer(x, mk, mv, hqs, ck, cv, lb_param, g_mix, w_in, hg_norm_g, ln_v_g, ln_v_b, wmix, bcol,
                   w_out, g_cross, w_cq, w_co, tm):
    B, L, _ = x.shape
    nt = L // tm
    T = B * nt
    npairs = ck.shape[0] // (2 * T)
    assert npairs * 2 * T == ck.shape[0]

    def cur(t):
        return jnp.minimum(t, T - 1)

    def prev(t):
        return jnp.maximum(t - 1, 0)

    consts = (lb_param, g_mix, w_in, hg_norm_g, ln_v_g, ln_v_b, wmix, bcol, w_out, g_cross, w_cq, w_co)
    kv_spec = pl.BlockSpec((2 * npairs, N_MEM, X_HEADS, X_HEAD_DIM), lambda t: (cur(t), 0, 0, 0))
    in_specs = [
        pl.BlockSpec((1, tm, D_MODEL), lambda t: (cur(t) // nt, cur(t) % nt, 0)),
        pl.BlockSpec((1, N_MEM, D_MODEL), lambda t: (prev(t) // nt, 0, 0)),
        pl.BlockSpec((1, N_MEM, D_MODEL), lambda t: (prev(t) // nt, 0, 0)),
        pl.BlockSpec((8 * npairs, D_MODEL), lambda t: (cur(t), 0)), kv_spec, kv_spec,
    ] + [_const_spec(a.shape) for a in consts]
    out_specs = [
        pl.BlockSpec((1, tm, D_MODEL), lambda t: (prev(t) // nt, prev(t) % nt, 0)),
        pl.BlockSpec((1, HG_HEADS, HG_DK, HG_DK), lambda t: (cur(t) // nt, 0, 0, 0)),
        pl.BlockSpec((8 * npairs, D_MODEL), lambda t: (cur(t), 0)),
    ]
    scratch = [
        pltpu.VMEM((tm, 2 * HG_WIDTH), F32),
        pltpu.VMEM((tm, IN_WIDTH - 2 * HG_WIDTH), F32),
        pltpu.VMEM((tm, D_MODEL), BF16),
        pltpu.VMEM((tm, HG_WIDTH), F32), pltpu.VMEM((tm, HG_WIDTH), F32),
        pltpu.VMEM((tm, HG_WIDTH), F32),
        pltpu.VMEM((tm // CHUNK * HG_HEADS, CHUNK, CHUNK), BF16),
        pltpu.VMEM((HG_DK, HG_DK), F32), pltpu.VMEM((HG_DK, HG_DK), F32),
        pltpu.VMEM((HG_DK, HG_DK), F32), pltpu.VMEM((HG_DK, HG_DK), F32),
        pltpu.VMEM((tm, D_MODEL), F32),
        pltpu.VMEM((tm, D_MODEL), BF16),
        pltpu.VMEM((tm, D_MODEL), BF16),
    ]
    return pl.pallas_call(
        functools.partial(_mixattn_kernel, tm=tm, nt=nt, npairs=npairs),
        grid=(T + 1,),
        in_specs=in_specs,
        out_specs=out_specs,
        out_shape=[jax.ShapeDtypeStruct((B, L, D_MODEL), F32),
                   jax.ShapeDtypeStruct((B, HG_HEADS, HG_DK, HG_DK), F32),
                   jax.ShapeDtypeStruct(hqs.shape, F32)],
        scratch_shapes=scratch,
        compiler_params=pltpu.CompilerParams(dimension_semantics=("arbitrary",),
                                             vmem_limit_bytes=VMEM_LIMIT_BYTES),
        name="mix_attn",
    )(x, mk, mv, hqs, ck, cv, *consts)


def _mlp_kernel(x2_ref, gm_ref, wup_ref, wdn_ref, gf_ref, y_ref, hm_s):
    x2 = x2_ref[0]
    y_ref[0] = _rmsnorm(x2 + _mlp(_rmsnorm(x2, gm_ref[...]).astype(BF16), wup_ref, wdn_ref, hm_s),
                        gf_ref[...])


def _mlp_layer(x2, g_mlp, w_up, w_down, g_final, tm):
    B, L, _ = x2.shape
    nt = L // tm
    consts = (g_mlp, w_up, w_down, g_final)
    tile = pl.BlockSpec((1, tm, D_MODEL), lambda t: (t // nt, t % nt, 0))
    return pl.pallas_call(
        _mlp_kernel,
        grid=(B * nt,),
        in_specs=[tile] + [_const_spec(a.shape) for a in consts],
        out_specs=tile,
        out_shape=jax.ShapeDtypeStruct((B, L, D_MODEL), F32),
        scratch_shapes=[pltpu.VMEM((tm, D_FF), BF16)],
        compiler_params=pltpu.CompilerParams(dimension_semantics=("arbitrary",),
                                             vmem_limit_bytes=VMEM_LIMIT_BYTES),
        name="mlp_prompt",
    )(x2, *consts)


def _mix_sample_kernel(x_ref, s0_ref, lb_ref, gmix_ref, win_ref, hgn_ref, lng_ref, lnb_ref, wmix_ref,
                       bcol_ref, wout_ref, gc_ref, wcq_ref, h_ref, sout_ref, vn_ref, hq_ref,
                       z_ref, ocat_ref, q_s, k_s, b_s, wm_s, bc_s, *, tm):
    x = x_ref[...].reshape(tm, D_MODEL)
    z_ref[...] = jnp.dot(_rmsnorm(x, gmix_ref[...]).astype(BF16), win_ref[...],
                         preferred_element_type=F32)
    lb_all = _lower_bound(lb_ref)
    mask = _chunk_masks(True)
    row = lax.broadcasted_iota(jnp.int32, (CHUNK, CHUNK), 0)
    col = lax.broadcasted_iota(jnp.int32, (CHUNK, CHUNK), 1)
    eye = row == col

    @pl.when(pl.program_id(0) == 0)
    def _():
        pick = (row % 4 == col).astype(BF16)
        for g in range(CM_GROUPS):
            corner = jnp.where((row < 4) & (col < 4), wmix_ref[g], 0.0)
            wm_s[g] = _bdot_nt(_bdot(pick, corner), pick)
        pos = lax.broadcasted_iota(jnp.int32, (CHUNK, CM_GROUPS), 0) % 4
        bc = jnp.zeros((CHUNK, CM_GROUPS), F32)
        for t in range(4):
            bc = jnp.where(pos == t, bcol_ref[t:t + 1, :], bc)
        bc_s[...] = bc
    r16 = lax.broadcasted_iota(jnp.int32, (SUB, HG_DK), 0)

    def gate_body(c, dmax):
        rows = pl.ds(pl.multiple_of(c * CHUNK, CHUNK), CHUNK)
        q, k, bcum = _gates(z_ref[rows, 0:HG_WIDTH], z_ref[rows, HG_WIDTH:2 * HG_WIDTH], lb_all, 4)
        q_s[rows, :] = q
        k_s[rows, :] = k
        b_s[rows, :] = bcum
        return jnp.maximum(dmax, _block_decay(bcum, True))

    dmax = lax.fori_loop(0, tm // CHUNK, gate_body, jnp.zeros((1, HG_WIDTH), F32))
    fast_ok = jnp.max(dmax) < FAST_MAX_DECAY

    def chunk_body(c, carry, *, fast):
        rows = pl.ds(pl.multiple_of(c * CHUNK, CHUNK), CHUNK)
        for h in range(HG_HEADS):
            cs = slice(h * HG_DK, (h + 1) * HG_DK)
            q, k, bcum = q_s[rows, cs], k_s[rows, cs], b_s[rows, cs]
            v = z_ref[rows, 2 * HG_WIDTH + h * HG_DK:2 * HG_WIDTH + (h + 1) * HG_DK]
            zg = z_ref[rows, 3 * HG_WIDTH + h * HG_DK:3 * HG_WIDTH + (h + 1) * HG_DK]
            a = _scores_fast(q, k, bcum, True) if fast else _scores_exact(q, k, bcum, True, mask)
            o = _bdot(jnp.where(mask, a, 0.0), v)
            eb = jnp.exp(bcum)
            qd = q * eb
            o_parts = []
            for gb in range(CHUNK // SUB):
                blk = slice(gb * SUB, (gb + 1) * SUB)
                qd_b, v_b, eb_b, k_b, b_b = qd[blk], v[blk], eb[blk], k[blk], bcum[blk]
                inter = jnp.zeros((SUB, HG_DK), F32)
                for j in range(4):
                    req = c * (CHUNK // 4) + gb * 4 + j
                    s0 = s0_ref[req, h]
                    last = 4 * j + 3
                    inter = jnp.where(r16 // 4 == j, _bdot(qd_b, s0), inter)
                    kd = jnp.where(r16 // 4 == j, k_b * jnp.exp(b_b[last:last + 1] - b_b), 0.0)
                    upd = lax.dot_general(kd.astype(BF16), v_b.astype(BF16),
                                          (((0,), (0,)), ((), ())), preferred_element_type=F32)
                    dcol = jnp.sum(jnp.where(eye, eb_b[last:last + 1], 0.0), axis=-1, keepdims=True)
                    sout_ref[req, h] = dcol * s0 + upd
                o_parts.append(inter)
            o = o + jnp.concatenate(o_parts, axis=0)
            ocat_ref[rows, cs] = _hgrn_out(o, zg, hgn_ref[:, cs])
        vn = _gmlp_chunk(z_ref, 4 * HG_WIDTH, rows, lng_ref, lnb_ref, _masked_mix_weights(wm_s, mask),
                         bc_s, ocat_ref)
        vn_ref[pl.ds(pl.multiple_of(c * (CHUNK // 4), CHUNK // 4), CHUNK // 4)] = vn.reshape(
            CHUNK // 4, 4, CM_WIDTH)
        return carry

    @pl.when(fast_ok)
    def _():
        lax.fori_loop(0, tm // CHUNK, functools.partial(chunk_body, fast=True), 0)

    @pl.when(jnp.logical_not(fast_ok))
    def _():
        lax.fori_loop(0, tm // CHUNK, functools.partial(chunk_body, fast=False), 0)

    h = x + jnp.dot(ocat_ref[...], wout_ref[...], preferred_element_type=F32)
    h_ref[0] = h
    hq_ref[...] = jnp.dot(_rmsnorm(h, gc_ref[...]).astype(BF16), wcq_ref[...], preferred_element_type=F32)


def _mix_sample(x, s0, lb_param, g_mix, w_in, hg_norm_g, ln_v_g, ln_v_b, wmix, bcol, w_out, g_cross,
                w_cq, tm):
    nreq = tm // 4
    nt = x.shape[0] // nreq
    consts = (lb_param, g_mix, w_in, hg_norm_g, ln_v_g, ln_v_b, wmix, bcol, w_out, g_cross, w_cq)
    in_specs = [
        pl.BlockSpec((nreq, 4, D_MODEL), lambda i: (i, 0, 0)),
        pl.BlockSpec((nreq, HG_HEADS, HG_DK, HG_DK), lambda i: (i, 0, 0, 0)),
    ] + [_const_spec(a.shape) for a in consts]
    out_specs = [
        pl.BlockSpec((1, tm, D_MODEL), lambda i: (i, 0, 0)),
        pl.BlockSpec((nreq, HG_HEADS, HG_DK, HG_DK), lambda i: (i, 0, 0, 0)),
        pl.BlockSpec((nreq, 4, CM_WIDTH), lambda i: (i, 0, 0)),
        pl.BlockSpec((tm, D_MODEL), lambda i: (i, 0)),
    ]
    return pl.pallas_call(
        functools.partial(_mix_sample_kernel, tm=tm),
        grid=(nt,),
        in_specs=in_specs,
        out_specs=out_specs,
        out_shape=[jax.ShapeDtypeStruct((nt, tm, D_MODEL), F32),
                   jax.ShapeDtypeStruct(s0.shape, F32),
                   jax.ShapeDtypeStruct((nt * nreq, 4, CM_WIDTH), F32),
                   jax.ShapeDtypeStruct((nt * tm, D_MODEL), F32)],
        scratch_shapes=[pltpu.VMEM((tm, IN_WIDTH), F32), pltpu.VMEM((tm, D_MODEL), BF16)]
        + [pltpu.VMEM((tm, HG_WIDTH), F32)] * 3
        + [pltpu.VMEM((CM_GROUPS, CHUNK, CHUNK), F32), pltpu.VMEM((CHUNK, CM_GROUPS), F32)],
        compiler_params=pltpu.CompilerParams(dimension_semantics=("arbitrary",),
                                             vmem_limit_bytes=VMEM_LIMIT_BYTES),
        name="mix_sample",
    )(x, s0, *consts)


def _memkv_kernel(m_ref, g_ref, wk_ref, wv_ref, k_ref, v_ref, kb_ref, vb_ref):
    mn = _rmsnorm(m_ref[...], g_ref[...]).astype(BF16)
    k = jnp.dot(mn, wk_ref[...], preferred_element_type=F32)
    v = jnp.dot(mn, wv_ref[...], preferred_element_type=F32)
    kb_ref[...] = k.astype(BF16)
    vb_ref[...] = v.astype(BF16)
    for src, dst in ((k, k_ref), (v, v_ref)):
        for r in range(dst.shape[0]):
            for h in range(X_HEADS):
                dst[r, :, h, :] = src[r * N_MEM:(r + 1) * N_MEM, h * X_HEAD_DIM:(h + 1) * X_HEAD_DIM]


def _memkv(mem, g_mem, w_ck, w_cv, tm):
    n = mem.shape[0]
    return pl.pallas_call(
        _memkv_kernel,
        grid=(n // tm,),
        in_specs=[pl.BlockSpec((tm, D_MODEL), lambda i: (i, 0)), _const_spec(g_mem.shape),
                  _const_spec(w_ck.shape), _const_spec(w_cv.shape)],
        out_specs=[pl.BlockSpec((tm // N_MEM, N_MEM, X_HEADS, X_HEAD_DIM), lambda i: (i, 0, 0, 0))] * 2
        + [pl.BlockSpec((tm, D_MODEL), lambda i: (i, 0))] * 2,
        out_shape=[jax.ShapeDtypeStruct((n // N_MEM, N_MEM, X_HEADS, X_HEAD_DIM), F32)] * 2
        + [jax.ShapeDtypeStruct((n, D_MODEL), BF16)] * 2,
        compiler_params=pltpu.CompilerParams(dimension_semantics=("arbitrary",),
                                             vmem_limit_bytes=VMEM_LIMIT_BYTES),
        name="memkv",
    )(mem, g_mem, w_ck, w_cv)


def _post_sample_kernel(x_ref, ca_ref, wco_ref, gm_ref, wup_ref, wdn_ref, gf_ref, y_ref, hm_s):
    x = x_ref[0] + jnp.dot(ca_ref[...].astype(BF16), wco_ref[...], preferred_element_type=F32)
    x = x + _mlp(_rmsnorm(x, gm_ref[...]).astype(BF16), wup_ref, wdn_ref, hm_s)
    y_ref[...] = _rmsnorm(x, gf_ref[...]).reshape(y_ref.shape)


def _post_sample(x, ca, w_co, g_mlp, w_up, w_down, g_final):
    nt, tm, _ = x.shape
    in_specs = [
        pl.BlockSpec((1, tm, D_MODEL), lambda i: (i, 0, 0)),
        pl.BlockSpec((tm, D_MODEL), lambda i: (i, 0)),
        _const_spec(w_co.shape), _const_spec(g_mlp.shape), _const_spec(w_up.shape),
        _const_spec(w_down.shape), _const_spec(g_final.shape),
    ]
    return pl.pallas_call(
        _post_sample_kernel,
        grid=(nt,),
        in_specs=in_specs,
        out_specs=pl.BlockSpec((tm // 4, 4, D_MODEL), lambda i: (i, 0, 0)),
        out_shape=jax.ShapeDtypeStruct((nt * tm // 4, 4, D_MODEL), F32),
        scratch_shapes=[pltpu.VMEM((tm, D_FF), BF16)],
        compiler_params=pltpu.CompilerParams(dimension_semantics=("arbitrary",),
                                             vmem_limit_bytes=VMEM_LIMIT_BYTES),
        name="post_sample",
    )(x, ca, w_co, g_mlp, w_up, w_down, g_final)


def _cast_kernel(*refs):
    n = len(refs) // 2
    for src, dst in zip(refs[:n], refs[n:]):
        dst[...] = src[...].astype(BF16)


def _cast_weights(*ws):
    steps = 8
    specs = [pl.BlockSpec((w.shape[0] // steps, w.shape[1]), lambda i: (i, 0)) for w in ws]
    return pl.pallas_call(
        _cast_kernel,
        grid=(steps,),
        in_specs=specs,
        out_specs=specs,
        out_shape=[jax.ShapeDtypeStruct(w.shape, BF16) for w in ws],
        compiler_params=pltpu.CompilerParams(dimension_semantics=("arbitrary",),
                                             vmem_limit_bytes=VMEM_LIMIT_BYTES),
        name="cast_weights",
    )(*ws)


def kernel(x_prompt, x_sample, mem_prompt, state_hgrn, cache_mem_k, cache_mem_v, lb_param, g_mix,
           w_in, hg_norm_g, ln_v_g, ln_v_b, w_s, b_s, w_out, g_cross, g_mem, w_cq, w_ck, w_cv, w_co,
           g_mlp, w_up, w_down, g_final):
    B, L, _ = x_prompt.shape
    DB, DL, _ = x_sample.shape
    assert DL == 4 and g_mix.shape[0] == 1

    row = lambda a: a.reshape(1, -1)
    win_b, wout_b, wcq_b, wco_b, wck_b, wcv_b, wup_b, wdn_b = _cast_weights(
        w_in[0], w_out[0], w_cq[0], w_co[0], w_ck[0], w_cv[0], w_up[0], w_down[0])
    gmix, hgn, lng, lnb = row(g_mix[0]), row(hg_norm_g[0]), row(ln_v_g[0]), row(ln_v_b[0])
    gcr, gmem, gmlp, gfin = row(g_cross[0]), row(g_mem[0]), row(g_mlp[0]), row(g_final)
    wmix = w_s[0]
    bcol = b_s[0].T

    tm_s = 128
    h_s, s_s, vn_s, hq_s = _mix_sample(x_sample, state_hgrn[0], lb_param, gmix, win_b, hgn, lng, lnb, wmix,
                                       bcol, wout_b, gcr, wcq_b, tm=tm_s)

    mk, mv, mk_b, mv_b = _memkv(mem_prompt.reshape(B * N_MEM, D_MODEL), gmem, wck_b, wcv_b, tm=512)
    x2_p, s_p, ca_s = _mixattn_layer(x_prompt, mk_b.reshape(B, N_MEM, D_MODEL),
                                     mv_b.reshape(B, N_MEM, D_MODEL), hq_s, cache_mem_k[0], cache_mem_v[0],
                                     lb_param, gmix, win_b, hgn, lng, lnb, wmix, bcol, wout_b, gcr, wcq_b,
                                     wco_b, tm=512)
    y_p = _mlp_layer(x2_p, gmlp, wup_b, wdn_b, gfin, tm=1024)

    y_s = _post_sample(h_s.reshape(1, DB * DL, D_MODEL), ca_s, wco_b, gmlp, wup_b, wdn_b, gfin)

    return (y_p, y_s, s_p[None], s_s[None],
            mk[None], mv[None],
            vn_s[None])
```

```python
import functools

import jax
import jax.numpy as jnp
from jax import lax
from jax.experimental import pallas as pl
from jax.experimental.pallas import tpu as pltpu

F32 = jnp.float32
BF16 = jnp.bfloat16

D_MODEL = 1024
HG_WIDTH = 512
HG_HEADS = 4
HG_DK = 128
CM_WIDTH = 512
CM_GROUPS = 4
CM_GROUP_DIM = 128
IN_WIDTH = 4 * HG_WIDTH + 2 * CM_WIDTH
N_MEM = 256
X_HEADS = 4
X_HEAD_DIM = 256
D_FF = 4096
EPS = 1e-6

CHUNK = 128
SUBLANES = 8
SUB = 2 * SUBLANES
VMEM_LIMIT_BYTES = 62 * 1024 * 1024
LOG2E = 1.4426950408889634
FAST_BLOCK = 32
FAST_MAX_DECAY = 80.0


def _bdot(a, b):
    return jnp.dot(a.astype(BF16), b.astype(BF16), preferred_element_type=F32)


def _bdot_nt(a, b):
    return lax.dot_general(a.astype(BF16), b.astype(BF16), (((1,), (1,)), ((), ())),
                           preferred_element_type=F32)


def _rmsnorm(x, g):
    ms = jnp.mean(x * x, axis=-1, keepdims=True)
    return x * lax.rsqrt(ms + EPS) * g


def _sigmoid(x):
    return 0.5 * jnp.tanh(0.5 * x) + 0.5


def _gelu(x):
    return 0.5 * x * (1.0 + jnp.tanh(0.7978845608028654 * (x + 0.044715 * (x * x * x))))


def _softmax_rows(s):
    m = jnp.max(s, axis=-1, keepdims=True)
    e = jnp.exp(s - m)
    return e / jnp.sum(e, axis=-1, keepdims=True)


def _seg_cumsum(x, seg):
    n = x.shape[0]
    pos = lax.broadcasted_iota(jnp.int32, x.shape, 0) % seg
    s = 1
    while s < min(seg, SUBLANES):
        x = x + jnp.where(pos >= s, pltpu.roll(x, s, 0), 0.0)
        s *= 2
    while s < seg:
        parts = []
        for r0 in range(0, n, seg):
            parts.append(x[r0:r0 + s])
            parts.append(x[r0 + s:r0 + seg] + x[r0:r0 + seg - s])
        x = jnp.concatenate(parts, axis=0)
        s *= 2
    return x


def _chunk_cumsum(x):
    n = x.shape[0]
    tril = (lax.broadcasted_iota(jnp.int32, (n, n), 0)
            >= lax.broadcasted_iota(jnp.int32, (n, n), 1)).astype(BF16)
    hi = x.astype(BF16)
    r1 = x - hi.astype(F32)
    mid = r1.astype(BF16)
    lo = (r1 - mid.astype(F32)).astype(BF16)
    return (jnp.dot(tril, hi, preferred_element_type=F32) + jnp.dot(tril, mid, preferred_element_type=F32)
            + jnp.dot(tril, lo, preferred_element_type=F32))


def _lower_bound(lb_ref):
    lbp = lb_ref[...]
    lbe = jnp.exp(lbp - jnp.max(lbp, axis=0, keepdims=True))
    return lbe[0:1] / jnp.sum(lbe, axis=0, keepdims=True)


def _gates(zq, zf, lb, seg):
    half = 0.5 - 0.5 * lb
    hth = half * jnp.tanh(0.5 * zf)
    logf = jnp.log((lb + half) + hth)
    bcum = _chunk_cumsum(logf) if seg == logf.shape[0] else _seg_cumsum(logf, seg)
    hq = 0.5 * zq
    return hq * jnp.tanh(hq) + hq, half - hth, bcum


def _block_decay(bcum, sample):
    if sample:
        return jnp.max(-bcum, axis=0, keepdims=True)
    d = -bcum[FAST_BLOCK - 1:FAST_BLOCK]
    for i in range(1, CHUNK // FAST_BLOCK):
        n0, n1 = i * FAST_BLOCK, (i + 1) * FAST_BLOCK
        d = jnp.maximum(d, bcum[n0 - 1:n0] - bcum[n1 - 1:n1])
    return d


def _chunk_masks(sample):
    row = lax.broadcasted_iota(jnp.int32, (CHUNK, CHUNK), 0)
    col = lax.broadcasted_iota(jnp.int32, (CHUNK, CHUNK), 1)
    if sample:
        return (row // 4 == col // 4) & (row >= col)
    return row >= col


def _scores_fast(q, k, bcum, sample):
    if sample:
        return _bdot_nt(q * jnp.exp(bcum), k * jnp.exp(-bcum))
    a_rows = []
    zero = jnp.zeros((1, HG_DK), F32)
    kt, prev_ref = None, zero
    for i in range(CHUNK // FAST_BLOCK):
        n0, n1 = i * FAST_BLOCK, (i + 1) * FAST_BLOCK
        ref_b = bcum[n0 - 1:n0] if i else zero
        qt = q[n0:n1] * jnp.exp(bcum[n0:n1] - ref_b)
        kt_blk = k[n0:n1] * jnp.exp(ref_b - bcum[n0:n1])
        kt = kt_blk if kt is None else jnp.concatenate([kt * jnp.exp(ref_b - prev_ref), kt_blk], axis=0)
        prev_ref = ref_b
        kt_b = kt.astype(BF16)
        if n1 < CHUNK:
            kt_b = jnp.concatenate([kt_b, jnp.zeros((CHUNK - n1, HG_DK), BF16)], axis=0)
        a_rows.append(_bdot_nt(qt, kt_b))
    return jnp.concatenate(a_rows, axis=0)


def _scores_exact(q, k, bcum, sample, mask):
    col8 = lax.broadcasted_iota(jnp.int32, (SUBLANES, CHUNK), 1)
    b2 = bcum * LOG2E
    cexp = b2 - jnp.log2(k)
    a_rows = []
    for gb in range(CHUNK // SUB):
        lo = slice(gb * SUB, gb * SUB + SUBLANES)
        hi = slice(gb * SUB + SUBLANES, (gb + 1) * SUB)
        a_lo = jnp.zeros((SUBLANES, CHUNK), F32)
        a_hi = jnp.zeros((SUBLANES, CHUNK), F32)
        for s in range(SUB):
            sg = gb * SUB + s
            c_s = cexp[sg:sg + 1]
            if s < SUBLANES:
                p = q[lo] * jnp.exp2(b2[lo] - c_s)
                a_lo = jnp.where(col8 == sg, jnp.sum(p, axis=-1, keepdims=True), a_lo)
            if (not sample) or s >= SUBLANES:
                p = q[hi] * jnp.exp2(b2[hi] - c_s)
                a_hi = jnp.where(col8 == sg, jnp.sum(p, axis=-1, keepdims=True), a_hi)
        a_rows.append(a_lo)
        a_rows.append(a_hi)
    a = jnp.concatenate(a_rows, axis=0)
    if sample:
        return a
    o_rows = [jnp.zeros((SUB, CHUNK), F32)]
    for i in range(1, CHUNK // SUB):
        n = i * SUB
        ref_b = bcum[n - 1:n]
        qt = q[n:n + SUB] * jnp.exp(bcum[n:n + SUB] - ref_b)
        kt = k[:n] * jnp.exp(ref_b - bcum[:n])
        kt = jnp.concatenate([kt, jnp.zeros((CHUNK - n, HG_DK), F32)], axis=0)
        o_rows.append(_bdot_nt(qt, kt))
    return jnp.where(mask, a, 0.0) + jnp.concatenate(o_rows, axis=0)


def _hgrn_out(o, zg, g):
    o = o * lax.rsqrt(jnp.mean(o * o, axis=-1, keepdims=True) + EPS) * g
    return (o * _sigmoid(zg)).astype(BF16)


def _masked_mix_weights(wmix_ref, mask):
    return [jnp.where(mask, wmix_ref[g], 0.0).astype(BF16) for g in range(CM_GROUPS)]


def _gmlp_chunk(z_ref, c0, rows, lng_ref, lnb_ref, wms, bcol_ref, ocat_ref):
    u = _gelu(z_ref[rows, c0:c0 + CM_WIDTH])
    gv = _gelu(z_ref[rows, c0 + CM_WIDTH:c0 + 2 * CM_WIDTH])
    mu = jnp.mean(gv, axis=-1, keepdims=True)
    dv = gv - mu
    var = jnp.mean(dv * dv, axis=-1, keepdims=True)
    vn = dv * lax.rsqrt(var + EPS) * lng_ref[...] + lnb_ref[...]
    for g in range(CM_GROUPS):
        gs = slice(g * CM_GROUP_DIM, (g + 1) * CM_GROUP_DIM)
        mixed = _bdot(wms[g], vn[:, gs]) + bcol_ref[:, g:g + 1]
        ocat_ref[rows, HG_WIDTH + g * CM_GROUP_DIM:HG_WIDTH + (g + 1) * CM_GROUP_DIM] = (
            u[:, gs] * mixed).astype(BF16)
    return vn


def _sample_pair_probs(hq8, ck_ref):
    nrow = 8 * X_HEADS
    rh = lax.broadcasted_iota(jnp.int32, (nrow, N_MEM * X_HEADS), 0) // 8
    ch = lax.broadcasted_iota(jnp.int32, (nrow, N_MEM * X_HEADS), 1) % X_HEADS
    q = jnp.concatenate([hq8[:, h * X_HEAD_DIM:(h + 1) * X_HEAD_DIM] for h in range(X_HEADS)],
                        axis=0).astype(BF16)
    probs = []
    for r in range(2):
        k2 = ck_ref[r].reshape(N_MEM * X_HEADS, X_HEAD_DIM)
        s = jnp.where(rh == ch, _bdot_nt(q, k2) * (X_HEAD_DIM ** -0.5), -jnp.inf)
        probs.append(_softmax_rows(s))
    return probs


def _sample_pair_context(probs, cv_ref, ca_ref):
    first = (lax.broadcasted_iota(jnp.int32, (8 * X_HEADS, X_HEAD_DIM), 0) % 8) < 4
    outs = [_bdot(probs[r], cv_ref[r].reshape(N_MEM * X_HEADS, X_HEAD_DIM)) for r in range(2)]
    o = jnp.where(first, outs[0], outs[1])
    for h in range(X_HEADS):
        ca_ref[:, h * X_HEAD_DIM:(h + 1) * X_HEAD_DIM] = o[8 * h:8 * h + 8]


def _mlp(hn_bf16, wup_ref, wdn_ref, hm_s):
    for c in range(D_FF // D_MODEL):
        fs = slice(c * D_MODEL, (c + 1) * D_MODEL)
        hm = jnp.maximum(jnp.dot(hn_bf16, wup_ref[:, fs], preferred_element_type=F32), 0.0)
        hm_s[:, fs] = (hm * hm).astype(BF16)
    return jnp.dot(hm_s[...], wdn_ref[...], preferred_element_type=F32)


def _const_spec(shape):
    nd = len(shape)
    return pl.BlockSpec(shape, lambda *_: (0,) * nd, pipeline_mode=pl.Buffered(1))


def _mixattn_kernel(x_ref, mk_ref, mv_ref, hqs_ref, ck_ref, cv_ref, lb_ref, gmix_ref, win_ref, hgn_ref,
                    lng_ref, lnb_ref, wmix_ref, bcol_ref, wout_ref, gc_ref, wcq_ref, wco_ref,
                    x2_ref, sout_ref, cas_ref,
                    zg_ref, z_ref, ocat_ref, q_s, k_s, b_s, a_s, st0, st1, st2, st3, h1_s, h1n_s, ca_s,
                    *, tm, nt, npairs):
    t = pl.program_id(0)
    i = lax.rem(t, nt)
    nch = tm // CHUNK
    nj = nch * HG_HEADS
    mask = _chunk_masks(False)
    st_refs = (st0, st1, st2, st3)

    @pl.when(t == 0)
    def _():
        h1_s[...] = jnp.zeros_like(h1_s)
        h1n_s[...] = jnp.zeros_like(h1n_s)

    @pl.when(i == 0)
    def _():
        for st in st_refs:
            st[...] = jnp.zeros_like(st)

    x = x_ref[0]
    xn = _rmsnorm(x, gmix_ref[...]).astype(BF16)
    ngate = 2 * HG_WIDTH
    zg_ref[...] = jnp.dot(xn, win_ref[:, 0:ngate], preferred_element_type=F32)
    z_ref[...] = jnp.dot(xn, win_ref[:, ngate:IN_WIDTH], preferred_element_type=F32)
    lb_all = _lower_bound(lb_ref)
    dmax = jnp.zeros((1, HG_WIDTH), F32)
    for c in range(nch):
        rows = slice(c * CHUNK, (c + 1) * CHUNK)
        q, k, bcum = _gates(zg_ref[rows, 0:HG_WIDTH], zg_ref[rows, HG_WIDTH:2 * HG_WIDTH], lb_all, CHUNK)
        q_s[rows, :] = q
        k_s[rows, :] = k
        b_s[rows, :] = bcum
        dmax = jnp.maximum(dmax, _block_decay(bcum, False))
    fast_ok = jnp.max(dmax) < FAST_MAX_DECAY

    def fill_scores(j, fast):
        c, h = divmod(j, HG_HEADS)
        rows = slice(c * CHUNK, (c + 1) * CHUNK)
        cs = slice(h * HG_DK, (h + 1) * HG_DK)
        q, k, bcum = q_s[rows, cs], k_s[rows, cs], b_s[rows, cs]
        a = _scores_fast(q, k, bcum, False) if fast else _scores_exact(q, k, bcum, False, mask)
        a_s[j] = jnp.where(mask, a, 0.0).astype(BF16)

    def head_chunk(j, wms):
        c, h = divmod(j, HG_HEADS)
        rows = slice(c * CHUNK, (c + 1) * CHUNK)
        cs = slice(h * HG_DK, (h + 1) * HG_DK)
        q, k, bcum = q_s[rows, cs], k_s[rows, cs], b_s[rows, cs]
        v = z_ref[rows, h * HG_DK:(h + 1) * HG_DK]
        zg = z_ref[rows, HG_WIDTH + h * HG_DK:HG_WIDTH + (h + 1) * HG_DK]
        eb = jnp.exp(bcum)
        kd = k * jnp.exp(bcum[CHUNK - 1:CHUNK] - bcum)
        st = st_refs[h][...]
        o = jnp.dot(a_s[j], v.astype(BF16), preferred_element_type=F32)
        o = o + _bdot_nt(q * eb, st)
        st_refs[h][...] = st * eb[CHUNK - 1:CHUNK] + _bdot(v.T, kd)
        ocat_ref[rows, cs] = _hgrn_out(o, zg, hgn_ref[:, cs])
        if h == HG_HEADS - 1:
            _gmlp_chunk(z_ref, 2 * HG_WIDTH, rows, lng_ref, lnb_ref, wms, bcol_ref, ocat_ref)

    def region2(fast):
        wms = _masked_mix_weights(wmix_ref, mask)
        scale = X_HEAD_DIM ** -0.5
        heads = [slice(h * X_HEAD_DIM, (h + 1) * X_HEAD_DIM) for h in range(X_HEADS)]
        nq = nj // X_HEADS
        h1n = h1n_s[...]
        hq = []
        for h, hs in enumerate(heads):
            hq.append(jnp.dot(h1n, wcq_ref[:, hs], preferred_element_type=F32))
            for j in range(h * nq, (h + 1) * nq):
                fill_scores(j, fast)
        probs = [_softmax_rows(_bdot_nt(hq[h], mk_ref[0, :, hs]) * scale) for h, hs in enumerate(heads)]
        sprobs = [_sample_pair_probs(hqs_ref[8 * p:8 * p + 8, :], ck_ref.at[2 * p:2 * p + 2])
                  for p in range(npairs)]
        for j in range(0, nq):
            head_chunk(j, wms)
        for h, hs in enumerate(heads):
            ca_s[:, hs] = _bdot(probs[h], mv_ref[0, :, hs]).astype(BF16)
        ca = ca_s[...]
        for h, hs in enumerate(heads):
            x2_ref[0, :, hs] = h1_s[:, hs] + jnp.dot(ca, wco_ref[:, hs], preferred_element_type=F32)
            for j in range((h + 1) * nq, min(nj, (h + 2) * nq)):
                head_chunk(j, wms)
        for p in range(npairs):
            _sample_pair_context(sprobs[p], cv_ref.at[2 * p:2 * p + 2], cas_ref.at[8 * p:8 * p + 8, :])
        h1 = x_ref[0] + jnp.dot(ocat_ref[...], wout_ref[...], preferred_element_type=F32)
        h1_s[...] = h1
        h1n_s[...] = _rmsnorm(h1, gc_ref[...]).astype(BF16)

    @pl.when(fast_ok)
    def _():
        region2(True)

    @pl.when(jnp.logical_not(fast_ok))
    def _():
        region2(False)

    @pl.when(i == nt - 1)
    def _():
        for h in range(HG_HEADS):
            sout_ref[0, h] = st_refs[h][...].T


def _mixattn_layer(x, mk, mv, hqs, ck, cv, lb_param, g_mix, w_in, hg_norm_g, ln_v_g, ln_v_b, wmix, bcol,
                   w_out, g_cross, w_cq, w_co, tm):
    B, L, _ = x.shape
    nt = L // tm
    T = B * nt
    npairs = ck.shape[0] // (2 * T)
    assert npairs * 2 * T == ck.shape[0]

    def cur(t):
        return jnp.minimum(t, T - 1)

    def prev(t):
        return jnp.maximum(t - 1, 0)

    consts = (lb_param, g_mix, w_in, hg_norm_g, ln_v_g, ln_v_b, wmix, bcol, w_out, g_cross, w_cq, w_co)
    kv_spec = pl.BlockSpec((2 * npairs, N_MEM, X_HEADS, X_HEAD_DIM), lambda t: (cur(t), 0, 0, 0))
    in_specs = [
        pl.BlockSpec((1, tm, D_MODEL), lambda t: (cur(t) // nt, cur(t) % nt, 0)),
        pl.BlockSpec((1, N_MEM, D_MODEL), lambda t: (prev(t) // nt, 0, 0)),
        pl.BlockSpec((1, N_MEM, D_MODEL), lambda t: (prev(t) // nt, 0, 0)),
        pl.BlockSpec((8 * npairs, D_MODEL), lambda t: (cur(t), 0)), kv_spec, kv_spec,
    ] + [_const_spec(a.shape) for a in consts]
    out_specs = [
        pl.BlockSpec((1, tm, D_MODEL), lambda t: (prev(t) // nt, prev(t) % nt, 0)),
        pl.BlockSpec((1, HG_HEADS, HG_DK, HG_DK), lambda t: (cur(t) // nt, 0, 0, 0)),
        pl.BlockSpec((8 * npairs, D_MODEL), lambda t: (cur(t), 0)),
    ]
    scratch = [
        pltpu.VMEM((tm, 2 * HG_WIDTH), F32),
        pltpu.VMEM((tm, IN_WIDTH - 2 * HG_WIDTH), F32),
        pltpu.VMEM((tm, D_MODEL), BF16),
        pltpu.VMEM((tm, HG_WIDTH), F32), pltpu.VMEM((tm, HG_WIDTH), F32),
        pltpu.VMEM((tm, HG_WIDTH), F32),
        pltpu.VMEM((tm // CHUNK * HG_HEADS, CHUNK, CHUNK), BF16),
        pltpu.VMEM((HG_DK, HG_DK), F32), pltpu.VMEM((HG_DK, HG_DK), F32),
        pltpu.VMEM((HG_DK, HG_DK), F32), pltpu.VMEM((HG_DK, HG_DK), F32),
        pltpu.VMEM((tm, D_MODEL), F32),
        pltpu.VMEM((tm, D_MODEL), BF16),
        pltpu.VMEM((tm, D_MODEL), BF16),
    ]
    return pl.pallas_call(
        functools.partial(_mixattn_kernel, tm=tm, nt=nt, npairs=npairs),
        grid=(T + 1,),
        in_specs=in_specs,
        out_specs=out_specs,
        out_shape=[jax.ShapeDtypeStruct((B, L, D_MODEL), F32),
                   jax.ShapeDtypeStruct((B, HG_HEADS, HG_DK, HG_DK), F32),
                   jax.ShapeDtypeStruct(hqs.shape, F32)],
        scratch_shapes=scratch,
        compiler_params=pltpu.CompilerParams(dimension_semantics=("arbitrary",),
                                             vmem_limit_bytes=VMEM_LIMIT_BYTES),
        name="mix_attn",
    )(x, mk, mv, hqs, ck, cv, *consts)


def _mlp_kernel(x2_ref, gm_ref, wup_ref, wdn_ref, gf_ref, y_ref, hm_s):
    x2 = x2_ref[0]
    y_ref[0] = _rmsnorm(x2 + _mlp(_rmsnorm(x2, gm_ref[...]).astype(BF16), wup_ref, wdn_ref, hm_s),
                        gf_ref[...])


def _mlp_layer(x2, g_mlp, w_up, w_down, g_final, tm):
    B, L, _ = x2.shape
    nt = L // tm
    consts = (g_mlp, w_up, w_down, g_final)
    tile = pl.BlockSpec((1, tm, D_MODEL), lambda t: (t // nt, t % nt, 0))
    return pl.pallas_call(
        _mlp_kernel,
        grid=(B * nt,),
        in_specs=[tile] + [_const_spec(a.shape) for a in consts],
        out_specs=tile,
        out_shape=jax.ShapeDtypeStruct((B, L, D_MODEL), F32),
        scratch_shapes=[pltpu.VMEM((tm, D_FF), BF16)],
        compiler_params=pltpu.CompilerParams(dimension_semantics=("arbitrary",),
                                             vmem_limit_bytes=VMEM_LIMIT_BYTES),
        name="mlp_prompt",
    )(x2, *consts)


def _mix_sample_kernel(x_ref, s0_ref, lb_ref, gmix_ref, win_ref, hgn_ref, lng_ref, lnb_ref, wmix_ref,
                       bcol_ref, wout_ref, gc_ref, wcq_ref, h_ref, sout_ref, vn_ref, hq_ref,
                       z_ref, ocat_ref, q_s, k_s, b_s, wm_s, bc_s, *, tm):
    x = x_ref[...].reshape(tm, D_MODEL)
    z_ref[...] = jnp.dot(_rmsnorm(x, gmix_ref[...]).astype(BF16), win_ref[...],
                         preferred_element_type=F32)
    lb_all = _lower_bound(lb_ref)
    mask = _chunk_masks(True)
    row = lax.broadcasted_iota(jnp.int32, (CHUNK, CHUNK), 0)
    col = lax.broadcasted_iota(jnp.int32, (CHUNK, CHUNK), 1)
    eye = row == col

    @pl.when(pl.program_id(0) == 0)
    def _():
        pick = (row % 4 == col).astype(BF16)
        for g in range(CM_GROUPS):
            corner = jnp.where((row < 4) & (col < 4), wmix_ref[g], 0.0)
            wm_s[g] = _bdot_nt(_bdot(pick, corner), pick)
        pos = lax.broadcasted_iota(jnp.int32, (CHUNK, CM_GROUPS), 0) % 4
        bc = jnp.zeros((CHUNK, CM_GROUPS), F32)
        for t in range(4):
            bc = jnp.where(pos == t, bcol_ref[t:t + 1, :], bc)
        bc_s[...] = bc
    r16 = lax.broadcasted_iota(jnp.int32, (SUB, HG_DK), 0)

    def gate_body(c, dmax):
        rows = pl.ds(pl.multiple_of(c * CHUNK, CHUNK), CHUNK)
        q, k, bcum = _gates(z_ref[rows, 0:HG_WIDTH], z_ref[rows, HG_WIDTH:2 * HG_WIDTH], lb_all, 4)
        q_s[rows, :] = q
        k_s[rows, :] = k
        b_s[rows, :] = bcum
        return jnp.maximum(dmax, _block_decay(bcum, True))

    dmax = lax.fori_loop(0, tm // CHUNK, gate_body, jnp.zeros((1, HG_WIDTH), F32))
    fast_ok = jnp.max(dmax) < FAST_MAX_DECAY

    def chunk_body(c, carry, *, fast):
        rows = pl.ds(pl.multiple_of(c * CHUNK, CHUNK), CHUNK)
        for h in range(HG_HEADS):
            cs = slice(h * HG_DK, (h + 1) * HG_DK)
            q, k, bcum = q_s[rows, cs], k_s[rows, cs], b_s[rows, cs]
            v = z_ref[rows, 2 * HG_WIDTH + h * HG_DK:2 * HG_WIDTH + (h + 1) * HG_DK]
            zg = z_ref[rows, 3 * HG_WIDTH + h * HG_DK:3 * HG_WIDTH + (h + 1) * HG_DK]
            a = _scores_fast(q, k, bcum, True) if fast else _scores_exact(q, k, bcum, True, mask)
            o = _bdot(jnp.where(mask, a, 0.0), v)
            eb = jnp.exp(bcum)
            qd = q * eb
            o_parts = []
            for gb in range(CHUNK // SUB):
                blk = slice(gb * SUB, (gb + 1) * SUB)
                qd_b, v_b, eb_b, k_b, b_b = qd[blk], v[blk], eb[blk], k[blk], bcum[blk]
                inter = jnp.zeros((SUB, HG_DK), F32)
                for j in range(4):
                    req = c * (CHUNK // 4) + gb * 4 + j
                    s0 = s0_ref[req, h]
                    last = 4 * j + 3
                    inter = jnp.where(r16 // 4 == j, _bdot(qd_b, s0), inter)
                    kd = jnp.where(r16 // 4 == j, k_b * jnp.exp(b_b[last:last + 1] - b_b), 0.0)
                    upd = lax.dot_general(kd.astype(BF16), v_b.astype(BF16),
                                          (((0,), (0,)), ((), ())), preferred_element_type=F32)
                    dcol = jnp.sum(jnp.where(eye, eb_b[last:last + 1], 0.0), axis=-1, keepdims=True)
                    sout_ref[req, h] = dcol * s0 + upd
                o_parts.append(inter)
            o = o + jnp.concatenate(o_parts, axis=0)
            ocat_ref[rows, cs] = _hgrn_out(o, zg, hgn_ref[:, cs])
        vn = _gmlp_chunk(z_ref, 4 * HG_WIDTH, rows, lng_ref, lnb_ref, _masked_mix_weights(wm_s, mask),
                         bc_s, ocat_ref)
        vn_ref[pl.ds(pl.multiple_of(c * (CHUNK // 4), CHUNK // 4), CHUNK // 4)] = vn.reshape(
            CHUNK // 4, 4, CM_WIDTH)
        return carry

    @pl.when(fast_ok)
    def _():
        lax.fori_loop(0, tm // CHUNK, functools.partial(chunk_body, fast=True), 0)

    @pl.when(jnp.logical_not(fast_ok))
    def _():
        lax.fori_loop(0, tm // CHUNK, functools.partial(chunk_body, fast=False), 0)

    h = x + jnp.dot(ocat_ref[...], wout_ref[...], preferred_element_type=F32)
    h_ref[0] = h
    hq_ref[...] = jnp.dot(_rmsnorm(h, gc_ref[...]).astype(BF16), wcq_ref[...], preferred_element_type=F32)


def _mix_sample(x, s0, lb_param, g_mix, w_in, hg_norm_g, ln_v_g, ln_v_b, wmix, bcol, w_out, g_cross,
                w_cq, tm):
    nreq = tm // 4
    nt = x.shape[0] // nreq
    consts = (lb_param, g_mix, w_in, hg_norm_g, ln_v_g, ln_v_b, wmix, bcol, w_out, g_cross, w_cq)
    in_specs = [
        pl.BlockSpec((nreq, 4, D_MODEL), lambda i: (i, 0, 0)),
        pl.BlockSpec((nreq, HG_HEADS, HG_DK, HG_DK), lambda i: (i, 0, 0, 0)),
    ] + [_const_spec(a.shape) for a in consts]
    out_specs = [
        pl.BlockSpec((1, tm, D_MODEL), lambda i: (i, 0, 0)),
        pl.BlockSpec((nreq, HG_HEADS, HG_DK, HG_DK), lambda i: (i, 0, 0, 0)),
        pl.BlockSpec((nreq, 4, CM_WIDTH), lambda i: (i, 0, 0)),
        pl.BlockSpec((tm, D_MODEL), lambda i: (i, 0)),
    ]
    return pl.pallas_call(
        functools.partial(_mix_sample_kernel, tm=tm),
        grid=(nt,),
        in_specs=in_specs,
        out_specs=out_specs,
        out_shape=[jax.ShapeDtypeStruct((nt, tm, D_MODEL), F32),
                   jax.ShapeDtypeStruct(s0.shape, F32),
                   jax.ShapeDtypeStruct((nt * nreq, 4, CM_WIDTH), F32),
                   jax.ShapeDtypeStruct((nt * tm, D_MODEL), F32)],
        scratch_shapes=[pltpu.VMEM((tm, IN_WIDTH), F32), pltpu.VMEM((tm, D_MODEL), BF16)]
        + [pltpu.VMEM((tm, HG_WIDTH), F32)] * 3
        + [pltpu.VMEM((CM_GROUPS, CHUNK, CHUNK), F32), pltpu.VMEM((CHUNK, CM_GROUPS), F32)],
        compiler_params=pltpu.CompilerParams(dimension_semantics=("arbitrary",),
                                             vmem_limit_bytes=VMEM_LIMIT_BYTES),
        name="mix_sample",
    )(x, s0, *consts)


def _memkv_kernel(m_ref, g_ref, wk_ref, wv_ref, k_ref, v_ref, kb_ref, vb_ref):
    mn = _rmsnorm(m_ref[...], g_ref[...]).astype(BF16)
    k = jnp.dot(mn, wk_ref[...], preferred_element_type=F32)
    v = jnp.dot(mn, wv_ref[...], preferred_element_type=F32)
    kb_ref[...] = k.astype(BF16)
    vb_ref[...] = v.astype(BF16)
    for src, dst in ((k, k_ref), (v, v_ref)):
        for r in range(dst.shape[0]):
            for h in range(X_HEADS):
                dst[r, :, h, :] = src[r * N_MEM:(r + 1) * N_MEM, h * X_HEAD_DIM:(h + 1) * X_HEAD_DIM]


def _memkv(mem, g_mem, w_ck, w_cv, tm):
    n = mem.shape[0]
    return pl.pallas_call(
        _memkv_kernel,
        grid=(n // tm,),
        in_specs=[pl.BlockSpec((tm, D_MODEL), lambda i: (i, 0)), _const_spec(g_mem.shape),
                  _const_spec(w_ck.shape), _const_spec(w_cv.shape)],
        out_specs=[pl.BlockSpec((tm // N_MEM, N_MEM, X_HEADS, X_HEAD_DIM), lambda i: (i, 0, 0, 0))] * 2
        + [pl.BlockSpec((tm, D_MODEL), lambda i: (i, 0))] * 2,
        out_shape=[jax.ShapeDtypeStruct((n // N_MEM, N_MEM, X_HEADS, X_HEAD_DIM), F32)] * 2
        + [jax.ShapeDtypeStruct((n, D_MODEL), BF16)] * 2,
        compiler_params=pltpu.CompilerParams(dimension_semantics=("arbitrary",),
                                             vmem_limit_bytes=VMEM_LIMIT_BYTES),
        name="memkv",
    )(mem, g_mem, w_ck, w_cv)


def _post_sample_kernel(x_ref, ca_ref, wco_ref, gm_ref, wup_ref, wdn_ref, gf_ref, y_ref, hn_s, acc_s):
    p = pl.program_id(0)

    @pl.when(p == 0)
    def _():
        x2 = x_ref[0] + jnp.dot(ca_ref[...].astype(BF16), wco_ref[...], preferred_element_type=F32)
        acc_s[...] = x2
        hn_s[...] = _rmsnorm(x2, gm_ref[...]).astype(BF16)

    hm = jnp.maximum(jnp.dot(hn_s[...], wup_ref[...], preferred_element_type=F32), 0.0)
    acc_s[...] += jnp.dot((hm * hm).astype(BF16), wdn_ref[...], preferred_element_type=F32)

    @pl.when(p == pl.num_programs(0) - 1)
    def _():
        y_ref[...] = _rmsnorm(acc_s[...], gf_ref[...]).reshape(y_ref.shape)


def _post_sample(x, ca, w_co, g_mlp, w_up, w_down, g_final):
    _, tm, _ = x.shape
    in_specs = [
        pl.BlockSpec((1, tm, D_MODEL), lambda p: (0, 0, 0)),
        pl.BlockSpec((tm, D_MODEL), lambda p: (0, 0)),
        _const_spec(w_co.shape), _const_spec(g_mlp.shape),
        pl.BlockSpec((D_MODEL, D_MODEL), lambda p: (0, p)),
        pl.BlockSpec((D_MODEL, D_MODEL), lambda p: (p, 0)),
        _const_spec(g_final.shape),
    ]
    return pl.pallas_call(
        _post_sample_kernel,
        grid=(D_FF // D_MODEL,),
        in_specs=in_specs,
        out_specs=pl.BlockSpec((tm // 4, 4, D_MODEL), lambda p: (0, 0, 0)),
        out_shape=jax.ShapeDtypeStruct((tm // 4, 4, D_MODEL), F32),
        scratch_shapes=[pltpu.VMEM((tm, D_MODEL), BF16), pltpu.VMEM((tm, D_MODEL), F32)],
        compiler_params=pltpu.CompilerParams(dimension_semantics=("arbitrary",),
                                             vmem_limit_bytes=VMEM_LIMIT_BYTES),
        name="post_sample",
    )(x, ca, w_co, g_mlp, w_up, w_down, g_final)


def _cast_kernel(*refs):
    n = len(refs) // 2
    for src, dst in zip(refs[:n], refs[n:]):
        dst[...] = src[...].astype(BF16)


def _cast_weights(*ws):
    steps = 16
    specs = [pl.BlockSpec((w.shape[0] // steps, w.shape[1]), lambda i: (i, 0)) for w in ws]
    return pl.pallas_call(
        _cast_kernel,
        grid=(steps,),
        in_specs=specs,
        out_specs=specs,
        out_shape=[jax.ShapeDtypeStruct(w.shape, BF16) for w in ws],
        compiler_params=pltpu.CompilerParams(dimension_semantics=("arbitrary",),
                                             vmem_limit_bytes=VMEM_LIMIT_BYTES),
        name="cast_weights",
    )(*ws)


def kernel(x_prompt, x_sample, mem_prompt, state_hgrn, cache_mem_k, cache_mem_v, lb_param, g_mix,
           w_in, hg_norm_g, ln_v_g, ln_v_b, w_s, b_s, w_out, g_cross, g_mem, w_cq, w_ck, w_cv, w_co,
           g_mlp, w_up, w_down, g_final):
    B, L, _ = x_prompt.shape
    DB, DL, _ = x_sample.shape
    assert DL == 4 and g_mix.shape[0] == 1

    row = lambda a: a.reshape(1, -1)
    win_b, wout_b, wcq_b, wco_b, wck_b, wcv_b, wup_b, wdn_b = _cast_weights(
        w_in[0], w_out[0], w_cq[0], w_co[0], w_ck[0], w_cv[0], w_up[0], w_down[0])
    gmix, hgn, lng, lnb = row(g_mix[0]), row(hg_norm_g[0]), row(ln_v_g[0]), row(ln_v_b[0])
    gcr, gmem, gmlp, gfin = row(g_cross[0]), row(g_mem[0]), row(g_mlp[0]), row(g_final)
    wmix = w_s[0]
    bcol = b_s[0].T

    tm_s = 128
    h_s, s_s, vn_s, hq_s = _mix_sample(x_sample, state_hgrn[0], lb_param, gmix, win_b, hgn, lng, lnb, wmix,
                                       bcol, wout_b, gcr, wcq_b, tm=tm_s)

    mk, mv, mk_b, mv_b = _memkv(mem_prompt.reshape(B * N_MEM, D_MODEL), gmem, wck_b, wcv_b, tm=512)
    x2_p, s_p, ca_s = _mixattn_layer(x_prompt, mk_b.reshape(B, N_MEM, D_MODEL),
                                     mv_b.reshape(B, N_MEM, D_MODEL), hq_s, cache_mem_k[0], cache_mem_v[0],
                                     lb_param, gmix, win_b, hgn, lng, lnb, wmix, bcol, wout_b, gcr, wcq_b,
                                     wco_b, tm=512)
    y_p = _mlp_layer(x2_p, gmlp, wup_b, wdn_b, gfin, tm=1024)

    y_s = _post_sample(h_s.reshape(1, DB * DL, D_MODEL), ca_s, wco_b, gmlp, wup_b, wdn_b, gfin)

    return (y_p, y_s, s_p[None], s_s[None],
            mk[None], mv[None],
            vn_s[None])
```

```python
import functools

import jax
import jax.numpy as jnp
from jax import lax
from jax.experimental import pallas as pl
from jax.experimental.pallas import tpu as pltpu

F32 = jnp.float32
BF16 = jnp.bfloat16

D_MODEL = 1024
HG_WIDTH = 512
HG_HEADS = 4
HG_DK = 128
CM_WIDTH = 512
CM_GROUPS = 4
CM_GROUP_DIM = 128
IN_WIDTH = 4 * HG_WIDTH + 2 * CM_WIDTH
N_MEM = 256
X_HEADS = 4
X_HEAD_DIM = 256
D_FF = 4096
EPS = 1e-6

CHUNK = 128
SUBLANES = 8
SUB = 2 * SUBLANES
VMEM_LIMIT_BYTES = 62 * 1024 * 1024
LOG2E = 1.4426950408889634
FAST_BLOCK = 32
FAST_MAX_DECAY = 80.0


def _bdot(a, b):
    return jnp.dot(a.astype(BF16), b.astype(BF16), preferred_element_type=F32)


def _bdot_nt(a, b):
    return lax.dot_general(a.astype(BF16), b.astype(BF16), (((1,), (1,)), ((), ())),
                           preferred_element_type=F32)


def _rmsnorm(x, g):
    ms = jnp.mean(x * x, axis=-1, keepdims=True)
    return x * lax.rsqrt(ms + EPS) * g


def _sigmoid(x):
    return 0.5 * jnp.tanh(0.5 * x) + 0.5


def _gelu(x):
    return 0.5 * x * (1.0 + jnp.tanh(0.7978845608028654 * (x + 0.044715 * (x * x * x))))


def _softmax_rows(s):
    m = jnp.max(s, axis=-1, keepdims=True)
    e = jnp.exp(s - m)
    return e / jnp.sum(e, axis=-1, keepdims=True)


def _seg_cumsum(x, seg):
    n = x.shape[0]
    pos = lax.broadcasted_iota(jnp.int32, x.shape, 0) % seg
    s = 1
    while s < min(seg, SUBLANES):
        x = x + jnp.where(pos >= s, pltpu.roll(x, s, 0), 0.0)
        s *= 2
    while s < seg:
        parts = []
        for r0 in range(0, n, seg):
            parts.append(x[r0:r0 + s])
            parts.append(x[r0 + s:r0 + seg] + x[r0:r0 + seg - s])
        x = jnp.concatenate(parts, axis=0)
        s *= 2
    return x


def _chunk_cumsum(x):
    n = x.shape[0]
    tril = (lax.broadcasted_iota(jnp.int32, (n, n), 0)
            >= lax.broadcasted_iota(jnp.int32, (n, n), 1)).astype(BF16)
    hi = x.astype(BF16)
    r1 = x - hi.astype(F32)
    mid = r1.astype(BF16)
    lo = (r1 - mid.astype(F32)).astype(BF16)
    return (jnp.dot(tril, hi, preferred_element_type=F32) + jnp.dot(tril, mid, preferred_element_type=F32)
            + jnp.dot(tril, lo, preferred_element_type=F32))


def _lower_bound(lb_ref):
    lbp = lb_ref[...]
    lbe = jnp.exp(lbp - jnp.max(lbp, axis=0, keepdims=True))
    return lbe[0:1] / jnp.sum(lbe, axis=0, keepdims=True)


def _gates(zq, zf, lb, seg):
    half = 0.5 - 0.5 * lb
    hth = half * jnp.tanh(0.5 * zf)
    logf = jnp.log((lb + half) + hth)
    bcum = _chunk_cumsum(logf) if seg == logf.shape[0] else _seg_cumsum(logf, seg)
    hq = 0.5 * zq
    return hq * jnp.tanh(hq) + hq, half - hth, bcum


def _block_decay(bcum, sample):
    if sample:
        return jnp.max(-bcum, axis=0, keepdims=True)
    d = -bcum[FAST_BLOCK - 1:FAST_BLOCK]
    for i in range(1, CHUNK // FAST_BLOCK):
        n0, n1 = i * FAST_BLOCK, (i + 1) * FAST_BLOCK
        d = jnp.maximum(d, bcum[n0 - 1:n0] - bcum[n1 - 1:n1])
    return d


def _chunk_masks(sample):
    row = lax.broadcasted_iota(jnp.int32, (CHUNK, CHUNK), 0)
    col = lax.broadcasted_iota(jnp.int32, (CHUNK, CHUNK), 1)
    if sample:
        return (row // 4 == col // 4) & (row >= col)
    return row >= col


def _scores_fast(q, k, bcum, sample):
    if sample:
        return _bdot_nt(q * jnp.exp(bcum), k * jnp.exp(-bcum))
    a_rows = []
    zero = jnp.zeros((1, HG_DK), F32)
    kt, prev_ref = None, zero
    for i in range(CHUNK // FAST_BLOCK):
        n0, n1 = i * FAST_BLOCK, (i + 1) * FAST_BLOCK
        ref_b = bcum[n0 - 1:n0] if i else zero
        qt = q[n0:n1] * jnp.exp(bcum[n0:n1] - ref_b)
        kt_blk = k[n0:n1] * jnp.exp(ref_b - bcum[n0:n1])
        kt = kt_blk if kt is None else jnp.concatenate([kt * jnp.exp(ref_b - prev_ref), kt_blk], axis=0)
        prev_ref = ref_b
        kt_b = kt.astype(BF16)
        if n1 < CHUNK:
            kt_b = jnp.concatenate([kt_b, jnp.zeros((CHUNK - n1, HG_DK), BF16)], axis=0)
        a_rows.append(_bdot_nt(qt, kt_b))
    return jnp.concatenate(a_rows, axis=0)


def _scores_exact(q, k, bcum, sample, mask):
    col8 = lax.broadcasted_iota(jnp.int32, (SUBLANES, CHUNK), 1)
    b2 = bcum * LOG2E
    cexp = b2 - jnp.log2(k)
    a_rows = []
    for gb in range(CHUNK // SUB):
        lo = slice(gb * SUB, gb * SUB + SUBLANES)
        hi = slice(gb * SUB + SUBLANES, (gb + 1) * SUB)
        a_lo = jnp.zeros((SUBLANES, CHUNK), F32)
        a_hi = jnp.zeros((SUBLANES, CHUNK), F32)
        for s in range(SUB):
            sg = gb * SUB + s
            c_s = cexp[sg:sg + 1]
            if s < SUBLANES:
                p = q[lo] * jnp.exp2(b2[lo] - c_s)
                a_lo = jnp.where(col8 == sg, jnp.sum(p, axis=-1, keepdims=True), a_lo)
            if (not sample) or s >= SUBLANES:
                p = q[hi] * jnp.exp2(b2[hi] - c_s)
                a_hi = jnp.where(col8 == sg, jnp.sum(p, axis=-1, keepdims=True), a_hi)
        a_rows.append(a_lo)
        a_rows.append(a_hi)
    a = jnp.concatenate(a_rows, axis=0)
    if sample:
        return a
    o_rows = [jnp.zeros((SUB, CHUNK), F32)]
    for i in range(1, CHUNK // SUB):
        n = i * SUB
        ref_b = bcum[n - 1:n]
        qt = q[n:n + SUB] * jnp.exp(bcum[n:n + SUB] - ref_b)
        kt = k[:n] * jnp.exp(ref_b - bcum[:n])
        kt = jnp.concatenate([kt, jnp.zeros((CHUNK - n, HG_DK), F32)], axis=0)
        o_rows.append(_bdot_nt(qt, kt))
    return jnp.where(mask, a, 0.0) + jnp.concatenate(o_rows, axis=0)


def _hgrn_out(o, zg, g):
    o = o * lax.rsqrt(jnp.mean(o * o, axis=-1, keepdims=True) + EPS) * g
    return (o * _sigmoid(zg)).astype(BF16)


def _masked_mix_weights(wmix_ref, mask):
    return [jnp.where(mask, wmix_ref[g], 0.0).astype(BF16) for g in range(CM_GROUPS)]


def _gmlp_chunk(z_ref, c0, rows, lng_ref, lnb_ref, wms, bcol_ref, ocat_ref):
    u = _gelu(z_ref[rows, c0:c0 + CM_WIDTH])
    gv = _gelu(z_ref[rows, c0 + CM_WIDTH:c0 + 2 * CM_WIDTH])
    mu = jnp.mean(gv, axis=-1, keepdims=True)
    dv = gv - mu
    var = jnp.mean(dv * dv, axis=-1, keepdims=True)
    vn = dv * lax.rsqrt(var + EPS) * lng_ref[...] + lnb_ref[...]
    for g in range(CM_GROUPS):
        gs = slice(g * CM_GROUP_DIM, (g + 1) * CM_GROUP_DIM)
        mixed = _bdot(wms[g], vn[:, gs]) + bcol_ref[:, g:g + 1]
        ocat_ref[rows, HG_WIDTH + g * CM_GROUP_DIM:HG_WIDTH + (g + 1) * CM_GROUP_DIM] = (
            u[:, gs] * mixed).astype(BF16)
    return vn


def _sample_pair_probs(hq8, ck_ref):
    nrow = 8 * X_HEADS
    rh = lax.broadcasted_iota(jnp.int32, (nrow, N_MEM * X_HEADS), 0) // 8
    ch = lax.broadcasted_iota(jnp.int32, (nrow, N_MEM * X_HEADS), 1) % X_HEADS
    q = jnp.concatenate([hq8[:, h * X_HEAD_DIM:(h + 1) * X_HEAD_DIM] for h in range(X_HEADS)],
                        axis=0).astype(BF16)
    probs = []
    for r in range(2):
        k2 = ck_ref[r].reshape(N_MEM * X_HEADS, X_HEAD_DIM)
        s = jnp.where(rh == ch, _bdot_nt(q, k2) * (X_HEAD_DIM ** -0.5), -jnp.inf)
        probs.append(_softmax_rows(s))
    return probs


def _sample_pair_context(probs, cv_ref, ca_ref):
    first = (lax.broadcasted_iota(jnp.int32, (8 * X_HEADS, X_HEAD_DIM), 0) % 8) < 4
    outs = [_bdot(probs[r], cv_ref[r].reshape(N_MEM * X_HEADS, X_HEAD_DIM)) for r in range(2)]
    o = jnp.where(first, outs[0], outs[1])
    for h in range(X_HEADS):
        ca_ref[:, h * X_HEAD_DIM:(h + 1) * X_HEAD_DIM] = o[8 * h:8 * h + 8]


def _mlp(hn_bf16, wup_ref, wdn_ref, hm_s):
    for c in range(D_FF // D_MODEL):
        fs = slice(c * D_MODEL, (c + 1) * D_MODEL)
        hm = jnp.maximum(jnp.dot(hn_bf16, wup_ref[:, fs], preferred_element_type=F32), 0.0)
        hm_s[:, fs] = (hm * hm).astype(BF16)
    return jnp.dot(hm_s[...], wdn_ref[...], preferred_element_type=F32)


def _const_spec(shape):
    nd = len(shape)
    return pl.BlockSpec(shape, lambda *_: (0,) * nd, pipeline_mode=pl.Buffered(1))


def _mixattn_kernel(x_ref, mk_ref, mv_ref, wupf_ref, wdnf_ref, lb_ref, gmix_ref, win_ref, hgn_ref,
                    lng_ref, lnb_ref, wmix_ref, bcol_ref, wout_ref, gc_ref, wcq_ref, wco_ref,
                    x2_ref, sout_ref, wupb_ref, wdnb_ref,
                    zg_ref, z_ref, ocat_ref, q_s, k_s, b_s, a_s, st0, st1, st2, st3, h1_s, h1n_s, ca_s,
                    *, tm, nt):
    t = pl.program_id(0)
    i = lax.rem(t, nt)
    nch = tm // CHUNK
    nj = nch * HG_HEADS
    mask = _chunk_masks(False)
    st_refs = (st0, st1, st2, st3)

    @pl.when(t == 0)
    def _():
        h1_s[...] = jnp.zeros_like(h1_s)
        h1n_s[...] = jnp.zeros_like(h1n_s)

    @pl.when(i == 0)
    def _():
        for st in st_refs:
            st[...] = jnp.zeros_like(st)

    wupb_ref[...] = wupf_ref[...].astype(BF16)
    wdnb_ref[...] = wdnf_ref[...].astype(BF16)

    x = x_ref[0]
    xn = _rmsnorm(x, gmix_ref[...]).astype(BF16)
    ngate = 2 * HG_WIDTH
    zg_ref[...] = jnp.dot(xn, win_ref[:, 0:ngate], preferred_element_type=F32)
    z_ref[...] = jnp.dot(xn, win_ref[:, ngate:IN_WIDTH], preferred_element_type=F32)
    lb_all = _lower_bound(lb_ref)
    dmax = jnp.zeros((1, HG_WIDTH), F32)
    for c in range(nch):
        rows = slice(c * CHUNK, (c + 1) * CHUNK)
        q, k, bcum = _gates(zg_ref[rows, 0:HG_WIDTH], zg_ref[rows, HG_WIDTH:2 * HG_WIDTH], lb_all, CHUNK)
        q_s[rows, :] = q
        k_s[rows, :] = k
        b_s[rows, :] = bcum
        dmax = jnp.maximum(dmax, _block_decay(bcum, False))
    fast_ok = jnp.max(dmax) < FAST_MAX_DECAY

    def fill_scores(j, fast):
        c, h = divmod(j, HG_HEADS)
        rows = slice(c * CHUNK, (c + 1) * CHUNK)
        cs = slice(h * HG_DK, (h + 1) * HG_DK)
        q, k, bcum = q_s[rows, cs], k_s[rows, cs], b_s[rows, cs]
        a = _scores_fast(q, k, bcum, False) if fast else _scores_exact(q, k, bcum, False, mask)
        a_s[j] = jnp.where(mask, a, 0.0).astype(BF16)

    def head_chunk(j, wms):
        c, h = divmod(j, HG_HEADS)
        rows = slice(c * CHUNK, (c + 1) * CHUNK)
        cs = slice(h * HG_DK, (h + 1) * HG_DK)
        q, k, bcum = q_s[rows, cs], k_s[rows, cs], b_s[rows, cs]
        v = z_ref[rows, h * HG_DK:(h + 1) * HG_DK]
        zg = z_ref[rows, HG_WIDTH + h * HG_DK:HG_WIDTH + (h + 1) * HG_DK]
        eb = jnp.exp(bcum)
        kd = k * jnp.exp(bcum[CHUNK - 1:CHUNK] - bcum)
        st = st_refs[h][...]
        o = jnp.dot(a_s[j], v.astype(BF16), preferred_element_type=F32)
        o = o + _bdot_nt(q * eb, st)
        st_refs[h][...] = st * eb[CHUNK - 1:CHUNK] + _bdot(v.T, kd)
        ocat_ref[rows, cs] = _hgrn_out(o, zg, hgn_ref[:, cs])
        if h == HG_HEADS - 1:
            _gmlp_chunk(z_ref, 2 * HG_WIDTH, rows, lng_ref, lnb_ref, wms, bcol_ref, ocat_ref)

    def region2(fast):
        wms = _masked_mix_weights(wmix_ref, mask)
        scale = X_HEAD_DIM ** -0.5
        heads = [slice(h * X_HEAD_DIM, (h + 1) * X_HEAD_DIM) for h in range(X_HEADS)]
        nq = nj // X_HEADS
        h1n = h1n_s[...]
        hq = []
        for h, hs in enumerate(heads):
            hq.append(jnp.dot(h1n, wcq_ref[:, hs], preferred_element_type=F32))
            for j in range(h * nq, (h + 1) * nq):
                fill_scores(j, fast)
        probs = [_softmax_rows(_bdot_nt(hq[h], mk_ref[0, :, hs]) * scale) for h, hs in enumerate(heads)]
        for j in range(0, nq):
            head_chunk(j, wms)
        for h, hs in enumerate(heads):
            ca_s[:, hs] = _bdot(probs[h], mv_ref[0, :, hs]).astype(BF16)
        ca = ca_s[...]
        for h, hs in enumerate(heads):
            x2_ref[0, :, hs] = h1_s[:, hs] + jnp.dot(ca, wco_ref[:, hs], preferred_element_type=F32)
            for j in range((h + 1) * nq, min(nj, (h + 2) * nq)):
                head_chunk(j, wms)
        h1 = x_ref[0] + jnp.dot(ocat_ref[...], wout_ref[...], preferred_element_type=F32)
        h1_s[...] = h1
        h1n_s[...] = _rmsnorm(h1, gc_ref[...]).astype(BF16)

    @pl.when(fast_ok)
    def _():
        region2(True)

    @pl.when(jnp.logical_not(fast_ok))
    def _():
        region2(False)

    @pl.when(i == nt - 1)
    def _():
        for h in range(HG_HEADS):
            sout_ref[0, h] = st_refs[h][...].T


def _mixattn_layer(x, mk, mv, w_up, w_down, lb_param, g_mix, w_in, hg_norm_g, ln_v_g, ln_v_b, wmix, bcol,
                   w_out, g_cross, w_cq, w_co, tm):
    B, L, _ = x.shape
    nt = L // tm
    T = B * nt
    assert w_up.shape[0] % T == 0 and w_down.shape[0] % T == 0
    up_rows, dn_rows = w_up.shape[0] // T, w_down.shape[0] // T

    def cur(t):
        return jnp.minimum(t, T - 1)

    def prev(t):
        return jnp.maximum(t - 1, 0)

    consts = (lb_param, g_mix, w_in, hg_norm_g, ln_v_g, ln_v_b, wmix, bcol, w_out, g_cross, w_cq, w_co)
    up_spec = pl.BlockSpec((up_rows, w_up.shape[1]), lambda t: (cur(t), 0))
    dn_spec = pl.BlockSpec((dn_rows, w_down.shape[1]), lambda t: (cur(t), 0))
    in_specs = [
        pl.BlockSpec((1, tm, D_MODEL), lambda t: (cur(t) // nt, cur(t) % nt, 0)),
        pl.BlockSpec((1, N_MEM, D_MODEL), lambda t: (prev(t) // nt, 0, 0)),
        pl.BlockSpec((1, N_MEM, D_MODEL), lambda t: (prev(t) // nt, 0, 0)),
        up_spec, dn_spec,
    ] + [_const_spec(a.shape) for a in consts]
    out_specs = [
        pl.BlockSpec((1, tm, D_MODEL), lambda t: (prev(t) // nt, prev(t) % nt, 0)),
        pl.BlockSpec((1, HG_HEADS, HG_DK, HG_DK), lambda t: (cur(t) // nt, 0, 0, 0)),
        up_spec, dn_spec,
    ]
    scratch = [
        pltpu.VMEM((tm, 2 * HG_WIDTH), F32),
        pltpu.VMEM((tm, IN_WIDTH - 2 * HG_WIDTH), F32),
        pltpu.VMEM((tm, D_MODEL), BF16),
        pltpu.VMEM((tm, HG_WIDTH), F32), pltpu.VMEM((tm, HG_WIDTH), F32),
        pltpu.VMEM((tm, HG_WIDTH), F32),
        pltpu.VMEM((tm // CHUNK * HG_HEADS, CHUNK, CHUNK), BF16),
        pltpu.VMEM((HG_DK, HG_DK), F32), pltpu.VMEM((HG_DK, HG_DK), F32),
        pltpu.VMEM((HG_DK, HG_DK), F32), pltpu.VMEM((HG_DK, HG_DK), F32),
        pltpu.VMEM((tm, D_MODEL), F32),
        pltpu.VMEM((tm, D_MODEL), BF16),
        pltpu.VMEM((tm, D_MODEL), BF16),
    ]
    return pl.pallas_call(
        functools.partial(_mixattn_kernel, tm=tm, nt=nt),
        grid=(T + 1,),
        in_specs=in_specs,
        out_specs=out_specs,
        out_shape=[jax.ShapeDtypeStruct((B, L, D_MODEL), F32),
                   jax.ShapeDtypeStruct((B, HG_HEADS, HG_DK, HG_DK), F32),
                   jax.ShapeDtypeStruct(w_up.shape, BF16), jax.ShapeDtypeStruct(w_down.shape, BF16)],
        scratch_shapes=scratch,
        compiler_params=pltpu.CompilerParams(dimension_semantics=("arbitrary",),
                                             vmem_limit_bytes=VMEM_LIMIT_BYTES),
        name="mix_attn",
    )(x, mk, mv, w_up, w_down, *consts)


def _mlp_kernel(x2_ref, hqs_ref, ck_ref, cv_ref, gm_ref, wup_ref, wdn_ref, gf_ref, y_ref, cas_ref, hm_s,
                *, npairs):
    x2 = x2_ref[0]
    hn = _rmsnorm(x2, gm_ref[...]).astype(BF16)
    nsl = D_FF // D_MODEL
    probs = {}
    for p in range(nsl):
        fs = slice(p * D_MODEL, (p + 1) * D_MODEL)
        hm = jnp.maximum(jnp.dot(hn, wup_ref[:, fs], preferred_element_type=F32), 0.0)
        hm_s[:, fs] = (hm * hm).astype(BF16)
        if p < npairs:
            probs[p] = _sample_pair_probs(hqs_ref[8 * p:8 * p + 8, :], ck_ref.at[2 * p:2 * p + 2])
    acc = x2
    for p in range(nsl):
        fs = slice(p * D_MODEL, (p + 1) * D_MODEL)
        acc = acc + jnp.dot(hm_s[:, fs], wdn_ref[fs, :], preferred_element_type=F32)
        if p < npairs:
            _sample_pair_context(probs[p], cv_ref.at[2 * p:2 * p + 2], cas_ref.at[8 * p:8 * p + 8, :])
    y_ref[0] = _rmsnorm(acc, gf_ref[...])


def _mlp_layer(x2, hqs, ck, cv, g_mlp, w_up, w_down, g_final, tm):
    B, L, _ = x2.shape
    nt = L // tm
    T = B * nt
    npairs = ck.shape[0] // (2 * T)
    assert npairs * 2 * T == ck.shape[0] and npairs <= D_FF // D_MODEL
    consts = (g_mlp, w_up, w_down, g_final)
    kv_spec = pl.BlockSpec((2 * npairs, N_MEM, X_HEADS, X_HEAD_DIM), lambda t: (t, 0, 0, 0))
    return pl.pallas_call(
        functools.partial(_mlp_kernel, npairs=npairs),
        grid=(T,),
        in_specs=[pl.BlockSpec((1, tm, D_MODEL), lambda t: (t // nt, t % nt, 0)),
                  pl.BlockSpec((8 * npairs, D_MODEL), lambda t: (t, 0)), kv_spec, kv_spec]
        + [_const_spec(a.shape) for a in consts],
        out_specs=[pl.BlockSpec((1, tm, D_MODEL), lambda t: (t // nt, t % nt, 0)),
                   pl.BlockSpec((8 * npairs, D_MODEL), lambda t: (t, 0))],
        out_shape=[jax.ShapeDtypeStruct((B, L, D_MODEL), F32), jax.ShapeDtypeStruct(hqs.shape, F32)],
        scratch_shapes=[pltpu.VMEM((tm, D_FF), BF16)],
        compiler_params=pltpu.CompilerParams(dimension_semantics=("arbitrary",),
                                             vmem_limit_bytes=VMEM_LIMIT_BYTES),
        name="mlp_prompt",
    )(x2, hqs, ck, cv, *consts)


def _mix_sample_kernel(x_ref, s0_ref, lb_ref, gmix_ref, win_ref, hgn_ref, lng_ref, lnb_ref, wmix_ref,
                       bcol_ref, wout_ref, gc_ref, wcq_ref, h_ref, sout_ref, vn_ref, hq_ref,
                       z_ref, ocat_ref, q_s, k_s, b_s, wm_s, bc_s, *, tm):
    x = x_ref[...].reshape(tm, D_MODEL)
    z_ref[...] = jnp.dot(_rmsnorm(x, gmix_ref[...]).astype(BF16), win_ref[...],
                         preferred_element_type=F32)
    lb_all = _lower_bound(lb_ref)
    mask = _chunk_masks(True)
    row = lax.broadcasted_iota(jnp.int32, (CHUNK, CHUNK), 0)
    col = lax.broadcasted_iota(jnp.int32, (CHUNK, CHUNK), 1)
    eye = row == col

    @pl.when(pl.program_id(0) == 0)
    def _():
        pick = (row % 4 == col).astype(BF16)
        for g in range(CM_GROUPS):
            corner = jnp.where((row < 4) & (col < 4), wmix_ref[g], 0.0)
            wm_s[g] = _bdot_nt(_bdot(pick, corner), pick)
        pos = lax.broadcasted_iota(jnp.int32, (CHUNK, CM_GROUPS), 0) % 4
        bc = jnp.zeros((CHUNK, CM_GROUPS), F32)
        for t in range(4):
            bc = jnp.where(pos == t, bcol_ref[t:t + 1, :], bc)
        bc_s[...] = bc
    r16 = lax.broadcasted_iota(jnp.int32, (SUB, HG_DK), 0)

    def gate_body(c, dmax):
        rows = pl.ds(pl.multiple_of(c * CHUNK, CHUNK), CHUNK)
        q, k, bcum = _gates(z_ref[rows, 0:HG_WIDTH], z_ref[rows, HG_WIDTH:2 * HG_WIDTH], lb_all, 4)
        q_s[rows, :] = q
        k_s[rows, :] = k
        b_s[rows, :] = bcum
        return jnp.maximum(dmax, _block_decay(bcum, True))

    dmax = lax.fori_loop(0, tm // CHUNK, gate_body, jnp.zeros((1, HG_WIDTH), F32))
    fast_ok = jnp.max(dmax) < FAST_MAX_DECAY

    def chunk_body(c, carry, *, fast):
        rows = pl.ds(pl.multiple_of(c * CHUNK, CHUNK), CHUNK)
        for h in range(HG_HEADS):
            cs = slice(h * HG_DK, (h + 1) * HG_DK)
            q, k, bcum = q_s[rows, cs], k_s[rows, cs], b_s[rows, cs]
            v = z_ref[rows, 2 * HG_WIDTH + h * HG_DK:2 * HG_WIDTH + (h + 1) * HG_DK]
            zg = z_ref[rows, 3 * HG_WIDTH + h * HG_DK:3 * HG_WIDTH + (h + 1) * HG_DK]
            a = _scores_fast(q, k, bcum, True) if fast else _scores_exact(q, k, bcum, True, mask)
            o = _bdot(jnp.where(mask, a, 0.0), v)
            eb = jnp.exp(bcum)
            qd = q * eb
            o_parts = []
            for gb in range(CHUNK // SUB):
                blk = slice(gb * SUB, (gb + 1) * SUB)
                qd_b, v_b, eb_b, k_b, b_b = qd[blk], v[blk], eb[blk], k[blk], bcum[blk]
                inter = jnp.zeros((SUB, HG_DK), F32)
                for j in range(4):
                    req = c * (CHUNK // 4) + gb * 4 + j
                    s0 = s0_ref[req, h]
                    last = 4 * j + 3
                    inter = jnp.where(r16 // 4 == j, _bdot(qd_b, s0), inter)
                    kd = jnp.where(r16 // 4 == j, k_b * jnp.exp(b_b[last:last + 1] - b_b), 0.0)
                    upd = lax.dot_general(kd.astype(BF16), v_b.astype(BF16),
                                          (((0,), (0,)), ((), ())), preferred_element_type=F32)
                    dcol = jnp.sum(jnp.where(eye, eb_b[last:last + 1], 0.0), axis=-1, keepdims=True)
                    sout_ref[req, h] = dcol * s0 + upd
                o_parts.append(inter)
            o = o + jnp.concatenate(o_parts, axis=0)
            ocat_ref[rows, cs] = _hgrn_out(o, zg, hgn_ref[:, cs])
        vn = _gmlp_chunk(z_ref, 4 * HG_WIDTH, rows, lng_ref, lnb_ref, _masked_mix_weights(wm_s, mask),
                         bc_s, ocat_ref)
        vn_ref[pl.ds(pl.multiple_of(c * (CHUNK // 4), CHUNK // 4), CHUNK // 4)] = vn.reshape(
            CHUNK // 4, 4, CM_WIDTH)
        return carry

    @pl.when(fast_ok)
    def _():
        lax.fori_loop(0, tm // CHUNK, functools.partial(chunk_body, fast=True), 0)

    @pl.when(jnp.logical_not(fast_ok))
    def _():
        lax.fori_loop(0, tm // CHUNK, functools.partial(chunk_body, fast=False), 0)

    h = x + jnp.dot(ocat_ref[...], wout_ref[...], preferred_element_type=F32)
    h_ref[0] = h
    hq_ref[...] = jnp.dot(_rmsnorm(h, gc_ref[...]).astype(BF16), wcq_ref[...], preferred_element_type=F32)


def _mix_sample(x, s0, lb_param, g_mix, w_in, hg_norm_g, ln_v_g, ln_v_b, wmix, bcol, w_out, g_cross,
                w_cq, tm):
    nreq = tm // 4
    nt = x.shape[0] // nreq
    consts = (lb_param, g_mix, w_in, hg_norm_g, ln_v_g, ln_v_b, wmix, bcol, w_out, g_cross, w_cq)
    in_specs = [
        pl.BlockSpec((nreq, 4, D_MODEL), lambda i: (i, 0, 0)),
        pl.BlockSpec((nreq, HG_HEADS, HG_DK, HG_DK), lambda i: (i, 0, 0, 0)),
    ] + [_const_spec(a.shape) for a in consts]
    out_specs = [
        pl.BlockSpec((1, tm, D_MODEL), lambda i: (i, 0, 0)),
        pl.BlockSpec((nreq, HG_HEADS, HG_DK, HG_DK), lambda i: (i, 0, 0, 0)),
        pl.BlockSpec((nreq, 4, CM_WIDTH), lambda i: (i, 0, 0)),
        pl.BlockSpec((tm, D_MODEL), lambda i: (i, 0)),
    ]
    return pl.pallas_call(
        functools.partial(_mix_sample_kernel, tm=tm),
        grid=(nt,),
        in_specs=in_specs,
        out_specs=out_specs,
        out_shape=[jax.ShapeDtypeStruct((nt, tm, D_MODEL), F32),
                   jax.ShapeDtypeStruct(s0.shape, F32),
                   jax.ShapeDtypeStruct((nt * nreq, 4, CM_WIDTH), F32),
                   jax.ShapeDtypeStruct((nt * tm, D_MODEL), F32)],
        scratch_shapes=[pltpu.VMEM((tm, IN_WIDTH), F32), pltpu.VMEM((tm, D_MODEL), BF16)]
        + [pltpu.VMEM((tm, HG_WIDTH), F32)] * 3
        + [pltpu.VMEM((CM_GROUPS, CHUNK, CHUNK), F32), pltpu.VMEM((CHUNK, CM_GROUPS), F32)],
        compiler_params=pltpu.CompilerParams(dimension_semantics=("arbitrary",),
                                             vmem_limit_bytes=VMEM_LIMIT_BYTES),
        name="mix_sample",
    )(x, s0, *consts)


def _memkv_kernel(m_ref, g_ref, wk_ref, wv_ref, k_ref, v_ref, kb_ref, vb_ref):
    mn = _rmsnorm(m_ref[...], g_ref[...]).astype(BF16)
    k = jnp.dot(mn, wk_ref[...], preferred_element_type=F32)
    v = jnp.dot(mn, wv_ref[...], preferred_element_type=F32)
    kb_ref[...] = k.astype(BF16)
    vb_ref[...] = v.astype(BF16)
    for src, dst in ((k, k_ref), (v, v_ref)):
        for r in range(dst.shape[0]):
            for h in range(X_HEADS):
                dst[r, :, h, :] = src[r * N_MEM:(r + 1) * N_MEM, h * X_HEAD_DIM:(h + 1) * X_HEAD_DIM]


def _memkv(mem, g_mem, w_ck, w_cv, tm):
    n = mem.shape[0]
    return pl.pallas_call(
        _memkv_kernel,
        grid=(n // tm,),
        in_specs=[pl.BlockSpec((tm, D_MODEL), lambda i: (i, 0)), _const_spec(g_mem.shape),
                  _const_spec(w_ck.shape), _const_spec(w_cv.shape)],
        out_specs=[pl.BlockSpec((tm // N_MEM, N_MEM, X_HEADS, X_HEAD_DIM), lambda i: (i, 0, 0, 0))] * 2
        + [pl.BlockSpec((tm, D_MODEL), lambda i: (i, 0))] * 2,
        out_shape=[jax.ShapeDtypeStruct((n // N_MEM, N_MEM, X_HEADS, X_HEAD_DIM), F32)] * 2
        + [jax.ShapeDtypeStruct((n, D_MODEL), BF16)] * 2,
        compiler_params=pltpu.CompilerParams(dimension_semantics=("arbitrary",),
                                             vmem_limit_bytes=VMEM_LIMIT_BYTES),
        name="memkv",
    )(mem, g_mem, w_ck, w_cv)


def _post_sample_kernel(x_ref, ca_ref, wco_ref, gm_ref, wup_ref, wdn_ref, gf_ref, y_ref, hn_s, acc_s):
    p = pl.program_id(0)

    @pl.when(p == 0)
    def _():
        x2 = x_ref[0] + jnp.dot(ca_ref[...].astype(BF16), wco_ref[...], preferred_element_type=F32)
        acc_s[...] = x2
        hn_s[...] = _rmsnorm(x2, gm_ref[...]).astype(BF16)

    hm = jnp.maximum(jnp.dot(hn_s[...], wup_ref[...], preferred_element_type=F32), 0.0)
    acc_s[...] += jnp.dot((hm * hm).astype(BF16), wdn_ref[...], preferred_element_type=F32)

    @pl.when(p == pl.num_programs(0) - 1)
    def _():
        y_ref[...] = _rmsnorm(acc_s[...], gf_ref[...]).reshape(y_ref.shape)


def _post_sample(x, ca, w_co, g_mlp, w_up, w_down, g_final):
    _, tm, _ = x.shape
    in_specs = [
        pl.BlockSpec((1, tm, D_MODEL), lambda p: (0, 0, 0)),
        pl.BlockSpec((tm, D_MODEL), lambda p: (0, 0)),
        _const_spec(w_co.shape), _const_spec(g_mlp.shape),
        pl.BlockSpec((D_MODEL, D_MODEL), lambda p: (0, p)),
        pl.BlockSpec((D_MODEL, D_MODEL), lambda p: (p, 0)),
        _const_spec(g_final.shape),
    ]
    return pl.pallas_call(
        _post_sample_kernel,
        grid=(D_FF // D_MODEL,),
        in_specs=in_specs,
        out_specs=pl.BlockSpec((tm // 4, 4, D_MODEL), lambda p: (0, 0, 0)),
        out_shape=jax.ShapeDtypeStruct((tm // 4, 4, D_MODEL), F32),
        scratch_shapes=[pltpu.VMEM((tm, D_MODEL), BF16), pltpu.VMEM((tm, D_MODEL), F32)],
        compiler_params=pltpu.CompilerParams(dimension_semantics=("arbitrary",),
                                             vmem_limit_bytes=VMEM_LIMIT_BYTES),
        name="post_sample",
    )(x, ca, w_co, g_mlp, w_up, w_down, g_final)


def _cast_kernel(*refs):
    n = len(refs) // 2
    for src, dst in zip(refs[:n], refs[n:]):
        dst[...] = src[...].astype(BF16)


def _cast_weights(*ws):
    steps = 8
    specs = [pl.BlockSpec((w.shape[0] // steps, w.shape[1]), lambda i: (i, 0)) for w in ws]
    return pl.pallas_call(
        _cast_kernel,
        grid=(steps,),
        in_specs=specs,
        out_specs=specs,
        out_shape=[jax.ShapeDtypeStruct(w.shape, BF16) for w in ws],
        compiler_params=pltpu.CompilerParams(dimension_semantics=("arbitrary",),
                                             vmem_limit_bytes=VMEM_LIMIT_BYTES),
        name="cast_weights",
    )(*ws)


def kernel(x_prompt, x_sample, mem_prompt, state_hgrn, cache_mem_k, cache_mem_v, lb_param, g_mix,
           w_in, hg_norm_g, ln_v_g, ln_v_b, w_s, b_s, w_out, g_cross, g_mem, w_cq, w_ck, w_cv, w_co,
           g_mlp, w_up, w_down, g_final):
    B, L, _ = x_prompt.shape
    DB, DL, _ = x_sample.shape
    assert DL == 4 and g_mix.shape[0] == 1

    row = lambda a: a.reshape(1, -1)
    win_b, wout_b, wcq_b, wco_b, wck_b, wcv_b = _cast_weights(
        w_in[0], w_out[0], w_cq[0], w_co[0], w_ck[0], w_cv[0])
    gmix, hgn, lng, lnb = row(g_mix[0]), row(hg_norm_g[0]), row(ln_v_g[0]), row(ln_v_b[0])
    gcr, gmem, gmlp, gfin = row(g_cross[0]), row(g_mem[0]), row(g_mlp[0]), row(g_final)
    wmix = w_s[0]
    bcol = b_s[0].T

    tm_s = 128
    h_s, s_s, vn_s, hq_s = _mix_sample(x_sample, state_hgrn[0], lb_param, gmix, win_b, hgn, lng, lnb, wmix,
                                       bcol, wout_b, gcr, wcq_b, tm=tm_s)

    mk, mv, mk_b, mv_b = _memkv(mem_prompt.reshape(B * N_MEM, D_MODEL), gmem, wck_b, wcv_b, tm=512)
    x2_p, s_p, wup_b, wdn_b = _mixattn_layer(x_prompt, mk_b.reshape(B, N_MEM, D_MODEL),
                                             mv_b.reshape(B, N_MEM, D_MODEL), w_up[0], w_down[0],
                                             lb_param, gmix, win_b, hgn, lng, lnb, wmix, bcol, wout_b, gcr,
                                             wcq_b, wco_b, tm=512)
    y_p, ca_s = _mlp_layer(x2_p, hq_s, cache_mem_k[0], cache_mem_v[0], gmlp, wup_b, wdn_b, gfin, tm=512)

    y_s = _post_sample(h_s.reshape(1, DB * DL, D_MODEL), ca_s, wco_b, gmlp, wup_b, wdn_b, gfin)

    return (y_p, y_s, s_p[None], s_s[None],
            mk[None], mv[None],
            vn_s[None])
```

```python
import functools

import jax
import jax.numpy as jnp
from jax import lax
from jax.experimental import pallas as pl
from jax.experimental.pallas import tpu as pltpu

F32 = jnp.float32
BF16 = jnp.bfloat16

D_MODEL = 1024
HG_WIDTH = 512
HG_HEADS = 4
HG_DK = 128
CM_WIDTH = 512
CM_GROUPS = 4
CM_GROUP_DIM = 128
IN_WIDTH = 4 * HG_WIDTH + 2 * CM_WIDTH
N_MEM = 256
X_HEADS = 4
X_HEAD_DIM = 256
D_FF = 4096
EPS = 1e-6

CHUNK = 128
SUBLANES = 8
SUB = 2 * SUBLANES
VMEM_LIMIT_BYTES = 62 * 1024 * 1024
LOG2E = 1.4426950408889634
FAST_BLOCK = 32
FAST_MAX_DECAY = 80.0


def _bdot(a, b):
    return jnp.dot(a.astype(BF16), b.astype(BF16), preferred_element_type=F32)


def _bdot_nt(a, b):
    return lax.dot_general(a.astype(BF16), b.astype(BF16), (((1,), (1,)), ((), ())),
                           preferred_element_type=F32)


def _rmsnorm(x, g):
    ms = jnp.mean(x * x, axis=-1, keepdims=True)
    return x * lax.rsqrt(ms + EPS) * g


def _sigmoid(x):
    return 0.5 * jnp.tanh(0.5 * x) + 0.5


def _gelu(x):
    return 0.5 * x * (1.0 + jnp.tanh(0.7978845608028654 * (x + 0.044715 * (x * x * x))))


def _softmax_rows(s):
    m = jnp.max(s, axis=-1, keepdims=True)
    e = jnp.exp(s - m)
    return e / jnp.sum(e, axis=-1, keepdims=True)


def _seg_cumsum(x, seg):
    n = x.shape[0]
    pos = lax.broadcasted_iota(jnp.int32, x.shape, 0) % seg
    s = 1
    while s < min(seg, SUBLANES):
        x = x + jnp.where(pos >= s, pltpu.roll(x, s, 0), 0.0)
        s *= 2
    while s < seg:
        parts = []
        for r0 in range(0, n, seg):
            parts.append(x[r0:r0 + s])
            parts.append(x[r0 + s:r0 + seg] + x[r0:r0 + seg - s])
        x = jnp.concatenate(parts, axis=0)
        s *= 2
    return x


def _chunk_cumsum(x):
    n = x.shape[0]
    tril = (lax.broadcasted_iota(jnp.int32, (n, n), 0)
            >= lax.broadcasted_iota(jnp.int32, (n, n), 1)).astype(BF16)
    hi = x.astype(BF16)
    r1 = x - hi.astype(F32)
    mid = r1.astype(BF16)
    lo = (r1 - mid.astype(F32)).astype(BF16)
    return (jnp.dot(tril, hi, preferred_element_type=F32) + jnp.dot(tril, mid, preferred_element_type=F32)
            + jnp.dot(tril, lo, preferred_element_type=F32))


def _lower_bound(lb_ref):
    lbp = lb_ref[...]
    lbe = jnp.exp(lbp - jnp.max(lbp, axis=0, keepdims=True))
    return lbe[0:1] / jnp.sum(lbe, axis=0, keepdims=True)


def _gates(zq, zf, lb, seg):
    half = 0.5 - 0.5 * lb
    hth = half * jnp.tanh(0.5 * zf)
    logf = jnp.log((lb + half) + hth)
    bcum = _chunk_cumsum(logf) if seg == logf.shape[0] else _seg_cumsum(logf, seg)
    hq = 0.5 * zq
    return hq * jnp.tanh(hq) + hq, half - hth, bcum


def _block_decay(bcum, sample):
    if sample:
        return jnp.max(-bcum, axis=0, keepdims=True)
    d = -bcum[FAST_BLOCK - 1:FAST_BLOCK]
    for i in range(1, CHUNK // FAST_BLOCK):
        n0, n1 = i * FAST_BLOCK, (i + 1) * FAST_BLOCK
        d = jnp.maximum(d, bcum[n0 - 1:n0] - bcum[n1 - 1:n1])
    return d


def _chunk_masks(sample):
    row = lax.broadcasted_iota(jnp.int32, (CHUNK, CHUNK), 0)
    col = lax.broadcasted_iota(jnp.int32, (CHUNK, CHUNK), 1)
    if sample:
        return (row // 4 == col // 4) & (row >= col)
    return row >= col


def _scores_fast(q, k, bcum, sample):
    if sample:
        return _bdot_nt(q * jnp.exp(bcum), k * jnp.exp(-bcum))
    a_rows = []
    zero = jnp.zeros((1, HG_DK), F32)
    kt, prev_ref = None, zero
    for i in range(CHUNK // FAST_BLOCK):
        n0, n1 = i * FAST_BLOCK, (i + 1) * FAST_BLOCK
        ref_b = bcum[n0 - 1:n0] if i else zero
        qt = q[n0:n1] * jnp.exp(bcum[n0:n1] - ref_b)
        kt_blk = k[n0:n1] * jnp.exp(ref_b - bcum[n0:n1])
        kt = kt_blk if kt is None else jnp.concatenate([kt * jnp.exp(ref_b - prev_ref), kt_blk], axis=0)
        prev_ref = ref_b
        kt_b = kt.astype(BF16)
        if n1 < CHUNK:
            kt_b = jnp.concatenate([kt_b, jnp.zeros((CHUNK - n1, HG_DK), BF16)], axis=0)
        a_rows.append(_bdot_nt(qt, kt_b))
    return jnp.concatenate(a_rows, axis=0)


def _scores_exact(q, k, bcum, sample, mask):
    col8 = lax.broadcasted_iota(jnp.int32, (SUBLANES, CHUNK), 1)
    b2 = bcum * LOG2E
    cexp = b2 - jnp.log2(k)
    a_rows = []
    for gb in range(CHUNK // SUB):
        lo = slice(gb * SUB, gb * SUB + SUBLANES)
        hi = slice(gb * SUB + SUBLANES, (gb + 1) * SUB)
        a_lo = jnp.zeros((SUBLANES, CHUNK), F32)
        a_hi = jnp.zeros((SUBLANES, CHUNK), F32)
        for s in range(SUB):
            sg = gb * SUB + s
            c_s = cexp[sg:sg + 1]
            if s < SUBLANES:
                p = q[lo] * jnp.exp2(b2[lo] - c_s)
                a_lo = jnp.where(col8 == sg, jnp.sum(p, axis=-1, keepdims=True), a_lo)
            if (not sample) or s >= SUBLANES:
                p = q[hi] * jnp.exp2(b2[hi] - c_s)
                a_hi = jnp.where(col8 == sg, jnp.sum(p, axis=-1, keepdims=True), a_hi)
        a_rows.append(a_lo)
        a_rows.append(a_hi)
    a = jnp.concatenate(a_rows, axis=0)
    if sample:
        return a
    o_rows = [jnp.zeros((SUB, CHUNK), F32)]
    for i in range(1, CHUNK // SUB):
        n = i * SUB
        ref_b = bcum[n - 1:n]
        qt = q[n:n + SUB] * jnp.exp(bcum[n:n + SUB] - ref_b)
        kt = k[:n] * jnp.exp(ref_b - bcum[:n])
        kt = jnp.concatenate([kt, jnp.zeros((CHUNK - n, HG_DK), F32)], axis=0)
        o_rows.append(_bdot_nt(qt, kt))
    return jnp.where(mask, a, 0.0) + jnp.concatenate(o_rows, axis=0)


def _hgrn_out(o, zg, g):
    o = o * lax.rsqrt(jnp.mean(o * o, axis=-1, keepdims=True) + EPS) * g
    return (o * _sigmoid(zg)).astype(BF16)


def _masked_mix_weights(wmix_ref, mask):
    return [jnp.where(mask, wmix_ref[g], 0.0).astype(BF16) for g in range(CM_GROUPS)]


def _gmlp_chunk(z_ref, c0, rows, lng_ref, lnb_ref, wms, bcol_ref, ocat_ref):
    u = _gelu(z_ref[rows, c0:c0 + CM_WIDTH])
    gv = _gelu(z_ref[rows, c0 + CM_WIDTH:c0 + 2 * CM_WIDTH])
    mu = jnp.mean(gv, axis=-1, keepdims=True)
    dv = gv - mu
    var = jnp.mean(dv * dv, axis=-1, keepdims=True)
    vn = dv * lax.rsqrt(var + EPS) * lng_ref[...] + lnb_ref[...]
    for g in range(CM_GROUPS):
        gs = slice(g * CM_GROUP_DIM, (g + 1) * CM_GROUP_DIM)
        mixed = _bdot(wms[g], vn[:, gs]) + bcol_ref[:, g:g + 1]
        ocat_ref[rows, HG_WIDTH + g * CM_GROUP_DIM:HG_WIDTH + (g + 1) * CM_GROUP_DIM] = (
            u[:, gs] * mixed).astype(BF16)
    return vn


def _sample_pair_probs(hq8, ck_ref):
    nrow = 8 * X_HEADS
    rh = lax.broadcasted_iota(jnp.int32, (nrow, N_MEM * X_HEADS), 0) // 8
    ch = lax.broadcasted_iota(jnp.int32, (nrow, N_MEM * X_HEADS), 1) % X_HEADS
    q = jnp.concatenate([hq8[:, h * X_HEAD_DIM:(h + 1) * X_HEAD_DIM] for h in range(X_HEADS)],
                        axis=0).astype(BF16)
    probs = []
    for r in range(2):
        k2 = ck_ref[r].reshape(N_MEM * X_HEADS, X_HEAD_DIM)
        s = jnp.where(rh == ch, _bdot_nt(q, k2) * (X_HEAD_DIM ** -0.5), -jnp.inf)
        probs.append(_softmax_rows(s))
    return probs


def _sample_pair_context(probs, cv_ref, ca_ref):
    first = (lax.broadcasted_iota(jnp.int32, (8 * X_HEADS, X_HEAD_DIM), 0) % 8) < 4
    outs = [_bdot(probs[r], cv_ref[r].reshape(N_MEM * X_HEADS, X_HEAD_DIM)) for r in range(2)]
    o = jnp.where(first, outs[0], outs[1])
    for h in range(X_HEADS):
        ca_ref[:, h * X_HEAD_DIM:(h + 1) * X_HEAD_DIM] = o[8 * h:8 * h + 8]


def _mlp(hn_bf16, wup_ref, wdn_ref, hm_s):
    for c in range(D_FF // D_MODEL):
        fs = slice(c * D_MODEL, (c + 1) * D_MODEL)
        hm = jnp.maximum(jnp.dot(hn_bf16, wup_ref[:, fs], preferred_element_type=F32), 0.0)
        hm_s[:, fs] = (hm * hm).astype(BF16)
    return jnp.dot(hm_s[...], wdn_ref[...], preferred_element_type=F32)


def _const_spec(shape):
    nd = len(shape)
    return pl.BlockSpec(shape, lambda *_: (0,) * nd, pipeline_mode=pl.Buffered(1))


def _mixattn_kernel(x_ref, mk_ref, mv_ref, hqs_ref, ck_ref, cv_ref, lb_ref, gmix_ref, win_ref, hgn_ref,
                    lng_ref, lnb_ref, wmix_ref, bcol_ref, wout_ref, gc_ref, wcq_ref, wco_ref,
                    x2_ref, sout_ref, cas_ref,
                    zg_ref, z_ref, ocat_ref, q_s, k_s, b_s, a_s, st0, st1, st2, st3, h1_s, h1n_s, ca_s,
                    *, tm, nt, npairs):
    t = pl.program_id(0)
    i = lax.rem(t, nt)
    nch = tm // CHUNK
    nj = nch * HG_HEADS
    mask = _chunk_masks(False)
    st_refs = (st0, st1, st2, st3)

    @pl.when(t == 0)
    def _():
        h1_s[...] = jnp.zeros_like(h1_s)
        h1n_s[...] = jnp.zeros_like(h1n_s)

    @pl.when(i == 0)
    def _():
        for st in st_refs:
            st[...] = jnp.zeros_like(st)

    x = x_ref[0]
    xn = _rmsnorm(x, gmix_ref[...]).astype(BF16)
    ngate = 2 * HG_WIDTH
    zg_ref[...] = jnp.dot(xn, win_ref[:, 0:ngate], preferred_element_type=F32)
    z_ref[...] = jnp.dot(xn, win_ref[:, ngate:IN_WIDTH], preferred_element_type=F32)
    lb_all = _lower_bound(lb_ref)
    dmax = jnp.zeros((1, HG_WIDTH), F32)
    for c in range(nch):
        rows = slice(c * CHUNK, (c + 1) * CHUNK)
        q, k, bcum = _gates(zg_ref[rows, 0:HG_WIDTH], zg_ref[rows, HG_WIDTH:2 * HG_WIDTH], lb_all, CHUNK)
        q_s[rows, :] = q
        k_s[rows, :] = k
        b_s[rows, :] = bcum
        dmax = jnp.maximum(dmax, _block_decay(bcum, False))
    fast_ok = jnp.max(dmax) < FAST_MAX_DECAY

    def fill_scores(j, fast):
        c, h = divmod(j, HG_HEADS)
        rows = slice(c * CHUNK, (c + 1) * CHUNK)
        cs = slice(h * HG_DK, (h + 1) * HG_DK)
        q, k, bcum = q_s[rows, cs], k_s[rows, cs], b_s[rows, cs]
        a = _scores_fast(q, k, bcum, False) if fast else _scores_exact(q, k, bcum, False, mask)
        a_s[j] = jnp.where(mask, a, 0.0).astype(BF16)

    def head_chunk(j, wms):
        c, h = divmod(j, HG_HEADS)
        rows = slice(c * CHUNK, (c + 1) * CHUNK)
        cs = slice(h * HG_DK, (h + 1) * HG_DK)
        q, k, bcum = q_s[rows, cs], k_s[rows, cs], b_s[rows, cs]
        v = z_ref[rows, h * HG_DK:(h + 1) * HG_DK]
        zg = z_ref[rows, HG_WIDTH + h * HG_DK:HG_WIDTH + (h + 1) * HG_DK]
        eb = jnp.exp(bcum)
        kd = k * jnp.exp(bcum[CHUNK - 1:CHUNK] - bcum)
        st = st_refs[h][...]
        o = jnp.dot(a_s[j], v.astype(BF16), preferred_element_type=F32)
        o = o + _bdot_nt(q * eb, st)
        st_refs[h][...] = st * eb[CHUNK - 1:CHUNK] + _bdot(v.T, kd)
        ocat_ref[rows, cs] = _hgrn_out(o, zg, hgn_ref[:, cs])
        if h == HG_HEADS - 1:
            _gmlp_chunk(z_ref, 2 * HG_WIDTH, rows, lng_ref, lnb_ref, wms, bcol_ref, ocat_ref)

    def region2(fast):
        wms = _masked_mix_weights(wmix_ref, mask)
        scale = X_HEAD_DIM ** -0.5
        heads = [slice(h * X_HEAD_DIM, (h + 1) * X_HEAD_DIM) for h in range(X_HEADS)]
        nq = nj // X_HEADS
        h1n = h1n_s[...]
        hq = []
        for h, hs in enumerate(heads):
            hq.append(jnp.dot(h1n, wcq_ref[:, hs], preferred_element_type=F32))
            for j in range(h * nq, (h + 1) * nq):
                fill_scores(j, fast)
        probs = [_softmax_rows(_bdot_nt(hq[h], mk_ref[0, :, hs]) * scale) for h, hs in enumerate(heads)]
        sprobs = [_sample_pair_probs(hqs_ref[8 * p:8 * p + 8, :], ck_ref.at[2 * p:2 * p + 2])
                  for p in range(npairs)]
        for j in range(0, nq):
            head_chunk(j, wms)
        for h, hs in enumerate(heads):
            ca_s[:, hs] = _bdot(probs[h], mv_ref[0, :, hs]).astype(BF16)
        ca = ca_s[...]
        for h, hs in enumerate(heads):
            x2_ref[0, :, hs] = h1_s[:, hs] + jnp.dot(ca, wco_ref[:, hs], preferred_element_type=F32)
            for j in range((h + 1) * nq, min(nj, (h + 2) * nq)):
                head_chunk(j, wms)
        for p in range(npairs):
            _sample_pair_context(sprobs[p], cv_ref.at[2 * p:2 * p + 2], cas_ref.at[8 * p:8 * p + 8, :])
        h1 = x_ref[0] + jnp.dot(ocat_ref[...], wout_ref[...], preferred_element_type=F32)
        h1_s[...] = h1
        h1n_s[...] = _rmsnorm(h1, gc_ref[...]).astype(BF16)

    @pl.when(fast_ok)
    def _():
        region2(True)

    @pl.when(jnp.logical_not(fast_ok))
    def _():
        region2(False)

    @pl.when(i == nt - 1)
    def _():
        for h in range(HG_HEADS):
            sout_ref[0, h] = st_refs[h][...].T


def _mixattn_layer(x, mk, mv, hqs, ck, cv, lb_param, g_mix, w_in, hg_norm_g, ln_v_g, ln_v_b, wmix, bcol,
                   w_out, g_cross, w_cq, w_co, tm):
    B, L, _ = x.shape
    nt = L // tm
    T = B * nt
    npairs = ck.shape[0] // (2 * T)
    assert npairs * 2 * T == ck.shape[0]

    def cur(t):
        return jnp.minimum(t, T - 1)

    def prev(t):
        return jnp.maximum(t - 1, 0)

    consts = (lb_param, g_mix, w_in, hg_norm_g, ln_v_g, ln_v_b, wmix, bcol, w_out, g_cross, w_cq, w_co)
    kv_spec = pl.BlockSpec((2 * npairs, N_MEM, X_HEADS, X_HEAD_DIM), lambda t: (cur(t), 0, 0, 0))
    in_specs = [
        pl.BlockSpec((1, tm, D_MODEL), lambda t: (cur(t) // nt, cur(t) % nt, 0)),
        pl.BlockSpec((1, N_MEM, D_MODEL), lambda t: (prev(t) // nt, 0, 0)),
        pl.BlockSpec((1, N_MEM, D_MODEL), lambda t: (prev(t) // nt, 0, 0)),
        pl.BlockSpec((8 * npairs, D_MODEL), lambda t: (cur(t), 0)), kv_spec, kv_spec,
    ] + [_const_spec(a.shape) for a in consts]
    out_specs = [
        pl.BlockSpec((1, tm, D_MODEL), lambda t: (prev(t) // nt, prev(t) % nt, 0)),
        pl.BlockSpec((1, HG_HEADS, HG_DK, HG_DK), lambda t: (cur(t) // nt, 0, 0, 0)),
        pl.BlockSpec((8 * npairs, D_MODEL), lambda t: (cur(t), 0)),
    ]
    scratch = [
        pltpu.VMEM((tm, 2 * HG_WIDTH), F32),
        pltpu.VMEM((tm, IN_WIDTH - 2 * HG_WIDTH), F32),
        pltpu.VMEM((tm, D_MODEL), BF16),
        pltpu.VMEM((tm, HG_WIDTH), F32), pltpu.VMEM((tm, HG_WIDTH), F32),
        pltpu.VMEM((tm, HG_WIDTH), F32),
        pltpu.VMEM((tm // CHUNK * HG_HEADS, CHUNK, CHUNK), BF16),
        pltpu.VMEM((HG_DK, HG_DK), F32), pltpu.VMEM((HG_DK, HG_DK), F32),
        pltpu.VMEM((HG_DK, HG_DK), F32), pltpu.VMEM((HG_DK, HG_DK), F32),
        pltpu.VMEM((tm, D_MODEL), F32),
        pltpu.VMEM((tm, D_MODEL), BF16),
        pltpu.VMEM((tm, D_MODEL), BF16),
    ]
    return pl.pallas_call(
        functools.partial(_mixattn_kernel, tm=tm, nt=nt, npairs=npairs),
        grid=(T + 1,),
        in_specs=in_specs,
        out_specs=out_specs,
        out_shape=[jax.ShapeDtypeStruct((B, L, D_MODEL), F32),
                   jax.ShapeDtypeStruct((B, HG_HEADS, HG_DK, HG_DK), F32),
                   jax.ShapeDtypeStruct(hqs.shape, F32)],
        scratch_shapes=scratch,
        compiler_params=pltpu.CompilerParams(dimension_semantics=("arbitrary",),
                                             vmem_limit_bytes=VMEM_LIMIT_BYTES),
        name="mix_attn",
    )(x, mk, mv, hqs, ck, cv, *consts)


def _mlp_kernel(x2_ref, gm_ref, wup_ref, wdn_ref, gf_ref, y_ref, hm_s):
    x2 = x2_ref[0]
    y_ref[0] = _rmsnorm(x2 + _mlp(_rmsnorm(x2, gm_ref[...]).astype(BF16), wup_ref, wdn_ref, hm_s),
                        gf_ref[...])


def _mlp_layer(x2, g_mlp, w_up, w_down, g_final, tm):
    B, L, _ = x2.shape
    nt = L // tm
    consts = (g_mlp, w_up, w_down, g_final)
    tile = pl.BlockSpec((1, tm, D_MODEL), lambda t: (t // nt, t % nt, 0))
    return pl.pallas_call(
        _mlp_kernel,
        grid=(B * nt,),
        in_specs=[tile] + [_const_spec(a.shape) for a in consts],
        out_specs=tile,
        out_shape=jax.ShapeDtypeStruct((B, L, D_MODEL), F32),
        scratch_shapes=[pltpu.VMEM((tm, D_FF), BF16)],
        compiler_params=pltpu.CompilerParams(dimension_semantics=("arbitrary",),
                                             vmem_limit_bytes=VMEM_LIMIT_BYTES),
        name="mlp_prompt",
    )(x2, *consts)


def _mix_sample_kernel(x_ref, s0_ref, lb_ref, gmix_ref, win_ref, hgn_ref, lng_ref, lnb_ref, wmix_ref,
                       bcol_ref, wout_ref, gc_ref, wcq_ref, h_ref, sout_ref, vn_ref, hq_ref,
                       z_ref, ocat_ref, q_s, k_s, b_s, wm_s, bc_s, *, tm):
    x = x_ref[...].reshape(tm, D_MODEL)
    z_ref[...] = jnp.dot(_rmsnorm(x, gmix_ref[...]).astype(BF16), win_ref[...],
                         preferred_element_type=F32)
    lb_all = _lower_bound(lb_ref)
    mask = _chunk_masks(True)
    row = lax.broadcasted_iota(jnp.int32, (CHUNK, CHUNK), 0)
    col = lax.broadcasted_iota(jnp.int32, (CHUNK, CHUNK), 1)
    eye = row == col

    @pl.when(pl.program_id(0) == 0)
    def _():
        pick = (row % 4 == col).astype(BF16)
        for g in range(CM_GROUPS):
            corner = jnp.where((row < 4) & (col < 4), wmix_ref[g], 0.0)
            wm_s[g] = _bdot_nt(_bdot(pick, corner), pick)
        pos = lax.broadcasted_iota(jnp.int32, (CHUNK, CM_GROUPS), 0) % 4
        bc = jnp.zeros((CHUNK, CM_GROUPS), F32)
        for t in range(4):
            bc = jnp.where(pos == t, bcol_ref[t:t + 1, :], bc)
        bc_s[...] = bc
    r16 = lax.broadcasted_iota(jnp.int32, (SUB, HG_DK), 0)

    def gate_body(c, dmax):
        rows = pl.ds(pl.multiple_of(c * CHUNK, CHUNK), CHUNK)
        q, k, bcum = _gates(z_ref[rows, 0:HG_WIDTH], z_ref[rows, HG_WIDTH:2 * HG_WIDTH], lb_all, 4)
        q_s[rows, :] = q
        k_s[rows, :] = k
        b_s[rows, :] = bcum
        return jnp.maximum(dmax, _block_decay(bcum, True))

    dmax = lax.fori_loop(0, tm // CHUNK, gate_body, jnp.zeros((1, HG_WIDTH), F32))
    fast_ok = jnp.max(dmax) < FAST_MAX_DECAY

    def chunk_body(c, carry, *, fast):
        rows = pl.ds(pl.multiple_of(c * CHUNK, CHUNK), CHUNK)
        for h in range(HG_HEADS):
            cs = slice(h * HG_DK, (h + 1) * HG_DK)
            q, k, bcum = q_s[rows, cs], k_s[rows, cs], b_s[rows, cs]
            v = z_ref[rows, 2 * HG_WIDTH + h * HG_DK:2 * HG_WIDTH + (h + 1) * HG_DK]
            zg = z_ref[rows, 3 * HG_WIDTH + h * HG_DK:3 * HG_WIDTH + (h + 1) * HG_DK]
            a = _scores_fast(q, k, bcum, True) if fast else _scores_exact(q, k, bcum, True, mask)
            o = _bdot(jnp.where(mask, a, 0.0), v)
            eb = jnp.exp(bcum)
            qd = q * eb
            o_parts = []
            for gb in range(CHUNK // SUB):
                blk = slice(gb * SUB, (gb + 1) * SUB)
                qd_b, v_b, eb_b, k_b, b_b = qd[blk], v[blk], eb[blk], k[blk], bcum[blk]
                inter = jnp.zeros((SUB, HG_DK), F32)
                for j in range(4):
                    req = c * (CHUNK // 4) + gb * 4 + j
                    s0 = s0_ref[req, h]
                    last = 4 * j + 3
                    inter = jnp.where(r16 // 4 == j, _bdot(qd_b, s0), inter)
                    kd = jnp.where(r16 // 4 == j, k_b * jnp.exp(b_b[last:last + 1] - b_b), 0.0)
                    upd = lax.dot_general(kd.astype(BF16), v_b.astype(BF16),
                                          (((0,), (0,)), ((), ())), preferred_element_type=F32)
                    dcol = jnp.sum(jnp.where(eye, eb_b[last:last + 1], 0.0), axis=-1, keepdims=True)
                    sout_ref[req, h] = dcol * s0 + upd
                o_parts.append(inter)
            o = o + jnp.concatenate(o_parts, axis=0)
            ocat_ref[rows, cs] = _hgrn_out(o, zg, hgn_ref[:, cs])
        vn = _gmlp_chunk(z_ref, 4 * HG_WIDTH, rows, lng_ref, lnb_ref, _masked_mix_weights(wm_s, mask),
                         bc_s, ocat_ref)
        vn_ref[pl.ds(pl.multiple_of(c * (CHUNK // 4), CHUNK // 4), CHUNK // 4)] = vn.reshape(
            CHUNK // 4, 4, CM_WIDTH)
        return carry

    @pl.when(fast_ok)
    def _():
        lax.fori_loop(0, tm // CHUNK, functools.partial(chunk_body, fast=True), 0)

    @pl.when(jnp.logical_not(fast_ok))
    def _():
        lax.fori_loop(0, tm // CHUNK, functools.partial(chunk_body, fast=False), 0)

    h = x + jnp.dot(ocat_ref[...], wout_ref[...], preferred_element_type=F32)
    h_ref[0] = h
    hq_ref[...] = jnp.dot(_rmsnorm(h, gc_ref[...]).astype(BF16), wcq_ref[...], preferred_element_type=F32)


def _mix_sample(x, s0, lb_param, g_mix, w_in, hg_norm_g, ln_v_g, ln_v_b, wmix, bcol, w_out, g_cross,
                w_cq, tm):
    nreq = tm // 4
    nt = x.shape[0] // nreq
    consts = (lb_param, g_mix, w_in, hg_norm_g, ln_v_g, ln_v_b, wmix, bcol, w_out, g_cross, w_cq)
    in_specs = [
        pl.BlockSpec((nreq, 4, D_MODEL), lambda i: (i, 0, 0)),
        pl.BlockSpec((nreq, HG_HEADS, HG_DK, HG_DK), lambda i: (i, 0, 0, 0)),
    ] + [_const_spec(a.shape) for a in consts]
    out_specs = [
        pl.BlockSpec((1, tm, D_MODEL), lambda i: (i, 0, 0)),
        pl.BlockSpec((nreq, HG_HEADS, HG_DK, HG_DK), lambda i: (i, 0, 0, 0)),
        pl.BlockSpec((nreq, 4, CM_WIDTH), lambda i: (i, 0, 0)),
        pl.BlockSpec((tm, D_MODEL), lambda i: (i, 0)),
    ]
    return pl.pallas_call(
        functools.partial(_mix_sample_kernel, tm=tm),
        grid=(nt,),
        in_specs=in_specs,
        out_specs=out_specs,
        out_shape=[jax.ShapeDtypeStruct((nt, tm, D_MODEL), F32),
                   jax.ShapeDtypeStruct(s0.shape, F32),
                   jax.ShapeDtypeStruct((nt * nreq, 4, CM_WIDTH), F32),
                   jax.ShapeDtypeStruct((nt * tm, D_MODEL), F32)],
        scratch_shapes=[pltpu.VMEM((tm, IN_WIDTH), F32), pltpu.VMEM((tm, D_MODEL), BF16)]
        + [pltpu.VMEM((tm, HG_WIDTH), F32)] * 3
        + [pltpu.VMEM((CM_GROUPS, CHUNK, CHUNK), F32), pltpu.VMEM((CHUNK, CM_GROUPS), F32)],
        compiler_params=pltpu.CompilerParams(dimension_semantics=("arbitrary",),
                                             vmem_limit_bytes=VMEM_LIMIT_BYTES),
        name="mix_sample",
    )(x, s0, *consts)


def _memkv_kernel(m_ref, g_ref, wk_ref, wv_ref, k_ref, v_ref, kb_ref, vb_ref):
    mn = _rmsnorm(m_ref[...], g_ref[...]).astype(BF16)
    k = jnp.dot(mn, wk_ref[...], preferred_element_type=F32)
    v = jnp.dot(mn, wv_ref[...], preferred_element_type=F32)
    kb_ref[...] = k.astype(BF16)
    vb_ref[...] = v.astype(BF16)
    for src, dst in ((k, k_ref), (v, v_ref)):
        for r in range(dst.shape[0]):
            for h in range(X_HEADS):
                dst[r, :, h, :] = src[r * N_MEM:(r + 1) * N_MEM, h * X_HEAD_DIM:(h + 1) * X_HEAD_DIM]


def _memkv(mem, g_mem, w_ck, w_cv, tm):
    n = mem.shape[0]
    return pl.pallas_call(
        _memkv_kernel,
        grid=(n // tm,),
        in_specs=[pl.BlockSpec((tm, D_MODEL), lambda i: (i, 0)), _const_spec(g_mem.shape),
                  _const_spec(w_ck.shape), _const_spec(w_cv.shape)],
        out_specs=[pl.BlockSpec((tm // N_MEM, N_MEM, X_HEADS, X_HEAD_DIM), lambda i: (i, 0, 0, 0))] * 2
        + [pl.BlockSpec((tm, D_MODEL), lambda i: (i, 0))] * 2,
        out_shape=[jax.ShapeDtypeStruct((n // N_MEM, N_MEM, X_HEADS, X_HEAD_DIM), F32)] * 2
        + [jax.ShapeDtypeStruct((n, D_MODEL), BF16)] * 2,
        compiler_params=pltpu.CompilerParams(dimension_semantics=("arbitrary",),
                                             vmem_limit_bytes=VMEM_LIMIT_BYTES),
        name="memkv",
    )(mem, g_mem, w_ck, w_cv)


def _post_sample_kernel(x_ref, ca_ref, wco_ref, gm_ref, wup_ref, wdn_ref, gf_ref, y_ref, hn_s, acc_s):
    p = pl.program_id(0)

    @pl.when(p == 0)
    def _():
        x2 = x_ref[0] + jnp.dot(ca_ref[...].astype(BF16), wco_ref[...], preferred_element_type=F32)
        acc_s[...] = x2
        hn_s[...] = _rmsnorm(x2, gm_ref[...]).astype(BF16)

    hm = jnp.maximum(jnp.dot(hn_s[...], wup_ref[...], preferred_element_type=F32), 0.0)
    acc_s[...] += jnp.dot((hm * hm).astype(BF16), wdn_ref[...], preferred_element_type=F32)

    @pl.when(p == pl.num_programs(0) - 1)
    def _():
        y_ref[...] = _rmsnorm(acc_s[...], gf_ref[...]).reshape(y_ref.shape)


def _post_sample(x, ca, w_co, g_mlp, w_up, w_down, g_final):
    _, tm, _ = x.shape
    in_specs = [
        pl.BlockSpec((1, tm, D_MODEL), lambda p: (0, 0, 0)),
        pl.BlockSpec((tm, D_MODEL), lambda p: (0, 0)),
        _const_spec(w_co.shape), _const_spec(g_mlp.shape),
        pl.BlockSpec((D_MODEL, D_MODEL), lambda p: (0, p)),
        pl.BlockSpec((D_MODEL, D_MODEL), lambda p: (p, 0)),
        _const_spec(g_final.shape),
    ]
    return pl.pallas_call(
        _post_sample_kernel,
        grid=(D_FF // D_MODEL,),
        in_specs=in_specs,
        out_specs=pl.BlockSpec((tm // 4, 4, D_MODEL), lambda p: (0, 0, 0)),
        out_shape=jax.ShapeDtypeStruct((tm // 4, 4, D_MODEL), F32),
        scratch_shapes=[pltpu.VMEM((tm, D_MODEL), BF16), pltpu.VMEM((tm, D_MODEL), F32)],
        compiler_params=pltpu.CompilerParams(dimension_semantics=("arbitrary",),
                                             vmem_limit_bytes=VMEM_LIMIT_BYTES),
        name="post_sample",
    )(x, ca, w_co, g_mlp, w_up, w_down, g_final)


def _cast_kernel(*refs):
    n = len(refs) // 2
    for src, dst in zip(refs[:n], refs[n:]):
        dst[...] = src[...].astype(BF16)


def _cast_weights(*ws):
    steps = 8
    specs = [pl.BlockSpec((w.shape[0] // steps, w.shape[1]), lambda i: (i, 0)) for w in ws]
    return pl.pallas_call(
        _cast_kernel,
        grid=(steps,),
        in_specs=specs,
        out_specs=specs,
        out_shape=[jax.ShapeDtypeStruct(w.shape, BF16) for w in ws],
        compiler_params=pltpu.CompilerParams(dimension_semantics=("arbitrary",),
                                             vmem_limit_bytes=VMEM_LIMIT_BYTES),
        name="cast_weights",
    )(*ws)


def kernel(x_prompt, x_sample, mem_prompt, state_hgrn, cache_mem_k, cache_mem_v, lb_param, g_mix,
           w_in, hg_norm_g, ln_v_g, ln_v_b, w_s, b_s, w_out, g_cross, g_mem, w_cq, w_ck, w_cv, w_co,
           g_mlp, w_up, w_down, g_final):
    B, L, _ = x_prompt.shape
    DB, DL, _ = x_sample.shape
    assert DL == 4 and g_mix.shape[0] == 1

    row = lambda a: a.reshape(1, -1)
    win_b, wout_b, wcq_b, wco_b, wck_b, wcv_b, wup_b, wdn_b = _cast_weights(
        w_in[0], w_out[0], w_cq[0], w_co[0], w_ck[0], w_cv[0], w_up[0], w_down[0])
    gmix, hgn, lng, lnb = row(g_mix[0]), row(hg_norm_g[0]), row(ln_v_g[0]), row(ln_v_b[0])
    gcr, gmem, gmlp, gfin = row(g_cross[0]), row(g_mem[0]), row(g_mlp[0]), row(g_final)
    wmix = w_s[0]
    bcol = b_s[0].T

    tm_s = 128
    h_s, s_s, vn_s, hq_s = _mix_sample(x_sample, state_hgrn[0], lb_param, gmix, win_b, hgn, lng, lnb, wmix,
                                       bcol, wout_b, gcr, wcq_b, tm=tm_s)

    mk, mv, mk_b, mv_b = _memkv(mem_prompt.reshape(B * N_MEM, D_MODEL), gmem, wck_b, wcv_b, tm=512)
    x2_p, s_p, ca_s = _mixattn_layer(x_prompt, mk_b.reshape(B, N_MEM, D_MODEL),
                                     mv_b.reshape(B, N_MEM, D_MODEL), hq_s, cache_mem_k[0], cache_mem_v[0],
                                     lb_param, gmix, win_b, hgn, lng, lnb, wmix, bcol, wout_b, gcr, wcq_b,
                                     wco_b, tm=512)
    y_p = _mlp_layer(x2_p, gmlp, wup_b, wdn_b, gfin, tm=1024)

    y_s = _post_sample(h_s.reshape(1, DB * DL, D_MODEL), ca_s, wco_b, gmlp, wup_b, wdn_b, gfin)

    return (y_p, y_s, s_p[None], s_s[None],
            mk[None], mv[None],
            vn_s[None])
```

```python
import functools

import jax
import jax.numpy as jnp
from jax import lax
from jax.experimental import pallas as pl
from jax.experimental.pallas import tpu as pltpu

F32 = jnp.float32
BF16 = jnp.bfloat16

D_MODEL = 1024
HG_WIDTH = 512
HG_HEADS = 4
HG_DK = 128
CM_WIDTH = 512
CM_GROUPS = 4
CM_GROUP_DIM = 128
IN_WIDTH = 4 * HG_WIDTH + 2 * CM_WIDTH
N_MEM = 256
X_HEADS = 4
X_HEAD_DIM = 256
D_FF = 4096
EPS = 1e-6

CHUNK = 128
SUBLANES = 8
SUB = 2 * SUBLANES
VMEM_LIMIT_BYTES = 62 * 1024 * 1024
LOG2E = 1.4426950408889634
FAST_BLOCK = 32
FAST_MAX_DECAY = 80.0


def _bdot(a, b):
    return jnp.dot(a.astype(BF16), b.astype(BF16), preferred_element_type=F32)


def _bdot_nt(a, b):
    return lax.dot_general(a.astype(BF16), b.astype(BF16), (((1,), (1,)), ((), ())),
                           preferred_element_type=F32)


def _rmsnorm(x, g):
    ms = jnp.mean(x * x, axis=-1, keepdims=True)
    return x * lax.rsqrt(ms + EPS) * g


def _gelu(x):
    c = 0.7978845608028654
    return x * (0.5 + 0.5 * jnp.tanh(x * (c + (c * 0.044715) * (x * x))))


def _softmax_rows(s):
    m = jnp.max(s, axis=-1, keepdims=True)
    e = jnp.exp(s - m)
    return e * (1.0 / jnp.sum(e, axis=-1, keepdims=True))


def _seg_cumsum(x, seg):
    n = x.shape[0]
    pos = lax.broadcasted_iota(jnp.int32, x.shape, 0) % seg
    s = 1
    while s < min(seg, SUBLANES):
        x = x + jnp.where(pos >= s, pltpu.roll(x, s, 0), 0.0)
        s *= 2
    while s < seg:
        parts = []
        for r0 in range(0, n, seg):
            parts.append(x[r0:r0 + s])
            parts.append(x[r0 + s:r0 + seg] + x[r0:r0 + seg - s])
        x = jnp.concatenate(parts, axis=0)
        s *= 2
    return x


def _chunk_cumsum(x):
    n = x.shape[0]
    tril = (lax.broadcasted_iota(jnp.int32, (n, n), 0)
            >= lax.broadcasted_iota(jnp.int32, (n, n), 1)).astype(BF16)
    hi = x.astype(BF16)
    r1 = x - hi.astype(F32)
    mid = r1.astype(BF16)
    lo = (r1 - mid.astype(F32)).astype(BF16)
    return (jnp.dot(tril, hi, preferred_element_type=F32) + jnp.dot(tril, mid, preferred_element_type=F32)
            + jnp.dot(tril, lo, preferred_element_type=F32))


def _lower_bound(lb_ref):
    lbp = lb_ref[...]
    lbe = jnp.exp(lbp - jnp.max(lbp, axis=0, keepdims=True))
    return lbe[0:1] / jnp.sum(lbe, axis=0, keepdims=True)


def _gates(zq, zf, lb, seg):
    half = 0.5 - 0.5 * lb
    hth = half * jnp.tanh(0.5 * zf)
    logf = jnp.log((lb + half) + hth)
    bcum = _chunk_cumsum(logf) if seg == logf.shape[0] else _seg_cumsum(logf, seg)
    hq = 0.5 * zq
    return hq * jnp.tanh(hq) + hq, half - hth, bcum


def _block_decay(bcum, sample):
    if sample:
        return jnp.max(-bcum, axis=0, keepdims=True)
    d = -bcum[FAST_BLOCK - 1:FAST_BLOCK]
    for i in range(1, CHUNK // FAST_BLOCK):
        n0, n1 = i * FAST_BLOCK, (i + 1) * FAST_BLOCK
        d = jnp.maximum(d, bcum[n0 - 1:n0] - bcum[n1 - 1:n1])
    return d


def _chunk_masks(sample):
    row = lax.broadcasted_iota(jnp.int32, (CHUNK, CHUNK), 0)
    col = lax.broadcasted_iota(jnp.int32, (CHUNK, CHUNK), 1)
    if sample:
        return (row // 4 == col // 4) & (row >= col)
    return row >= col


def _scores_fast(q, k, bcum, sample):
    if sample:
        return _bdot_nt(q * jnp.exp(bcum), k * jnp.exp(-bcum))
    a_rows = []
    zero = jnp.zeros((1, HG_DK), F32)
    kt, prev_ref = None, zero
    for i in range(CHUNK // FAST_BLOCK):
        n0, n1 = i * FAST_BLOCK, (i + 1) * FAST_BLOCK
        ref_b = bcum[n0 - 1:n0] if i else zero
        qt = q[n0:n1] * jnp.exp(bcum[n0:n1] - ref_b)
        kt_blk = k[n0:n1] * jnp.exp(ref_b - bcum[n0:n1])
        kt = kt_blk if kt is None else jnp.concatenate([kt * jnp.exp(ref_b - prev_ref), kt_blk], axis=0)
        prev_ref = ref_b
        kt_b = kt.astype(BF16)
        if n1 < CHUNK:
            kt_b = jnp.concatenate([kt_b, jnp.zeros((CHUNK - n1, HG_DK), BF16)], axis=0)
        a_rows.append(_bdot_nt(qt, kt_b))
    return jnp.concatenate(a_rows, axis=0)


def _scores_exact(q, k, bcum, sample, mask):
    col8 = lax.broadcasted_iota(jnp.int32, (SUBLANES, CHUNK), 1)
    b2 = bcum * LOG2E
    cexp = b2 - jnp.log2(k)
    a_rows = []
    for gb in range(CHUNK // SUB):
        lo = slice(gb * SUB, gb * SUB + SUBLANES)
        hi = slice(gb * SUB + SUBLANES, (gb + 1) * SUB)
        a_lo = jnp.zeros((SUBLANES, CHUNK), F32)
        a_hi = jnp.zeros((SUBLANES, CHUNK), F32)
        for s in range(SUB):
            sg = gb * SUB + s
            c_s = cexp[sg:sg + 1]
            if s < SUBLANES:
                p = q[lo] * jnp.exp2(b2[lo] - c_s)
                a_lo = jnp.where(col8 == sg, jnp.sum(p, axis=-1, keepdims=True), a_lo)
            if (not sample) or s >= SUBLANES:
                p = q[hi] * jnp.exp2(b2[hi] - c_s)
                a_hi = jnp.where(col8 == sg, jnp.sum(p, axis=-1, keepdims=True), a_hi)
        a_rows.append(a_lo)
        a_rows.append(a_hi)
    a = jnp.concatenate(a_rows, axis=0)
    if sample:
        return a
    o_rows = [jnp.zeros((SUB, CHUNK), F32)]
    for i in range(1, CHUNK // SUB):
        n = i * SUB
        ref_b = bcum[n - 1:n]
        qt = q[n:n + SUB] * jnp.exp(bcum[n:n + SUB] - ref_b)
        kt = k[:n] * jnp.exp(ref_b - bcum[:n])
        kt = jnp.concatenate([kt, jnp.zeros((CHUNK - n, HG_DK), F32)], axis=0)
        o_rows.append(_bdot_nt(qt, kt))
    return jnp.where(mask, a, 0.0) + jnp.concatenate(o_rows, axis=0)


def _hgrn_out(o, zg, g):
    inv = lax.rsqrt(jnp.mean(o * o, axis=-1, keepdims=True) + EPS)
    return (o * inv * (0.5 * g) * (jnp.tanh(0.5 * zg) + 1.0)).astype(BF16)


def _masked_mix_weights(wmix_ref, mask):
    return [jnp.where(mask, wmix_ref[g], 0.0).astype(BF16) for g in range(CM_GROUPS)]


def _gmlp_chunk(z_ref, c0, rows, lng_ref, lnb_ref, wms, bcol_ref, ocat_ref):
    u = _gelu(z_ref[rows, c0:c0 + CM_WIDTH])
    gv = _gelu(z_ref[rows, c0 + CM_WIDTH:c0 + 2 * CM_WIDTH])
    mu = jnp.mean(gv, axis=-1, keepdims=True)
    dv = gv - mu
    var = jnp.mean(dv * dv, axis=-1, keepdims=True)
    vn = dv * lax.rsqrt(var + EPS) * lng_ref[...] + lnb_ref[...]
    for g in range(CM_GROUPS):
        gs = slice(g * CM_GROUP_DIM, (g + 1) * CM_GROUP_DIM)
        mixed = _bdot(wms[g], vn[:, gs]) + bcol_ref[:, g:g + 1]
        ocat_ref[rows, HG_WIDTH + g * CM_GROUP_DIM:HG_WIDTH + (g + 1) * CM_GROUP_DIM] = (
            u[:, gs] * mixed).astype(BF16)
    return vn


def _sample_pair_probs(hq8, ck_ref):
    nrow = 8 * X_HEADS
    rh = lax.broadcasted_iota(jnp.int32, (nrow, N_MEM * X_HEADS), 0) // 8
    ch = lax.broadcasted_iota(jnp.int32, (nrow, N_MEM * X_HEADS), 1) % X_HEADS
    q = jnp.concatenate([hq8[:, h * X_HEAD_DIM:(h + 1) * X_HEAD_DIM] for h in range(X_HEADS)],
                        axis=0).astype(BF16)
    probs = []
    for r in range(2):
        k2 = ck_ref[r].reshape(N_MEM * X_HEADS, X_HEAD_DIM)
        s = jnp.where(rh == ch, _bdot_nt(q, k2) * (X_HEAD_DIM ** -0.5), -jnp.inf)
        probs.append(_softmax_rows(s))
    return probs


def _sample_pair_context(probs, cv_ref, ca_ref):
    first = (lax.broadcasted_iota(jnp.int32, (8 * X_HEADS, X_HEAD_DIM), 0) % 8) < 4
    outs = [_bdot(probs[r], cv_ref[r].reshape(N_MEM * X_HEADS, X_HEAD_DIM)) for r in range(2)]
    o = jnp.where(first, outs[0], outs[1])
    for h in range(X_HEADS):
        ca_ref[:, h * X_HEAD_DIM:(h + 1) * X_HEAD_DIM] = o[8 * h:8 * h + 8]


def _mlp(hn_bf16, wup_ref, wdn_ref, hm_s):
    for c in range(D_FF // D_MODEL):
        fs = slice(c * D_MODEL, (c + 1) * D_MODEL)
        hm = jnp.maximum(jnp.dot(hn_bf16, wup_ref[:, fs], preferred_element_type=F32), 0.0)
        hm_s[:, fs] = (hm * hm).astype(BF16)
    return jnp.dot(hm_s[...], wdn_ref[...], preferred_element_type=F32)


def _const_spec(shape):
    nd = len(shape)
    return pl.BlockSpec(shape, lambda *_: (0,) * nd, pipeline_mode=pl.Buffered(1))


def _mixattn_kernel(x_ref, mk_ref, mv_ref, hqs_ref, ck_ref, cv_ref, lb_ref, gmix_ref, win_ref, hgn_ref,
                    lng_ref, lnb_ref, wmix_ref, bcol_ref, wout_ref, gc_ref, wcq_ref, wco_ref,
                    x2_ref, sout_ref, cas_ref,
                    zg_ref, z_ref, ocat_ref, q_s, k_s, b_s, a_s, st0, st1, st2, st3, h1_s, h1n_s, ca_s,
                    *, tm, nt, npairs):
    t = pl.program_id(0)
    i = lax.rem(t, nt)
    nch = tm // CHUNK
    nj = nch * HG_HEADS
    mask = _chunk_masks(False)
    st_refs = (st0, st1, st2, st3)

    @pl.when(t == 0)
    def _():
        h1_s[...] = jnp.zeros_like(h1_s)
        h1n_s[...] = jnp.zeros_like(h1n_s)

    @pl.when(i == 0)
    def _():
        for st in st_refs:
            st[...] = jnp.zeros_like(st)

    x = x_ref[0]
    xn = _rmsnorm(x, gmix_ref[...]).astype(BF16)
    ngate = 2 * HG_WIDTH
    zg_ref[...] = jnp.dot(xn, win_ref[:, 0:ngate], preferred_element_type=F32)
    z_ref[...] = jnp.dot(xn, win_ref[:, ngate:IN_WIDTH], preferred_element_type=F32)
    lb_all = _lower_bound(lb_ref)
    dmax = jnp.zeros((1, HG_WIDTH), F32)
    for c in range(nch):
        rows = slice(c * CHUNK, (c + 1) * CHUNK)
        q, k, bcum = _gates(zg_ref[rows, 0:HG_WIDTH], zg_ref[rows, HG_WIDTH:2 * HG_WIDTH], lb_all, CHUNK)
        q_s[rows, :] = q
        k_s[rows, :] = k
        b_s[rows, :] = bcum
        dmax = jnp.maximum(dmax, _block_decay(bcum, False))
    fast_ok = jnp.max(dmax) < FAST_MAX_DECAY

    def fill_scores(j, fast):
        c, h = divmod(j, HG_HEADS)
        rows = slice(c * CHUNK, (c + 1) * CHUNK)
        cs = slice(h * HG_DK, (h + 1) * HG_DK)
        q, k, bcum = q_s[rows, cs], k_s[rows, cs], b_s[rows, cs]
        a = _scores_fast(q, k, bcum, False) if fast else _scores_exact(q, k, bcum, False, mask)
        a_s[j] = jnp.where(mask, a, 0.0).astype(BF16)

    def head_chunk(j, wms):
        c, h = divmod(j, HG_HEADS)
        rows = slice(c * CHUNK, (c + 1) * CHUNK)
        cs = slice(h * HG_DK, (h + 1) * HG_DK)
        q, k, bcum = q_s[rows, cs], k_s[rows, cs], b_s[rows, cs]
        v = z_ref[rows, h * HG_DK:(h + 1) * HG_DK]
        zg = z_ref[rows, HG_WIDTH + h * HG_DK:HG_WIDTH + (h + 1) * HG_DK]
        eb = jnp.exp(bcum)
        kd = k * jnp.exp(bcum[CHUNK - 1:CHUNK] - bcum)
        st = st_refs[h][...]
        o = jnp.dot(a_s[j], v.astype(BF16), preferred_element_type=F32)
        o = o + _bdot_nt(q * eb, st)
        st_refs[h][...] = st * eb[CHUNK - 1:CHUNK] + _bdot(v.T, kd)
        ocat_ref[rows, cs] = _hgrn_out(o, zg, hgn_ref[:, cs])
        if h == HG_HEADS - 1:
            _gmlp_chunk(z_ref, 2 * HG_WIDTH, rows, lng_ref, lnb_ref, wms, bcol_ref, ocat_ref)

    def region2(fast):
        wms = _masked_mix_weights(wmix_ref, mask)
        scale = X_HEAD_DIM ** -0.5
        heads = [slice(h * X_HEAD_DIM, (h + 1) * X_HEAD_DIM) for h in range(X_HEADS)]
        nq = nj // X_HEADS
        h1n = h1n_s[...]
        hq = []
        for h, hs in enumerate(heads):
            hq.append(jnp.dot(h1n, wcq_ref[:, hs], preferred_element_type=F32))
            for j in range(h * nq, (h + 1) * nq):
                fill_scores(j, fast)
        probs = [_softmax_rows(_bdot_nt(hq[h], mk_ref[0, :, hs]) * scale) for h, hs in enumerate(heads)]
        sprobs = [_sample_pair_probs(hqs_ref[8 * p:8 * p + 8, :], ck_ref.at[2 * p:2 * p + 2])
                  for p in range(npairs)]
        for j in range(0, nq):
            head_chunk(j, wms)
        for h, hs in enumerate(heads):
            ca_s[:, hs] = _bdot(probs[h], mv_ref[0, :, hs]).astype(BF16)
        ca = ca_s[...]
        for h, hs in enumerate(heads):
            x2_ref[0, :, hs] = h1_s[:, hs] + jnp.dot(ca, wco_ref[:, hs], preferred_element_type=F32)
            for j in range((h + 1) * nq, min(nj, (h + 2) * nq)):
                head_chunk(j, wms)
        for p in range(npairs):
            _sample_pair_context(sprobs[p], cv_ref.at[2 * p:2 * p + 2], cas_ref.at[8 * p:8 * p + 8, :])
        h1 = x_ref[0] + jnp.dot(ocat_ref[...], wout_ref[...], preferred_element_type=F32)
        h1_s[...] = h1
        h1n_s[...] = _rmsnorm(h1, gc_ref[...]).astype(BF16)

    @pl.when(fast_ok)
    def _():
        region2(True)

    @pl.when(jnp.logical_not(fast_ok))
    def _():
        region2(False)

    @pl.when(i == nt - 1)
    def _():
        for h in range(HG_HEADS):
            sout_ref[0, h] = st_refs[h][...].T


def _mixattn_layer(x, mk, mv, hqs, ck, cv, lb_param, g_mix, w_in, hg_norm_g, ln_v_g, ln_v_b, wmix, bcol,
                   w_out, g_cross, w_cq, w_co, tm):
    B, L, _ = x.shape
    nt = L // tm
    T = B * nt
    npairs = ck.shape[0] // (2 * T)
    assert npairs * 2 * T == ck.shape[0]

    def cur(t):
        return jnp.minimum(t, T - 1)

    def prev(t):
        return jnp.maximum(t - 1, 0)

    consts = (lb_param, g_mix, w_in, hg_norm_g, ln_v_g, ln_v_b, wmix, bcol, w_out, g_cross, w_cq, w_co)
    kv_spec = pl.BlockSpec((2 * npairs, N_MEM, X_HEADS, X_HEAD_DIM), lambda t: (cur(t), 0, 0, 0))
    in_specs = [
        pl.BlockSpec((1, tm, D_MODEL), lambda t: (cur(t) // nt, cur(t) % nt, 0)),
        pl.BlockSpec((1, N_MEM, D_MODEL), lambda t: (prev(t) // nt, 0, 0)),
        pl.BlockSpec((1, N_MEM, D_MODEL), lambda t: (prev(t) // nt, 0, 0)),
        pl.BlockSpec((8 * npairs, D_MODEL), lambda t: (cur(t), 0)), kv_spec, kv_spec,
    ] + [_const_spec(a.shape) for a in consts]
    out_specs = [
        pl.BlockSpec((1, tm, D_MODEL), lambda t: (prev(t) // nt, prev(t) % nt, 0)),
        pl.BlockSpec((1, HG_HEADS, HG_DK, HG_DK), lambda t: (cur(t) // nt, 0, 0, 0)),
        pl.BlockSpec((8 * npairs, D_MODEL), lambda t: (cur(t), 0)),
    ]
    scratch = [
        pltpu.VMEM((tm, 2 * HG_WIDTH), F32),
        pltpu.VMEM((tm, IN_WIDTH - 2 * HG_WIDTH), F32),
        pltpu.VMEM((tm, D_MODEL), BF16),
        pltpu.VMEM((tm, HG_WIDTH), F32), pltpu.VMEM((tm, HG_WIDTH), F32),
        pltpu.VMEM((tm, HG_WIDTH), F32),
        pltpu.VMEM((tm // CHUNK * HG_HEADS, CHUNK, CHUNK), BF16),
        pltpu.VMEM((HG_DK, HG_DK), F32), pltpu.VMEM((HG_DK, HG_DK), F32),
        pltpu.VMEM((HG_DK, HG_DK), F32), pltpu.VMEM((HG_DK, HG_DK), F32),
        pltpu.VMEM((tm, D_MODEL), F32),
        pltpu.VMEM((tm, D_MODEL), BF16),
        pltpu.VMEM((tm, D_MODEL), BF16),
    ]
    return pl.pallas_call(
        functools.partial(_mixattn_kernel, tm=tm, nt=nt, npairs=npairs),
        grid=(T + 1,),
        in_specs=in_specs,
        out_specs=out_specs,
        out_shape=[jax.ShapeDtypeStruct((B, L, D_MODEL), F32),
                   jax.ShapeDtypeStruct((B, HG_HEADS, HG_DK, HG_DK), F32),
                   jax.ShapeDtypeStruct(hqs.shape, F32)],
        scratch_shapes=scratch,
        compiler_params=pltpu.CompilerParams(dimension_semantics=("arbitrary",),
                                             vmem_limit_bytes=VMEM_LIMIT_BYTES),
        name="mix_attn",
    )(x, mk, mv, hqs, ck, cv, *consts)


def _mlp_kernel(x2_ref, gm_ref, wup_ref, wdn_ref, gf_ref, y_ref, hm_s):
    x2 = x2_ref[0]
    y_ref[0] = _rmsnorm(x2 + _mlp(_rmsnorm(x2, gm_ref[...]).astype(BF16), wup_ref, wdn_ref, hm_s),
                        gf_ref[...])


def _mlp_layer(x2, g_mlp, w_up, w_down, g_final, tm):
    B, L, _ = x2.shape
    nt = L // tm
    consts = (g_mlp, w_up, w_down, g_final)
    tile = pl.BlockSpec((1, tm, D_MODEL), lambda t: (t // nt, t % nt, 0))
    return pl.pallas_call(
        _mlp_kernel,
        grid=(B * nt,),
        in_specs=[tile] + [_const_spec(a.shape) for a in consts],
        out_specs=tile,
        out_shape=jax.ShapeDtypeStruct((B, L, D_MODEL), F32),
        scratch_shapes=[pltpu.VMEM((tm, D_FF), BF16)],
        compiler_params=pltpu.CompilerParams(dimension_semantics=("arbitrary",),
                                             vmem_limit_bytes=VMEM_LIMIT_BYTES),
        name="mlp_prompt",
    )(x2, *consts)


def _mix_sample_kernel(x_ref, s0_ref, lb_ref, gmix_ref, win_ref, hgn_ref, lng_ref, lnb_ref, wmix_ref,
                       bcol_ref, wout_ref, gc_ref, wcq_ref, h_ref, sout_ref, vn_ref, hq_ref,
                       z_ref, ocat_ref, q_s, k_s, b_s, wm_s, bc_s, *, tm):
    x = x_ref[...].reshape(tm, D_MODEL)
    z_ref[...] = jnp.dot(_rmsnorm(x, gmix_ref[...]).astype(BF16), win_ref[...],
                         preferred_element_type=F32)
    lb_all = _lower_bound(lb_ref)
    mask = _chunk_masks(True)
    row = lax.broadcasted_iota(jnp.int32, (CHUNK, CHUNK), 0)
    col = lax.broadcasted_iota(jnp.int32, (CHUNK, CHUNK), 1)
    eye = row == col

    @pl.when(pl.program_id(0) == 0)
    def _():
        pick = (row % 4 == col).astype(BF16)
        for g in range(CM_GROUPS):
            corner = jnp.where((row < 4) & (col < 4), wmix_ref[g], 0.0)
            wm_s[g] = _bdot_nt(_bdot(pick, corner), pick)
        pos = lax.broadcasted_iota(jnp.int32, (CHUNK, CM_GROUPS), 0) % 4
        bc = jnp.zeros((CHUNK, CM_GROUPS), F32)
        for t in range(4):
            bc = jnp.where(pos == t, bcol_ref[t:t + 1, :], bc)
        bc_s[...] = bc
    r16 = lax.broadcasted_iota(jnp.int32, (SUB, HG_DK), 0)

    def gate_body(c, dmax):
        rows = pl.ds(pl.multiple_of(c * CHUNK, CHUNK), CHUNK)
        q, k, bcum = _gates(z_ref[rows, 0:HG_WIDTH], z_ref[rows, HG_WIDTH:2 * HG_WIDTH], lb_all, 4)
        q_s[rows, :] = q
        k_s[rows, :] = k
        b_s[rows, :] = bcum
        return jnp.maximum(dmax, _block_decay(bcum, True))

    dmax = lax.fori_loop(0, tm // CHUNK, gate_body, jnp.zeros((1, HG_WIDTH), F32))
    fast_ok = jnp.max(dmax) < FAST_MAX_DECAY

    def chunk_body(c, carry, *, fast):
        rows = pl.ds(pl.multiple_of(c * CHUNK, CHUNK), CHUNK)
        for h in range(HG_HEADS):
            cs = slice(h * HG_DK, (h + 1) * HG_DK)
            q, k, bcum = q_s[rows, cs], k_s[rows, cs], b_s[rows, cs]
            v = z_ref[rows, 2 * HG_WIDTH + h * HG_DK:2 * HG_WIDTH + (h + 1) * HG_DK]
            zg = z_ref[rows, 3 * HG_WIDTH + h * HG_DK:3 * HG_WIDTH + (h + 1) * HG_DK]
            a = _scores_fast(q, k, bcum, True) if fast else _scores_exact(q, k, bcum, True, mask)
            o = _bdot(jnp.where(mask, a, 0.0), v)
            eb = jnp.exp(bcum)
            qd = q * eb
            o_parts = []
            for gb in range(CHUNK // SUB):
                blk = slice(gb * SUB, (gb + 1) * SUB)
                qd_b, v_b, eb_b, k_b, b_b = qd[blk], v[blk], eb[blk], k[blk], bcum[blk]
                inter = jnp.zeros((SUB, HG_DK), F32)
                for j in range(4):
                    req = c * (CHUNK // 4) + gb * 4 + j
                    s0 = s0_ref[req, h]
                    last = 4 * j + 3
                    inter = jnp.where(r16 // 4 == j, _bdot(qd_b, s0), inter)
                    kd = jnp.where(r16 // 4 == j, k_b * jnp.exp(b_b[last:last + 1] - b_b), 0.0)
                    upd = lax.dot_general(kd.astype(BF16), v_b.astype(BF16),
                                          (((0,), (0,)), ((), ())), preferred_element_type=F32)
                    dcol = jnp.sum(jnp.where(eye, eb_b[last:last + 1], 0.0), axis=-1, keepdims=True)
                    sout_ref[req, h] = dcol * s0 + upd
                o_parts.append(inter)
            o = o + jnp.concatenate(o_parts, axis=0)
            ocat_ref[rows, cs] = _hgrn_out(o, zg, hgn_ref[:, cs])
        vn = _gmlp_chunk(z_ref, 4 * HG_WIDTH, rows, lng_ref, lnb_ref, _masked_mix_weights(wm_s, mask),
                         bc_s, ocat_ref)
        vn_ref[pl.ds(pl.multiple_of(c * (CHUNK // 4), CHUNK // 4), CHUNK // 4)] = vn.reshape(
            CHUNK // 4, 4, CM_WIDTH)
        return carry

    @pl.when(fast_ok)
    def _():
        lax.fori_loop(0, tm // CHUNK, functools.partial(chunk_body, fast=True), 0)

    @pl.when(jnp.logical_not(fast_ok))
    def _():
        lax.fori_loop(0, tm // CHUNK, functools.partial(chunk_body, fast=False), 0)

    h = x + jnp.dot(ocat_ref[...], wout_ref[...], preferred_element_type=F32)
    h_ref[0] = h
    hq_ref[...] = jnp.dot(_rmsnorm(h, gc_ref[...]).astype(BF16), wcq_ref[...], preferred_element_type=F32)


def _mix_sample(x, s0, lb_param, g_mix, w_in, hg_norm_g, ln_v_g, ln_v_b, wmix, bcol, w_out, g_cross,
                w_cq, tm):
    nreq = tm // 4
    nt = x.shape[0] // nreq
    consts = (lb_param, g_mix, w_in, hg_norm_g, ln_v_g, ln_v_b, wmix, bcol, w_out, g_cross, w_cq)
    in_specs = [
        pl.BlockSpec((nreq, 4, D_MODEL), lambda i: (i, 0, 0)),
        pl.BlockSpec((nreq, HG_HEADS, HG_DK, HG_DK), lambda i: (i, 0, 0, 0)),
    ] + [_const_spec(a.shape) for a in consts]
    out_specs = [
        pl.BlockSpec((1, tm, D_MODEL), lambda i: (i, 0, 0)),
        pl.BlockSpec((nreq, HG_HEADS, HG_DK, HG_DK), lambda i: (i, 0, 0, 0)),
        pl.BlockSpec((nreq, 4, CM_WIDTH), lambda i: (i, 0, 0)),
        pl.BlockSpec((tm, D_MODEL), lambda i: (i, 0)),
    ]
    return pl.pallas_call(
        functools.partial(_mix_sample_kernel, tm=tm),
        grid=(nt,),
        in_specs=in_specs,
        out_specs=out_specs,
        out_shape=[jax.ShapeDtypeStruct((nt, tm, D_MODEL), F32),
                   jax.ShapeDtypeStruct(s0.shape, F32),
                   jax.ShapeDtypeStruct((nt * nreq, 4, CM_WIDTH), F32),
                   jax.ShapeDtypeStruct((nt * tm, D_MODEL), F32)],
        scratch_shapes=[pltpu.VMEM((tm, IN_WIDTH), F32), pltpu.VMEM((tm, D_MODEL), BF16)]
        + [pltpu.VMEM((tm, HG_WIDTH), F32)] * 3
        + [pltpu.VMEM((CM_GROUPS, CHUNK, CHUNK), F32), pltpu.VMEM((CHUNK, CM_GROUPS), F32)],
        compiler_params=pltpu.CompilerParams(dimension_semantics=("arbitrary",),
                                             vmem_limit_bytes=VMEM_LIMIT_BYTES),
        name="mix_sample",
    )(x, s0, *consts)


def _memkv_kernel(m_ref, g_ref, wk_ref, wv_ref, k_ref, v_ref, kb_ref, vb_ref):
    mn = _rmsnorm(m_ref[...], g_ref[...]).astype(BF16)
    k = jnp.dot(mn, wk_ref[...], preferred_element_type=F32)
    v = jnp.dot(mn, wv_ref[...], preferred_element_type=F32)
    kb_ref[...] = k.astype(BF16)
    vb_ref[...] = v.astype(BF16)
    for src, dst in ((k, k_ref), (v, v_ref)):
        for r in range(dst.shape[0]):
            for h in range(X_HEADS):
                dst[r, :, h, :] = src[r * N_MEM:(r + 1) * N_MEM, h * X_HEAD_DIM:(h + 1) * X_HEAD_DIM]


def _memkv(mem, g_mem, w_ck, w_cv, tm):
    n = mem.shape[0]
    return pl.pallas_call(
        _memkv_kernel,
        grid=(n // tm,),
        in_specs=[pl.BlockSpec((tm, D_MODEL), lambda i: (i, 0)), _const_spec(g_mem.shape),
                  _const_spec(w_ck.shape), _const_spec(w_cv.shape)],
        out_specs=[pl.BlockSpec((tm // N_MEM, N_MEM, X_HEADS, X_HEAD_DIM), lambda i: (i, 0, 0, 0))] * 2
        + [pl.BlockSpec((tm, D_MODEL), lambda i: (i, 0))] * 2,
        out_shape=[jax.ShapeDtypeStruct((n // N_MEM, N_MEM, X_HEADS, X_HEAD_DIM), F32)] * 2
        + [jax.ShapeDtypeStruct((n, D_MODEL), BF16)] * 2,
        compiler_params=pltpu.CompilerParams(dimension_semantics=("arbitrary",),
                                             vmem_limit_bytes=VMEM_LIMIT_BYTES),
        name="memkv",
    )(mem, g_mem, w_ck, w_cv)


def _post_sample_kernel(x_ref, ca_ref, wco_ref, gm_ref, wup_ref, wdn_ref, gf_ref, y_ref, hn_s, acc_s):
    p = pl.program_id(0)

    @pl.when(p == 0)
    def _():
        x2 = x_ref[0] + jnp.dot(ca_ref[...].astype(BF16), wco_ref[...], preferred_element_type=F32)
        acc_s[...] = x2
        hn_s[...] = _rmsnorm(x2, gm_ref[...]).astype(BF16)

    hm = jnp.maximum(jnp.dot(hn_s[...], wup_ref[...], preferred_element_type=F32), 0.0)
    acc_s[...] += jnp.dot((hm * hm).astype(BF16), wdn_ref[...], preferred_element_type=F32)

    @pl.when(p == pl.num_programs(0) - 1)
    def _():
        y_ref[...] = _rmsnorm(acc_s[...], gf_ref[...]).reshape(y_ref.shape)


def _post_sample(x, ca, w_co, g_mlp, w_up, w_down, g_final):
    _, tm, _ = x.shape
    in_specs = [
        pl.BlockSpec((1, tm, D_MODEL), lambda p: (0, 0, 0)),
        pl.BlockSpec((tm, D_MODEL), lambda p: (0, 0)),
        _const_spec(w_co.shape), _const_spec(g_mlp.shape),
        pl.BlockSpec((D_MODEL, D_MODEL), lambda p: (0, p)),
        pl.BlockSpec((D_MODEL, D_MODEL), lambda p: (p, 0)),
        _const_spec(g_final.shape),
    ]
    return pl.pallas_call(
        _post_sample_kernel,
        grid=(D_FF // D_MODEL,),
        in_specs=in_specs,
        out_specs=pl.BlockSpec((tm // 4, 4, D_MODEL), lambda p: (0, 0, 0)),
        out_shape=jax.ShapeDtypeStruct((tm // 4, 4, D_MODEL), F32),
        scratch_shapes=[pltpu.VMEM((tm, D_MODEL), BF16), pltpu.VMEM((tm, D_MODEL), F32)],
        compiler_params=pltpu.CompilerParams(dimension_semantics=("arbitrary",),
                                             vmem_limit_bytes=VMEM_LIMIT_BYTES),
        name="post_sample",
    )(x, ca, w_co, g_mlp, w_up, w_down, g_final)


def _cast_kernel(*refs):
    n = len(refs) // 2
    for src, dst in zip(refs[:n], refs[n:]):
        dst[...] = src[...].astype(BF16)


def _cast_weights(*ws):
    steps = 8
    specs = [pl.BlockSpec((w.shape[0] // steps, w.shape[1]), lambda i: (i, 0)) for w in ws]
    return pl.pallas_call(
        _cast_kernel,
        grid=(steps,),
        in_specs=specs,
        out_specs=specs,
        out_shape=[jax.ShapeDtypeStruct(w.shape, BF16) for w in ws],
        compiler_params=pltpu.CompilerParams(dimension_semantics=("arbitrary",),
                                             vmem_limit_bytes=VMEM_LIMIT_BYTES),
        name="cast_weights",
    )(*ws)


def kernel(x_prompt, x_sample, mem_prompt, state_hgrn, cache_mem_k, cache_mem_v, lb_param, g_mix,
           w_in, hg_norm_g, ln_v_g, ln_v_b, w_s, b_s, w_out, g_cross, g_mem, w_cq, w_ck, w_cv, w_co,
           g_mlp, w_up, w_down, g_final):
    B, L, _ = x_prompt.shape
    DB, DL, _ = x_sample.shape
    assert DL == 4 and g_mix.shape[0] == 1

    row = lambda a: a.reshape(1, -1)
    win_b, wout_b, wcq_b, wco_b, wck_b, wcv_b, wup_b, wdn_b = _cast_weights(
        w_in[0], w_out[0], w_cq[0], w_co[0], w_ck[0], w_cv[0], w_up[0], w_down[0])
    gmix, hgn, lng, lnb = row(g_mix[0]), row(hg_norm_g[0]), row(ln_v_g[0]), row(ln_v_b[0])
    gcr, gmem, gmlp, gfin = row(g_cross[0]), row(g_mem[0]), row(g_mlp[0]), row(g_final)
    wmix = w_s[0]
    bcol = b_s[0].T

    tm_s = 128
    h_s, s_s, vn_s, hq_s = _mix_sample(x_sample, state_hgrn[0], lb_param, gmix, win_b, hgn, lng, lnb, wmix,
                                       bcol, wout_b, gcr, wcq_b, tm=tm_s)

    mk, mv, mk_b, mv_b = _memkv(mem_prompt.reshape(B * N_MEM, D_MODEL), gmem, wck_b, wcv_b, tm=512)
    x2_p, s_p, ca_s = _mixattn_layer(x_prompt, mk_b.reshape(B, N_MEM, D_MODEL),
                                     mv_b.reshape(B, N_MEM, D_MODEL), hq_s, cache_mem_k[0], cache_mem_v[0],
                                     lb_param, gmix, win_b, hgn, lng, lnb, wmix, bcol, wout_b, gcr, wcq_b,
                                     wco_b, tm=512)
    y_p = _mlp_layer(x2_p, gmlp, wup_b, wdn_b, gfin, tm=1024)

    y_s = _post_sample(h_s.reshape(1, DB * DL, D_MODEL), ca_s, wco_b, gmlp, wup_b, wdn_b, gfin)

    return (y_p, y_s, s_p[None], s_s[None],
            mk[None], mv[None],
            vn_s[None])
```

```python
import functools

import jax
import jax.numpy as jnp
from jax import lax
from jax.experimental import pallas as pl
from jax.experimental.pallas import tpu as pltpu

F32 = jnp.float32
BF16 = jnp.bfloat16

D_MODEL = 1024
HG_WIDTH = 512
HG_HEADS = 4
HG_DK = 128
CM_WIDTH = 512
CM_GROUPS = 4
CM_GROUP_DIM = 128
IN_WIDTH = 4 * HG_WIDTH + 2 * CM_WIDTH
N_MEM = 256
X_HEADS = 4
X_HEAD_DIM = 256
D_FF = 4096
EPS = 1e-6

CHUNK = 128
SUBLANES = 8
SUB = 2 * SUBLANES
VMEM_LIMIT_BYTES = 62 * 1024 * 1024
LOG2E = 1.4426950408889634
FAST_BLOCK = 32
FAST_MAX_DECAY = 80.0


def _bdot(a, b):
    return jnp.dot(a.astype(BF16), b.astype(BF16), preferred_element_type=F32)


def _bdot_nt(a, b):
    return lax.dot_general(a.astype(BF16), b.astype(BF16), (((1,), (1,)), ((), ())),
                           preferred_element_type=F32)


def _rmsnorm(x, g):
    ms = jnp.mean(x * x, axis=-1, keepdims=True)
    return x * lax.rsqrt(ms + EPS) * g


def _gelu(x):
    c = 0.7978845608028654
    return x * (0.5 + 0.5 * jnp.tanh(x * (c + (c * 0.044715) * (x * x))))


def _softmax_rows(s):
    m = jnp.max(s, axis=-1, keepdims=True)
    e = jnp.exp(s - m)
    return e * (1.0 / jnp.sum(e, axis=-1, keepdims=True))


def _seg_cumsum(x, seg):
    n = x.shape[0]
    pos = lax.broadcasted_iota(jnp.int32, x.shape, 0) % seg
    s = 1
    while s < min(seg, SUBLANES):
        x = x + jnp.where(pos >= s, pltpu.roll(x, s, 0), 0.0)
        s *= 2
    while s < seg:
        parts = []
        for r0 in range(0, n, seg):
            parts.append(x[r0:r0 + s])
            parts.append(x[r0 + s:r0 + seg] + x[r0:r0 + seg - s])
        x = jnp.concatenate(parts, axis=0)
        s *= 2
    return x


def _chunk_cumsum(x):
    n = x.shape[0]
    tril = (lax.broadcasted_iota(jnp.int32, (n, n), 0)
            >= lax.broadcasted_iota(jnp.int32, (n, n), 1)).astype(BF16)
    hi = x.astype(BF16)
    r1 = x - hi.astype(F32)
    mid = r1.astype(BF16)
    lo = (r1 - mid.astype(F32)).astype(BF16)
    return (jnp.dot(tril, hi, preferred_element_type=F32) + jnp.dot(tril, mid, preferred_element_type=F32)
            + jnp.dot(tril, lo, preferred_element_type=F32))


def _lower_bound(lb_ref):
    lbp = lb_ref[...]
    lbe = jnp.exp(lbp - jnp.max(lbp, axis=0, keepdims=True))
    return lbe[0:1] / jnp.sum(lbe, axis=0, keepdims=True)


def _gates(zq, zf, lb, seg):
    half = 0.5 - 0.5 * lb
    hth = half * jnp.tanh(0.5 * zf)
    logf = jnp.log((lb + half) + hth)
    bcum = _chunk_cumsum(logf) if seg == logf.shape[0] else _seg_cumsum(logf, seg)
    hq = 0.5 * zq
    return hq * jnp.tanh(hq) + hq, half - hth, bcum


def _block_decay(bcum, sample):
    if sample:
        return jnp.max(-bcum, axis=0, keepdims=True)
    d = -bcum[FAST_BLOCK - 1:FAST_BLOCK]
    for i in range(1, CHUNK // FAST_BLOCK):
        n0, n1 = i * FAST_BLOCK, (i + 1) * FAST_BLOCK
        d = jnp.maximum(d, bcum[n0 - 1:n0] - bcum[n1 - 1:n1])
    return d


def _chunk_masks(sample):
    row = lax.broadcasted_iota(jnp.int32, (CHUNK, CHUNK), 0)
    col = lax.broadcasted_iota(jnp.int32, (CHUNK, CHUNK), 1)
    if sample:
        return (row // 4 == col // 4) & (row >= col)
    return row >= col


def _scores_fast(q, k, bcum, sample):
    if sample:
        return _bdot_nt(q * jnp.exp(bcum), k * jnp.exp(-bcum))
    a_rows = []
    zero = jnp.zeros((1, HG_DK), F32)
    kt, prev_ref = None, zero
    for i in range(CHUNK // FAST_BLOCK):
        n0, n1 = i * FAST_BLOCK, (i + 1) * FAST_BLOCK
        ref_b = bcum[n0 - 1:n0] if i else zero
        qt = q[n0:n1] * jnp.exp(bcum[n0:n1] - ref_b)
        kt_blk = k[n0:n1] * jnp.exp(ref_b - bcum[n0:n1])
        kt = kt_blk if kt is None else jnp.concatenate([kt * jnp.exp(ref_b - prev_ref), kt_blk], axis=0)
        prev_ref = ref_b
        kt_b = kt.astype(BF16)
        if n1 < CHUNK:
            kt_b = jnp.concatenate([kt_b, jnp.zeros((CHUNK - n1, HG_DK), BF16)], axis=0)
        a_rows.append(_bdot_nt(qt, kt_b))
    return jnp.concatenate(a_rows, axis=0)


def _scores_exact(q, k, bcum, sample, mask):
    col8 = lax.broadcasted_iota(jnp.int32, (SUBLANES, CHUNK), 1)
    b2 = bcum * LOG2E
    cexp = b2 - jnp.log2(k)
    a_rows = []
    for gb in range(CHUNK // SUB):
        lo = slice(gb * SUB, gb * SUB + SUBLANES)
        hi = slice(gb * SUB + SUBLANES, (gb + 1) * SUB)
        a_lo = jnp.zeros((SUBLANES, CHUNK), F32)
        a_hi = jnp.zeros((SUBLANES, CHUNK), F32)
        for s in range(SUB):
            sg = gb * SUB + s
            c_s = cexp[sg:sg + 1]
            if s < SUBLANES:
                p = q[lo] * jnp.exp2(b2[lo] - c_s)
                a_lo = jnp.where(col8 == sg, jnp.sum(p, axis=-1, keepdims=True), a_lo)
            if (not sample) or s >= SUBLANES:
                p = q[hi] * jnp.exp2(b2[hi] - c_s)
                a_hi = jnp.where(col8 == sg, jnp.sum(p, axis=-1, keepdims=True), a_hi)
        a_rows.append(a_lo)
        a_rows.append(a_hi)
    a = jnp.concatenate(a_rows, axis=0)
    if sample:
        return a
    o_rows = [jnp.zeros((SUB, CHUNK), F32)]
    for i in range(1, CHUNK // SUB):
        n = i * SUB
        ref_b = bcum[n - 1:n]
        qt = q[n:n + SUB] * jnp.exp(bcum[n:n + SUB] - ref_b)
        kt = k[:n] * jnp.exp(ref_b - bcum[:n])
        kt = jnp.concatenate([kt, jnp.zeros((CHUNK - n, HG_DK), F32)], axis=0)
        o_rows.append(_bdot_nt(qt, kt))
    return jnp.where(mask, a, 0.0) + jnp.concatenate(o_rows, axis=0)


def _hgrn_out(o, zg, g):
    inv = lax.rsqrt(jnp.mean(o * o, axis=-1, keepdims=True) + EPS)
    return (o * inv * (0.5 * g) * (jnp.tanh(0.5 * zg) + 1.0)).astype(BF16)


def _masked_mix_weights(wmix_ref, mask):
    return [jnp.where(mask, wmix_ref[g], 0.0).astype(BF16) for g in range(CM_GROUPS)]


def _gmlp_chunk(z_ref, c0, rows, lng_ref, lnb_ref, wms, bcol_ref, ocat_ref):
    u = _gelu(z_ref[rows, c0:c0 + CM_WIDTH])
    gv = _gelu(z_ref[rows, c0 + CM_WIDTH:c0 + 2 * CM_WIDTH])
    mu = jnp.mean(gv, axis=-1, keepdims=True)
    dv = gv - mu
    var = jnp.mean(dv * dv, axis=-1, keepdims=True)
    vn = dv * lax.rsqrt(var + EPS) * lng_ref[...] + lnb_ref[...]
    for g in range(CM_GROUPS):
        gs = slice(g * CM_GROUP_DIM, (g + 1) * CM_GROUP_DIM)
        mixed = _bdot(wms[g], vn[:, gs]) + bcol_ref[:, g:g + 1]
        ocat_ref[rows, HG_WIDTH + g * CM_GROUP_DIM:HG_WIDTH + (g + 1) * CM_GROUP_DIM] = (
            u[:, gs] * mixed).astype(BF16)
    return vn


def _sample_pair_probs(hq8, ck_ref):
    nrow = 8 * X_HEADS
    rh = lax.broadcasted_iota(jnp.int32, (nrow, N_MEM * X_HEADS), 0) // 8
    ch = lax.broadcasted_iota(jnp.int32, (nrow, N_MEM * X_HEADS), 1) % X_HEADS
    q = jnp.concatenate([hq8[:, h * X_HEAD_DIM:(h + 1) * X_HEAD_DIM] for h in range(X_HEADS)],
                        axis=0).astype(BF16)
    probs = []
    for r in range(2):
        k2 = ck_ref[r].reshape(N_MEM * X_HEADS, X_HEAD_DIM)
        s = jnp.where(rh == ch, _bdot_nt(q, k2) * (X_HEAD_DIM ** -0.5), -jnp.inf)
        probs.append(_softmax_rows(s))
    return probs


def _sample_pair_context(probs, cv_ref, ca_ref):
    first = (lax.broadcasted_iota(jnp.int32, (8 * X_HEADS, X_HEAD_DIM), 0) % 8) < 4
    outs = [_bdot(probs[r], cv_ref[r].reshape(N_MEM * X_HEADS, X_HEAD_DIM)) for r in range(2)]
    o = jnp.where(first, outs[0], outs[1])
    for h in range(X_HEADS):
        ca_ref[:, h * X_HEAD_DIM:(h + 1) * X_HEAD_DIM] = o[8 * h:8 * h + 8]


def _mlp(hn_bf16, wup_ref, wdn_ref, hm_s):
    for c in range(D_FF // D_MODEL):
        fs = slice(c * D_MODEL, (c + 1) * D_MODEL)
        hm = jnp.maximum(jnp.dot(hn_bf16, wup_ref[:, fs], preferred_element_type=F32), 0.0)
        hm_s[:, fs] = (hm * hm).astype(BF16)
    return jnp.dot(hm_s[...], wdn_ref[...], preferred_element_type=F32)


def _const_spec(shape):
    nd = len(shape)
    return pl.BlockSpec(shape, lambda *_: (0,) * nd, pipeline_mode=pl.Buffered(1))


def _mixattn_kernel(x_ref, mk_ref, mv_ref, hqs_ref, ck_ref, cv_ref, lb_ref, gmix_ref, win_ref, hgn_ref,
                    lng_ref, lnb_ref, wmix_ref, bcol_ref, wout_ref, gc_ref, wcq_ref, wco_ref,
                    x2_ref, sout_ref, cas_ref,
                    zg_ref, z_ref, ocat_ref, q_s, k_s, b_s, a_s, st0, st1, st2, st3, h1_s, h1n_s, ca_s,
                    *, tm, nt, npairs):
    t = pl.program_id(0)
    i = lax.rem(t, nt)
    nch = tm // CHUNK
    nj = nch * HG_HEADS
    mask = _chunk_masks(False)
    st_refs = (st0, st1, st2, st3)

    @pl.when(t == 0)
    def _():
        h1_s[...] = jnp.zeros_like(h1_s)
        h1n_s[...] = jnp.zeros_like(h1n_s)

    @pl.when(i == 0)
    def _():
        for st in st_refs:
            st[...] = jnp.zeros_like(st)

    x = x_ref[0]
    xn = _rmsnorm(x, gmix_ref[...]).astype(BF16)
    ngate = 2 * HG_WIDTH
    zg_ref[...] = jnp.dot(xn, win_ref[:, 0:ngate], preferred_element_type=F32)
    zlast = IN_WIDTH - HG_WIDTH
    z_ref[:, 0:zlast - ngate] = jnp.dot(xn, win_ref[:, ngate:zlast], preferred_element_type=F32)
    lb_all = _lower_bound(lb_ref)
    dmax = jnp.zeros((1, HG_WIDTH), F32)
    for c in range(nch):
        rows = slice(c * CHUNK, (c + 1) * CHUNK)
        q, k, bcum = _gates(zg_ref[rows, 0:HG_WIDTH], zg_ref[rows, HG_WIDTH:2 * HG_WIDTH], lb_all, CHUNK)
        q_s[rows, :] = q
        k_s[rows, :] = k
        b_s[rows, :] = bcum
        dmax = jnp.maximum(dmax, _block_decay(bcum, False))
    fast_ok = jnp.max(dmax) < FAST_MAX_DECAY
    z_ref[:, zlast - ngate:IN_WIDTH - ngate] = jnp.dot(xn, win_ref[:, zlast:IN_WIDTH], preferred_element_type=F32)

    def fill_scores(j, fast):
        c, h = divmod(j, HG_HEADS)
        rows = slice(c * CHUNK, (c + 1) * CHUNK)
        cs = slice(h * HG_DK, (h + 1) * HG_DK)
        q, k, bcum = q_s[rows, cs], k_s[rows, cs], b_s[rows, cs]
        a = _scores_fast(q, k, bcum, False) if fast else _scores_exact(q, k, bcum, False, mask)
        a_s[j] = jnp.where(mask, a, 0.0).astype(BF16)

    def head_chunk(j, wms):
        c, h = divmod(j, HG_HEADS)
        rows = slice(c * CHUNK, (c + 1) * CHUNK)
        cs = slice(h * HG_DK, (h + 1) * HG_DK)
        q, k, bcum = q_s[rows, cs], k_s[rows, cs], b_s[rows, cs]
        v = z_ref[rows, h * HG_DK:(h + 1) * HG_DK]
        zg = z_ref[rows, HG_WIDTH + h * HG_DK:HG_WIDTH + (h + 1) * HG_DK]
        eb = jnp.exp(bcum)
        kd = k * jnp.exp(bcum[CHUNK - 1:CHUNK] - bcum)
        st = st_refs[h][...]
        o = jnp.dot(a_s[j], v.astype(BF16), preferred_element_type=F32)
        o = o + _bdot_nt(q * eb, st)
        st_refs[h][...] = st * eb[CHUNK - 1:CHUNK] + _bdot(v.T, kd)
        ocat_ref[rows, cs] = _hgrn_out(o, zg, hgn_ref[:, cs])
        if h == HG_HEADS - 1:
            _gmlp_chunk(z_ref, 2 * HG_WIDTH, rows, lng_ref, lnb_ref, wms, bcol_ref, ocat_ref)

    def region2(fast):
        wms = _masked_mix_weights(wmix_ref, mask)
        scale = X_HEAD_DIM ** -0.5
        heads = [slice(h * X_HEAD_DIM, (h + 1) * X_HEAD_DIM) for h in range(X_HEADS)]
        nq = nj // X_HEADS
        h1n = h1n_s[...]
        hq = []
        for h, hs in enumerate(heads):
            hq.append(jnp.dot(h1n, wcq_ref[:, hs], preferred_element_type=F32))
            for j in range(h * nq, (h + 1) * nq):
                fill_scores(j, fast)
        probs = [_softmax_rows(_bdot_nt(hq[h], mk_ref[0, :, hs]) * scale) for h, hs in enumerate(heads)]
        sprobs = [_sample_pair_probs(hqs_ref[8 * p:8 * p + 8, :], ck_ref.at[2 * p:2 * p + 2])
                  for p in range(npairs)]
        for j in range(0, nq):
            head_chunk(j, wms)
        for h, hs in enumerate(heads):
            ca_s[:, hs] = _bdot(probs[h], mv_ref[0, :, hs]).astype(BF16)
        ca = ca_s[...]
        for h, hs in enumerate(heads):
            x2_ref[0, :, hs] = h1_s[:, hs] + jnp.dot(ca, wco_ref[:, hs], preferred_element_type=F32)
            for j in range((h + 1) * nq, min(nj, (h + 2) * nq)):
                head_chunk(j, wms)
        for p in range(npairs):
            _sample_pair_context(sprobs[p], cv_ref.at[2 * p:2 * p + 2], cas_ref.at[8 * p:8 * p + 8, :])
        h1 = x_ref[0] + jnp.dot(ocat_ref[...], wout_ref[...], preferred_element_type=F32)
        h1_s[...] = h1
        h1n_s[...] = _rmsnorm(h1, gc_ref[...]).astype(BF16)

    @pl.when(fast_ok)
    def _():
        region2(True)

    @pl.when(jnp.logical_not(fast_ok))
    def _():
        region2(False)

    @pl.when(i == nt - 1)
    def _():
        for h in range(HG_HEADS):
            sout_ref[0, h] = st_refs[h][...].T


def _mixattn_layer(x, mk, mv, hqs, ck, cv, lb_param, g_mix, w_in, hg_norm_g, ln_v_g, ln_v_b, wmix, bcol,
                   w_out, g_cross, w_cq, w_co, tm):
    B, L, _ = x.shape
    nt = L // tm
    T = B * nt
    npairs = ck.shape[0] // (2 * T)
    assert npairs * 2 * T == ck.shape[0]

    def cur(t):
        return jnp.minimum(t, T - 1)

    def prev(t):
        return jnp.maximum(t - 1, 0)

    consts = (lb_param, g_mix, w_in, hg_norm_g, ln_v_g, ln_v_b, wmix, bcol, w_out, g_cross, w_cq, w_co)
    kv_spec = pl.BlockSpec((2 * npairs, N_MEM, X_HEADS, X_HEAD_DIM), lambda t: (cur(t), 0, 0, 0))
    in_specs = [
        pl.BlockSpec((1, tm, D_MODEL), lambda t: (cur(t) // nt, cur(t) % nt, 0)),
        pl.BlockSpec((1, N_MEM, D_MODEL), lambda t: (prev(t) // nt, 0, 0)),
        pl.BlockSpec((1, N_MEM, D_MODEL), lambda t: (prev(t) // nt, 0, 0)),
        pl.BlockSpec((8 * npairs, D_MODEL), lambda t: (cur(t), 0)), kv_spec, kv_spec,
    ] + [_const_spec(a.shape) for a in consts]
    out_specs = [
        pl.BlockSpec((1, tm, D_MODEL), lambda t: (prev(t) // nt, prev(t) % nt, 0)),
        pl.BlockSpec((1, HG_HEADS, HG_DK, HG_DK), lambda t: (cur(t) // nt, 0, 0, 0)),
        pl.BlockSpec((8 * npairs, D_MODEL), lambda t: (cur(t), 0)),
    ]
    scratch = [
        pltpu.VMEM((tm, 2 * HG_WIDTH), F32),
        pltpu.VMEM((tm, IN_WIDTH - 2 * HG_WIDTH), F32),
        pltpu.VMEM((tm, D_MODEL), BF16),
        pltpu.VMEM((tm, HG_WIDTH), F32), pltpu.VMEM((tm, HG_WIDTH), F32),
        pltpu.VMEM((tm, HG_WIDTH), F32),
        pltpu.VMEM((tm // CHUNK * HG_HEADS, CHUNK, CHUNK), BF16),
        pltpu.VMEM((HG_DK, HG_DK), F32), pltpu.VMEM((HG_DK, HG_DK), F32),
        pltpu.VMEM((HG_DK, HG_DK), F32), pltpu.VMEM((HG_DK, HG_DK), F32),
        pltpu.VMEM((tm, D_MODEL), F32),
        pltpu.VMEM((tm, D_MODEL), BF16),
        pltpu.VMEM((tm, D_MODEL), BF16),
    ]
    return pl.pallas_call(
        functools.partial(_mixattn_kernel, tm=tm, nt=nt, npairs=npairs),
        grid=(T + 1,),
        in_specs=in_specs,
        out_specs=out_specs,
        out_shape=[jax.ShapeDtypeStruct((B, L, D_MODEL), F32),
                   jax.ShapeDtypeStruct((B, HG_HEADS, HG_DK, HG_DK), F32),
                   jax.ShapeDtypeStruct(hqs.shape, F32)],
        scratch_shapes=scratch,
        compiler_params=pltpu.CompilerParams(dimension_semantics=("arbitrary",),
                                             vmem_limit_bytes=VMEM_LIMIT_BYTES),
        name="mix_attn",
    )(x, mk, mv, hqs, ck, cv, *consts)


def _mlp_kernel(x2_ref, gm_ref, wup_ref, wdn_ref, gf_ref, y_ref, hm_s):
    x2 = x2_ref[0]
    y_ref[0] = _rmsnorm(x2 + _mlp(_rmsnorm(x2, gm_ref[...]).astype(BF16), wup_ref, wdn_ref, hm_s),
                        gf_ref[...])


def _mlp_layer(x2, g_mlp, w_up, w_down, g_final, tm):
    B, L, _ = x2.shape
    nt = L // tm
    consts = (g_mlp, w_up, w_down, g_final)
    tile = pl.BlockSpec((1, tm, D_MODEL), lambda t: (t // nt, t % nt, 0))
    return pl.pallas_call(
        _mlp_kernel,
        grid=(B * nt,),
        in_specs=[tile] + [_const_spec(a.shape) for a in consts],
        out_specs=tile,
        out_shape=jax.ShapeDtypeStruct((B, L, D_MODEL), F32),
        scratch_shapes=[pltpu.VMEM((tm, D_FF), BF16)],
        compiler_params=pltpu.CompilerParams(dimension_semantics=("arbitrary",),
                                             vmem_limit_bytes=VMEM_LIMIT_BYTES),
        name="mlp_prompt",
    )(x2, *consts)


def _mix_sample_kernel(x_ref, s0_ref, lb_ref, gmix_ref, win_ref, hgn_ref, lng_ref, lnb_ref, wmix_ref,
                       bcol_ref, wout_ref, gc_ref, wcq_ref, h_ref, sout_ref, vn_ref, hq_ref,
                       z_ref, ocat_ref, q_s, k_s, b_s, wm_s, bc_s, *, tm):
    x = x_ref[...].reshape(tm, D_MODEL)
    z_ref[...] = jnp.dot(_rmsnorm(x, gmix_ref[...]).astype(BF16), win_ref[...],
                         preferred_element_type=F32)
    lb_all = _lower_bound(lb_ref)
    mask = _chunk_masks(True)
    row = lax.broadcasted_iota(jnp.int32, (CHUNK, CHUNK), 0)
    col = lax.broadcasted_iota(jnp.int32, (CHUNK, CHUNK), 1)
    eye = row == col

    @pl.when(pl.program_id(0) == 0)
    def _():
        pick = (row % 4 == col).astype(BF16)
        for g in range(CM_GROUPS):
            corner = jnp.where((row < 4) & (col < 4), wmix_ref[g], 0.0)
            wm_s[g] = _bdot_nt(_bdot(pick, corner), pick)
        pos = lax.broadcasted_iota(jnp.int32, (CHUNK, CM_GROUPS), 0) % 4
        bc = jnp.zeros((CHUNK, CM_GROUPS), F32)
        for t in range(4):
            bc = jnp.where(pos == t, bcol_ref[t:t + 1, :], bc)
        bc_s[...] = bc
    r16 = lax.broadcasted_iota(jnp.int32, (SUB, HG_DK), 0)

    def gate_body(c, dmax):
        rows = pl.ds(pl.multiple_of(c * CHUNK, CHUNK), CHUNK)
        q, k, bcum = _gates(z_ref[rows, 0:HG_WIDTH], z_ref[rows, HG_WIDTH:2 * HG_WIDTH], lb_all, 4)
        q_s[rows, :] = q
        k_s[rows, :] = k
        b_s[rows, :] = bcum
        return jnp.maximum(dmax, _block_decay(bcum, True))

    dmax = lax.fori_loop(0, tm // CHUNK, gate_body, jnp.zeros((1, HG_WIDTH), F32))
    fast_ok = jnp.max(dmax) < FAST_MAX_DECAY

    def chunk_body(c, carry, *, fast):
        rows = pl.ds(pl.multiple_of(c * CHUNK, CHUNK), CHUNK)
        for h in range(HG_HEADS):
            cs = slice(h * HG_DK, (h + 1) * HG_DK)
            q, k, bcum = q_s[rows, cs], k_s[rows, cs], b_s[rows, cs]
            v = z_ref[rows, 2 * HG_WIDTH + h * HG_DK:2 * HG_WIDTH + (h + 1) * HG_DK]
            zg = z_ref[rows, 3 * HG_WIDTH + h * HG_DK:3 * HG_WIDTH + (h + 1) * HG_DK]
            a = _scores_fast(q, k, bcum, True) if fast else _scores_exact(q, k, bcum, True, mask)
            o = _bdot(jnp.where(mask, a, 0.0), v)
            eb = jnp.exp(bcum)
            qd = q * eb
            o_parts = []
            for gb in range(CHUNK // SUB):
                blk = slice(gb * SUB, (gb + 1) * SUB)
                qd_b, v_b, eb_b, k_b, b_b = qd[blk], v[blk], eb[blk], k[blk], bcum[blk]
                inter = jnp.zeros((SUB, HG_DK), F32)
                for j in range(4):
                    req = c * (CHUNK // 4) + gb * 4 + j
                    s0 = s0_ref[req, h]
                    last = 4 * j + 3
                    inter = jnp.where(r16 // 4 == j, _bdot(qd_b, s0), inter)
                    kd = jnp.where(r16 // 4 == j, k_b * jnp.exp(b_b[last:last + 1] - b_b), 0.0)
                    upd = lax.dot_general(kd.astype(BF16), v_b.astype(BF16),
                                          (((0,), (0,)), ((), ())), preferred_element_type=F32)
                    dcol = jnp.sum(jnp.where(eye, eb_b[last:last + 1], 0.0), axis=-1, keepdims=True)
                    sout_ref[req, h] = dcol * s0 + upd
                o_parts.append(inter)
            o = o + jnp.concatenate(o_parts, axis=0)
            ocat_ref[rows, cs] = _hgrn_out(o, zg, hgn_ref[:, cs])
        vn = _gmlp_chunk(z_ref, 4 * HG_WIDTH, rows, lng_ref, lnb_ref, _masked_mix_weights(wm_s, mask),
                         bc_s, ocat_ref)
        vn_ref[pl.ds(pl.multiple_of(c * (CHUNK // 4), CHUNK // 4), CHUNK // 4)] = vn.reshape(
            CHUNK // 4, 4, CM_WIDTH)
        return carry

    @pl.when(fast_ok)
    def _():
        lax.fori_loop(0, tm // CHUNK, functools.partial(chunk_body, fast=True), 0)

    @pl.when(jnp.logical_not(fast_ok))
    def _():
        lax.fori_loop(0, tm // CHUNK, functools.partial(chunk_body, fast=False), 0)

    h = x + jnp.dot(ocat_ref[...], wout_ref[...], preferred_element_type=F32)
    h_ref[0] = h
    hq_ref[...] = jnp.dot(_rmsnorm(h, gc_ref[...]).astype(BF16), wcq_ref[...], preferred_element_type=F32)


def _mix_sample(x, s0, lb_param, g_mix, w_in, hg_norm_g, ln_v_g, ln_v_b, wmix, bcol, w_out, g_cross,
                w_cq, tm):
    nreq = tm // 4
    nt = x.shape[0] // nreq
    consts = (lb_param, g_mix, w_in, hg_norm_g, ln_v_g, ln_v_b, wmix, bcol, w_out, g_cross, w_cq)
    in_specs = [
        pl.BlockSpec((nreq, 4, D_MODEL), lambda i: (i, 0, 0)),
        pl.BlockSpec((nreq, HG_HEADS, HG_DK, HG_DK), lambda i: (i, 0, 0, 0)),
    ] + [_const_spec(a.shape) for a in consts]
    out_specs = [
        pl.BlockSpec((1, tm, D_MODEL), lambda i: (i, 0, 0)),
        pl.BlockSpec((nreq, HG_HEADS, HG_DK, HG_DK), lambda i: (i, 0, 0, 0)),
        pl.BlockSpec((nreq, 4, CM_WIDTH), lambda i: (i, 0, 0)),
        pl.BlockSpec((tm, D_MODEL), lambda i: (i, 0)),
    ]
    return pl.pallas_call(
        functools.partial(_mix_sample_kernel, tm=tm),
        grid=(nt,),
        in_specs=in_specs,
        out_specs=out_specs,
        out_shape=[jax.ShapeDtypeStruct((nt, tm, D_MODEL), F32),
                   jax.ShapeDtypeStruct(s0.shape, F32),
                   jax.ShapeDtypeStruct((nt * nreq, 4, CM_WIDTH), F32),
                   jax.ShapeDtypeStruct((nt * tm, D_MODEL), F32)],
        scratch_shapes=[pltpu.VMEM((tm, IN_WIDTH), F32), pltpu.VMEM((tm, D_MODEL), BF16)]
        + [pltpu.VMEM((tm, HG_WIDTH), F32)] * 3
        + [pltpu.VMEM((CM_GROUPS, CHUNK, CHUNK), F32), pltpu.VMEM((CHUNK, CM_GROUPS), F32)],
        compiler_params=pltpu.CompilerParams(dimension_semantics=("arbitrary",),
                                             vmem_limit_bytes=VMEM_LIMIT_BYTES),
        name="mix_sample",
    )(x, s0, *consts)


def _memkv_kernel(m_ref, g_ref, wk_ref, wv_ref, k_ref, v_ref, kb_ref, vb_ref):
    mn = _rmsnorm(m_ref[...], g_ref[...]).astype(BF16)
    k = jnp.dot(mn, wk_ref[...], preferred_element_type=F32)
    v = jnp.dot(mn, wv_ref[...], preferred_element_type=F32)
    kb_ref[...] = k.astype(BF16)
    vb_ref[...] = v.astype(BF16)
    for src, dst in ((k, k_ref), (v, v_ref)):
        for r in range(dst.shape[0]):
            for h in range(X_HEADS):
                dst[r, :, h, :] = src[r * N_MEM:(r + 1) * N_MEM, h * X_HEAD_DIM:(h + 1) * X_HEAD_DIM]


def _memkv(mem, g_mem, w_ck, w_cv, tm):
    n = mem.shape[0]
    return pl.pallas_call(
        _memkv_kernel,
        grid=(n // tm,),
        in_specs=[pl.BlockSpec((tm, D_MODEL), lambda i: (i, 0)), _const_spec(g_mem.shape),
                  _const_spec(w_ck.shape), _const_spec(w_cv.shape)],
        out_specs=[pl.BlockSpec((tm // N_MEM, N_MEM, X_HEADS, X_HEAD_DIM), lambda i: (i, 0, 0, 0))] * 2
        + [pl.BlockSpec((tm, D_MODEL), lambda i: (i, 0))] * 2,
        out_shape=[jax.ShapeDtypeStruct((n // N_MEM, N_MEM, X_HEADS, X_HEAD_DIM), F32)] * 2
        + [jax.ShapeDtypeStruct((n, D_MODEL), BF16)] * 2,
        compiler_params=pltpu.CompilerParams(dimension_semantics=("arbitrary",),
                                             vmem_limit_bytes=VMEM_LIMIT_BYTES),
        name="memkv",
    )(mem, g_mem, w_ck, w_cv)


def _post_sample_kernel(x_ref, ca_ref, wco_ref, gm_ref, wup_ref, wdn_ref, gf_ref, y_ref, hn_s, acc_s):
    p = pl.program_id(0)

    @pl.when(p == 0)
    def _():
        x2 = x_ref[0] + jnp.dot(ca_ref[...].astype(BF16), wco_ref[...], preferred_element_type=F32)
        acc_s[...] = x2
        hn_s[...] = _rmsnorm(x2, gm_ref[...]).astype(BF16)

    hm = jnp.maximum(jnp.dot(hn_s[...], wup_ref[...], preferred_element_type=F32), 0.0)
    acc_s[...] += jnp.dot((hm * hm).astype(BF16), wdn_ref[...], preferred_element_type=F32)

    @pl.when(p == pl.num_programs(0) - 1)
    def _():
        y_ref[...] = _rmsnorm(acc_s[...], gf_ref[...]).reshape(y_ref.shape)


def _post_sample(x, ca, w_co, g_mlp, w_up, w_down, g_final):
    _, tm, _ = x.shape
    in_specs = [
        pl.BlockSpec((1, tm, D_MODEL), lambda p: (0, 0, 0)),
        pl.BlockSpec((tm, D_MODEL), lambda p: (0, 0)),
        _const_spec(w_co.shape), _const_spec(g_mlp.shape),
        pl.BlockSpec((D_MODEL, D_MODEL), lambda p: (0, p)),
        pl.BlockSpec((D_MODEL, D_MODEL), lambda p: (p, 0)),
        _const_spec(g_final.shape),
    ]
    return pl.pallas_call(
        _post_sample_kernel,
        grid=(D_FF // D_MODEL,),
        in_specs=in_specs,
        out_specs=pl.BlockSpec((tm // 4, 4, D_MODEL), lambda p: (0, 0, 0)),
        out_shape=jax.ShapeDtypeStruct((tm // 4, 4, D_MODEL), F32),
        scratch_shapes=[pltpu.VMEM((tm, D_MODEL), BF16), pltpu.VMEM((tm, D_MODEL), F32)],
        compiler_params=pltpu.CompilerParams(dimension_semantics=("arbitrary",),
                                             vmem_limit_bytes=VMEM_LIMIT_BYTES),
        name="post_sample",
    )(x, ca, w_co, g_mlp, w_up, w_down, g_final)


def _cast_kernel(*refs):
    n = len(refs) // 2
    for src, dst in zip(refs[:n], refs[n:]):
        dst[...] = src[...].astype(BF16)


def _cast_weights(*ws):
    steps = 8
    specs = [pl.BlockSpec((w.shape[0] // steps, w.shape[1]), lambda i: (i, 0)) for w in ws]
    return pl.pallas_call(
        _cast_kernel,
        grid=(steps,),
        in_specs=specs,
        out_specs=specs,
        out_shape=[jax.ShapeDtypeStruct(w.shape, BF16) for w in ws],
        compiler_params=pltpu.CompilerParams(dimension_semantics=("arbitrary",),
                                             vmem_limit_bytes=VMEM_LIMIT_BYTES),
        name="cast_weights",
    )(*ws)


def kernel(x_prompt, x_sample, mem_prompt, state_hgrn, cache_mem_k, cache_mem_v, lb_param, g_mix,
           w_in, hg_norm_g, ln_v_g, ln_v_b, w_s, b_s, w_out, g_cross, g_mem, w_cq, w_ck, w_cv, w_co,
           g_mlp, w_up, w_down, g_final):
    B, L, _ = x_prompt.shape
    DB, DL, _ = x_sample.shape
    assert DL == 4 and g_mix.shape[0] == 1

    row = lambda a: a.reshape(1, -1)
    win_b, wout_b, wcq_b, wco_b, wck_b, wcv_b, wup_b, wdn_b = _cast_weights(
        w_in[0], w_out[0], w_cq[0], w_co[0], w_ck[0], w_cv[0], w_up[0], w_down[0])
    gmix, hgn, lng, lnb = row(g_mix[0]), row(hg_norm_g[0]), row(ln_v_g[0]), row(ln_v_b[0])
    gcr, gmem, gmlp, gfin = row(g_cross[0]), row(g_mem[0]), row(g_mlp[0]), row(g_final)
    wmix = w_s[0]
    bcol = b_s[0].T

    tm_s = 128
    h_s, s_s, vn_s, hq_s = _mix_sample(x_sample, state_hgrn[0], lb_param, gmix, win_b, hgn, lng, lnb, wmix,
                                       bcol, wout_b, gcr, wcq_b, tm=tm_s)

    mk, mv, mk_b, mv_b = _memkv(mem_prompt.reshape(B * N_MEM, D_MODEL), gmem, wck_b, wcv_b, tm=512)
    x2_p, s_p, ca_s = _mixattn_layer(x_prompt, mk_b.reshape(B, N_MEM, D_MODEL),
                                     mv_b.reshape(B, N_MEM, D_MODEL), hq_s, cache_mem_k[0], cache_mem_v[0],
                                     lb_param, gmix, win_b, hgn, lng, lnb, wmix, bcol, wout_b, gcr, wcq_b,
                                     wco_b, tm=512)
    y_p = _mlp_layer(x2_p, gmlp, wup_b, wdn_b, gfin, tm=1024)

    y_s = _post_sample(h_s.reshape(1, DB * DL, D_MODEL), ca_s, wco_b, gmlp, wup_b, wdn_b, gfin)

    return (y_p, y_s, s_p[None], s_s[None],
            mk[None], mv[None],
            vn_s[None])
```
